```python
import math
import jax
import jax.numpy as jnp
from jax import lax
import numpy as np

D_MODEL = 2048
BATCH = 2
SEQ = 4096
DEPTH = 4

GRID_W = 64
CTX_LEN = 256
N_MIXERS = 3
RMS_EPS = 1e-6
SSM_EXPAND = 2
SSM_INNER = SSM_EXPAND * D_MODEL
SSM_HEAD_DIM = 64
SSM_HEADS = SSM_INNER // SSM_HEAD_DIM
SSM_STATE = 128
SSM_GROUPS = 8
SSM_CONV_W = 3
SSM_CHUNK = 128
SSM_CONV_DIM = SSM_INNER + 2 * SSM_GROUPS * SSM_STATE
SSM_IN_DIM = SSM_INNER + SSM_CONV_DIM + 2 * SSM_HEADS
HG_KEY_DIM = 128
HG_HEADS = D_MODEL // HG_KEY_DIM
HG_VAL_DIM = D_MODEL // HG_HEADS
HG_KDIM = HG_HEADS * HG_KEY_DIM
HG_IN_DIM = 3 * HG_KDIM + 2 * D_MODEL
HG_CHUNK = 64
NA_HEAD_DIM = 128
NA_HEADS = D_MODEL // NA_HEAD_DIM
NA_WIN_R = 8
NA_WIN_C = 16
ROPE_BASE = 10000.0
FFN_HIDDEN = 5632
FFN_CONV_W = 3

kernel_name = 'hybrid_ssd_hgrn2_natten_convffn_trunk'


def rms_norm(x, g):
    xf = x.astype(jnp.float32)
    y = xf * lax.rsqrt(jnp.mean(xf * xf, axis=-1, keepdims=True) + RMS_EPS)
    return (y * g).astype(x.dtype)


def dwconv_centred(x, w, b):
    k_w = w.shape[0]
    left = k_w // 2
    length = x.shape[1]
    xp = jnp.pad(x, ((0, 0), (left, k_w - 1 - left), (0, 0)))
    y = b
    for j in range(k_w):
        y = y + xp[:, j:j + length] * w[j]
    return y


def _rev(t, rev):
    return jnp.flip(t, axis=1) if rev else t


def _adaln(cvec, w, b):
    return jnp.split(jax.nn.silu(cvec) @ w + b, 6, axis=-1)


def ssd_chunked(x, dt, a, b_in, c_in, s0):
    bsz, length, n_h, p_dim = x.shape
    n_g, n_s = b_in.shape[2], b_in.shape[3]
    hg, nc, q_len = n_h // n_g, length // SSM_CHUNK, SSM_CHUNK
    xc = x.reshape(bsz, nc, q_len, n_g, hg, p_dim)
    dtc = dt.reshape(bsz, nc, q_len, n_g, hg)
    bc = b_in.reshape(bsz, nc, q_len, n_g, n_s)
    cc = c_in.reshape(bsz, nc, q_len, n_g, n_s)
    acum = jnp.cumsum(dtc * a.reshape(n_g, hg), axis=2)
    lower = jnp.tril(jnp.ones((q_len, q_len), dtype=bool))
    seg = acum[:, :, :, None] - acum[:, :, None]
    decay = jnp.exp(jnp.where(lower[:, :, None, None], seg, -jnp.inf))
    cb = jnp.einsum('bcign,bcjgn->bcijg', cc, bc)
    scores = cb[..., None] * decay * dtc[:, :, None]
    y_intra = jnp.einsum('bcijgh,bcjghp->bcighp', scores, xc)
    to_end = jnp.exp(acum[:, :, -1:] - acum) * dtc
    chunk_states = jnp.einsum('bcqgn,bcqgh,bcqghp->bcghpn', bc, to_end, xc)
    chunk_decay = jnp.exp(acum[:, :, -1])

    def step(s, inp):
        dec, st = inp
        return (dec[..., None, None] * s + st).astype(s.dtype), s

    s_fin, s_in = lax.scan(step, s0, (jnp.moveaxis(chunk_decay, 1, 0), jnp.moveaxis(chunk_states, 1, 0)))
    s_in = jnp.moveaxis(s_in, 0, 1)
    y_inter = jnp.einsum('bcqgn,bcghpn->bcqghp', cc, s_in) * jnp.exp(acum)[..., None]
    return (y_intra + y_inter).reshape(bsz, length, n_h, p_dim), s_fin


def _ssd_inputs(h, w_in, conv_w, conv_b):
    bsz, length, _ = h.shape
    proj = h @ w_in
    z = proj[..., :SSM_INNER]
    xbc = proj[..., SSM_INNER:SSM_INNER + SSM_CONV_DIM]
    dt_raw = proj[..., SSM_INNER + SSM_CONV_DIM:]
    xbc = jax.nn.silu(dwconv_centred(xbc, conv_w, conv_b))
    gn = SSM_GROUPS * SSM_STATE
    xs = xbc[..., :SSM_INNER].reshape(bsz, length, SSM_HEADS, SSM_HEAD_DIM)
    b_in = xbc[..., SSM_INNER:SSM_INNER + gn].reshape(bsz, length, SSM_GROUPS, SSM_STATE)
    c_in = xbc[..., SSM_INNER + gn:].reshape(bsz, length, SSM_GROUPS, SSM_STATE)
    return z, xs, b_in, c_in, dt_raw


def mamba2_mixer(h_ctx, h_lat, w_in, conv_w, conv_b, dt_bias, a_log, d_skip, norm_g, w_out, ctx_out):
    zc, xc, bc, cc, dtc_raw = _ssd_inputs(h_ctx, w_in, conv_w, conv_b)
    zl, xl, bl, cl, dtl_raw = _ssd_inputs(h_lat, w_in, conv_w, conv_b)
    bsz = h_lat.shape[0]
    yc = jnp.zeros_like(xc)
    yl = jnp.zeros_like(xl)
    for d in range(2):
        rev = d == 1
        a = -jnp.exp(a_log[d])
        cols = slice(d * SSM_HEADS, (d + 1) * SSM_HEADS)
        s0 = jnp.zeros((bsz, SSM_GROUPS, SSM_HEADS // SSM_GROUPS, SSM_HEAD_DIM, SSM_STATE), xl.dtype)
        dt_c = jax.nn.softplus(dtc_raw[..., cols] + dt_bias[d])
        y_d, s_ctx = ssd_chunked(_rev(xc, rev), _rev(dt_c, rev), a, _rev(bc, rev), _rev(cc, rev), s0)
        yc = yc + _rev(y_d, rev) + d_skip[d][:, None] * xc
        dt_l = jax.nn.softplus(dtl_raw[..., cols] + dt_bias[d])
        y_d, _ = ssd_chunked(_rev(xl, rev), _rev(dt_l, rev), a, _rev(bl, rev), _rev(cl, rev), s_ctx)
        yl = yl + _rev(y_d, rev) + d_skip[d][:, None] * xl

    def gate_out(y, z):
        b_, length = y.shape[:2]
        return rms_norm(y.reshape(b_, length, SSM_INNER) * jax.nn.silu(z), norm_g) @ w_out

    return (gate_out(yc, zc) if ctx_out else None), gate_out(yl, zl)


def hgrn2_lower_bounds(lb_param):
    cs = jnp.cumsum(jax.nn.softmax(lb_param.astype(jnp.float32), axis=1), axis=1)
    return cs - cs[:, :1]


def gla_chunked(q, k, v, logf, s0):
    bsz, length, n_h, _ = q.shape
    nc, q_len = length // HG_CHUNK, HG_CHUNK

    def chunks(t):
        return t.reshape(bsz, nc, q_len, n_h, t.shape[-1]).transpose(1, 0, 3, 2, 4)

    lower = jnp.tril(jnp.ones((q_len, q_len), dtype=bool))

    def step(s, inp):
        qc, kc, vc, gc = inp
        bcum = jnp.cumsum(gc, axis=2)
        seg = bcum[:, :, :, None, :] - bcum[:, :, None, :, :]
        decay = jnp.exp(jnp.where(lower[:, :, None], seg, -jnp.inf))
        attn = jnp.einsum('bhic,bhjc,bhijc->bhij', qc, kc, decay)
        o = jnp.einsum('bhij,bhjv->bhiv', attn, vc) + jnp.einsum('bhic,bhcv->bhiv', qc * jnp.exp(bcum), s)
        blast = bcum[:, :, -1:]
        s_new = jnp.exp(blast[:, :, 0, :, None]) * s + jnp.einsum('bhjc,bhjv->bhcv', kc * jnp.exp(blast - bcum), vc)
        return s_new.astype(s.dtype), o

    s_fin, o = lax.scan(step, s0, (chunks(q), chunks(k), chunks(v), chunks(logf)))
    return o.transpose(1, 0, 3, 2, 4).reshape(bsz, length, n_h, v.shape[-1]), s_fin


def _log_forget(f, lb):
    return jnp.logaddexp(jnp.log(lb), jnp.log1p(-lb) + jax.nn.log_sigmoid(f.astype(jnp.float32)))


def _hgrn2_inputs(h, w_in, lb):
    bsz, length, _ = h.shape
    q, v, f_fw, f_bw, g = jnp.split(
        h @ w_in, [HG_KDIM, HG_KDIM + D_MODEL, 2 * HG_KDIM + D_MODEL, 3 * HG_KDIM + D_MODEL], axis=-1)

    def heads(t):
        return t.reshape(bsz, length, HG_HEADS, -1)

    logf = [heads(_log_forget(f, lb[d]).astype(h.dtype)) for d, f in enumerate((f_fw, f_bw))]
    return heads(jax.nn.silu(q)), heads(v), logf, g


def hgrn2_mixer(h_ctx, h_lat, w_in, lb, norm_g, w_out, ctx_out):
    qc, vc, logf_c, gc = _hgrn2_inputs(h_ctx, w_in, lb)
    ql, vl, logf_l, gl = _hgrn2_inputs(h_lat, w_in, lb)
    bsz = h_lat.shape[0]
    oc = jnp.zeros_like(vc)
    ol = jnp.zeros_like(vl)
    for d in range(2):
        rev = d == 1
        s0 = jnp.zeros((bsz, HG_HEADS, HG_KEY_DIM, HG_VAL_DIM), h_lat.dtype)
        kc = -jnp.expm1(logf_c[d])
        o_d, s_ctx = gla_chunked(_rev(qc, rev), _rev(kc, rev), _rev(vc, rev), _rev(logf_c[d], rev), s0)
        oc = oc + _rev(o_d, rev)
        kl = -jnp.expm1(logf_l[d])
        o_d, _ = gla_chunked(_rev(ql, rev), _rev(kl, rev), _rev(vl, rev), _rev(logf_l[d], rev), s_ctx)
        ol = ol + _rev(o_d, rev)

    def readout(o, g):
        b_, length = o.shape[:2]
        o = rms_norm(o, norm_g.reshape(HG_HEADS, HG_VAL_DIM)).reshape(b_, length, D_MODEL)
        return (o * jax.nn.silu(g)) @ w_out

    return (readout(oc, gc) if ctx_out else None), readout(ol, gl)


def _rope_1d(v, pos):
    half = v.shape[-1] // 2
    inv = ROPE_BASE ** (-jnp.arange(half, dtype=jnp.float32) / half)
    ang = pos.astype(jnp.float32)[:, None] * inv[None]
    cos, sin = jnp.cos(ang)[:, None, :], jnp.sin(ang)[:, None, :]
    v1, v2 = v[..., :half], v[..., half:]
    return jnp.concatenate([v1 * cos - v2 * sin, v2 * cos + v1 * sin], axis=-1).astype(v.dtype)


def axial_rope(v, row, col):
    half = v.shape[-1] // 2
    return jnp.concatenate([_rope_1d(v[..., :half], row), _rope_1d(v[..., half:], col)], axis=-1)


def na_mixer(h_ctx, h_lat, w_qkv, rpb, w_out, ctx_out):
    bsz, length, _ = h_lat.shape
    rows = length // GRID_W
    win_r = min(NA_WIN_R, rows)
    scale = NA_HEAD_DIM ** -0.5

    def heads(t):
        return t.reshape(t.shape[0], t.shape[1], NA_HEADS, NA_HEAD_DIM)

    q, k, v = [heads(t) for t in jnp.split(h_lat @ w_qkv, 3, axis=-1)]
    qc, kc, vc = [heads(t) for t in jnp.split(h_ctx @ w_qkv, 3, axis=-1)]
    t_idx = jnp.arange(length)
    row, col = t_idx // GRID_W, t_idx % GRID_W
    q = axial_rope(q, row, col)
    k = axial_rope(k, row, col)

    qg = q.reshape(bsz, rows, GRID_W, NA_HEADS, NA_HEAD_DIM)
    kg = k.reshape(bsz, rows, GRID_W, NA_HEADS, NA_HEAD_DIM)
    vg = v.reshape(bsz, rows, GRID_W, NA_HEADS, NA_HEAD_DIM)
    r_idx = jnp.arange(rows)
    r0 = jnp.clip(r_idx - win_r // 2, 0, rows - win_r)
    key_rows = r0[:, None] + jnp.arange(win_r)[None]
    k_win = kg[:, key_rows]
    v_win = vg[:, key_rows]
    c_idx = jnp.arange(GRID_W)
    c0 = jnp.clip(c_idx - NA_WIN_C // 2, 0, GRID_W - NA_WIN_C)
    col_in = (c_idx[None] >= c0[:, None]) & (c_idx[None] < c0[:, None] + NA_WIN_C)
    dr = key_rows - r_idx[:, None] + NA_WIN_R - 1
    dc = jnp.clip(c_idx[None] - c_idx[:, None] + NA_WIN_C - 1, 0, 2 * NA_WIN_C - 2)
    bias = rpb[:, dr[:, None, :, None], dc[None, :, None, :]]
    bias = jnp.where(col_in[None, None, :, None, :], bias, -jnp.inf)

    s_loc = jnp.einsum('brqhd,brwkhd->bhrqwk', qg, k_win) * scale + bias[None]
    s_ctx = jnp.einsum('brqhd,bkhd->bhrqk', qg, kc) * scale
    n_loc = win_r * GRID_W
    s_all = jnp.concatenate([s_loc.reshape(bsz, NA_HEADS, rows, GRID_W, n_loc), s_ctx], axis=-1)
    p = jax.nn.softmax(s_all.astype(jnp.float32), axis=-1).astype(v.dtype)
    p_loc = p[..., :n_loc].reshape(bsz, NA_HEADS, rows, GRID_W, win_r, GRID_W)
    p_ctx = p[..., n_loc:]
    o = jnp.einsum('bhrqwk,brwkhd->brqhd', p_loc, v_win) + jnp.einsum('bhrqk,bkhd->brqhd', p_ctx, vc)
    y_lat = o.reshape(bsz, length, D_MODEL) @ w_out

    y_ctx = None
    if ctx_out:
        sc = jnp.einsum('bqhd,bkhd->bhqk', qc, kc) * scale
        pc = jax.nn.softmax(sc.astype(jnp.float32), axis=-1).astype(vc.dtype)
        oc = jnp.einsum('bhqk,bkhd->bqhd', pc, vc)
        y_ctx = oc.reshape(bsz, oc.shape[1], D_MODEL) @ w_out
    return y_ctx, y_lat


def conv_ffn(h, w_up, conv_w, conv_b, w_down):
    u = dwconv_centred(h @ w_up, conv_w, conv_b)
    a, v = jnp.split(u, 2, axis=-1)
    return (jax.nn.silu(a) * v) @ w_down


def setup_inputs(seed: int = 0) -> dict:
    key = jax.random.key(seed)
    ks = iter(jax.random.split(key, 32))
    f32 = jnp.float32
    dm = D_MODEL

    def nrm(shape, scale):
        return jax.random.normal(next(ks), shape, f32) * scale

    def gain(shape):
        return 1.0 + nrm(shape, 0.05)

    n_a = len(range(0, DEPTH, N_MIXERS))
    n_b = len(range(1, DEPTH, N_MIXERS))
    n_c = len(range(2, DEPTH, N_MIXERS))
    dt0 = jnp.exp(jax.random.uniform(next(ks), (n_a, 2, SSM_HEADS), f32, math.log(1e-3), math.log(1e-1)))
    dt_bias = dt0 + jnp.log(-jnp.expm1(-dt0))
    a_log = jnp.log(jax.random.uniform(next(ks), (n_a, 2, SSM_HEADS), f32, 1.0, 16.0))
    return {
        'x': nrm((BATCH, SEQ, dm), 1.0),
        'c': nrm((BATCH, dm), 1.0),
        'ctx': nrm((BATCH, CTX_LEN, dm), 1.0),
        'c_ctx': nrm((dm,), 1.0),
        'w_mod': nrm((DEPTH, dm, 6 * dm), 0.5 * dm ** -0.5),
        'b_mod': nrm((DEPTH, 6 * dm), 0.02),
        'norm_g': gain((DEPTH, 4, dm)),
        'ffn_w_up': nrm((DEPTH, dm, 2 * FFN_HIDDEN), dm ** -0.5),
        'ffn_conv_w': nrm((DEPTH, FFN_CONV_W, 2 * FFN_HIDDEN), FFN_CONV_W ** -0.5),
        'ffn_conv_b': nrm((DEPTH, 2 * FFN_HIDDEN), 0.02),
        'ffn_w_down': nrm((DEPTH, FFN_HIDDEN, dm), FFN_HIDDEN ** -0.5),
        'ssm_w_in': nrm((n_a, dm, SSM_IN_DIM), dm ** -0.5),
        'ssm_conv_w': nrm((n_a, SSM_CONV_W, SSM_CONV_DIM), SSM_CONV_W ** -0.5),
        'ssm_conv_b': nrm((n_a, SSM_CONV_DIM), 0.02),
        'ssm_dt_bias': dt_bias,
        'ssm_a_log': a_log,
        'ssm_d': 1.0 + nrm((n_a, 2, SSM_HEADS), 0.1),
        'ssm_norm_g': gain((n_a, SSM_INNER)),
        'ssm_w_out': nrm((n_a, SSM_INNER, dm), SSM_INNER ** -0.5),
        'hg_w_in': nrm((n_b, dm, HG_IN_DIM), dm ** -0.5),
        'hg_lb': 1.0 + nrm((2, DEPTH, HG_KDIM), 0.1),
        'hg_norm_g': gain((n_b, dm)),
        'hg_w_out': nrm((n_b, dm, dm), dm ** -0.5),
        'na_w_qkv': nrm((n_c, dm, 3 * dm), dm ** -0.5),
        'na_rpb': nrm((n_c, NA_HEADS, 2 * NA_WIN_R - 1, 2 * NA_WIN_C - 1), 0.02),
        'na_w_out': nrm((n_c, dm, dm), dm ** -0.5),
    }


def reference(x, c, ctx, c_ctx, w_mod, b_mod, norm_g, ffn_w_up, ffn_conv_w, ffn_conv_b, ffn_w_down,
              ssm_w_in, ssm_conv_w, ssm_conv_b, ssm_dt_bias, ssm_a_log, ssm_d, ssm_norm_g, ssm_w_out,
              hg_w_in, hg_lb, hg_norm_g, hg_w_out, na_w_qkv, na_rpb, na_w_out):
    lower_bounds = hgrn2_lower_bounds(hg_lb)
    x_lat, x_ctx = x, ctx
    for i in range(DEPTH):
        last = i == DEPTH - 1
        kind, slot = i % N_MIXERS, i // N_MIXERS
        sh, sc, gt, fsh, fsc, fgt = [m[:, None, :] for m in _adaln(c, w_mod[i], b_mod[i])]
        csh, csc, cgt, cfsh, cfsc, cfgt = _adaln(c_ctx, w_mod[i], b_mod[i])
        h_lat = rms_norm(x_lat, norm_g[i, 0]) * (1.0 + sc) + sh
        h_ctx = rms_norm(x_ctx, norm_g[i, 0]) * (1.0 + csc) + csh
        if kind == 0:
            y_ctx, y_lat = mamba2_mixer(h_ctx, h_lat, ssm_w_in[slot], ssm_conv_w[slot], ssm_conv_b[slot],
                                        ssm_dt_bias[slot], ssm_a_log[slot], ssm_d[slot], ssm_norm_g[slot],
                                        ssm_w_out[slot], not last)
        elif kind == 1:
            y_ctx, y_lat = hgrn2_mixer(h_ctx, h_lat, hg_w_in[slot], lower_bounds[:, i], hg_norm_g[slot],
                                       hg_w_out[slot], not last)
        else:
            y_ctx, y_lat = na_mixer(h_ctx, h_lat, na_w_qkv[slot], na_rpb[slot], na_w_out[slot], not last)
        x_lat = x_lat + gt * rms_norm(y_lat, norm_g[i, 1])
        h_lat = rms_norm(x_lat, norm_g[i, 2]) * (1.0 + fsc) + fsh
        x_lat = x_lat + fgt * rms_norm(conv_ffn(h_lat, ffn_w_up[i], ffn_conv_w[i], ffn_conv_b[i], ffn_w_down[i]),
                                       norm_g[i, 3])
        if not last:
            x_ctx = x_ctx + cgt * rms_norm(y_ctx, norm_g[i, 1])
            h_ctx = rms_norm(x_ctx, norm_g[i, 2]) * (1.0 + cfsc) + cfsh
            x_ctx = x_ctx + cfgt * rms_norm(conv_ffn(h_ctx, ffn_w_up[i], ffn_conv_w[i], ffn_conv_b[i],
                                                     ffn_w_down[i]), norm_g[i, 3])
    return x_lat
```

```python
import functools

import numpy as np
import jax
import jax.numpy as jnp
from jax import lax
from jax.experimental import pallas as pl
from jax.experimental.pallas import tpu as pltpu

F32 = jnp.float32
BF16 = jnp.bfloat16

D_MODEL = 2048
BATCH = 2
SEQ = 4096
DEPTH = 4
GRID_W = 64
CTX_LEN = 256
N_MIXERS = 3
RMS_EPS = 1e-6
SSM_INNER = 4096
SSM_HEAD_DIM = 64
SSM_HEADS = 64
SSM_STATE = 128
SSM_GROUPS = 8
SSM_CONV_DIM = SSM_INNER + 2 * SSM_GROUPS * SSM_STATE
SSM_IN_DIM = SSM_INNER + SSM_CONV_DIM + 2 * SSM_HEADS
SSM_CHUNK = 128
HG_HEADS = 16
HG_KEY_DIM = 128
HG_IN_DIM = 5 * D_MODEL
HG_CHUNK = 64
NA_HEADS = 16
NA_HEAD_DIM = 128
NA_WIN_R = 8
NA_WIN_C = 16
ROPE_BASE = 10000.0
FFN_HIDDEN = 5632

T_LAT = BATCH * SEQ
T_CTX = BATCH * CTX_LEN
T_ALL = T_LAT + T_CTX
LANES = 128
SUBLANES = 8
ONES_ROWS = 16
MIB = 1024 * 1024


def _cparams(sem, vmem_mib):
    return pltpu.CompilerParams(dimension_semantics=sem, vmem_limit_bytes=vmem_mib * MIB)


def _silu(v):
    return v * jax.nn.sigmoid(v)


def _rms(v):
    return v * lax.rsqrt(jnp.mean(v * v, axis=-1, keepdims=True) + RMS_EPS)


def _softplus(v):
    return jnp.maximum(v, 0.0) + jnp.log1p(jnp.exp(-jnp.abs(v)))


def _dot(a, b):
    return jnp.dot(a, b, preferred_element_type=F32)


def _dot_nt(a, b):
    return lax.dot_general(a, b, (((1,), (1,)), ((), ())), preferred_element_type=F32)


def _dot_tn(a, b):
    return lax.dot_general(a, b, (((0,), (0,)), ((), ())), preferred_element_type=F32)


def _dot_sel(m01, v, terms):
    acc = None
    rem = v
    for _ in range(terms):
        part = rem.astype(BF16)
        rem = rem - part.astype(F32)
        d = _dot(m01, part)
        acc = d if acc is None else acc + d
    return acc


def _dot_sel_rhs(v, m01):
    acc = None
    rem = v
    for _ in range(3):
        part = rem.astype(BF16)
        rem = rem - part.astype(F32)
        d = _dot(part, m01)
        acc = d if acc is None else acc + d
    return acc


def _mod_row(row0):
    return jnp.where(row0 >= T_LAT, 2, row0 // SEQ)


def _seg_edges(row0, rows):
    gr = row0 + lax.broadcasted_iota(jnp.int32, (rows, 1), 0)
    pos = jnp.where(gr >= T_LAT, (gr - T_LAT) % CTX_LEN, gr % SEQ)
    length = jnp.where(gr >= T_LAT, CTX_LEN, SEQ)
    return pos == 0, pos == length - 1


def _shift_rows(cur, prev8, next8, row0):
    rows = cur.shape[0]
    first, last = _seg_edges(row0, rows)
    ridx = lax.broadcasted_iota(jnp.int32, (rows, 1), 0)
    up = jnp.where(ridx == 0, prev8[SUBLANES - 1:SUBLANES, :], pltpu.roll(cur, 1, axis=0))
    up = jnp.where(first, 0.0, up)
    dn = jnp.where(ridx == rows - 1, next8[0:1, :], pltpu.roll(cur, rows - 1, axis=0))
    dn = jnp.where(last, 0.0, dn)
    return up, dn


def _halo_specs(tm, width, col_fn):
    per = tm // SUBLANES
    last_blk = T_ALL // SUBLANES - 1
    prev = pl.BlockSpec((SUBLANES, width), lambda i, j: (jnp.maximum(i * per - 1, 0), col_fn(j)))
    nxt = pl.BlockSpec((SUBLANES, width), lambda i, j: (jnp.minimum((i + 1) * per, last_blk), col_fn(j)))
    return prev, nxt


def _adaln_body(c_ref, w_ref, b_ref, o_ref):
    s = _silu(c_ref[...]).astype(BF16)
    o_ref[...] = _dot(s, w_ref[...].astype(BF16)) + b_ref[...]


def _adaln(cvec8, w_mod, b_mod):
    tn = 1024
    n_out = 6 * D_MODEL
    return pl.pallas_call(
        _adaln_body,
        grid=(DEPTH, n_out // tn),
        in_specs=[
            pl.BlockSpec((SUBLANES, D_MODEL), lambda l, j: (0, 0)),
            pl.BlockSpec((None, D_MODEL, tn), lambda l, j: (l, 0, j)),
            pl.BlockSpec((None, 1, tn), lambda l, j: (l, 0, j)),
        ],
        out_specs=pl.BlockSpec((None, SUBLANES, tn), lambda l, j: (l, 0, j)),
        out_shape=jax.ShapeDtypeStruct((DEPTH, SUBLANES, n_out), F32),
        compiler_params=_cparams(("arbitrary", "arbitrary"), 40),
        name="adaln",
    )(cvec8, w_mod, b_mod.reshape(DEPTH, 1, n_out))


def _fused_mm(prologue, row_ins, aux_ins, w, *, tm, tn, vmem_mib, name):
    k_dim, n_dim = w.shape
    n_row, n_aux = len(row_ins), len(aux_ins)

    def body(*refs):
        rows = refs[:n_row]
        auxs = refs[n_row:n_row + n_aux]
        w_ref, o_ref, h_ref = refs[n_row + n_aux:]

        @pl.when(pl.program_id(1) == 0)
        def _():
            h_ref[...] = prologue(pl.program_id(0) * tm, *rows, *auxs).astype(BF16)

        o_ref[...] = _dot(h_ref[...], w_ref[...])

    in_specs = []
    for (arr, width, cb) in row_ins:
        if isinstance(arr, tuple):
            in_specs.append(pl.BlockSpec((None, tm, width),
                                         functools.partial(lambda i, j, ld, cb: (ld, i, cb), ld=arr[1], cb=cb)))
        else:
            in_specs.append(pl.BlockSpec((tm, width), functools.partial(lambda i, j, cb: (i, cb), cb=cb)))
    in_specs += [pl.BlockSpec(a.shape, functools.partial(lambda i, j, nd: (0,) * nd, nd=a.ndim)) for a in aux_ins]
    in_specs += [pl.BlockSpec((k_dim, tn), lambda i, j: (0, j))]
    return pl.pallas_call(
        body,
        grid=(T_ALL // tm, n_dim // tn),
        in_specs=in_specs,
        out_specs=pl.BlockSpec((tm, tn), lambda i, j: (i, j)),
        out_shape=jax.ShapeDtypeStruct((T_ALL, n_dim), F32),
        scratch_shapes=[pltpu.VMEM((tm, k_dim), BF16)],
        compiler_params=_cparams(("arbitrary", "arbitrary"), vmem_mib),
        name=name,
    )(*[a[0] if isinstance(a, tuple) else a for (a, _, _) in row_ins], *aux_ins, w)


def _norm_mod_prologue(shift_idx, scale_idx):
    def prologue(row0, x_ref, g_ref, mod_ref):
        r = _mod_row(row0)
        sh = mod_ref[pl.ds(r, 1), shift_idx * D_MODEL:(shift_idx + 1) * D_MODEL]
        sc = mod_ref[pl.ds(r, 1), scale_idx * D_MODEL:(scale_idx + 1) * D_MODEL]
        return _rms(x_ref[...]) * g_ref[...] * (1.0 + sc) + sh
    return prologue


def _norm_mod_mm(xs, g_row, mod_l, shift_idx, scale_idx, w, tn, name):
    return _fused_mm(_norm_mod_prologue(shift_idx, scale_idx), [(xs, D_MODEL, 0)], [g_row.reshape(1, D_MODEL), mod_l],
                     w, tm=512, tn=tn, vmem_mib=48, name=name)


def _ssd_gate_prologue(row0, yf_ref, yb_ref, z_ref, g_ref):
    y = (yf_ref[...] + yb_ref[...]) * _silu(z_ref[...])
    return _rms(y) * g_ref[...]


def _hg_readout_prologue(row0, of_ref, ob_ref, gate_ref, g_ref):
    o = of_ref[...] + ob_ref[...]
    parts = [_rms(o[:, h * LANES:(h + 1) * LANES]) for h in range(HG_HEADS)]
    return jnp.concatenate(parts, axis=-1) * g_ref[...] * _silu(gate_ref[...])


def _plain_prologue(row0, o_ref):
    return o_ref[...]


def _resid_body(gate_idx, tm, x_ref, y_ref, g_ref, mod_ref, o_ref):
    r = _mod_row(pl.program_id(0) * tm)
    gate = mod_ref[pl.ds(r, 1), gate_idx * D_MODEL:(gate_idx + 1) * D_MODEL]
    o_ref[...] = x_ref[...] + gate * (_rms(y_ref[...]) * g_ref[...])


def _resid_norm(xs, y, g_row, mod_l, gate_idx):
    tm = 512
    row = pl.BlockSpec((tm, D_MODEL), lambda i: (i, 0))
    return pl.pallas_call(
        functools.partial(_resid_body, gate_idx, tm),
        grid=(T_ALL // tm,),
        in_specs=[row, row, pl.BlockSpec((1, D_MODEL), lambda i: (0, 0)),
                  pl.BlockSpec(mod_l.shape, lambda i: (0, 0))],
        out_specs=row,
        out_shape=jax.ShapeDtypeStruct((T_ALL, D_MODEL), F32),
        compiler_params=_cparams(("arbitrary",), 40),
        name="resid_norm",
    )(xs, y, g_row.reshape(1, D_MODEL), mod_l)


def _ffn_down_body(tm, a_ref, ap_ref, an_ref, v_ref, vp_ref, vn_ref, wa_ref, wv_ref, ba_ref, bv_ref, wd_ref, o_ref):
    row0 = pl.program_id(0) * tm

    def conv(cur_ref, prev_ref, next_ref, w_ref, b_ref):
        cur = cur_ref[...]
        up, dn = _shift_rows(cur, prev_ref[...], next_ref[...], row0)
        return b_ref[...] + up * w_ref[0:1, :] + cur * w_ref[1:2, :] + dn * w_ref[2:3, :]

    act = _silu(conv(a_ref, ap_ref, an_ref, wa_ref, ba_ref)) * conv(v_ref, vp_ref, vn_ref, wv_ref, bv_ref)
    part = _dot(act.astype(BF16), wd_ref[...])

    @pl.when(pl.program_id(1) == 0)
    def _():
        o_ref[...] = part

    @pl.when(pl.program_id(1) != 0)
    def _():
        o_ref[...] += part


def _ffn_down(u, conv_w, conv_b, w_down):
    tm, tk = 512, 1408
    kt = FFN_HIDDEN // tk
    a_prev, a_next = _halo_specs(tm, tk, lambda k: k)
    v_prev, v_next = _halo_specs(tm, tk, lambda k: k + kt)
    return pl.pallas_call(
        functools.partial(_ffn_down_body, tm),
        grid=(T_ALL // tm, kt),
        in_specs=[
            pl.BlockSpec((tm, tk), lambda i, k: (i, k)), a_prev, a_next,
            pl.BlockSpec((tm, tk), lambda i, k: (i, k + kt)), v_prev, v_next,
            pl.BlockSpec((3, tk), lambda i, k: (0, k)),
            pl.BlockSpec((3, tk), lambda i, k: (0, k + kt)),
            pl.BlockSpec((1, tk), lambda i, k: (0, k)),
            pl.BlockSpec((1, tk), lambda i, k: (0, k + kt)),
            pl.BlockSpec((tk, D_MODEL), lambda i, k: (k, 0)),
        ],
        out_specs=pl.BlockSpec((tm, D_MODEL), lambda i, k: (i, 0)),
        out_shape=jax.ShapeDtypeStruct((T_ALL, D_MODEL), F32),
        compiler_params=_cparams(("arbitrary", "arbitrary"), 56),
        name="ffn_down",
    )(u, u, u, u, u, u, conv_w, conv_w, conv_b.reshape(1, -1), conv_b.reshape(1, -1), w_down)


def _ssd_conv_body(tm, cur_ref, prev_ref, next_ref, w_ref, b_ref, o_ref):
    cur = cur_ref[...]
    up, dn = _shift_rows(cur, prev_ref[...], next_ref[...], pl.program_id(0) * tm)
    o_ref[...] = _silu(b_ref[...] + up * w_ref[0:1, :] + cur * w_ref[1:2, :] + dn * w_ref[2:3, :])


def _ssd_conv(proj, conv_w, conv_b):
    tm, tc = 512, 2048
    col0 = SSM_INNER // tc
    prev, nxt = _halo_specs(tm, tc, lambda j: j + col0)
    return pl.pallas_call(
        functools.partial(_ssd_conv_body, tm),
        grid=(T_ALL // tm, SSM_CONV_DIM // tc),
        in_specs=[pl.BlockSpec((tm, tc), lambda i, j: (i, j + col0)), prev, nxt,
                  pl.BlockSpec((3, tc), lambda i, j: (0, j)), pl.BlockSpec((1, tc), lambda i, j: (0, j))],
        out_specs=pl.BlockSpec((tm, tc), lambda i, j: (i, j)),
        out_shape=jax.ShapeDtypeStruct((T_ALL, SSM_CONV_DIM), F32),
        compiler_params=_cparams(("arbitrary", "arbitrary"), 40),
        name="ssd_conv",
    )(proj, proj, proj, conv_w, conv_b.reshape(1, -1))


def _scan_chunk_index(d, b, s, n_lat, n_ctx):
    ctx0 = BATCH * n_lat + b * n_ctx
    fwd = jnp.where(s < n_ctx, ctx0 + s, b * n_lat + s - n_ctx)
    bwd = jnp.where(s < n_ctx, ctx0 + n_ctx - 1 - s, b * n_lat + n_lat - 1 - (s - n_ctx))
    return jnp.where(d == 0, fwd, bwd)


def _ssd_scan_consts():
    q = SSM_CHUNK
    i = np.arange(q)
    pfx = np.zeros((2, q + ONES_ROWS, q), np.float32)
    pfx[0, :q] = i[:, None] >= i[None, :]
    pfx[1, :q] = i[:, None] <= i[None, :]
    pfx[:, q:] = 1.0
    expand = np.repeat(np.eye(SSM_HEADS, dtype=np.float32), SSM_HEAD_DIM, axis=1)
    return jnp.asarray(pfx, BF16), jnp.asarray(expand, BF16)


def _ssd_scan_body(x_ref, b_ref, c_ref, dtraw_ref, dtb_ref, alog_ref, dskip_ref, pfx_ref, exp_ref, y_ref, s_ref):
    q = SSM_CHUNK
    hp = 2 * SSM_HEAD_DIM
    d = pl.program_id(0)

    @pl.when(pl.program_id(2) == 0)
    def _():
        s_ref[...] = jnp.zeros_like(s_ref)

    def pick(v):
        return jnp.where(d == 0, v[:, :SSM_HEADS], v[:, SSM_HEADS:])

    dt2 = _softplus(dtraw_ref[...] + dtb_ref[...])
    cum2 = _dot_sel(pfx_ref[...], dt2 * (-jnp.exp(alog_ref[...])), 3)
    acum2 = cum2[:q]
    acum = pick(acum2)
    tot = pick(cum2[q:q + SUBLANES])
    acum_t2 = acum2.T
    acum_t = jnp.where(d == 0, acum_t2[:SSM_HEADS], acum_t2[SSM_HEADS:])

    expand = exp_ref[...]
    acum_wide = _dot_sel_rhs(acum, expand)
    dt_wide = _dot_sel_rhs(pick(dt2), expand)
    tot_wide = _dot_sel_rhs(tot, expand)[0:1]

    ii = lax.broadcasted_iota(jnp.int32, (q, q), 0)
    jj = lax.broadcasted_iota(jnp.int32, (q, q), 1)
    causal = (jj - ii) * (1 - 2 * d) <= 0
    low_half = lax.broadcasted_iota(jnp.int32, (q, hp), 1) < SSM_HEAD_DIM

    x = x_ref[...]
    xdt = x * dt_wide
    xdt_bf = xdt.astype(BF16)
    heads_per_group = SSM_HEADS // SSM_GROUPS
    for g in range(SSM_GROUPS):
        b_g = b_ref[:, g * SSM_STATE:(g + 1) * SSM_STATE].astype(BF16)
        c_g32 = c_ref[:, g * SSM_STATE:(g + 1) * SSM_STATE]
        cb = _dot_nt(c_g32.astype(BF16), b_g)
        for pair in range(heads_per_group // 2):
            lo = (g * heads_per_group + 2 * pair) * SSM_HEAD_DIM
            rhs = jnp.concatenate([xdt_bf[:, lo:lo + hp], s_ref[:, lo:lo + hp].astype(BF16)], axis=0)
            halves = []
            for sub in range(2):
                h = g * heads_per_group + 2 * pair + sub
                col = acum[:, h:h + 1]
                scores = jnp.where(causal, jnp.exp(col - acum_t[h:h + 1, :]) * cb, 0.0)
                c_dec = c_g32 * jnp.exp(col)
                lhs = jnp.concatenate([scores.astype(BF16), c_dec.astype(BF16)], axis=1)
                halves.append(_dot(lhs, rhs))
            y_pair = jnp.where(low_half, halves[0], halves[1])
            y_ref[:, lo:lo + hp] = y_pair + dskip_ref[:, lo:lo + hp] * x[:, lo:lo + hp]

    xw = (xdt * jnp.exp(tot_wide - acum_wide)).astype(BF16)
    dec = jnp.exp(tot_wide)
    gw = SSM_INNER // SSM_GROUPS
    for g in range(SSM_GROUPS):
        b_g = b_ref[:, g * SSM_STATE:(g + 1) * SSM_STATE].astype(BF16)
        s_ref[:, g * gw:(g + 1) * gw] = (s_ref[:, g * gw:(g + 1) * gw] * dec[:, g * gw:(g + 1) * gw]
                                         + _dot_tn(b_g, xw[:, g * gw:(g + 1) * gw]))


def _ssd_scan(xbc, proj, dt_bias, a_log, d_skip):
    q = SSM_CHUNK
    n_lat, n_ctx = SEQ // q, CTX_LEN // q
    pfx, expand = _ssd_scan_consts()
    n_heads2 = 2 * SSM_HEADS
    dt_col = (SSM_INNER + SSM_CONV_DIM) // n_heads2
    b_col = SSM_INNER // (SSM_GROUPS * SSM_STATE)
    d_wide = jnp.repeat(d_skip, SSM_HEAD_DIM, axis=1).reshape(2, 1, SSM_INNER)

    def chunk(d, b, s):
        return _scan_chunk_index(d, b, s, n_lat, n_ctx)

    return pl.pallas_call(
        _ssd_scan_body,
        grid=(2, BATCH, n_lat + n_ctx),
        in_specs=[
            pl.BlockSpec((q, SSM_INNER), lambda d, b, s: (chunk(d, b, s), 0)),
            pl.BlockSpec((q, SSM_GROUPS * SSM_STATE), lambda d, b, s: (chunk(d, b, s), b_col)),
            pl.BlockSpec((q, SSM_GROUPS * SSM_STATE), lambda d, b, s: (chunk(d, b, s), b_col + 1)),
            pl.BlockSpec((q, n_heads2), lambda d, b, s: (chunk(d, b, s), dt_col)),
            pl.BlockSpec((1, n_heads2), lambda d, b, s: (0, 0)),
            pl.BlockSpec((1, n_heads2), lambda d, b, s: (0, 0)),
            pl.BlockSpec((None, 1, SSM_INNER), lambda d, b, s: (d, 0, 0)),
            pl.BlockSpec((None, q + ONES_ROWS, q), lambda d, b, s: (d, 0, 0)),
            pl.BlockSpec((SSM_HEADS, SSM_INNER), lambda d, b, s: (0, 0)),
        ],
        out_specs=pl.BlockSpec((None, q, SSM_INNER), lambda d, b, s: (d, chunk(d, b, s), 0)),
        out_shape=jax.ShapeDtypeStruct((2, T_ALL, SSM_INNER), F32),
        scratch_shapes=[pltpu.VMEM((SSM_STATE, SSM_INNER), F32)],
        compiler_params=_cparams(("arbitrary", "arbitrary", "arbitrary"), 48),
        name="ssd_scan",
    )(xbc, xbc, xbc, proj, dt_bias.reshape(1, n_heads2), a_log.reshape(1, n_heads2), d_wide, pfx, expand)


def _gla_consts():
    q = HG_CHUNK
    levels = int(np.log2(q))
    i = np.arange(q)
    mats = [(i[:, None] >= i[None, :]).astype(np.float32), (i[None, :] > i[:, None]).astype(np.float32)]
    masks = [np.eye(q, dtype=np.float32)]
    for lv in range(levels):
        s = 1 << lv
        blk, pos = i // (2 * s), i % (2 * s)
        bound = blk * 2 * s + s - 1
        right = pos >= s
        t = i[None, :]
        m_right = right[:, None] & (t > bound[:, None]) & (t <= i[:, None])
        m_left = (~right)[:, None] & (t > i[:, None]) & (t <= bound[:, None])
        mats.append((m_right | m_left).astype(np.float32))
        masks.append((right[:, None] & (~right)[None, :] & (blk[:, None] == blk[None, :])).astype(np.float32))
    mats.append(np.ones((ONES_ROWS, q), np.float32))
    fwd_m, fwd_k = np.concatenate(mats, axis=0), np.stack(masks)
    flip_rows = np.concatenate([np.arange(k * q, (k + 1) * q)[::-1] for k in range(levels + 2)]
                               + [np.arange((levels + 2) * q, (levels + 2) * q + ONES_ROWS)])
    bwd_m = fwd_m[flip_rows][:, ::-1]
    bwd_k = fwd_k[:, ::-1, ::-1]
    return (jnp.asarray(np.stack([fwd_m, bwd_m]), BF16), jnp.asarray(np.stack([fwd_k, bwd_k]), F32), levels)


def _gla_body(levels, layer, q_ref, v_ref, f_ref, lbp_ref, mats_ref, masks_ref, o_ref, st_ref):
    q = HG_CHUNK

    @pl.when(pl.program_id(2) == 0)
    def _():
        st_ref[...] = jnp.zeros_like(st_ref)

    lbp = lbp_ref[...]
    e = jnp.exp(lbp - jnp.max(lbp, axis=0, keepdims=True))
    sm = e / jnp.sum(e, axis=0, keepdims=True)
    lb = jnp.sum(sm[1:layer + 1], axis=0, keepdims=True) if layer > 0 else jnp.zeros((1, lbp.shape[1]), F32)

    f = f_ref[...]
    log_sig = jnp.minimum(f, 0.0) - jnp.log1p(jnp.exp(-jnp.abs(f)))
    la = jnp.log(lb)
    lc = jnp.log1p(-lb) + log_sig
    logf = jnp.maximum(la, lc) + jnp.log1p(jnp.exp(-jnp.abs(la - lc)))
    key = (1.0 - lb) * jax.nn.sigmoid(-f)
    qs = _silu(q_ref[...])

    sums = _dot_sel(mats_ref[...], logf, 2)
    q_in = (qs * jnp.exp(sums[:q])).astype(BF16)
    k_out = (key * jnp.exp(sums[q:2 * q])).astype(BF16)
    dec = jnp.exp(sums[(levels + 2) * q:(levels + 2) * q + 1])
    qs_bf, key_bf = qs.astype(BF16), key.astype(BF16)
    q_lv, k_lv = [], []
    for lv in range(levels):
        fac = jnp.exp(sums[(lv + 2) * q:(lv + 3) * q])
        q_lv.append((qs * fac).astype(BF16))
        k_lv.append((key * fac).astype(BF16))
    v_bf = v_ref[...].astype(BF16)

    for h in range(HG_HEADS):
        sl = slice(h * LANES, (h + 1) * LANES)
        attn = masks_ref[0] * _dot_nt(qs_bf[:, sl], key_bf[:, sl])
        for lv in range(levels):
            attn = attn + masks_ref[lv + 1] * _dot_nt(q_lv[lv][:, sl], k_lv[lv][:, sl])
        st = st_ref[h]
        o_ref[:, sl] = _dot(attn.astype(BF16), v_bf[:, sl]) + _dot_nt(q_in[:, sl], st.astype(BF16))
        st_ref[h] = st * dec[:, sl] + _dot_tn(v_bf[:, sl], k_out[:, sl])


def _gla_scan(proj, hg_lb, layer):
    q = HG_CHUNK
    n_lat, n_ctx = SEQ // q, CTX_LEN // q
    mats, masks, levels = _gla_consts()

    def chunk(d, b, s):
        return _scan_chunk_index(d, b, s, n_lat, n_ctx)

    return pl.pallas_call(
        functools.partial(_gla_body, levels, layer),
        grid=(2, BATCH, n_lat + n_ctx),
        in_specs=[
            pl.BlockSpec((q, D_MODEL), lambda d, b, s: (chunk(d, b, s), 0)),
            pl.BlockSpec((q, D_MODEL), lambda d, b, s: (chunk(d, b, s), 1)),
            pl.BlockSpec((q, D_MODEL), lambda d, b, s: (chunk(d, b, s), 2 + d)),
            pl.BlockSpec((None, DEPTH, D_MODEL), lambda d, b, s: (d, 0, 0)),
            pl.BlockSpec((None,) + mats.shape[1:], lambda d, b, s: (d, 0, 0)),
            pl.BlockSpec((None,) + masks.shape[1:], lambda d, b, s: (d, 0, 0, 0)),
        ],
        out_specs=pl.BlockSpec((None, q, D_MODEL), lambda d, b, s: (d, chunk(d, b, s), 0)),
        out_shape=jax.ShapeDtypeStruct((2, T_ALL, D_MODEL), F32),
        scratch_shapes=[pltpu.VMEM((HG_HEADS, LANES, HG_KEY_DIM), F32)],
        compiler_params=_cparams(("arbitrary", "arbitrary", "arbitrary"), 48),
        name="gla_scan",
    )(proj, proj, proj, hg_lb, mats, masks)


def _na_bias_body(rpb_ref, o_ref):
    h = pl.program_id(0)
    n_dc = 2 * NA_WIN_C - 1
    qc = lax.broadcasted_iota(jnp.int32, (GRID_W, LANES), 0)
    kc = lax.broadcasted_iota(jnp.int32, (GRID_W, LANES), 1) % GRID_W
    c0 = jnp.clip(qc - NA_WIN_C // 2, 0, GRID_W - NA_WIN_C)
    col_in = (kc >= c0) & (kc < c0 + NA_WIN_C)
    dc = jnp.clip(kc - qc + NA_WIN_C - 1, 0, n_dc - 1)
    low = lax.broadcasted_iota(jnp.int32, (GRID_W, LANES), 1) < GRID_W
    tiles = []
    for dr in range(2 * NA_WIN_R - 1):
        t = jnp.zeros((GRID_W, LANES), F32)
        for k in range(n_dc):
            t = jnp.where(dc == k, rpb_ref[(h * (2 * NA_WIN_R - 1) + dr) * n_dc + k], t)
        tiles.append(jnp.where(col_in, t, -jnp.inf))
    for dr in range(2 * NA_WIN_R - 2):
        o_ref[dr] = jnp.where(low, tiles[dr], tiles[dr + 1])


def _na_bias_table(rpb):
    n_dr = 2 * NA_WIN_R - 2
    return pl.pallas_call(
        _na_bias_body,
        grid=(NA_HEADS,),
        in_specs=[pl.BlockSpec(memory_space=pltpu.SMEM)],
        out_specs=pl.BlockSpec((None, n_dr, GRID_W, LANES), lambda h: (h, 0, 0, 0)),
        out_shape=jax.ShapeDtypeStruct((NA_HEADS, n_dr, GRID_W, LANES), F32),
        compiler_params=_cparams(("arbitrary",), 32),
        name="na_bias",
    )(rpb.reshape(-1))


def _rope_tables():
    quarter = NA_HEAD_DIM // 4
    inv = ROPE_BASE ** (-np.arange(quarter, dtype=np.float64) / quarter)
    t = np.arange(SEQ)
    row, col = t // GRID_W, t % GRID_W
    ang = np.concatenate([row[:, None] * inv[None], row[:, None] * inv[None],
                          col[:, None] * inv[None], col[:, None] * inv[None]], axis=1)
    sign = np.tile(np.concatenate([-np.ones(quarter), np.ones(quarter)]), 2)[None]
    return jnp.asarray(np.cos(ang), F32), jnp.asarray(np.sin(ang) * sign, F32)


def _rope(v, cos, sin_signed):
    quarter = NA_HEAD_DIM // 4
    lane = lax.broadcasted_iota(jnp.int32, v.shape, 1)
    first = (lane % (2 * quarter)) < quarter
    partner = jnp.where(first, pltpu.roll(v, NA_HEAD_DIM - quarter, axis=1), pltpu.roll(v, quarter, axis=1))
    return v * cos + partner * sin_signed


def _na_body(rows_per_step, q_ref, k_ref, v_ref, kc_ref, vc_ref, cosq_ref, sinq_ref, cos_ref, sin_ref, bias_ref,
             o_ref, kr_ref, vb_ref):
    rb = pl.program_id(2)
    n_rows = SEQ // GRID_W
    scale = NA_HEAD_DIM ** -0.5

    @pl.when(rb == 0)
    def _():
        kr_ref[...] = _rope(k_ref[...], cos_ref[...], sin_ref[...]).astype(BF16)
        vb_ref[...] = v_ref[...].astype(BF16)

    qr = (_rope(q_ref[...], cosq_ref[...], sinq_ref[...]) * scale).astype(BF16)
    kc = kc_ref[...].astype(BF16)
    vc = vc_ref[...].astype(BF16)
    for lr in range(rows_per_step):
        r = rb * rows_per_step + lr
        r0 = jnp.clip(r - NA_WIN_R // 2, 0, n_rows - NA_WIN_R)
        start = pl.multiple_of(r0 * GRID_W, GRID_W)
        k_win = kr_ref[pl.ds(start, NA_WIN_R * GRID_W), :]
        v_win = vb_ref[pl.ds(start, NA_WIN_R * GRID_W), :]
        q_r = qr[lr * GRID_W:(lr + 1) * GRID_W]
        dr0 = r0 - r + NA_WIN_R - 1
        bias = jnp.concatenate([bias_ref[dr0 + 2 * p] for p in range(NA_WIN_R // 2)], axis=1)
        s_loc = _dot_nt(q_r, k_win) + bias
        s_ctx = _dot_nt(q_r, kc)
        m = jnp.maximum(jnp.max(s_loc, axis=-1, keepdims=True), jnp.max(s_ctx, axis=-1, keepdims=True))
        p_loc = jnp.exp(s_loc - m)
        p_ctx = jnp.exp(s_ctx - m)
        denom = jnp.sum(p_loc, axis=-1, keepdims=True) + jnp.sum(p_ctx, axis=-1, keepdims=True)
        o = _dot(p_loc.astype(BF16), v_win) + _dot(p_ctx.astype(BF16), vc)
        o_ref[lr * GRID_W:(lr + 1) * GRID_W, :] = o / denom


def _na_attention(qkv, bias_tbl):
    rows_per_step = 8
    tq = rows_per_step * GRID_W
    steps = SEQ // tq
    cos, sin = _rope_tables()
    hd = NA_HEAD_DIM
    ctx_blk0 = T_LAT // CTX_LEN
    return pl.pallas_call(
        functools.partial(_na_body, rows_per_step),
        grid=(BATCH, NA_HEADS, steps),
        in_specs=[
            pl.BlockSpec((tq, hd), lambda b, h, r: (b * steps + r, h)),
            pl.BlockSpec((SEQ, hd), lambda b, h, r: (b, NA_HEADS + h)),
            pl.BlockSpec((SEQ, hd), lambda b, h, r: (b, 2 * NA_HEADS + h)),
            pl.BlockSpec((CTX_LEN, hd), lambda b, h, r: (ctx_blk0 + b, NA_HEADS + h)),
            pl.BlockSpec((CTX_LEN, hd), lambda b, h, r: (ctx_blk0 + b, 2 * NA_HEADS + h)),
            pl.BlockSpec((tq, hd), lambda b, h, r: (r, 0)),
            pl.BlockSpec((tq, hd), lambda b, h, r: (r, 0)),
            pl.BlockSpec((SEQ, hd), lambda b, h, r: (0, 0)),
            pl.BlockSpec((SEQ, hd), lambda b, h, r: (0, 0)),
            pl.BlockSpec((None,) + bias_tbl.shape[1:], lambda b, h, r: (h, 0, 0, 0)),
        ],
        out_specs=pl.BlockSpec((tq, hd), lambda b, h, r: (b * steps + r, h)),
        out_shape=jax.ShapeDtypeStruct((T_LAT, D_MODEL), F32),
        scratch_shapes=[pltpu.VMEM((SEQ, hd), BF16), pltpu.VMEM((SEQ, hd), BF16)],
        compiler_params=_cparams(("arbitrary", "arbitrary", "arbitrary"), 48),
        name="na_attention",
    )(qkv, qkv, qkv, qkv, qkv, cos, sin, cos, sin, bias_tbl)


def _ctx_attn_body(q_ref, k_ref, v_ref, o_ref):
    scale = NA_HEAD_DIM ** -0.5
    s = _dot_nt((q_ref[...] * scale).astype(BF16), k_ref[...].astype(BF16))
    p = jnp.exp(s - jnp.max(s, axis=-1, keepdims=True))
    o = _dot(p.astype(BF16), v_ref[...].astype(BF16))
    o_ref[...] = o / jnp.sum(p, axis=-1, keepdims=True)


def _ctx_attention(qkv):
    hd = NA_HEAD_DIM
    ctx_blk0 = T_LAT // CTX_LEN
    return pl.pallas_call(
        _ctx_attn_body,
        grid=(BATCH, NA_HEADS),
        in_specs=[
            pl.BlockSpec((CTX_LEN, hd), lambda b, h: (ctx_blk0 + b, h)),
            pl.BlockSpec((CTX_LEN, hd), lambda b, h: (ctx_blk0 + b, NA_HEADS + h)),
            pl.BlockSpec((CTX_LEN, hd), lambda b, h: (ctx_blk0 + b, 2 * NA_HEADS + h)),
        ],
        out_specs=pl.BlockSpec((CTX_LEN, hd), lambda b, h: (b, h)),
        out_shape=jax.ShapeDtypeStruct((T_CTX, D_MODEL), F32),
        compiler_params=_cparams(("arbitrary", "arbitrary"), 32),
        name="ctx_attention",
    )(qkv, qkv, qkv)


def kernel(x, c, ctx, c_ctx, w_mod, b_mod, norm_g, ffn_w_up, ffn_conv_w, ffn_conv_b, ffn_w_down, ssm_w_in, ssm_conv_w, ssm_conv_b, ssm_dt_bias, ssm_a_log, ssm_d, ssm_norm_g, ssm_w_out, hg_w_in, hg_lb, hg_norm_g, hg_w_out, na_w_qkv, na_rpb, na_w_out):
    xs = jnp.concatenate([x.reshape(T_LAT, D_MODEL), ctx.reshape(T_CTX, D_MODEL)], axis=0)
    cvec8 = jnp.concatenate([c, c_ctx[None], jnp.zeros((SUBLANES - BATCH - 1, D_MODEL), F32)], axis=0)
    mod = _adaln(cvec8, w_mod, b_mod)

    for i in range(DEPTH):
        kind, slot = i % N_MIXERS, i // N_MIXERS
        mod_l = mod[i]
        if kind == 0:
            proj = _norm_mod_mm(xs, norm_g[i, 0], mod_l, 0, 1, ssm_w_in[slot].astype(BF16), 1152, "ssd_in")
            xbc = _ssd_conv(proj, ssm_conv_w[slot], ssm_conv_b[slot])
            y2 = _ssd_scan(xbc, proj, ssm_dt_bias[slot], ssm_a_log[slot], ssm_d[slot])
            y = _fused_mm(_ssd_gate_prologue,
                          [((y2, 0), SSM_INNER, 0), ((y2, 1), SSM_INNER, 0), (proj, SSM_INNER, 0)],
                          [ssm_norm_g[slot].reshape(1, SSM_INNER)], ssm_w_out[slot].astype(BF16),
                          tm=256, tn=1024, vmem_mib=56, name="ssd_out")
        elif kind == 1:
            proj = _norm_mod_mm(xs, norm_g[i, 0], mod_l, 0, 1, hg_w_in[slot].astype(BF16), 1024, "hg_in")
            o2 = _gla_scan(proj, hg_lb, i)
            y = _fused_mm(_hg_readout_prologue,
                          [((o2, 0), D_MODEL, 0), ((o2, 1), D_MODEL, 0), (proj, D_MODEL, 4)],
                          [hg_norm_g[slot].reshape(1, D_MODEL)], hg_w_out[slot].astype(BF16),
                          tm=512, tn=1024, vmem_mib=48, name="hg_out")
        else:
            qkv = _norm_mod_mm(xs, norm_g[i, 0], mod_l, 0, 1, na_w_qkv[slot].astype(BF16), 1024, "na_in")
            o = jnp.concatenate([_na_attention(qkv, _na_bias_table(na_rpb[slot])), _ctx_attention(qkv)], axis=0)
            y = _fused_mm(_plain_prologue, [(o, D_MODEL, 0)], [], na_w_out[slot].astype(BF16),
                          tm=512, tn=1024, vmem_mib=40, name="na_out")
        xs = _resid_norm(xs, y, norm_g[i, 1], mod_l, 2)
        u = _norm_mod_mm(xs, norm_g[i, 2], mod_l, 3, 4, ffn_w_up[i].astype(BF16), 1024, "ffn_up")
        f = _ffn_down(u, ffn_conv_w[i], ffn_conv_b[i], ffn_w_down[i].astype(BF16))
        xs = _resid_norm(xs, f, norm_g[i, 3], mod_l, 5)

    return xs[:T_LAT].reshape(BATCH, SEQ, D_MODEL)
```

```python
import functools

import numpy as np
import jax
import jax.numpy as jnp
from jax import lax
from jax.experimental import pallas as pl
from jax.experimental.pallas import tpu as pltpu

F32 = jnp.float32
BF16 = jnp.bfloat16

D_MODEL = 2048
BATCH = 2
SEQ = 4096
DEPTH = 4
GRID_W = 64
CTX_LEN = 256
N_MIXERS = 3
RMS_EPS = 1e-6
SSM_INNER = 4096
SSM_HEAD_DIM = 64
SSM_HEADS = 64
SSM_STATE = 128
SSM_GROUPS = 8
SSM_CONV_DIM = SSM_INNER + 2 * SSM_GROUPS * SSM_STATE
SSM_IN_DIM = SSM_INNER + SSM_CONV_DIM + 2 * SSM_HEADS
SSM_CHUNK = 128
HG_HEADS = 16
HG_KEY_DIM = 128
HG_IN_DIM = 5 * D_MODEL
HG_CHUNK = 64
NA_HEADS = 16
NA_HEAD_DIM = 128
NA_WIN_R = 8
NA_WIN_C = 16
ROPE_BASE = 10000.0
FFN_HIDDEN = 5632

T_LAT = BATCH * SEQ
T_CTX = BATCH * CTX_LEN
T_ALL = T_LAT + T_CTX
LANES = 128
SUBLANES = 8
BF16_ROWS = 16
ONES_ROWS = 16
ROW_TILE = 512
FFN_COL_CHUNK = 256
MIB = 1024 * 1024
LOG2E = 1.4426950408889634


def _cparams(sem, vmem_mib):
    return pltpu.CompilerParams(dimension_semantics=sem, vmem_limit_bytes=vmem_mib * MIB)


def _silu(v):
    return v * jax.nn.sigmoid(v)


def _rms(v):
    return v * lax.rsqrt(jnp.mean(v * v, axis=-1, keepdims=True) + RMS_EPS)


def _log1p_unit(e):
    return jnp.log(1.0 + e)


def _softplus(v):
    return jnp.maximum(v, 0.0) + _log1p_unit(jnp.exp(-jnp.abs(v)))


def _dot(a, b):
    return jnp.dot(a, b, preferred_element_type=F32)


def _dot_nt(a, b):
    return lax.dot_general(a, b, (((1,), (1,)), ((), ())), preferred_element_type=F32)


def _dot_tn(a, b):
    return lax.dot_general(a, b, (((0,), (0,)), ((), ())), preferred_element_type=F32)


def _bf16_pieces(v, terms):
    pieces = []
    rem = v
    for _ in range(terms):
        part = rem.astype(BF16)
        rem = rem - part.astype(F32)
        pieces.append(part)
    return pieces


def _dot_sel(m01_rep, v, terms):
    return _dot(m01_rep, jnp.concatenate(_bf16_pieces(v, terms), axis=0))


def _dot_sel_rhs(v, m01_rep, terms):
    return _dot(jnp.concatenate(_bf16_pieces(v, terms), axis=1), m01_rep)


def _mod_row(row0):
    return jnp.where(row0 >= T_LAT, 2, row0 // SEQ)


def _mod_vec(mod_ref, r, idx):
    return mod_ref[pl.ds(r, 1), idx * D_MODEL:(idx + 1) * D_MODEL]


def _norm_mod(x, g, mod_ref, r, shift_idx, scale_idx):
    return _rms(x) * g * (1.0 + _mod_vec(mod_ref, r, scale_idx)) + _mod_vec(mod_ref, r, shift_idx)


def _seg_edges(row0, rows):
    gr = row0 + lax.broadcasted_iota(jnp.int32, (rows, 1), 0)
    pos = jnp.where(gr >= T_LAT, (gr - T_LAT) % CTX_LEN, gr % SEQ)
    length = jnp.where(gr >= T_LAT, CTX_LEN, SEQ)
    return pos == 0, pos == length - 1


def _shift_rows(cur, prev_row, next_row, row0):
    rows = cur.shape[0]
    first, last = _seg_edges(row0, rows)
    ridx = lax.broadcasted_iota(jnp.int32, (rows, 1), 0)
    up = jnp.where(ridx == 0, prev_row, pltpu.roll(cur, 1, axis=0))
    up = jnp.where(first, 0.0, up)
    dn = jnp.where(ridx == rows - 1, next_row, pltpu.roll(cur, rows - 1, axis=0))
    dn = jnp.where(last, 0.0, dn)
    return up, dn


def _token_conv(cur, prev_row, next_row, row0, w_ref, b_ref):
    up, dn = _shift_rows(cur, prev_row, next_row, row0)
    return b_ref[...] + up * w_ref[0:1, :] + cur * w_ref[1:2, :] + dn * w_ref[2:3, :]


def _halo_specs(tm, halo, width, n_rows, col_fn):
    per = tm // halo
    last_blk = n_rows // halo - 1
    prev = pl.BlockSpec((halo, width), lambda i, j: (jnp.maximum(i * per - 1, 0), col_fn(j)))
    nxt = pl.BlockSpec((halo, width), lambda i, j: (jnp.minimum((i + 1) * per, last_blk), col_fn(j)))
    return prev, nxt


def _adaln_body(c_ref, w_ref, b_ref, o_ref):
    s = _silu(c_ref[...]).astype(BF16)
    o_ref[...] = _dot(s, w_ref[...].astype(BF16)) + b_ref[...]


def _adaln(cvec8, w_mod, b_mod):
    tn = 1024
    n_out = 6 * D_MODEL
    return pl.pallas_call(
        _adaln_body,
        grid=(DEPTH, n_out // tn),
        in_specs=[
            pl.BlockSpec((SUBLANES, D_MODEL), lambda l, j: (0, 0)),
            pl.BlockSpec((None, D_MODEL, tn), lambda l, j: (l, 0, j)),
            pl.BlockSpec((None, 1, tn), lambda l, j: (l, 0, j)),
        ],
        out_specs=pl.BlockSpec((None, SUBLANES, tn), lambda l, j: (l, 0, j)),
        out_shape=jax.ShapeDtypeStruct((DEPTH, SUBLANES, n_out), F32),
        compiler_params=_cparams(("arbitrary", "arbitrary"), 40),
        name="adaln",
    )(cvec8, w_mod, b_mod.reshape(DEPTH, 1, n_out))


def _prep_body(x_ref, g_ref, mod_ref, o_ref):
    r = _mod_row(pl.program_id(0) * ROW_TILE)
    o_ref[...] = _norm_mod(x_ref[...], g_ref[...], mod_ref, r, 0, 1).astype(BF16)


def _prep(xs, g_row, mod_l):
    row = pl.BlockSpec((ROW_TILE, D_MODEL), lambda i: (i, 0))
    return pl.pallas_call(
        _prep_body,
        grid=(T_ALL // ROW_TILE,),
        in_specs=[row, pl.BlockSpec((1, D_MODEL), lambda i: (0, 0)), pl.BlockSpec(mod_l.shape, lambda i: (0, 0))],
        out_specs=row,
        out_shape=jax.ShapeDtypeStruct((T_ALL, D_MODEL), BF16),
        compiler_params=_cparams(("arbitrary",), 32),
        name="prep",
    )(xs, g_row.reshape(1, D_MODEL), mod_l)


def _in_proj_body(h_ref, w_ref, o_ref, wb_ref):
    @pl.when(pl.program_id(1) == 0)
    def _():
        wb_ref[...] = w_ref[...].astype(BF16)

    o_ref[...] = _dot(h_ref[...], wb_ref[...])


def _in_proj(h, w_stack, slot, tn, name):
    tm = T_ALL // 8
    n_dim = w_stack.shape[2]
    return pl.pallas_call(
        _in_proj_body,
        grid=(n_dim // tn, T_ALL // tm),
        in_specs=[pl.BlockSpec((tm, D_MODEL), lambda j, i: (i, 0)),
                  pl.BlockSpec((None, D_MODEL, tn), lambda j, i: (slot, 0, j))],
        out_specs=pl.BlockSpec((tm, tn), lambda j, i: (i, j)),
        out_shape=jax.ShapeDtypeStruct((T_ALL, n_dim), F32),
        scratch_shapes=[pltpu.VMEM((D_MODEL, tn), BF16)],
        compiler_params=_cparams(("arbitrary", "arbitrary"), 56),
        name=name,
    )(h, w_stack)


def _ssd_gate_prologue(yf_ref, yb_ref, z_ref, g_ref):
    v = (yf_ref[...] + yb_ref[...]) * _silu(z_ref[...])
    return v * g_ref[...], jnp.sum(v * v, axis=-1, keepdims=True)


def _hg_readout_prologue(of_ref, ob_ref, gate_ref, g_ref):
    o = of_ref[...] + ob_ref[...]
    parts = [_rms(o[:, h * LANES:(h + 1) * LANES]) for h in range(o.shape[1] // LANES)]
    return jnp.concatenate(parts, axis=-1) * g_ref[...] * _silu(gate_ref[...]), None


def _plain_prologue(o_ref):
    return o_ref[...], None


def _out_proj_body(prologue, n_in, k_dim, row_norm, *refs):
    ins = refs[:n_in]
    w_ref, x_ref, g1_ref, g2_ref, mod_ref, xo_ref, ho_ref, acc_ref, ssq_ref = refs[n_in:]
    k = pl.program_id(1)
    part, sq = prologue(*ins)
    contrib = _dot(part.astype(BF16), w_ref[...])

    @pl.when(k == 0)
    def _():
        acc_ref[...] = contrib
        if row_norm:
            ssq_ref[...] = sq

    @pl.when(k != 0)
    def _():
        acc_ref[...] += contrib
        if row_norm:
            ssq_ref[...] += sq

    @pl.when(k == pl.num_programs(1) - 1)
    def _():
        y = acc_ref[...]
        if row_norm:
            y = y * lax.rsqrt(ssq_ref[...] / k_dim + RMS_EPS)
        r = _mod_row(pl.program_id(0) * ROW_TILE)
        xn = x_ref[...] + _mod_vec(mod_ref, r, 2) * (_rms(y) * g1_ref[...])
        xo_ref[...] = xn
        ho_ref[...] = _norm_mod(xn, g2_ref[...], mod_ref, r, 3, 4).astype(BF16)


def _out_proj(prologue, row_ins, gain, w_stack, slot, xs, g_res, g_ffn, mod_l, *, n_rows, tk, row_norm, name):
    k_dim = w_stack.shape[1]
    tm = ROW_TILE
    in_specs, args = [], []
    for arr, cb0 in row_ins:
        if isinstance(arr, tuple):
            in_specs.append(pl.BlockSpec((None, tm, tk), functools.partial(
                lambda i, k, ld, cb0: (ld, i, cb0 + k), ld=arr[1], cb0=cb0)))
            args.append(arr[0])
        else:
            in_specs.append(pl.BlockSpec((tm, tk), functools.partial(lambda i, k, cb0: (i, cb0 + k), cb0=cb0)))
            args.append(arr)
    if gain is not None:
        in_specs.append(pl.BlockSpec((1, tk), lambda i, k: (0, k)))
        args.append(gain)
    n_in = len(args)
    row = pl.BlockSpec((tm, D_MODEL), lambda i, k: (i, 0))
    vec = pl.BlockSpec((1, D_MODEL), lambda i, k: (0, 0))
    in_specs += [pl.BlockSpec((None, tk, D_MODEL), lambda i, k: (slot, k, 0)), row, vec, vec,
                 pl.BlockSpec(mod_l.shape, lambda i, k: (0, 0))]
    return pl.pallas_call(
        functools.partial(_out_proj_body, prologue, n_in, k_dim, row_norm),
        grid=(n_rows // tm, k_dim // tk),
        in_specs=in_specs,
        out_specs=[row, row],
        out_shape=[jax.ShapeDtypeStruct((n_rows, D_MODEL), F32), jax.ShapeDtypeStruct((n_rows, D_MODEL), BF16)],
        scratch_shapes=[pltpu.VMEM((tm, D_MODEL), F32), pltpu.VMEM((tm, 1), F32)],
        compiler_params=_cparams(("arbitrary", "arbitrary"), 56),
        name=name,
    )(*args, w_stack, xs, g_res.reshape(1, D_MODEL), g_ffn.reshape(1, D_MODEL), mod_l)


def _ffn_body(emit_h, h_ref, hp_ref, hn_ref, wa_ref, wv_ref, cwa_ref, cwv_ref, cba_ref, cbv_ref, wd_ref, x_ref,
              g_res_ref, mod_ref, g_next_ref, modn_ref, *out_and_scratch):
    if emit_h:
        xo_ref, ho_ref, acc_ref = out_and_scratch
    else:
        xo_ref, acc_ref = out_and_scratch
    k = pl.program_id(1)
    row0 = pl.program_id(0) * ROW_TILE
    h = h_ref[...]
    halo = jnp.concatenate([hp_ref[...], hn_ref[...]], axis=0)

    def branch(w_ref, cw_ref, cb_ref, cs):
        w = w_ref[:, cs]
        u = _dot(h, w)
        uh = _dot(halo, w)
        return _token_conv(u, uh[BF16_ROWS - 1:BF16_ROWS], uh[BF16_ROWS:BF16_ROWS + 1], row0,
                           cw_ref.at[:, cs], cb_ref.at[:, cs])

    contrib = None
    for c in range(wa_ref.shape[1] // FFN_COL_CHUNK):
        cs = pl.ds(c * FFN_COL_CHUNK, FFN_COL_CHUNK)
        act = _silu(branch(wa_ref, cwa_ref, cba_ref, cs)) * branch(wv_ref, cwv_ref, cbv_ref, cs)
        part = _dot(act.astype(BF16), wd_ref[cs, :])
        contrib = part if contrib is None else contrib + part

    @pl.when(k == 0)
    def _():
        acc_ref[...] = contrib

    @pl.when(k != 0)
    def _():
        acc_ref[...] += contrib

    @pl.when(k == pl.num_programs(1) - 1)
    def _():
        r = _mod_row(row0)
        xn = x_ref[...] + _mod_vec(mod_ref, r, 5) * (_rms(acc_ref[...]) * g_res_ref[...])
        xo_ref[...] = xn
        if emit_h:
            ho_ref[...] = _norm_mod(xn, g_next_ref[...], modn_ref, r, 0, 1).astype(BF16)


def _ffn(h, xs, layer, w_up, conv_w, conv_b, w_down, g_res, mod_l, g_next, mod_next, *, n_rows, emit_h):
    tm, tk = ROW_TILE, 512
    kt = FFN_HIDDEN // tk
    h_prev, h_next = _halo_specs(tm, BF16_ROWS, D_MODEL, n_rows, lambda k: 0)
    row = pl.BlockSpec((tm, D_MODEL), lambda i, k: (i, 0))
    vec = pl.BlockSpec((1, D_MODEL), lambda i, k: (0, 0))
    tab = pl.BlockSpec(mod_l.shape, lambda i, k: (0, 0))
    out_specs = [row, row] if emit_h else [row]
    out_shape = [jax.ShapeDtypeStruct((n_rows, D_MODEL), F32)]
    if emit_h:
        out_shape.append(jax.ShapeDtypeStruct((n_rows, D_MODEL), BF16))
    conv_b3 = conv_b.reshape(DEPTH, 1, 2 * FFN_HIDDEN)
    res = pl.pallas_call(
        functools.partial(_ffn_body, emit_h),
        grid=(n_rows // tm, kt),
        in_specs=[
            row, h_prev, h_next,
            pl.BlockSpec((None, D_MODEL, tk), lambda i, k: (layer, 0, k)),
            pl.BlockSpec((None, D_MODEL, tk), lambda i, k: (layer, 0, k + kt)),
            pl.BlockSpec((None, 3, tk), lambda i, k: (layer, 0, k)),
            pl.BlockSpec((None, 3, tk), lambda i, k: (layer, 0, k + kt)),
            pl.BlockSpec((None, 1, tk), lambda i, k: (layer, 0, k)),
            pl.BlockSpec((None, 1, tk), lambda i, k: (layer, 0, k + kt)),
            pl.BlockSpec((None, tk, D_MODEL), lambda i, k: (layer, k, 0)),
            row, vec, tab, vec, tab,
        ],
        out_specs=out_specs,
        out_shape=out_shape,
        scratch_shapes=[pltpu.VMEM((tm, D_MODEL), F32)],
        compiler_params=_cparams(("arbitrary", "arbitrary"), 56),
        name="conv_ffn",
    )(h, h, h, w_up, w_up, conv_w, conv_w, conv_b3, conv_b3, w_down, xs, g_res.reshape(1, D_MODEL), mod_l,
      g_next.reshape(1, D_MODEL), mod_next)
    return (res[0], res[1]) if emit_h else (res[0], None)


def _ssd_conv_body(tm, cur_ref, prev_ref, next_ref, w_ref, b_ref, o_ref):
    u = _token_conv(cur_ref[...], prev_ref[SUBLANES - 1:SUBLANES, :], next_ref[0:1, :], pl.program_id(0) * tm,
                    w_ref, b_ref)
    o_ref[...] = _silu(u)


def _ssd_conv(proj, conv_w, conv_b):
    tm, tc = 512, 2048
    col0 = SSM_INNER // tc
    prev, nxt = _halo_specs(tm, SUBLANES, tc, T_ALL, lambda j: j + col0)
    return pl.pallas_call(
        functools.partial(_ssd_conv_body, tm),
        grid=(T_ALL // tm, SSM_CONV_DIM // tc),
        in_specs=[pl.BlockSpec((tm, tc), lambda i, j: (i, j + col0)), prev, nxt,
                  pl.BlockSpec((3, tc), lambda i, j: (0, j)), pl.BlockSpec((1, tc), lambda i, j: (0, j))],
        out_specs=pl.BlockSpec((tm, tc), lambda i, j: (i, j)),
        out_shape=jax.ShapeDtypeStruct((T_ALL, SSM_CONV_DIM), F32),
        compiler_params=_cparams(("arbitrary", "arbitrary"), 40),
        name="ssd_conv",
    )(proj, proj, proj, conv_w, conv_b.reshape(1, -1))


def _scan_chunk_index(d, b, s, n_lat, n_ctx):
    ctx0 = BATCH * n_lat + b * n_ctx
    fwd = jnp.where(s < n_ctx, ctx0 + s, b * n_lat + s - n_ctx)
    bwd = jnp.where(s < n_ctx, ctx0 + n_ctx - 1 - s, b * n_lat + n_lat - 1 - (s - n_ctx))
    return jnp.where(d == 0, fwd, bwd)


SSD_CUM_TERMS = 3
SSD_DT_TERMS = 2


def _ssd_scan_consts():
    q = SSM_CHUNK
    i = np.arange(q)
    pfx = np.zeros((2, q + ONES_ROWS, q), np.float32)
    pfx[0, :q] = i[:, None] >= i[None, :]
    pfx[1, :q] = i[:, None] <= i[None, :]
    pfx[:, q:] = 1.0
    pfx = np.tile(pfx, (1, 1, SSD_CUM_TERMS))
    expand = np.repeat(np.eye(SSM_HEADS, dtype=np.float32), SSM_HEAD_DIM, axis=1)
    expand = np.tile(expand, (SSD_DT_TERMS, 1))
    return jnp.asarray(pfx, BF16), jnp.asarray(expand, BF16)


def _ssd_scan_body(x_ref, b_ref, c_ref, dtraw_ref, dtb_ref, alog_ref, dskip_ref, pfx_ref, exp_ref, y_ref, s_ref):
    q = SSM_CHUNK
    hp = 2 * SSM_HEAD_DIM
    d = pl.program_id(0)

    @pl.when(pl.program_id(2) == 0)
    def _():
        s_ref[...] = jnp.zeros_like(s_ref)

    def pick(v):
        return jnp.where(d == 0, v[:, :SSM_HEADS], v[:, SSM_HEADS:])

    dt2 = _softplus(dtraw_ref[...] + dtb_ref[...])
    cum2 = _dot_sel(pfx_ref[...], dt2 * (-LOG2E * jnp.exp(alog_ref[...])), SSD_CUM_TERMS)
    acum2 = cum2[:q]
    acum = pick(acum2)
    tot = pick(cum2[q:q + SUBLANES])[0:1]
    acum_t2 = acum2.T
    acum_t = jnp.where(d == 0, acum_t2[:SSM_HEADS], acum_t2[SSM_HEADS:])
    dt_wide = _dot_sel_rhs(pick(dt2), exp_ref[...], SSD_DT_TERMS)

    ii = lax.broadcasted_iota(jnp.int32, (q, q), 0)
    jj = lax.broadcasted_iota(jnp.int32, (q, q), 1)
    causal = (jj - ii) * (1 - 2 * d) <= 0
    low_half = lax.broadcasted_iota(jnp.int32, (q, hp), 1) < SSM_HEAD_DIM

    heads_per_group = SSM_HEADS // SSM_GROUPS
    for g in range(SSM_GROUPS):
        b_g = b_ref[:, g * SSM_STATE:(g + 1) * SSM_STATE].astype(BF16)
        c_g32 = c_ref[:, g * SSM_STATE:(g + 1) * SSM_STATE]
        cb = jnp.where(causal, _dot_nt(c_g32.astype(BF16), b_g), 0.0)
        for pair in range(heads_per_group // 2):
            lo = (g * heads_per_group + 2 * pair) * SSM_HEAD_DIM
            x_pair = x_ref[:, lo:lo + hp]
            xdt = x_pair * dt_wide[:, lo:lo + hp]
            rhs = jnp.concatenate([xdt.astype(BF16), s_ref[:, lo:lo + hp].astype(BF16)], axis=0)
            halves, cols, tots = [], [], []
            for sub in range(2):
                h = g * heads_per_group + 2 * pair + sub
                col = jnp.broadcast_to(acum[:, h:h + 1], (q, q))
                seg = jnp.where(causal, col - acum_t[h:h + 1, :], 0.0)
                scores = jnp.exp2(seg) * cb
                c_dec = c_g32 * jnp.exp2(col)
                lhs = jnp.concatenate([scores.astype(BF16), c_dec.astype(BF16)], axis=1)
                halves.append(_dot(lhs, rhs))
                cols.append(col)
                tots.append(jnp.broadcast_to(tot[:, h:h + 1], (1, hp)))
            y_pair = jnp.where(low_half, halves[0], halves[1])
            y_ref[:, lo:lo + hp] = y_pair + dskip_ref[:, lo:lo + hp] * x_pair
            tot_pair = jnp.where(low_half[0:1], tots[0], tots[1])
            to_end = jnp.exp2(tot_pair - jnp.where(low_half, cols[0], cols[1]))
            s_ref[:, lo:lo + hp] = (s_ref[:, lo:lo + hp] * jnp.exp2(tot_pair)
                                    + _dot_tn(b_g, (xdt * to_end).astype(BF16)))


def _ssd_scan(xbc, proj, dt_bias, a_log, d_skip):
    q = SSM_CHUNK
    n_lat, n_ctx = SEQ // q, CTX_LEN // q
    pfx, expand = _ssd_scan_consts()
    n_heads2 = 2 * SSM_HEADS
    dt_col = (SSM_INNER + SSM_CONV_DIM) // n_heads2
    b_col = SSM_INNER // (SSM_GROUPS * SSM_STATE)
    d_wide = jnp.repeat(d_skip, SSM_HEAD_DIM, axis=1).reshape(2, 1, SSM_INNER)

    def chunk(d, b, s):
        return _scan_chunk_index(d, b, s, n_lat, n_ctx)

    return pl.pallas_call(
        _ssd_scan_body,
        grid=(2, BATCH, n_lat + n_ctx),
        in_specs=[
            pl.BlockSpec((q, SSM_INNER), lambda d, b, s: (chunk(d, b, s), 0)),
            pl.BlockSpec((q, SSM_GROUPS * SSM_STATE), lambda d, b, s: (chunk(d, b, s), b_col)),
            pl.BlockSpec((q, SSM_GROUPS * SSM_STATE), lambda d, b, s: (chunk(d, b, s), b_col + 1)),
            pl.BlockSpec((q, n_heads2), lambda d, b, s: (chunk(d, b, s), dt_col)),
            pl.BlockSpec((1, n_heads2), lambda d, b, s: (0, 0)),
            pl.BlockSpec((1, n_heads2), lambda d, b, s: (0, 0)),
            pl.BlockSpec((None, 1, SSM_INNER), lambda d, b, s: (d, 0, 0)),
            pl.BlockSpec((None,) + pfx.shape[1:], lambda d, b, s: (d, 0, 0)),
            pl.BlockSpec(expand.shape, lambda d, b, s: (0, 0)),
        ],
        out_specs=pl.BlockSpec((None, q, SSM_INNER), lambda d, b, s: (d, chunk(d, b, s), 0)),
        out_shape=jax.ShapeDtypeStruct((2, T_ALL, SSM_INNER), F32),
        scratch_shapes=[pltpu.VMEM((SSM_STATE, SSM_INNER), F32)],
        compiler_params=_cparams(("arbitrary", "arbitrary", "arbitrary"), 48),
        name="ssd_scan",
    )(xbc, xbc, xbc, proj, dt_bias.reshape(1, n_heads2), a_log.reshape(1, n_heads2), d_wide, pfx, expand)


GLA_SUM_TERMS = 2


def _gla_consts():
    q = HG_CHUNK
    levels = int(np.log2(q))
    i = np.arange(q)
    mats = [(i[:, None] >= i[None, :]).astype(np.float32), (i[None, :] > i[:, None]).astype(np.float32)]
    masks = [np.eye(q, dtype=np.float32)]
    for lv in range(levels):
        s = 1 << lv
        blk, pos = i // (2 * s), i % (2 * s)
        bound = blk * 2 * s + s - 1
        right = pos >= s
        t = i[None, :]
        m_right = right[:, None] & (t > bound[:, None]) & (t <= i[:, None])
        m_left = (~right)[:, None] & (t > i[:, None]) & (t <= bound[:, None])
        mats.append((m_right | m_left).astype(np.float32))
        masks.append((right[:, None] & (~right)[None, :] & (blk[:, None] == blk[None, :])).astype(np.float32))
    mats.append(np.ones((ONES_ROWS, q), np.float32))
    fwd_m, fwd_k = np.concatenate(mats, axis=0), np.stack(masks)
    flip_rows = np.concatenate([np.arange(k * q, (k + 1) * q)[::-1] for k in range(levels + 2)]
                               + [np.arange((levels + 2) * q, (levels + 2) * q + ONES_ROWS)])
    bwd_m = fwd_m[flip_rows][:, ::-1]
    bwd_k = fwd_k[:, ::-1, ::-1]
    mats2 = np.tile(np.stack([fwd_m, bwd_m]), (1, 1, GLA_SUM_TERMS))
    return (jnp.asarray(mats2, BF16), jnp.asarray(np.stack([fwd_k, bwd_k]), F32), levels)


def _gla_body(levels, layer, q_ref, v_ref, f_ref, lbp_ref, mats_ref, masks_ref, o_ref, st_ref):
    q = HG_CHUNK

    @pl.when(pl.program_id(2) == 0)
    def _():
        st_ref[...] = jnp.zeros_like(st_ref)

    lbp = lbp_ref[...]
    e = jnp.exp(lbp - jnp.max(lbp, axis=0, keepdims=True))
    sm = e / jnp.sum(e, axis=0, keepdims=True)
    lb = jnp.sum(sm[1:layer + 1], axis=0, keepdims=True) if layer > 0 else jnp.zeros((1, lbp.shape[1]), F32)

    f = f_ref[...]
    ef = jnp.exp(-jnp.abs(f))
    log_sig = jnp.minimum(f, 0.0) - _log1p_unit(ef)
    la = jnp.log(lb)
    lc = jnp.log1p(-lb) + log_sig
    logf = jnp.maximum(la, lc) + _log1p_unit(jnp.exp(-jnp.abs(la - lc)))
    key = (1.0 - lb) * (jnp.where(f >= 0.0, ef, 1.0) / (1.0 + ef))
    qs = _silu(q_ref[...])

    sums = _dot_sel(mats_ref[...], logf, GLA_SUM_TERMS)
    q_in = (qs * jnp.exp(sums[:q])).astype(BF16)
    k_out = (key * jnp.exp(sums[q:2 * q])).astype(BF16)
    dec = jnp.exp(sums[(levels + 2) * q:(levels + 2) * q + 1])
    qs_bf, key_bf = qs.astype(BF16), key.astype(BF16)
    q_lv, k_lv = [], []
    for lv in range(levels):
        fac = jnp.exp(sums[(lv + 2) * q:(lv + 3) * q])
        q_lv.append((qs * fac).astype(BF16))
        k_lv.append((key * fac).astype(BF16))
    v_bf = v_ref[...].astype(BF16)

    for h in range(HG_HEADS):
        sl = slice(h * LANES, (h + 1) * LANES)
        attn = masks_ref[0] * _dot_nt(qs_bf[:, sl], key_bf[:, sl])
        for lv in range(levels):
            attn = attn + masks_ref[lv + 1] * _dot_nt(q_lv[lv][:, sl], k_lv[lv][:, sl])
        st = st_ref[h]
        o_ref[:, sl] = _dot(attn.astype(BF16), v_bf[:, sl]) + _dot_nt(q_in[:, sl], st.astype(BF16))
        st_ref[h] = st * dec[:, sl] + _dot_tn(v_bf[:, sl], k_out[:, sl])


def _gla_scan(proj, hg_lb, layer):
    q = HG_CHUNK
    n_lat, n_ctx = SEQ // q, CTX_LEN // q
    mats, masks, levels = _gla_consts()

    def chunk(d, b, s):
        return _scan_chunk_index(d, b, s, n_lat, n_ctx)

    return pl.pallas_call(
        functools.partial(_gla_body, levels, layer),
        grid=(2, BATCH, n_lat + n_ctx),
        in_specs=[
            pl.BlockSpec((q, D_MODEL), lambda d, b, s: (chunk(d, b, s), 0)),
            pl.BlockSpec((q, D_MODEL), lambda d, b, s: (chunk(d, b, s), 1)),
            pl.BlockSpec((q, D_MODEL), lambda d, b, s: (chunk(d, b, s), 2 + d)),
            pl.BlockSpec((None, DEPTH, D_MODEL), lambda d, b, s: (d, 0, 0)),
            pl.BlockSpec((None,) + mats.shape[1:], lambda d, b, s: (d, 0, 0)),
            pl.BlockSpec((None,) + masks.shape[1:], lambda d, b, s: (d, 0, 0, 0)),
        ],
        out_specs=pl.BlockSpec((None, q, D_MODEL), lambda d, b, s: (d, chunk(d, b, s), 0)),
        out_shape=jax.ShapeDtypeStruct((2, T_ALL, D_MODEL), F32),
        scratch_shapes=[pltpu.VMEM((HG_HEADS, LANES, HG_KEY_DIM), F32)],
        compiler_params=_cparams(("arbitrary", "arbitrary", "arbitrary"), 48),
        name="gla_scan",
    )(proj, proj, proj, hg_lb, mats, masks)


def _na_bias_body(rpb_ref, o_ref):
    h = pl.program_id(0)
    n_dc = 2 * NA_WIN_C - 1
    qc = lax.broadcasted_iota(jnp.int32, (GRID_W, LANES), 0)
    kc = lax.broadcasted_iota(jnp.int32, (GRID_W, LANES), 1) % GRID_W
    c0 = jnp.clip(qc - NA_WIN_C // 2, 0, GRID_W - NA_WIN_C)
    col_in = (kc >= c0) & (kc < c0 + NA_WIN_C)
    dc = jnp.clip(kc - qc + NA_WIN_C - 1, 0, n_dc - 1)
    low = lax.broadcasted_iota(jnp.int32, (GRID_W, LANES), 1) < GRID_W
    tiles = []
    for dr in range(2 * NA_WIN_R - 1):
        t = jnp.zeros((GRID_W, LANES), F32)
        for k in range(n_dc):
            t = jnp.where(dc == k, rpb_ref[(h * (2 * NA_WIN_R - 1) + dr) * n_dc + k], t)
        tiles.append(jnp.where(col_in, t, -jnp.inf))
    for dr in range(2 * NA_WIN_R - 2):
        o_ref[dr] = jnp.where(low, tiles[dr], tiles[dr + 1])


def _na_bias_table(rpb):
    n_dr = 2 * NA_WIN_R - 2
    return pl.pallas_call(
        _na_bias_body,
        grid=(NA_HEADS,),
        in_specs=[pl.BlockSpec(memory_space=pltpu.SMEM)],
        out_specs=pl.BlockSpec((None, n_dr, GRID_W, LANES), lambda h: (h, 0, 0, 0)),
        out_shape=jax.ShapeDtypeStruct((NA_HEADS, n_dr, GRID_W, LANES), F32),
        compiler_params=_cparams(("arbitrary",), 32),
        name="na_bias",
    )(rpb.reshape(-1))


def _rope_tables():
    quarter = NA_HEAD_DIM // 4
    inv = ROPE_BASE ** (-np.arange(quarter, dtype=np.float64) / quarter)
    t = np.arange(SEQ)
    row, col = t // GRID_W, t % GRID_W
    ang = np.concatenate([row[:, None] * inv[None], row[:, None] * inv[None],
                          col[:, None] * inv[None], col[:, None] * inv[None]], axis=1)
    sign = np.tile(np.concatenate([-np.ones(quarter), np.ones(quarter)]), 2)[None]
    return jnp.asarray(np.cos(ang), F32), jnp.asarray(np.sin(ang) * sign, F32)


def _rope(v, cos, sin_signed):
    quarter = NA_HEAD_DIM // 4
    lane = lax.broadcasted_iota(jnp.int32, v.shape, 1)
    first = (lane % (2 * quarter)) < quarter
    partner = jnp.where(first, pltpu.roll(v, NA_HEAD_DIM - quarter, axis=1), pltpu.roll(v, quarter, axis=1))
    return v * cos + partner * sin_signed


def _na_body(rows_per_step, q_ref, k_ref, v_ref, kc_ref, vc_ref, cosq_ref, sinq_ref, cos_ref, sin_ref, bias_ref,
             o_ref, kr_ref, vb_ref):
    rb = pl.program_id(2)
    n_rows = SEQ // GRID_W
    scale = NA_HEAD_DIM ** -0.5

    @pl.when(rb == 0)
    def _():
        kr_ref[...] = _rope(k_ref[...], cos_ref[...], sin_ref[...]).astype(BF16)
        vb_ref[...] = v_ref[...].astype(BF16)

    qr = (_rope(q_ref[...], cosq_ref[...], sinq_ref[...]) * scale).astype(BF16)
    kc = kc_ref[...].astype(BF16)
    vc = vc_ref[...].astype(BF16)
    for lr in range(rows_per_step):
        r = rb * rows_per_step + lr
        r0 = jnp.clip(r - NA_WIN_R // 2, 0, n_rows - NA_WIN_R)
        start = pl.multiple_of(r0 * GRID_W, GRID_W)
        k_win = kr_ref[pl.ds(start, NA_WIN_R * GRID_W), :]
        v_win = vb_ref[pl.ds(start, NA_WIN_R * GRID_W), :]
        q_r = qr[lr * GRID_W:(lr + 1) * GRID_W]
        dr0 = r0 - r + NA_WIN_R - 1
        bias = jnp.concatenate([bias_ref[dr0 + 2 * p] for p in range(NA_WIN_R // 2)], axis=1)
        s_loc = _dot_nt(q_r, k_win) + bias
        s_ctx = _dot_nt(q_r, kc)
        m = jnp.maximum(jnp.max(s_loc, axis=-1, keepdims=True), jnp.max(s_ctx, axis=-1, keepdims=True))
        p_loc = jnp.exp(s_loc - m)
        p_ctx = jnp.exp(s_ctx - m)
        denom = jnp.sum(p_loc, axis=-1, keepdims=True) + jnp.sum(p_ctx, axis=-1, keepdims=True)
        o = _dot(p_loc.astype(BF16), v_win) + _dot(p_ctx.astype(BF16), vc)
        o_ref[lr * GRID_W:(lr + 1) * GRID_W, :] = (o / denom).astype(o_ref.dtype)


def _na_attention(qkv, bias_tbl):
    rows_per_step = 8
    tq = rows_per_step * GRID_W
    steps = SEQ // tq
    cos, sin = _rope_tables()
    hd = NA_HEAD_DIM
    ctx_blk0 = T_LAT // CTX_LEN
    return pl.pallas_call(
        functools.partial(_na_body, rows_per_step),
        grid=(BATCH, NA_HEADS, steps),
        in_specs=[
            pl.BlockSpec((tq, hd), lambda b, h, r: (b * steps + r, h)),
            pl.BlockSpec((SEQ, hd), lambda b, h, r: (b, NA_HEADS + h)),
            pl.BlockSpec((SEQ, hd), lambda b, h, r: (b, 2 * NA_HEADS + h)),
            pl.BlockSpec((CTX_LEN, hd), lambda b, h, r: (ctx_blk0 + b, NA_HEADS + h)),
            pl.BlockSpec((CTX_LEN, hd), lambda b, h, r: (ctx_blk0 + b, 2 * NA_HEADS + h)),
            pl.BlockSpec((tq, hd), lambda b, h, r: (r, 0)),
            pl.BlockSpec((tq, hd), lambda b, h, r: (r, 0)),
            pl.BlockSpec((SEQ, hd), lambda b, h, r: (0, 0)),
            pl.BlockSpec((SEQ, hd), lambda b, h, r: (0, 0)),
            pl.BlockSpec((None,) + bias_tbl.shape[1:], lambda b, h, r: (h, 0, 0, 0)),
        ],
        out_specs=pl.BlockSpec((tq, hd), lambda b, h, r: (b * steps + r, h)),
        out_shape=jax.ShapeDtypeStruct((T_LAT, D_MODEL), BF16),
        scratch_shapes=[pltpu.VMEM((SEQ, hd), BF16), pltpu.VMEM((SEQ, hd), BF16)],
        compiler_params=_cparams(("arbitrary", "arbitrary", "arbitrary"), 48),
        name="na_attention",
    )(qkv, qkv, qkv, qkv, qkv, cos, sin, cos, sin, bias_tbl)


def _ctx_attn_body(q_ref, k_ref, v_ref, o_ref):
    scale = NA_HEAD_DIM ** -0.5
    s = _dot_nt((q_ref[...] * scale).astype(BF16), k_ref[...].astype(BF16))
    p = jnp.exp(s - jnp.max(s, axis=-1, keepdims=True))
    o = _dot(p.astype(BF16), v_ref[...].astype(BF16))
    o_ref[...] = (o / jnp.sum(p, axis=-1, keepdims=True)).astype(o_ref.dtype)


def _ctx_attention(qkv):
    hd = NA_HEAD_DIM
    ctx_blk0 = T_LAT // CTX_LEN
    return pl.pallas_call(
        _ctx_attn_body,
        grid=(BATCH, NA_HEADS),
        in_specs=[
            pl.BlockSpec((CTX_LEN, hd), lambda b, h: (ctx_blk0 + b, h)),
            pl.BlockSpec((CTX_LEN, hd), lambda b, h: (ctx_blk0 + b, NA_HEADS + h)),
            pl.BlockSpec((CTX_LEN, hd), lambda b, h: (ctx_blk0 + b, 2 * NA_HEADS + h)),
        ],
        out_specs=pl.BlockSpec((CTX_LEN, hd), lambda b, h: (b, h)),
        out_shape=jax.ShapeDtypeStruct((T_CTX, D_MODEL), BF16),
        compiler_params=_cparams(("arbitrary", "arbitrary"), 32),
        name="ctx_attention",
    )(qkv, qkv, qkv)


def kernel(x, c, ctx, c_ctx, w_mod, b_mod, norm_g, ffn_w_up, ffn_conv_w, ffn_conv_b, ffn_w_down, ssm_w_in, ssm_conv_w, ssm_conv_b, ssm_dt_bias, ssm_a_log, ssm_d, ssm_norm_g, ssm_w_out, hg_w_in, hg_lb, hg_norm_g, hg_w_out, na_w_qkv, na_rpb, na_w_out):
    xs = jnp.concatenate([x.reshape(T_LAT, D_MODEL), ctx.reshape(T_CTX, D_MODEL)], axis=0)
    cvec8 = jnp.concatenate([c, c_ctx[None], jnp.zeros((SUBLANES - BATCH - 1, D_MODEL), F32)], axis=0)
    mod = _adaln(cvec8, w_mod, b_mod)
    ffn_up_bf, ffn_down_bf = ffn_w_up.astype(BF16), ffn_w_down.astype(BF16)
    ssm_out_bf, hg_out_bf, na_out_bf = ssm_w_out.astype(BF16), hg_w_out.astype(BF16), na_w_out.astype(BF16)

    h = _prep(xs, norm_g[0, 0], mod[0])
    for i in range(DEPTH):
        last = i == DEPTH - 1
        kind, slot = i % N_MIXERS, i // N_MIXERS
        mod_l = mod[i]
        n_rows = T_LAT if last else T_ALL
        op = dict(xs=xs, g_res=norm_g[i, 1], g_ffn=norm_g[i, 2], mod_l=mod_l, n_rows=n_rows, slot=slot)
        if kind == 0:
            proj = _in_proj(h, ssm_w_in, slot, 1152, "ssd_in")
            xbc = _ssd_conv(proj, ssm_conv_w[slot], ssm_conv_b[slot])
            y2 = _ssd_scan(xbc, proj, ssm_dt_bias[slot], ssm_a_log[slot], ssm_d[slot])
            xs, hf = _out_proj(_ssd_gate_prologue, [((y2, 0), 0), ((y2, 1), 0), (proj, 0)],
                               ssm_norm_g[slot].reshape(1, SSM_INNER), ssm_out_bf, tk=1024, row_norm=True,
                               name="ssd_out", **op)
        elif kind == 1:
            proj = _in_proj(h, hg_w_in, slot, 1024, "hg_in")
            o2 = _gla_scan(proj, hg_lb, i)
            tk = 1024
            xs, hf = _out_proj(_hg_readout_prologue, [((o2, 0), 0), ((o2, 1), 0), (proj, 4 * D_MODEL // tk)],
                               hg_norm_g[slot].reshape(1, D_MODEL), hg_out_bf, tk=tk, row_norm=False,
                               name="hg_out", **op)
        else:
            qkv = _in_proj(h, na_w_qkv, slot, 1024, "na_in")
            o = jnp.concatenate([_na_attention(qkv, _na_bias_table(na_rpb[slot])), _ctx_attention(qkv)], axis=0)
            xs, hf = _out_proj(_plain_prologue, [(o, 0)], None, na_out_bf, tk=1024, row_norm=False,
                               name="na_out", **op)
        nxt = min(i + 1, DEPTH - 1)
        xs, h = _ffn(hf, xs, i, ffn_up_bf, ffn_conv_w, ffn_conv_b, ffn_down_bf, norm_g[i, 3], mod_l,
                     norm_g[nxt, 0], mod[nxt], n_rows=n_rows, emit_h=not last)

    return xs.reshape(BATCH, SEQ, D_MODEL)
```

```python
import functools

import numpy as np
import jax
import jax.numpy as jnp
from jax import lax
from jax.experimental import pallas as pl
from jax.experimental.pallas import tpu as pltpu

F32 = jnp.float32
BF16 = jnp.bfloat16

D_MODEL = 2048
BATCH = 2
SEQ = 4096
DEPTH = 4
GRID_W = 64
CTX_LEN = 256
N_MIXERS = 3
RMS_EPS = 1e-6
SSM_INNER = 4096
SSM_HEAD_DIM = 64
SSM_HEADS = 64
SSM_STATE = 128
SSM_GROUPS = 8
SSM_CONV_DIM = SSM_INNER + 2 * SSM_GROUPS * SSM_STATE
SSM_IN_DIM = SSM_INNER + SSM_CONV_DIM + 2 * SSM_HEADS
SSM_CHUNK = 128
HG_HEADS = 16
HG_KEY_DIM = 128
HG_IN_DIM = 5 * D_MODEL
HG_CHUNK = 64
NA_HEADS = 16
NA_HEAD_DIM = 128
NA_WIN_R = 8
NA_WIN_C = 16
ROPE_BASE = 10000.0
FFN_HIDDEN = 5632

T_LAT = BATCH * SEQ
T_CTX = BATCH * CTX_LEN
T_ALL = T_LAT + T_CTX
LANES = 128
SUBLANES = 8
BF16_ROWS = 16
ONES_ROWS = 16
ROW_TILE = 512
FFN_COL_CHUNK = 256
FFN_ROW_CHUNKS = 2
MIB = 1024 * 1024
LOG2E = 1.4426950408889634


def _cparams(sem, vmem_mib):
    return pltpu.CompilerParams(dimension_semantics=sem, vmem_limit_bytes=vmem_mib * MIB)


def _silu(v):
    return v * jax.nn.sigmoid(v)


def _rms(v):
    return v * lax.rsqrt(jnp.mean(v * v, axis=-1, keepdims=True) + RMS_EPS)


def _log1p_unit(e):
    return jnp.log(1.0 + e)


def _softplus(v):
    return jnp.maximum(v, 0.0) + _log1p_unit(jnp.exp(-jnp.abs(v)))


def _dot(a, b):
    return jnp.dot(a, b, preferred_element_type=F32)


def _dot_nt(a, b):
    return lax.dot_general(a, b, (((1,), (1,)), ((), ())), preferred_element_type=F32)


def _dot_tn(a, b):
    return lax.dot_general(a, b, (((0,), (0,)), ((), ())), preferred_element_type=F32)


def _bf16_pieces(v, terms):
    pieces = []
    rem = v
    for _ in range(terms):
        part = rem.astype(BF16)
        rem = rem - part.astype(F32)
        pieces.append(part)
    return pieces


def _dot_sel(m01_rep, v, terms):
    return _dot(m01_rep, jnp.concatenate(_bf16_pieces(v, terms), axis=0))


def _dot_sel_rhs(v, m01_rep, terms):
    return _dot(jnp.concatenate(_bf16_pieces(v, terms), axis=1), m01_rep)


def _mod_row(row0):
    return jnp.where(row0 >= T_LAT, 2, row0 // SEQ)


def _mod_vec(mod_ref, r, idx):
    return mod_ref[pl.ds(r, 1), idx * D_MODEL:(idx + 1) * D_MODEL]


def _norm_mod(x, g, mod_ref, r, shift_idx, scale_idx):
    return _rms(x) * g * (1.0 + _mod_vec(mod_ref, r, scale_idx)) + _mod_vec(mod_ref, r, shift_idx)


def _seg_edges(row0, rows):
    gr = row0 + lax.broadcasted_iota(jnp.int32, (rows, 1), 0)
    pos = jnp.where(gr >= T_LAT, (gr - T_LAT) % CTX_LEN, gr % SEQ)
    length = jnp.where(gr >= T_LAT, CTX_LEN, SEQ)
    return pos == 0, pos == length - 1


def _shift_rows(cur, prev_row, next_row, row0):
    rows = cur.shape[0]
    first, last = _seg_edges(row0, rows)
    ridx = lax.broadcasted_iota(jnp.int32, (rows, 1), 0)
    up = jnp.where(ridx == 0, prev_row, pltpu.roll(cur, 1, axis=0))
    up = jnp.where(first, 0.0, up)
    dn = jnp.where(ridx == rows - 1, next_row, pltpu.roll(cur, rows - 1, axis=0))
    dn = jnp.where(last, 0.0, dn)
    return up, dn


def _token_conv(cur, prev_row, next_row, row0, w_ref, b_ref):
    up, dn = _shift_rows(cur, prev_row, next_row, row0)
    return b_ref[...] + up * w_ref[0:1, :] + cur * w_ref[1:2, :] + dn * w_ref[2:3, :]


def _halo_specs(tm, halo, width, n_rows, col_fn):
    per = tm // halo
    last_blk = n_rows // halo - 1
    prev = pl.BlockSpec((halo, width), lambda i, j: (jnp.maximum(i * per - 1, 0), col_fn(j)))
    nxt = pl.BlockSpec((halo, width), lambda i, j: (jnp.minimum((i + 1) * per, last_blk), col_fn(j)))
    return prev, nxt


def _adaln_body(c_ref, w_ref, b_ref, o_ref):
    s = _silu(c_ref[...]).astype(BF16)
    o_ref[...] = _dot(s, w_ref[...].astype(BF16)) + b_ref[...]


def _adaln(cvec8, w_mod, b_mod):
    tn = 1024
    n_out = 6 * D_MODEL
    return pl.pallas_call(
        _adaln_body,
        grid=(DEPTH, n_out // tn),
        in_specs=[
            pl.BlockSpec((SUBLANES, D_MODEL), lambda l, j: (0, 0)),
            pl.BlockSpec((None, D_MODEL, tn), lambda l, j: (l, 0, j)),
            pl.BlockSpec((None, 1, tn), lambda l, j: (l, 0, j)),
        ],
        out_specs=pl.BlockSpec((None, SUBLANES, tn), lambda l, j: (l, 0, j)),
        out_shape=jax.ShapeDtypeStruct((DEPTH, SUBLANES, n_out), F32),
        compiler_params=_cparams(("arbitrary", "arbitrary"), 40),
        name="adaln",
    )(cvec8, w_mod, b_mod.reshape(DEPTH, 1, n_out))


def _prep_body(x_ref, g_ref, mod_ref, o_ref):
    r = _mod_row(pl.program_id(0) * ROW_TILE)
    o_ref[...] = _norm_mod(x_ref[...], g_ref[...], mod_ref, r, 0, 1).astype(BF16)


def _prep(xs, g_row, mod_l):
    row = pl.BlockSpec((ROW_TILE, D_MODEL), lambda i: (i, 0))
    return pl.pallas_call(
        _prep_body,
        grid=(T_ALL // ROW_TILE,),
        in_specs=[row, pl.BlockSpec((1, D_MODEL), lambda i: (0, 0)), pl.BlockSpec(mod_l.shape, lambda i: (0, 0))],
        out_specs=row,
        out_shape=jax.ShapeDtypeStruct((T_ALL, D_MODEL), BF16),
        compiler_params=_cparams(("arbitrary",), 32),
        name="prep",
    )(xs, g_row.reshape(1, D_MODEL), mod_l)


def _in_proj_body(h_ref, w_ref, o_ref, wb_ref):
    @pl.when(pl.program_id(1) == 0)
    def _():
        wb_ref[...] = w_ref[...].astype(BF16)

    o_ref[...] = _dot(h_ref[...], wb_ref[...])


def _in_proj(h, w_stack, slot, tn, name):
    tm = T_ALL // 8
    n_dim = w_stack.shape[2]
    return pl.pallas_call(
        _in_proj_body,
        grid=(n_dim // tn, T_ALL // tm),
        in_specs=[pl.BlockSpec((tm, D_MODEL), lambda j, i: (i, 0)),
                  pl.BlockSpec((None, D_MODEL, tn), lambda j, i: (slot, 0, j))],
        out_specs=pl.BlockSpec((tm, tn), lambda j, i: (i, j)),
        out_shape=jax.ShapeDtypeStruct((T_ALL, n_dim), F32),
        scratch_shapes=[pltpu.VMEM((D_MODEL, tn), BF16)],
        compiler_params=_cparams(("arbitrary", "arbitrary"), 56),
        name=name,
    )(h, w_stack)


def _ssd_gate_prologue(yf_ref, yb_ref, z_ref, g_ref):
    v = (yf_ref[...] + yb_ref[...]) * _silu(z_ref[...])
    return v * g_ref[...], jnp.sum(v * v, axis=-1, keepdims=True)


def _hg_readout_prologue(of_ref, ob_ref, gate_ref, g_ref):
    o = of_ref[...] + ob_ref[...]
    parts = [_rms(o[:, h * LANES:(h + 1) * LANES]) for h in range(o.shape[1] // LANES)]
    return jnp.concatenate(parts, axis=-1) * g_ref[...] * _silu(gate_ref[...]), None


def _plain_prologue(o_ref):
    return o_ref[...], None


def _out_proj_body(prologue, n_in, k_dim, row_norm, *refs):
    ins = refs[:n_in]
    w_ref, x_ref, g1_ref, g2_ref, mod_ref, xo_ref, ho_ref, acc_ref, ssq_ref = refs[n_in:]
    k = pl.program_id(1)
    part, sq = prologue(*ins)
    contrib = _dot(part.astype(BF16), w_ref[...])

    @pl.when(k == 0)
    def _():
        acc_ref[...] = contrib
        if row_norm:
            ssq_ref[...] = sq

    @pl.when(k != 0)
    def _():
        acc_ref[...] += contrib
        if row_norm:
            ssq_ref[...] += sq

    @pl.when(k == pl.num_programs(1) - 1)
    def _():
        y = acc_ref[...]
        if row_norm:
            y = y * lax.rsqrt(ssq_ref[...] / k_dim + RMS_EPS)
        r = _mod_row(pl.program_id(0) * ROW_TILE)
        xn = x_ref[...] + _mod_vec(mod_ref, r, 2) * (_rms(y) * g1_ref[...])
        xo_ref[...] = xn
        ho_ref[...] = _norm_mod(xn, g2_ref[...], mod_ref, r, 3, 4).astype(BF16)


def _out_proj(prologue, row_ins, gain, w_stack, slot, xs, g_res, g_ffn, mod_l, *, n_rows, tk, row_norm, name):
    k_dim = w_stack.shape[1]
    tm = ROW_TILE
    in_specs, args = [], []
    for arr, cb0 in row_ins:
        if isinstance(arr, tuple):
            in_specs.append(pl.BlockSpec((None, tm, tk), functools.partial(
                lambda i, k, ld, cb0: (ld, i, cb0 + k), ld=arr[1], cb0=cb0)))
            args.append(arr[0])
        else:
            in_specs.append(pl.BlockSpec((tm, tk), functools.partial(lambda i, k, cb0: (i, cb0 + k), cb0=cb0)))
            args.append(arr)
    if gain is not None:
        in_specs.append(pl.BlockSpec((1, tk), lambda i, k: (0, k)))
        args.append(gain)
    n_in = len(args)
    row = pl.BlockSpec((tm, D_MODEL), lambda i, k: (i, 0))
    vec = pl.BlockSpec((1, D_MODEL), lambda i, k: (0, 0))
    in_specs += [pl.BlockSpec((None, tk, D_MODEL), lambda i, k: (slot, k, 0)), row, vec, vec,
                 pl.BlockSpec(mod_l.shape, lambda i, k: (0, 0))]
    return pl.pallas_call(
        functools.partial(_out_proj_body, prologue, n_in, k_dim, row_norm),
        grid=(n_rows // tm, k_dim // tk),
        in_specs=in_specs,
        out_specs=[row, row],
        out_shape=[jax.ShapeDtypeStruct((n_rows, D_MODEL), F32), jax.ShapeDtypeStruct((n_rows, D_MODEL), BF16)],
        scratch_shapes=[pltpu.VMEM((tm, D_MODEL), F32), pltpu.VMEM((tm, 1), F32)],
        compiler_params=_cparams(("arbitrary", "arbitrary"), 56),
        name=name,
    )(*args, w_stack, xs, g_res.reshape(1, D_MODEL), g_ffn.reshape(1, D_MODEL), mod_l)


def _ffn_body(emit_h, has_ctx, h_ref, hp_ref, hn_ref, wa_ref, wv_ref, cwa_ref, cwv_ref, cba_ref, cbv_ref, wd_ref,
              x_ref, g_res_ref, mod_ref, g_next_ref, modn_ref, *out_and_scratch):
    if emit_h:
        xo_ref, ho_ref, hall_ref, acc_ref = out_and_scratch
    else:
        xo_ref, hall_ref, acc_ref = out_and_scratch
    k = pl.program_id(1)
    tm = ROW_TILE
    row0 = pl.program_id(0) * tm
    pad = BF16_ROWS
    rows = tm + 2 * pad

    @pl.when(k == 0)
    def _():
        hall_ref[pl.ds(0, pad), :] = hp_ref[...]
        hall_ref[pl.ds(pad, tm), :] = h_ref[...]
        hall_ref[pl.ds(pad + tm, pad), :] = hn_ref[...]

    first, last = _seg_edges(row0, tm)
    rc = tm // FFN_ROW_CHUNKS
    rr = rc + 2 * pad

    @pl.when(k == 0)
    def _():
        acc_ref[...] = jnp.zeros_like(acc_ref)

    for r in range(FFN_ROW_CHUNKS):
        hall = hall_ref[pl.ds(r * rc, rr), :]
        f_r, l_r = first[r * rc:(r + 1) * rc], last[r * rc:(r + 1) * rc]

        def branch(w_ref, cw_ref, cb_ref, cs):
            u = _dot(hall, w_ref[:, cs])
            up = jnp.where(f_r, 0.0, pltpu.roll(u, 1, axis=0)[pad:pad + rc])
            dn = jnp.where(l_r, 0.0, pltpu.roll(u, rr - 1, axis=0)[pad:pad + rc])
            return cb_ref[:, cs] + up * cw_ref[0:1, cs] + u[pad:pad + rc] * cw_ref[1:2, cs] + dn * cw_ref[2:3, cs]

        for c in range(wa_ref.shape[1] // FFN_COL_CHUNK):
            cs = pl.ds(c * FFN_COL_CHUNK, FFN_COL_CHUNK)
            act = _silu(branch(wa_ref, cwa_ref, cba_ref, cs)) * branch(wv_ref, cwv_ref, cbv_ref, cs)
            acc_ref[pl.ds(r * rc, rc), :] += _dot(act.astype(BF16), wd_ref[cs, :])

    @pl.when(k == pl.num_programs(1) - 1)
    def _():
        r = _mod_row(row0)
        xn = x_ref[...] + _mod_vec(mod_ref, r, 5) * (_rms(acc_ref[...]) * g_res_ref[...])
        xo_ref[...] = xn
        if emit_h:
            ho_ref[...] = _norm_mod(xn, g_next_ref[...], modn_ref, r, 0, 1).astype(BF16)


def _ffn(h, xs, layer, w_up, conv_w, conv_b, w_down, g_res, mod_l, g_next, mod_next, *, n_rows, emit_h):
    tm, tk = ROW_TILE, 512
    kt = FFN_HIDDEN // tk
    h_prev, h_next = _halo_specs(tm, BF16_ROWS, D_MODEL, n_rows, lambda k: 0)
    row = pl.BlockSpec((tm, D_MODEL), lambda i, k: (i, 0))
    vec = pl.BlockSpec((1, D_MODEL), lambda i, k: (0, 0))
    tab = pl.BlockSpec(mod_l.shape, lambda i, k: (0, 0))
    out_specs = [row, row] if emit_h else [row]
    out_shape = [jax.ShapeDtypeStruct((n_rows, D_MODEL), F32)]
    if emit_h:
        out_shape.append(jax.ShapeDtypeStruct((n_rows, D_MODEL), BF16))
    conv_b3 = conv_b.reshape(DEPTH, 1, 2 * FFN_HIDDEN)

    res = pl.pallas_call(
        functools.partial(_ffn_body, emit_h, n_rows > T_LAT),
        grid=(n_rows // tm, kt),
        in_specs=[
            row, h_prev, h_next,
            pl.BlockSpec((None, D_MODEL, tk), lambda i, k: (layer, 0, k)),
            pl.BlockSpec((None, D_MODEL, tk), lambda i, k: (layer, 0, k + kt)),
            pl.BlockSpec((None, 3, tk), lambda i, k: (layer, 0, k)),
            pl.BlockSpec((None, 3, tk), lambda i, k: (layer, 0, k + kt)),
            pl.BlockSpec((None, 1, tk), lambda i, k: (layer, 0, k)),
            pl.BlockSpec((None, 1, tk), lambda i, k: (layer, 0, k + kt)),
            pl.BlockSpec((None, tk, D_MODEL), lambda i, k: (layer, k, 0)),
            row, vec, tab, vec, tab,
        ],
        out_specs=out_specs,
        out_shape=out_shape,
        scratch_shapes=[pltpu.VMEM((tm + 2 * BF16_ROWS, D_MODEL), BF16), pltpu.VMEM((tm, D_MODEL), F32)],
        compiler_params=_cparams(("arbitrary", "arbitrary"), 56),
        name="conv_ffn",
    )(h, h, h, w_up, w_up, conv_w, conv_w, conv_b3, conv_b3, w_down, xs, g_res.reshape(1, D_MODEL), mod_l,
      g_next.reshape(1, D_MODEL), mod_next)
    return (res[0], res[1]) if emit_h else (res[0], None)


def _ssd_conv_body(tm, cur_ref, prev_ref, next_ref, w_ref, b_ref, o_ref):
    u = _token_conv(cur_ref[...], prev_ref[SUBLANES - 1:SUBLANES, :], next_ref[0:1, :], pl.program_id(0) * tm,
                    w_ref, b_ref)
    o_ref[...] = _silu(u)


def _ssd_conv(proj, conv_w, conv_b):
    tm, tc = 512, 2048
    col0 = SSM_INNER // tc
    prev, nxt = _halo_specs(tm, SUBLANES, tc, T_ALL, lambda j: j + col0)
    return pl.pallas_call(
        functools.partial(_ssd_conv_body, tm),
        grid=(T_ALL // tm, SSM_CONV_DIM // tc),
        in_specs=[pl.BlockSpec((tm, tc), lambda i, j: (i, j + col0)), prev, nxt,
                  pl.BlockSpec((3, tc), lambda i, j: (0, j)), pl.BlockSpec((1, tc), lambda i, j: (0, j))],
        out_specs=pl.BlockSpec((tm, tc), lambda i, j: (i, j)),
        out_shape=jax.ShapeDtypeStruct((T_ALL, SSM_CONV_DIM), F32),
        compiler_params=_cparams(("arbitrary", "arbitrary"), 40),
        name="ssd_conv",
    )(proj, proj, proj, conv_w, conv_b.reshape(1, -1))


def _scan_chunk_index(d, b, s, n_lat, n_ctx):
    ctx0 = BATCH * n_lat + b * n_ctx
    fwd = jnp.where(s < n_ctx, ctx0 + s, b * n_lat + s - n_ctx)
    bwd = jnp.where(s < n_ctx, ctx0 + n_ctx - 1 - s, b * n_lat + n_lat - 1 - (s - n_ctx))
    return jnp.where(d == 0, fwd, bwd)


SSD_CUM_TERMS = 3
SSD_DT_TERMS = 2


def _ssd_scan_consts():
    q = SSM_CHUNK
    i = np.arange(q)
    pfx = np.zeros((2, q + ONES_ROWS, q), np.float32)
    pfx[0, :q] = i[:, None] >= i[None, :]
    pfx[1, :q] = i[:, None] <= i[None, :]
    pfx[:, q:] = 1.0
    pfx = np.tile(pfx, (1, 1, SSD_CUM_TERMS))
    expand = np.repeat(np.eye(SSM_HEADS, dtype=np.float32), SSM_HEAD_DIM, axis=1)
    expand = np.tile(expand, (SSD_DT_TERMS, 1))
    return jnp.asarray(pfx, BF16), jnp.asarray(expand, BF16)


def _ssd_scan_body(x_ref, b_ref, c_ref, dtraw_ref, dtb_ref, alog_ref, dskip_ref, pfx_ref, exp_ref, y_ref, s_ref):
    q = SSM_CHUNK
    hp = 2 * SSM_HEAD_DIM
    d = pl.program_id(0)

    @pl.when(pl.program_id(2) == 0)
    def _():
        s_ref[...] = jnp.zeros_like(s_ref)

    def pick(v):
        return jnp.where(d == 0, v[:, :SSM_HEADS], v[:, SSM_HEADS:])

    dt2 = _softplus(dtraw_ref[...] + dtb_ref[...])
    cum2 = _dot_sel(pfx_ref[...], dt2 * (-LOG2E * jnp.exp(alog_ref[...])), SSD_CUM_TERMS)
    acum2 = cum2[:q]
    acum = pick(acum2)
    tot = pick(cum2[q:q + SUBLANES])[0:1]
    acum_t2 = acum2.T
    acum_t = jnp.where(d == 0, acum_t2[:SSM_HEADS], acum_t2[SSM_HEADS:])
    dt_wide = _dot_sel_rhs(pick(dt2), exp_ref[...], SSD_DT_TERMS)

    ii = lax.broadcasted_iota(jnp.int32, (q, q), 0)
    jj = lax.broadcasted_iota(jnp.int32, (q, q), 1)
    causal = (jj - ii) * (1 - 2 * d) <= 0
    low_half = lax.broadcasted_iota(jnp.int32, (q, hp), 1) < SSM_HEAD_DIM

    heads_per_group = SSM_HEADS // SSM_GROUPS
    for g in range(SSM_GROUPS):
        b_g = b_ref[:, g * SSM_STATE:(g + 1) * SSM_STATE].astype(BF16)
        c_g32 = c_ref[:, g * SSM_STATE:(g + 1) * SSM_STATE]
        cb = jnp.where(causal, _dot_nt(c_g32.astype(BF16), b_g), 0.0)
        for pair in range(heads_per_group // 2):
            lo = (g * heads_per_group + 2 * pair) * SSM_HEAD_DIM
            x_pair = x_ref[:, lo:lo + hp]
            xdt = x_pair * dt_wide[:, lo:lo + hp]
            rhs = jnp.concatenate([xdt.astype(BF16), s_ref[:, lo:lo + hp].astype(BF16)], axis=0)
            halves, cols, tots = [], [], []
            for sub in range(2):
                h = g * heads_per_group + 2 * pair + sub
                col = jnp.broadcast_to(acum[:, h:h + 1], (q, q))
                seg = jnp.where(causal, col - acum_t[h:h + 1, :], 0.0)
                scores = jnp.exp2(seg) * cb
                c_dec = c_g32 * jnp.exp2(col)
                lhs = jnp.concatenate([scores.astype(BF16), c_dec.astype(BF16)], axis=1)
                halves.append(_dot(lhs, rhs))
                cols.append(col)
                tots.append(jnp.broadcast_to(tot[:, h:h + 1], (1, hp)))
            y_pair = jnp.where(low_half, halves[0], halves[1])
            y_ref[:, lo:lo + hp] = y_pair + dskip_ref[:, lo:lo + hp] * x_pair
            tot_pair = jnp.where(low_half[0:1], tots[0], tots[1])
            to_end = jnp.exp2(tot_pair - jnp.where(low_half, cols[0], cols[1]))
            s_ref[:, lo:lo + hp] = (s_ref[:, lo:lo + hp] * jnp.exp2(tot_pair)
                                    + _dot_tn(b_g, (xdt * to_end).astype(BF16)))


def _ssd_scan(xbc, proj, dt_bias, a_log, d_skip):
    q = SSM_CHUNK
    n_lat, n_ctx = SEQ // q, CTX_LEN // q
    pfx, expand = _ssd_scan_consts()
    n_heads2 = 2 * SSM_HEADS
    dt_col = (SSM_INNER + SSM_CONV_DIM) // n_heads2
    b_col = SSM_INNER // (SSM_GROUPS * SSM_STATE)
    d_wide = jnp.repeat(d_skip, SSM_HEAD_DIM, axis=1).reshape(2, 1, SSM_INNER)

    def chunk(d, b, s):
        return _scan_chunk_index(d, b, s, n_lat, n_ctx)

    return pl.pallas_call(
        _ssd_scan_body,
        grid=(2, BATCH, n_lat + n_ctx),
        in_specs=[
            pl.BlockSpec((q, SSM_INNER), lambda d, b, s: (chunk(d, b, s), 0)),
            pl.BlockSpec((q, SSM_GROUPS * SSM_STATE), lambda d, b, s: (chunk(d, b, s), b_col)),
            pl.BlockSpec((q, SSM_GROUPS * SSM_STATE), lambda d, b, s: (chunk(d, b, s), b_col + 1)),
            pl.BlockSpec((q, n_heads2), lambda d, b, s: (chunk(d, b, s), dt_col)),
            pl.BlockSpec((1, n_heads2), lambda d, b, s: (0, 0)),
            pl.BlockSpec((1, n_heads2), lambda d, b, s: (0, 0)),
            pl.BlockSpec((None, 1, SSM_INNER), lambda d, b, s: (d, 0, 0)),
            pl.BlockSpec((None,) + pfx.shape[1:], lambda d, b, s: (d, 0, 0)),
            pl.BlockSpec(expand.shape, lambda d, b, s: (0, 0)),
        ],
        out_specs=pl.BlockSpec((None, q, SSM_INNER), lambda d, b, s: (d, chunk(d, b, s), 0)),
        out_shape=jax.ShapeDtypeStruct((2, T_ALL, SSM_INNER), F32),
        scratch_shapes=[pltpu.VMEM((SSM_STATE, SSM_INNER), F32)],
        compiler_params=_cparams(("arbitrary", "arbitrary", "arbitrary"), 48),
        name="ssd_scan",
    )(xbc, xbc, xbc, proj, dt_bias.reshape(1, n_heads2), a_log.reshape(1, n_heads2), d_wide, pfx, expand)


GLA_SUM_TERMS = 2


def _gla_consts():
    q = HG_CHUNK
    levels = int(np.log2(q))
    i = np.arange(q)
    mats = [(i[:, None] >= i[None, :]).astype(np.float32), (i[None, :] > i[:, None]).astype(np.float32)]
    masks = [np.eye(q, dtype=np.float32)]
    for lv in range(levels):
        s = 1 << lv
        blk, pos = i // (2 * s), i % (2 * s)
        bound = blk * 2 * s + s - 1
        right = pos >= s
        t = i[None, :]
        m_right = right[:, None] & (t > bound[:, None]) & (t <= i[:, None])
        m_left = (~right)[:, None] & (t > i[:, None]) & (t <= bound[:, None])
        mats.append((m_right | m_left).astype(np.float32))
        masks.append((right[:, None] & (~right)[None, :] & (blk[:, None] == blk[None, :])).astype(np.float32))
    mats.append(np.ones((ONES_ROWS, q), np.float32))
    fwd_m, fwd_k = np.concatenate(mats, axis=0), np.stack(masks)
    flip_rows = np.concatenate([np.arange(k * q, (k + 1) * q)[::-1] for k in range(levels + 2)]
                               + [np.arange((levels + 2) * q, (levels + 2) * q + ONES_ROWS)])
    bwd_m = fwd_m[flip_rows][:, ::-1]
    bwd_k = fwd_k[:, ::-1, ::-1]
    mats2 = np.tile(np.stack([fwd_m, bwd_m]), (1, 1, GLA_SUM_TERMS))
    return (jnp.asarray(mats2, BF16), jnp.asarray(np.stack([fwd_k, bwd_k]), F32), levels)


def _gla_body(levels, layer, q_ref, v_ref, f_ref, lbp_ref, mats_ref, masks_ref, o_ref, st_ref):
    q = HG_CHUNK

    @pl.when(pl.program_id(2) == 0)
    def _():
        st_ref[...] = jnp.zeros_like(st_ref)

    lbp = lbp_ref[...]
    e = jnp.exp(lbp - jnp.max(lbp, axis=0, keepdims=True))
    sm = e / jnp.sum(e, axis=0, keepdims=True)
    lb = jnp.sum(sm[1:layer + 1], axis=0, keepdims=True) if layer > 0 else jnp.zeros((1, lbp.shape[1]), F32)

    f = f_ref[...]
    ef = jnp.exp(-jnp.abs(f))
    log_sig = jnp.minimum(f, 0.0) - _log1p_unit(ef)
    la = jnp.log(lb)
    lc = jnp.log1p(-lb) + log_sig
    logf = jnp.maximum(la, lc) + _log1p_unit(jnp.exp(-jnp.abs(la - lc)))
    key = (1.0 - lb) * (jnp.where(f >= 0.0, ef, 1.0) / (1.0 + ef))
    qs = _silu(q_ref[...])

    sums = _dot_sel(mats_ref[...], logf, GLA_SUM_TERMS)
    q_in = (qs * jnp.exp(sums[:q])).astype(BF16)
    k_out = (key * jnp.exp(sums[q:2 * q])).astype(BF16)
    dec = jnp.exp(sums[(levels + 2) * q:(levels + 2) * q + 1])
    qs_bf, key_bf = qs.astype(BF16), key.astype(BF16)
    q_lv, k_lv = [], []
    for lv in range(levels):
        fac = jnp.exp(sums[(lv + 2) * q:(lv + 3) * q])
        q_lv.append((qs * fac).astype(BF16))
        k_lv.append((key * fac).astype(BF16))
    v_bf = v_ref[...].astype(BF16)

    for h in range(HG_HEADS):
        sl = slice(h * LANES, (h + 1) * LANES)
        attn = masks_ref[0] * _dot_nt(qs_bf[:, sl], key_bf[:, sl])
        for lv in range(levels):
            attn = attn + masks_ref[lv + 1] * _dot_nt(q_lv[lv][:, sl], k_lv[lv][:, sl])
        st = st_ref[h]
        o_ref[:, sl] = _dot(attn.astype(BF16), v_bf[:, sl]) + _dot_nt(q_in[:, sl], st.astype(BF16))
        st_ref[h] = st * dec[:, sl] + _dot_tn(v_bf[:, sl], k_out[:, sl])


def _gla_scan(proj, hg_lb, layer):
    q = HG_CHUNK
    n_lat, n_ctx = SEQ // q, CTX_LEN // q
    mats, masks, levels = _gla_consts()

    def chunk(d, b, s):
        return _scan_chunk_index(d, b, s, n_lat, n_ctx)

    return pl.pallas_call(
        functools.partial(_gla_body, levels, layer),
        grid=(2, BATCH, n_lat + n_ctx),
        in_specs=[
            pl.BlockSpec((q, D_MODEL), lambda d, b, s: (chunk(d, b, s), 0)),
            pl.BlockSpec((q, D_MODEL), lambda d, b, s: (chunk(d, b, s), 1)),
            pl.BlockSpec((q, D_MODEL), lambda d, b, s: (chunk(d, b, s), 2 + d)),
            pl.BlockSpec((None, DEPTH, D_MODEL), lambda d, b, s: (d, 0, 0)),
            pl.BlockSpec((None,) + mats.shape[1:], lambda d, b, s: (d, 0, 0)),
            pl.BlockSpec((None,) + masks.shape[1:], lambda d, b, s: (d, 0, 0, 0)),
        ],
        out_specs=pl.BlockSpec((None, q, D_MODEL), lambda d, b, s: (d, chunk(d, b, s), 0)),
        out_shape=jax.ShapeDtypeStruct((2, T_ALL, D_MODEL), F32),
        scratch_shapes=[pltpu.VMEM((HG_HEADS, LANES, HG_KEY_DIM), F32)],
        compiler_params=_cparams(("arbitrary", "arbitrary", "arbitrary"), 48),
        name="gla_scan",
    )(proj, proj, proj, hg_lb, mats, masks)


def _na_bias_body(rpb_ref, o_ref):
    h = pl.program_id(0)
    n_dc = 2 * NA_WIN_C - 1
    qc = lax.broadcasted_iota(jnp.int32, (GRID_W, LANES), 0)
    kc = lax.broadcasted_iota(jnp.int32, (GRID_W, LANES), 1) % GRID_W
    c0 = jnp.clip(qc - NA_WIN_C // 2, 0, GRID_W - NA_WIN_C)
    col_in = (kc >= c0) & (kc < c0 + NA_WIN_C)
    dc = jnp.clip(kc - qc + NA_WIN_C - 1, 0, n_dc - 1)
    low = lax.broadcasted_iota(jnp.int32, (GRID_W, LANES), 1) < GRID_W
    tiles = []
    for dr in range(2 * NA_WIN_R - 1):
        t = jnp.zeros((GRID_W, LANES), F32)
        for k in range(n_dc):
            t = jnp.where(dc == k, rpb_ref[(h * (2 * NA_WIN_R - 1) + dr) * n_dc + k], t)
        tiles.append(jnp.where(col_in, t, -jnp.inf))
    for dr in range(2 * NA_WIN_R - 2):
        o_ref[dr] = jnp.where(low, tiles[dr], tiles[dr + 1])


def _na_bias_table(rpb):
    n_dr = 2 * NA_WIN_R - 2
    return pl.pallas_call(
        _na_bias_body,
        grid=(NA_HEADS,),
        in_specs=[pl.BlockSpec(memory_space=pltpu.SMEM)],
        out_specs=pl.BlockSpec((None, n_dr, GRID_W, LANES), lambda h: (h, 0, 0, 0)),
        out_shape=jax.ShapeDtypeStruct((NA_HEADS, n_dr, GRID_W, LANES), F32),
        compiler_params=_cparams(("arbitrary",), 32),
        name="na_bias",
    )(rpb.reshape(-1))


def _rope_tables():
    quarter = NA_HEAD_DIM // 4
    inv = ROPE_BASE ** (-np.arange(quarter, dtype=np.float64) / quarter)
    t = np.arange(SEQ)
    row, col = t // GRID_W, t % GRID_W
    ang = np.concatenate([row[:, None] * inv[None], row[:, None] * inv[None],
                          col[:, None] * inv[None], col[:, None] * inv[None]], axis=1)
    sign = np.tile(np.concatenate([-np.ones(quarter), np.ones(quarter)]), 2)[None]
    return jnp.asarray(np.cos(ang), F32), jnp.asarray(np.sin(ang) * sign, F32)


def _rope(v, cos, sin_signed):
    quarter = NA_HEAD_DIM // 4
    lane = lax.broadcasted_iota(jnp.int32, v.shape, 1)
    first = (lane % (2 * quarter)) < quarter
    partner = jnp.where(first, pltpu.roll(v, NA_HEAD_DIM - quarter, axis=1), pltpu.roll(v, quarter, axis=1))
    return v * cos + partner * sin_signed


def _na_body(rows_per_step, q_ref, k_ref, v_ref, kc_ref, vc_ref, cosq_ref, sinq_ref, cos_ref, sin_ref, bias_ref,
             o_ref, kr_ref, vb_ref):
    rb = pl.program_id(2)
    n_rows = SEQ // GRID_W
    scale = NA_HEAD_DIM ** -0.5

    @pl.when(rb == 0)
    def _():
        kr_ref[...] = _rope(k_ref[...], cos_ref[...], sin_ref[...]).astype(BF16)
        vb_ref[...] = v_ref[...].astype(BF16)

    qr = (_rope(q_ref[...], cosq_ref[...], sinq_ref[...]) * scale).astype(BF16)
    kc = kc_ref[...].astype(BF16)
    vc = vc_ref[...].astype(BF16)
    starts, s_rows = [], []
    for lr in range(rows_per_step):
        r = rb * rows_per_step + lr
        r0 = jnp.clip(r - NA_WIN_R // 2, 0, n_rows - NA_WIN_R)
        start = pl.multiple_of(r0 * GRID_W, GRID_W)
        dr0 = r0 - r + NA_WIN_R - 1
        bias = jnp.concatenate([bias_ref[dr0 + 2 * p] for p in range(NA_WIN_R // 2)], axis=1)
        k_win = kr_ref[pl.ds(start, NA_WIN_R * GRID_W), :]
        s_rows.append(_dot_nt(qr[lr * GRID_W:(lr + 1) * GRID_W], k_win) + bias)
        starts.append(start)
    s_loc = jnp.concatenate(s_rows, axis=0)
    s_ctx = _dot_nt(qr, kc)
    m = jnp.maximum(jnp.max(s_loc, axis=-1, keepdims=True), jnp.max(s_ctx, axis=-1, keepdims=True))
    p_loc = jnp.exp(s_loc - m)
    p_ctx = jnp.exp(s_ctx - m)
    denom = jnp.sum(p_loc, axis=-1, keepdims=True) + jnp.sum(p_ctx, axis=-1, keepdims=True)
    p_loc = p_loc.astype(BF16)
    o_loc = jnp.concatenate(
        [_dot(p_loc[lr * GRID_W:(lr + 1) * GRID_W], vb_ref[pl.ds(starts[lr], NA_WIN_R * GRID_W), :])
         for lr in range(rows_per_step)], axis=0)
    o = o_loc + _dot(p_ctx.astype(BF16), vc)
    o_ref[...] = (o / denom).astype(o_ref.dtype)


def _na_attention(qkv, bias_tbl):
    rows_per_step = 16
    tq = rows_per_step * GRID_W
    steps = SEQ // tq
    cos, sin = _rope_tables()
    hd = NA_HEAD_DIM
    ctx_blk0 = T_LAT // CTX_LEN
    return pl.pallas_call(
        functools.partial(_na_body, rows_per_step),
        grid=(BATCH, NA_HEADS, steps),
        in_specs=[
            pl.BlockSpec((tq, hd), lambda b, h, r: (b * steps + r, h)),
            pl.BlockSpec((SEQ, hd), lambda b, h, r: (b, NA_HEADS + h)),
            pl.BlockSpec((SEQ, hd), lambda b, h, r: (b, 2 * NA_HEADS + h)),
            pl.BlockSpec((CTX_LEN, hd), lambda b, h, r: (ctx_blk0 + b, NA_HEADS + h)),
            pl.BlockSpec((CTX_LEN, hd), lambda b, h, r: (ctx_blk0 + b, 2 * NA_HEADS + h)),
            pl.BlockSpec((tq, hd), lambda b, h, r: (r, 0)),
            pl.BlockSpec((tq, hd), lambda b, h, r: (r, 0)),
            pl.BlockSpec((SEQ, hd), lambda b, h, r: (0, 0)),
            pl.BlockSpec((SEQ, hd), lambda b, h, r: (0, 0)),
            pl.BlockSpec((None,) + bias_tbl.shape[1:], lambda b, h, r: (h, 0, 0, 0)),
        ],
        out_specs=pl.BlockSpec((tq, hd), lambda b, h, r: (b * steps + r, h)),
        out_shape=jax.ShapeDtypeStruct((T_LAT, D_MODEL), BF16),
        scratch_shapes=[pltpu.VMEM((SEQ, hd), BF16), pltpu.VMEM((SEQ, hd), BF16)],
        compiler_params=_cparams(("arbitrary", "arbitrary", "arbitrary"), 48),
        name="na_attention",
    )(qkv, qkv, qkv, qkv, qkv, cos, sin, cos, sin, bias_tbl)


def _ctx_attn_body(q_ref, k_ref, v_ref, o_ref):
    scale = NA_HEAD_DIM ** -0.5
    s = _dot_nt((q_ref[...] * scale).astype(BF16), k_ref[...].astype(BF16))
    p = jnp.exp(s - jnp.max(s, axis=-1, keepdims=True))
    o = _dot(p.astype(BF16), v_ref[...].astype(BF16))
    o_ref[...] = (o / jnp.sum(p, axis=-1, keepdims=True)).astype(o_ref.dtype)


def _ctx_attention(qkv):
    hd = NA_HEAD_DIM
    ctx_blk0 = T_LAT // CTX_LEN
    return pl.pallas_call(
        _ctx_attn_body,
        grid=(BATCH, NA_HEADS),
        in_specs=[
            pl.BlockSpec((CTX_LEN, hd), lambda b, h: (ctx_blk0 + b, h)),
            pl.BlockSpec((CTX_LEN, hd), lambda b, h: (ctx_blk0 + b, NA_HEADS + h)),
            pl.BlockSpec((CTX_LEN, hd), lambda b, h: (ctx_blk0 + b, 2 * NA_HEADS + h)),
        ],
        out_specs=pl.BlockSpec((CTX_LEN, hd), lambda b, h: (b, h)),
        out_shape=jax.ShapeDtypeStruct((T_CTX, D_MODEL), BF16),
        compiler_params=_cparams(("arbitrary", "arbitrary"), 32),
        name="ctx_attention",
    )(qkv, qkv, qkv)


def kernel(x, c, ctx, c_ctx, w_mod, b_mod, norm_g, ffn_w_up, ffn_conv_w, ffn_conv_b, ffn_w_down, ssm_w_in, ssm_conv_w, ssm_conv_b, ssm_dt_bias, ssm_a_log, ssm_d, ssm_norm_g, ssm_w_out, hg_w_in, hg_lb, hg_norm_g, hg_w_out, na_w_qkv, na_rpb, na_w_out):
    xs = jnp.concatenate([x.reshape(T_LAT, D_MODEL), ctx.reshape(T_CTX, D_MODEL)], axis=0)
    cvec8 = jnp.concatenate([c, c_ctx[None], jnp.zeros((SUBLANES - BATCH - 1, D_MODEL), F32)], axis=0)
    mod = _adaln(cvec8, w_mod, b_mod)
    ffn_up_bf, ffn_down_bf = ffn_w_up.astype(BF16), ffn_w_down.astype(BF16)
    ssm_out_bf, hg_out_bf, na_out_bf = ssm_w_out.astype(BF16), hg_w_out.astype(BF16), na_w_out.astype(BF16)

    h = _prep(xs, norm_g[0, 0], mod[0])
    for i in range(DEPTH):
        last = i == DEPTH - 1
        kind, slot = i % N_MIXERS, i // N_MIXERS
        mod_l = mod[i]
        n_rows = T_LAT if last else T_ALL
        op = dict(xs=xs, g_res=norm_g[i, 1], g_ffn=norm_g[i, 2], mod_l=mod_l, n_rows=n_rows, slot=slot)
        if kind == 0:
            proj = _in_proj(h, ssm_w_in, slot, 1152, "ssd_in")
            xbc = _ssd_conv(proj, ssm_conv_w[slot], ssm_conv_b[slot])
            y2 = _ssd_scan(xbc, proj, ssm_dt_bias[slot], ssm_a_log[slot], ssm_d[slot])
            xs, hf = _out_proj(_ssd_gate_prologue, [((y2, 0), 0), ((y2, 1), 0), (proj, 0)],
                               ssm_norm_g[slot].reshape(1, SSM_INNER), ssm_out_bf, tk=1024, row_norm=True,
                               name="ssd_out", **op)
        elif kind == 1:
            proj = _in_proj(h, hg_w_in, slot, 1024, "hg_in")
            o2 = _gla_scan(proj, hg_lb, i)
            tk = 1024
            xs, hf = _out_proj(_hg_readout_prologue, [((o2, 0), 0), ((o2, 1), 0), (proj, 4 * D_MODEL // tk)],
                               hg_norm_g[slot].reshape(1, D_MODEL), hg_out_bf, tk=tk, row_norm=False,
                               name="hg_out", **op)
        else:
            qkv = _in_proj(h, na_w_qkv, slot, 1024, "na_in")
            o = jnp.concatenate([_na_attention(qkv, _na_bias_table(na_rpb[slot])), _ctx_attention(qkv)], axis=0)
            xs, hf = _out_proj(_plain_prologue, [(o, 0)], None, na_out_bf, tk=1024, row_norm=False,
                               name="na_out", **op)
        nxt = min(i + 1, DEPTH - 1)
        xs, h = _ffn(hf, xs, i, ffn_up_bf, ffn_conv_w, ffn_conv_b, ffn_down_bf, norm_g[i, 3], mod_l,
                     norm_g[nxt, 0], mod[nxt], n_rows=n_rows, emit_h=not last)

    return xs.reshape(BATCH, SEQ, D_MODEL)
```

```python
import functools

import numpy as np
import jax
import jax.numpy as jnp
from jax import lax
from jax.experimental import pallas as pl
from jax.experimental.pallas import tpu as pltpu

F32 = jnp.float32
BF16 = jnp.bfloat16

D_MODEL = 2048
BATCH = 2
SEQ = 4096
DEPTH = 4
GRID_W = 64
CTX_LEN = 256
N_MIXERS = 3
RMS_EPS = 1e-6
SSM_INNER = 4096
SSM_HEAD_DIM = 64
SSM_HEADS = 64
SSM_STATE = 128
SSM_GROUPS = 8
SSM_CONV_DIM = SSM_INNER + 2 * SSM_GROUPS * SSM_STATE
SSM_IN_DIM = SSM_INNER + SSM_CONV_DIM + 2 * SSM_HEADS
SSM_CHUNK = 128
HG_HEADS = 16
HG_KEY_DIM = 128
HG_IN_DIM = 5 * D_MODEL
HG_CHUNK = 64
NA_HEADS = 16
NA_HEAD_DIM = 128
NA_WIN_R = 8
NA_WIN_C = 16
ROPE_BASE = 10000.0
FFN_HIDDEN = 5632

T_LAT = BATCH * SEQ
T_CTX = BATCH * CTX_LEN
T_ALL = T_LAT + T_CTX
LANES = 128
SUBLANES = 8
BF16_ROWS = 16
ONES_ROWS = 16
ROW_TILE = 512
FFN_COL_CHUNK = 256
MIB = 1024 * 1024
LOG2E = 1.4426950408889634


def _cparams(sem, vmem_mib):
    return pltpu.CompilerParams(dimension_semantics=sem, vmem_limit_bytes=vmem_mib * MIB)


def _silu(v):
    return v * jax.nn.sigmoid(v)


def _rms(v):
    return v * lax.rsqrt(jnp.mean(v * v, axis=-1, keepdims=True) + RMS_EPS)


def _log1p_unit(e):
    return jnp.log(1.0 + e)


def _softplus(v):
    return jnp.maximum(v, 0.0) + _log1p_unit(jnp.exp(-jnp.abs(v)))


def _dot(a, b):
    return jnp.dot(a, b, preferred_element_type=F32)


def _dot_nt(a, b):
    return lax.dot_general(a, b, (((1,), (1,)), ((), ())), preferred_element_type=F32)


def _dot_tn(a, b):
    return lax.dot_general(a, b, (((0,), (0,)), ((), ())), preferred_element_type=F32)


def _bf16_pieces(v, terms):
    pieces = []
    rem = v
    for _ in range(terms):
        part = rem.astype(BF16)
        rem = rem - part.astype(F32)
        pieces.append(part)
    return pieces


def _dot_sel(m01_rep, v, terms):
    return _dot(m01_rep, jnp.concatenate(_bf16_pieces(v, terms), axis=0))


def _dot_sel_rhs(v, m01_rep, terms):
    return _dot(jnp.concatenate(_bf16_pieces(v, terms), axis=1), m01_rep)


def _mod_row(row0):
    return jnp.where(row0 >= T_LAT, 2, row0 // SEQ)


def _mod_vec(mod_ref, r, idx):
    return mod_ref[pl.ds(r, 1), idx * D_MODEL:(idx + 1) * D_MODEL]


def _norm_mod(x, g, mod_ref, r, shift_idx, scale_idx):
    return _rms(x) * g * (1.0 + _mod_vec(mod_ref, r, scale_idx)) + _mod_vec(mod_ref, r, shift_idx)


def _seg_edges(row0, rows):
    gr = row0 + lax.broadcasted_iota(jnp.int32, (rows, 1), 0)
    pos = jnp.where(gr >= T_LAT, (gr - T_LAT) % CTX_LEN, gr % SEQ)
    length = jnp.where(gr >= T_LAT, CTX_LEN, SEQ)
    return pos == 0, pos == length - 1


def _shift_rows(cur, prev_row, next_row, row0):
    rows = cur.shape[0]
    first, last = _seg_edges(row0, rows)
    ridx = lax.broadcasted_iota(jnp.int32, (rows, 1), 0)
    up = jnp.where(ridx == 0, prev_row, pltpu.roll(cur, 1, axis=0))
    up = jnp.where(first, 0.0, up)
    dn = jnp.where(ridx == rows - 1, next_row, pltpu.roll(cur, rows - 1, axis=0))
    dn = jnp.where(last, 0.0, dn)
    return up, dn


def _token_conv(cur, prev_row, next_row, row0, w_ref, b_ref):
    up, dn = _shift_rows(cur, prev_row, next_row, row0)
    return b_ref[...] + up * w_ref[0:1, :] + cur * w_ref[1:2, :] + dn * w_ref[2:3, :]


def _halo_specs(tm, halo, width, n_rows, col_fn):
    per = tm // halo
    last_blk = n_rows // halo - 1
    prev = pl.BlockSpec((halo, width), lambda i, j: (jnp.maximum(i * per - 1, 0), col_fn(j)))
    nxt = pl.BlockSpec((halo, width), lambda i, j: (jnp.minimum((i + 1) * per, last_blk), col_fn(j)))
    return prev, nxt


def _adaln_body(c_ref, w_ref, b_ref, o_ref):
    s = _silu(c_ref[...]).astype(BF16)
    o_ref[...] = _dot(s, w_ref[...].astype(BF16)) + b_ref[...]


def _adaln(cvec8, w_mod, b_mod):
    tn = 1024
    n_out = 6 * D_MODEL
    return pl.pallas_call(
        _adaln_body,
        grid=(DEPTH, n_out // tn),
        in_specs=[
            pl.BlockSpec((SUBLANES, D_MODEL), lambda l, j: (0, 0)),
            pl.BlockSpec((None, D_MODEL, tn), lambda l, j: (l, 0, j)),
            pl.BlockSpec((None, 1, tn), lambda l, j: (l, 0, j)),
        ],
        out_specs=pl.BlockSpec((None, SUBLANES, tn), lambda l, j: (l, 0, j)),
        out_shape=jax.ShapeDtypeStruct((DEPTH, SUBLANES, n_out), F32),
        compiler_params=_cparams(("arbitrary", "arbitrary"), 40),
        name="adaln",
    )(cvec8, w_mod, b_mod.reshape(DEPTH, 1, n_out))


def _prep_body(x_ref, g_ref, mod_ref, o_ref):
    r = _mod_row(pl.program_id(0) * ROW_TILE)
    o_ref[...] = _norm_mod(x_ref[...], g_ref[...], mod_ref, r, 0, 1).astype(BF16)


def _prep(xs, g_row, mod_l):
    row = pl.BlockSpec((ROW_TILE, D_MODEL), lambda i: (i, 0))
    return pl.pallas_call(
        _prep_body,
        grid=(T_ALL // ROW_TILE,),
        in_specs=[row, pl.BlockSpec((1, D_MODEL), lambda i: (0, 0)), pl.BlockSpec(mod_l.shape, lambda i: (0, 0))],
        out_specs=row,
        out_shape=jax.ShapeDtypeStruct((T_ALL, D_MODEL), BF16),
        compiler_params=_cparams(("arbitrary",), 32),
        name="prep",
    )(xs, g_row.reshape(1, D_MODEL), mod_l)


def _in_proj_body(h_ref, w_ref, o_ref, wb_ref):
    @pl.when(pl.program_id(1) == 0)
    def _():
        wb_ref[...] = w_ref[...].astype(BF16)

    o_ref[...] = _dot(h_ref[...], wb_ref[...])


def _in_proj(h, w_stack, slot, tn, name):
    tm = T_ALL // 8
    n_dim = w_stack.shape[2]
    return pl.pallas_call(
        _in_proj_body,
        grid=(n_dim // tn, T_ALL // tm),
        in_specs=[pl.BlockSpec((tm, D_MODEL), lambda j, i: (i, 0)),
                  pl.BlockSpec((None, D_MODEL, tn), lambda j, i: (slot, 0, j))],
        out_specs=pl.BlockSpec((tm, tn), lambda j, i: (i, j)),
        out_shape=jax.ShapeDtypeStruct((T_ALL, n_dim), F32),
        scratch_shapes=[pltpu.VMEM((D_MODEL, tn), BF16)],
        compiler_params=_cparams(("arbitrary", "arbitrary"), 56),
        name=name,
    )(h, w_stack)


def _ssd_gate_prologue(yf_ref, yb_ref, z_ref, g_ref):
    v = (yf_ref[...] + yb_ref[...]) * _silu(z_ref[...])
    return v * g_ref[...], jnp.sum(v * v, axis=-1, keepdims=True)


def _hg_readout_prologue(of_ref, ob_ref, gate_ref, g_ref):
    o = of_ref[...] + ob_ref[...]
    parts = [_rms(o[:, h * LANES:(h + 1) * LANES]) for h in range(o.shape[1] // LANES)]
    return jnp.concatenate(parts, axis=-1) * g_ref[...] * _silu(gate_ref[...]), None


def _plain_prologue(o_ref):
    return o_ref[...], None


def _out_proj_body(prologue, n_in, k_dim, row_norm, *refs):
    ins = refs[:n_in]
    w_ref, x_ref, g1_ref, g2_ref, mod_ref, xo_ref, ho_ref, acc_ref, ssq_ref = refs[n_in:]
    k = pl.program_id(1)
    part, sq = prologue(*ins)
    contrib = _dot(part.astype(BF16), w_ref[...])

    @pl.when(k == 0)
    def _():
        acc_ref[...] = contrib
        if row_norm:
            ssq_ref[...] = sq

    @pl.when(k != 0)
    def _():
        acc_ref[...] += contrib
        if row_norm:
            ssq_ref[...] += sq

    @pl.when(k == pl.num_programs(1) - 1)
    def _():
        y = acc_ref[...]
        if row_norm:
            y = y * lax.rsqrt(ssq_ref[...] / k_dim + RMS_EPS)
        r = _mod_row(pl.program_id(0) * ROW_TILE)
        xn = x_ref[...] + _mod_vec(mod_ref, r, 2) * (_rms(y) * g1_ref[...])
        xo_ref[...] = xn
        ho_ref[...] = _norm_mod(xn, g2_ref[...], mod_ref, r, 3, 4).astype(BF16)


def _out_proj(prologue, row_ins, gain, w_stack, slot, xs, g_res, g_ffn, mod_l, *, n_rows, tk, row_norm, name):
    k_dim = w_stack.shape[1]
    tm = ROW_TILE
    in_specs, args = [], []
    for arr, cb0 in row_ins:
        if isinstance(arr, tuple):
            in_specs.append(pl.BlockSpec((None, tm, tk), functools.partial(
                lambda i, k, ld, cb0: (ld, i, cb0 + k), ld=arr[1], cb0=cb0)))
            args.append(arr[0])
        else:
            in_specs.append(pl.BlockSpec((tm, tk), functools.partial(lambda i, k, cb0: (i, cb0 + k), cb0=cb0)))
            args.append(arr)
    if gain is not None:
        in_specs.append(pl.BlockSpec((1, tk), lambda i, k: (0, k)))
        args.append(gain)
    n_in = len(args)
    row = pl.BlockSpec((tm, D_MODEL), lambda i, k: (i, 0))
    vec = pl.BlockSpec((1, D_MODEL), lambda i, k: (0, 0))
    in_specs += [pl.BlockSpec((None, tk, D_MODEL), lambda i, k: (slot, k, 0)), row, vec, vec,
                 pl.BlockSpec(mod_l.shape, lambda i, k: (0, 0))]
    return pl.pallas_call(
        functools.partial(_out_proj_body, prologue, n_in, k_dim, row_norm),
        grid=(n_rows // tm, k_dim // tk),
        in_specs=in_specs,
        out_specs=[row, row],
        out_shape=[jax.ShapeDtypeStruct((n_rows, D_MODEL), F32), jax.ShapeDtypeStruct((n_rows, D_MODEL), BF16)],
        scratch_shapes=[pltpu.VMEM((tm, D_MODEL), F32), pltpu.VMEM((tm, 1), F32)],
        compiler_params=_cparams(("arbitrary", "arbitrary"), 56),
        name=name,
    )(*args, w_stack, xs, g_res.reshape(1, D_MODEL), g_ffn.reshape(1, D_MODEL), mod_l)


def _ffn_body(emit_h, h_ref, hp_ref, hn_ref, wa_ref, wv_ref, cwa_ref, cwv_ref, cba_ref, cbv_ref, wd_ref, x_ref,
              g_res_ref, mod_ref, g_next_ref, modn_ref, *out_and_scratch):
    if emit_h:
        xo_ref, ho_ref, acc_ref = out_and_scratch
    else:
        xo_ref, acc_ref = out_and_scratch
    k = pl.program_id(1)
    row0 = pl.program_id(0) * ROW_TILE
    h = h_ref[...]
    halo = jnp.concatenate([hp_ref[...], hn_ref[...]], axis=0)

    def branch(w_ref, cw_ref, cb_ref, cs):
        w = w_ref[:, cs]
        u = _dot(h, w)
        uh = _dot(halo, w)
        return _token_conv(u, uh[BF16_ROWS - 1:BF16_ROWS], uh[BF16_ROWS:BF16_ROWS + 1], row0,
                           cw_ref.at[:, cs], cb_ref.at[:, cs])

    contrib = None
    for c in range(wa_ref.shape[1] // FFN_COL_CHUNK):
        cs = pl.ds(c * FFN_COL_CHUNK, FFN_COL_CHUNK)
        act = _silu(branch(wa_ref, cwa_ref, cba_ref, cs)) * branch(wv_ref, cwv_ref, cbv_ref, cs)
        part = _dot(act.astype(BF16), wd_ref[cs, :])
        contrib = part if contrib is None else contrib + part

    @pl.when(k == 0)
    def _():
        acc_ref[...] = contrib

    @pl.when(k != 0)
    def _():
        acc_ref[...] += contrib

    @pl.when(k == pl.num_programs(1) - 1)
    def _():
        r = _mod_row(row0)
        xn = x_ref[...] + _mod_vec(mod_ref, r, 5) * (_rms(acc_ref[...]) * g_res_ref[...])
        xo_ref[...] = xn
        if emit_h:
            ho_ref[...] = _norm_mod(xn, g_next_ref[...], modn_ref, r, 0, 1).astype(BF16)


def _ffn(h, xs, layer, w_up, conv_w, conv_b, w_down, g_res, mod_l, g_next, mod_next, *, n_rows, emit_h):
    tm, tk = ROW_TILE, 512
    kt = FFN_HIDDEN // tk
    h_prev, h_next = _halo_specs(tm, BF16_ROWS, D_MODEL, n_rows, lambda k: 0)
    row = pl.BlockSpec((tm, D_MODEL), lambda i, k: (i, 0))
    vec = pl.BlockSpec((1, D_MODEL), lambda i, k: (0, 0))
    tab = pl.BlockSpec(mod_l.shape, lambda i, k: (0, 0))
    out_specs = [row, row] if emit_h else [row]
    out_shape = [jax.ShapeDtypeStruct((n_rows, D_MODEL), F32)]
    if emit_h:
        out_shape.append(jax.ShapeDtypeStruct((n_rows, D_MODEL), BF16))
    conv_b3 = conv_b.reshape(DEPTH, 1, 2 * FFN_HIDDEN)

    res = pl.pallas_call(
        functools.partial(_ffn_body, emit_h),
        grid=(n_rows // tm, kt),
        in_specs=[
            row, h_prev, h_next,
            pl.BlockSpec((None, D_MODEL, tk), lambda i, k: (layer, 0, k)),
            pl.BlockSpec((None, D_MODEL, tk), lambda i, k: (layer, 0, k + kt)),
            pl.BlockSpec((None, 3, tk), lambda i, k: (layer, 0, k)),
            pl.BlockSpec((None, 3, tk), lambda i, k: (layer, 0, k + kt)),
            pl.BlockSpec((None, 1, tk), lambda i, k: (layer, 0, k)),
            pl.BlockSpec((None, 1, tk), lambda i, k: (layer, 0, k + kt)),
            pl.BlockSpec((None, tk, D_MODEL), lambda i, k: (layer, k, 0)),
            row, vec, tab, vec, tab,
        ],
        out_specs=out_specs,
        out_shape=out_shape,
        scratch_shapes=[pltpu.VMEM((tm, D_MODEL), F32)],
        compiler_params=_cparams(("arbitrary", "arbitrary"), 56),
        name="conv_ffn",
    )(h, h, h, w_up, w_up, conv_w, conv_w, conv_b3, conv_b3, w_down, xs, g_res.reshape(1, D_MODEL), mod_l,
      g_next.reshape(1, D_MODEL), mod_next)
    return (res[0], res[1]) if emit_h else (res[0], None)


def _ssd_conv_body(tm, cur_ref, prev_ref, next_ref, w_ref, b_ref, o_ref):
    u = _token_conv(cur_ref[...], prev_ref[SUBLANES - 1:SUBLANES, :], next_ref[0:1, :], pl.program_id(0) * tm,
                    w_ref, b_ref)
    o_ref[...] = _silu(u)


def _ssd_conv(proj, conv_w, conv_b):
    tm, tc = 512, 2048
    col0 = SSM_INNER // tc
    prev, nxt = _halo_specs(tm, SUBLANES, tc, T_ALL, lambda j: j + col0)
    return pl.pallas_call(
        functools.partial(_ssd_conv_body, tm),
        grid=(T_ALL // tm, SSM_CONV_DIM // tc),
        in_specs=[pl.BlockSpec((tm, tc), lambda i, j: (i, j + col0)), prev, nxt,
                  pl.BlockSpec((3, tc), lambda i, j: (0, j)), pl.BlockSpec((1, tc), lambda i, j: (0, j))],
        out_specs=pl.BlockSpec((tm, tc), lambda i, j: (i, j)),
        out_shape=jax.ShapeDtypeStruct((T_ALL, SSM_CONV_DIM), F32),
        compiler_params=_cparams(("arbitrary", "arbitrary"), 40),
        name="ssd_conv",
    )(proj, proj, proj, conv_w, conv_b.reshape(1, -1))


def _scan_chunk_index(d, b, s, n_lat, n_ctx):
    ctx0 = BATCH * n_lat + b * n_ctx
    fwd = jnp.where(s < n_ctx, ctx0 + s, b * n_lat + s - n_ctx)
    bwd = jnp.where(s < n_ctx, ctx0 + n_ctx - 1 - s, b * n_lat + n_lat - 1 - (s - n_ctx))
    return jnp.where(d == 0, fwd, bwd)


SSD_CUM_TERMS = 3
SSD_DT_TERMS = 2


def _ssd_scan_consts():
    q = SSM_CHUNK
    i = np.arange(q)
    pfx = np.zeros((2, q + ONES_ROWS, q), np.float32)
    pfx[0, :q] = i[:, None] >= i[None, :]
    pfx[1, :q] = i[:, None] <= i[None, :]
    pfx[:, q:] = 1.0
    pfx = np.tile(pfx, (1, 1, SSD_CUM_TERMS))
    expand = np.repeat(np.eye(SSM_HEADS, dtype=np.float32), SSM_HEAD_DIM, axis=1)
    expand = np.tile(expand, (SSD_DT_TERMS, 1))
    return jnp.asarray(pfx, BF16), jnp.asarray(expand, BF16)


def _ssd_scan_body(x_ref, b_ref, c_ref, dtraw_ref, dtb_ref, alog_ref, dskip_ref, pfx_ref, exp_ref, y_ref, s_ref):
    q = SSM_CHUNK
    hp = 2 * SSM_HEAD_DIM
    d = pl.program_id(0)

    @pl.when(pl.program_id(2) == 0)
    def _():
        s_ref[...] = jnp.zeros_like(s_ref)

    def pick(v):
        return jnp.where(d == 0, v[:, :SSM_HEADS], v[:, SSM_HEADS:])

    dt2 = _softplus(dtraw_ref[...] + dtb_ref[...])
    cum2 = _dot_sel(pfx_ref[...], dt2 * (-LOG2E * jnp.exp(alog_ref[...])), SSD_CUM_TERMS)
    acum2 = cum2[:q]
    acum = pick(acum2)
    tot = pick(cum2[q:q + SUBLANES])[0:1]
    acum_t2 = acum2.T
    acum_t = jnp.where(d == 0, acum_t2[:SSM_HEADS], acum_t2[SSM_HEADS:])
    dt_wide = _dot_sel_rhs(pick(dt2), exp_ref[...], SSD_DT_TERMS)

    ii = lax.broadcasted_iota(jnp.int32, (q, q), 0)
    jj = lax.broadcasted_iota(jnp.int32, (q, q), 1)
    causal = (jj - ii) * (1 - 2 * d) <= 0
    low_half = lax.broadcasted_iota(jnp.int32, (q, hp), 1) < SSM_HEAD_DIM

    heads_per_group = SSM_HEADS // SSM_GROUPS
    n_pairs = SSM_HEADS // 2

    b_gs = [b_ref[:, g * SSM_STATE:(g + 1) * SSM_STATE].astype(BF16) for g in range(SSM_GROUPS)]
    c_gs = [c_ref[:, g * SSM_STATE:(g + 1) * SSM_STATE] for g in range(SSM_GROUPS)]
    cbs = [jnp.where(causal, _dot_nt(c_gs[g].astype(BF16), b_gs[g]), 0.0) for g in range(SSM_GROUPS)]

    cols, lhs = [], []
    for h in range(SSM_HEADS):
        g = h // heads_per_group
        col = jnp.broadcast_to(acum[:, h:h + 1], (q, q))
        seg = jnp.where(causal, col - acum_t[h:h + 1, :], 0.0)
        scores = jnp.exp2(seg) * cbs[g]
        c_dec = c_gs[g] * jnp.exp2(col)
        lhs.append(jnp.concatenate([scores.astype(BF16), c_dec.astype(BF16)], axis=1))
        cols.append(col)

    xdts = []
    for p in range(n_pairs):
        lo = p * hp
        x_pair = x_ref[:, lo:lo + hp]
        xdt = x_pair * dt_wide[:, lo:lo + hp]
        rhs = jnp.concatenate([xdt.astype(BF16), s_ref[:, lo:lo + hp].astype(BF16)], axis=0)
        y_pair = jnp.where(low_half, _dot(lhs[2 * p], rhs), _dot(lhs[2 * p + 1], rhs))
        y_ref[:, lo:lo + hp] = y_pair + dskip_ref[:, lo:lo + hp] * x_pair
        xdts.append(xdt)

    for p in range(n_pairs):
        lo = p * hp
        g = (2 * p) // heads_per_group
        tot_pair = jnp.where(low_half[0:1], jnp.broadcast_to(tot[:, 2 * p:2 * p + 1], (1, hp)),
                             jnp.broadcast_to(tot[:, 2 * p + 1:2 * p + 2], (1, hp)))
        to_end = jnp.exp2(tot_pair - jnp.where(low_half, cols[2 * p], cols[2 * p + 1]))
        s_ref[:, lo:lo + hp] = (s_ref[:, lo:lo + hp] * jnp.exp2(tot_pair)
                                + _dot_tn(b_gs[g], (xdts[p] * to_end).astype(BF16)))


def _ssd_scan(xbc, proj, dt_bias, a_log, d_skip):
    q = SSM_CHUNK
    n_lat, n_ctx = SEQ // q, CTX_LEN // q
    pfx, expand = _ssd_scan_consts()
    n_heads2 = 2 * SSM_HEADS
    dt_col = (SSM_INNER + SSM_CONV_DIM) // n_heads2
    b_col = SSM_INNER // (SSM_GROUPS * SSM_STATE)
    d_wide = jnp.repeat(d_skip, SSM_HEAD_DIM, axis=1).reshape(2, 1, SSM_INNER)

    def chunk(d, b, s):
        return _scan_chunk_index(d, b, s, n_lat, n_ctx)

    return pl.pallas_call(
        _ssd_scan_body,
        grid=(2, BATCH, n_lat + n_ctx),
        in_specs=[
            pl.BlockSpec((q, SSM_INNER), lambda d, b, s: (chunk(d, b, s), 0)),
            pl.BlockSpec((q, SSM_GROUPS * SSM_STATE), lambda d, b, s: (chunk(d, b, s), b_col)),
            pl.BlockSpec((q, SSM_GROUPS * SSM_STATE), lambda d, b, s: (chunk(d, b, s), b_col + 1)),
            pl.BlockSpec((q, n_heads2), lambda d, b, s: (chunk(d, b, s), dt_col)),
            pl.BlockSpec((1, n_heads2), lambda d, b, s: (0, 0)),
            pl.BlockSpec((1, n_heads2), lambda d, b, s: (0, 0)),
            pl.BlockSpec((None, 1, SSM_INNER), lambda d, b, s: (d, 0, 0)),
            pl.BlockSpec((None,) + pfx.shape[1:], lambda d, b, s: (d, 0, 0)),
            pl.BlockSpec(expand.shape, lambda d, b, s: (0, 0)),
        ],
        out_specs=pl.BlockSpec((None, q, SSM_INNER), lambda d, b, s: (d, chunk(d, b, s), 0)),
        out_shape=jax.ShapeDtypeStruct((2, T_ALL, SSM_INNER), F32),
        scratch_shapes=[pltpu.VMEM((SSM_STATE, SSM_INNER), F32)],
        compiler_params=_cparams(("arbitrary", "arbitrary", "arbitrary"), 48),
        name="ssd_scan",
    )(xbc, xbc, xbc, proj, dt_bias.reshape(1, n_heads2), a_log.reshape(1, n_heads2), d_wide, pfx, expand)


GLA_SUM_TERMS = 2


def _gla_consts():
    q = HG_CHUNK
    levels = int(np.log2(q))
    i = np.arange(q)
    mats = [(i[:, None] >= i[None, :]).astype(np.float32), (i[None, :] > i[:, None]).astype(np.float32)]
    masks = [np.eye(q, dtype=np.float32)]
    for lv in range(levels):
        s = 1 << lv
        blk, pos = i // (2 * s), i % (2 * s)
        bound = blk * 2 * s + s - 1
        right = pos >= s
        t = i[None, :]
        m_right = right[:, None] & (t > bound[:, None]) & (t <= i[:, None])
        m_left = (~right)[:, None] & (t > i[:, None]) & (t <= bound[:, None])
        mats.append((m_right | m_left).astype(np.float32))
        masks.append((right[:, None] & (~right)[None, :] & (blk[:, None] == blk[None, :])).astype(np.float32))
    mats.append(np.ones((ONES_ROWS, q), np.float32))
    fwd_m, fwd_k = np.concatenate(mats, axis=0), np.stack(masks)
    flip_rows = np.concatenate([np.arange(k * q, (k + 1) * q)[::-1] for k in range(levels + 2)]
                               + [np.arange((levels + 2) * q, (levels + 2) * q + ONES_ROWS)])
    bwd_m = fwd_m[flip_rows][:, ::-1]
    bwd_k = fwd_k[:, ::-1, ::-1]
    mats2 = np.tile(np.stack([fwd_m, bwd_m]), (1, 1, GLA_SUM_TERMS))
    return (jnp.asarray(mats2, BF16), jnp.asarray(np.stack([fwd_k, bwd_k]), F32), levels)


def _gla_body(levels, layer, q_ref, v_ref, f_ref, lbp_ref, mats_ref, masks_ref, o_ref, st_ref):
    q = HG_CHUNK

    @pl.when(pl.program_id(2) == 0)
    def _():
        st_ref[...] = jnp.zeros_like(st_ref)

    lbp = lbp_ref[...]
    e = jnp.exp(lbp - jnp.max(lbp, axis=0, keepdims=True))
    sm = e / jnp.sum(e, axis=0, keepdims=True)
    lb = jnp.sum(sm[1:layer + 1], axis=0, keepdims=True) if layer > 0 else jnp.zeros((1, lbp.shape[1]), F32)

    f = f_ref[...]
    ef = jnp.exp(-jnp.abs(f))
    log_sig = jnp.minimum(f, 0.0) - _log1p_unit(ef)
    la = jnp.log(lb)
    lc = jnp.log1p(-lb) + log_sig
    logf = jnp.maximum(la, lc) + _log1p_unit(jnp.exp(-jnp.abs(la - lc)))
    key = (1.0 - lb) * (jnp.where(f >= 0.0, ef, 1.0) / (1.0 + ef))
    qs = _silu(q_ref[...])

    sums = _dot_sel(mats_ref[...], logf, GLA_SUM_TERMS)
    q_in = (qs * jnp.exp(sums[:q])).astype(BF16)
    k_out = (key * jnp.exp(sums[q:2 * q])).astype(BF16)
    dec = jnp.exp(sums[(levels + 2) * q:(levels + 2) * q + 1])
    qs_bf, key_bf = qs.astype(BF16), key.astype(BF16)
    q_lv, k_lv = [], []
    for lv in range(levels):
        fac = jnp.exp(sums[(lv + 2) * q:(lv + 3) * q])
        q_lv.append((qs * fac).astype(BF16))
        k_lv.append((key * fac).astype(BF16))
    v_bf = v_ref[...].astype(BF16)

    heads = [slice(h * LANES, (h + 1) * LANES) for h in range(HG_HEADS)]
    attn = masks_ref[0][None] * jnp.stack([_dot_nt(qs_bf[:, sl], key_bf[:, sl]) for sl in heads], axis=0)
    for lv in range(levels):
        attn = attn + masks_ref[lv + 1][None] * jnp.stack(
            [_dot_nt(q_lv[lv][:, sl], k_lv[lv][:, sl]) for sl in heads], axis=0)
    attn = attn.astype(BF16)
    st_bf = st_ref[...].astype(BF16)
    o_ref[...] = jnp.concatenate(
        [_dot(attn[h], v_bf[:, sl]) + _dot_nt(q_in[:, sl], st_bf[h]) for h, sl in enumerate(heads)], axis=1)
    for h, sl in enumerate(heads):
        st_ref[h] = st_ref[h] * dec[:, sl] + _dot_tn(v_bf[:, sl], k_out[:, sl])


def _gla_scan(proj, hg_lb, layer):
    q = HG_CHUNK
    n_lat, n_ctx = SEQ // q, CTX_LEN // q
    mats, masks, levels = _gla_consts()

    def chunk(d, b, s):
        return _scan_chunk_index(d, b, s, n_lat, n_ctx)

    return pl.pallas_call(
        functools.partial(_gla_body, levels, layer),
        grid=(2, BATCH, n_lat + n_ctx),
        in_specs=[
            pl.BlockSpec((q, D_MODEL), lambda d, b, s: (chunk(d, b, s), 0)),
            pl.BlockSpec((q, D_MODEL), lambda d, b, s: (chunk(d, b, s), 1)),
            pl.BlockSpec((q, D_MODEL), lambda d, b, s: (chunk(d, b, s), 2 + d)),
            pl.BlockSpec((None, DEPTH, D_MODEL), lambda d, b, s: (d, 0, 0)),
            pl.BlockSpec((None,) + mats.shape[1:], lambda d, b, s: (d, 0, 0)),
            pl.BlockSpec((None,) + masks.shape[1:], lambda d, b, s: (d, 0, 0, 0)),
        ],
        out_specs=pl.BlockSpec((None, q, D_MODEL), lambda d, b, s: (d, chunk(d, b, s), 0)),
        out_shape=jax.ShapeDtypeStruct((2, T_ALL, D_MODEL), F32),
        scratch_shapes=[pltpu.VMEM((HG_HEADS, LANES, HG_KEY_DIM), F32)],
        compiler_params=_cparams(("arbitrary", "arbitrary", "arbitrary"), 48),
        name="gla_scan",
    )(proj, proj, proj, hg_lb, mats, masks)


def _na_bias_body(rpb_ref, o_ref):
    h = pl.program_id(0)
    n_dc = 2 * NA_WIN_C - 1
    qc = lax.broadcasted_iota(jnp.int32, (GRID_W, LANES), 0)
    kc = lax.broadcasted_iota(jnp.int32, (GRID_W, LANES), 1) % GRID_W
    c0 = jnp.clip(qc - NA_WIN_C // 2, 0, GRID_W - NA_WIN_C)
    col_in = (kc >= c0) & (kc < c0 + NA_WIN_C)
    dc = jnp.clip(kc - qc + NA_WIN_C - 1, 0, n_dc - 1)
    low = lax.broadcasted_iota(jnp.int32, (GRID_W, LANES), 1) < GRID_W
    tiles = []
    for dr in range(2 * NA_WIN_R - 1):
        t = jnp.zeros((GRID_W, LANES), F32)
        for k in range(n_dc):
            t = jnp.where(dc == k, rpb_ref[(h * (2 * NA_WIN_R - 1) + dr) * n_dc + k], t)
        tiles.append(jnp.where(col_in, t, -jnp.inf))
    for dr in range(2 * NA_WIN_R - 2):
        o_ref[dr] = jnp.where(low, tiles[dr], tiles[dr + 1])


def _na_bias_table(rpb):
    n_dr = 2 * NA_WIN_R - 2
    return pl.pallas_call(
        _na_bias_body,
        grid=(NA_HEADS,),
        in_specs=[pl.BlockSpec(memory_space=pltpu.SMEM)],
        out_specs=pl.BlockSpec((None, n_dr, GRID_W, LANES), lambda h: (h, 0, 0, 0)),
        out_shape=jax.ShapeDtypeStruct((NA_HEADS, n_dr, GRID_W, LANES), F32),
        compiler_params=_cparams(("arbitrary",), 32),
        name="na_bias",
    )(rpb.reshape(-1))


def _rope_tables():
    quarter = NA_HEAD_DIM // 4
    inv = ROPE_BASE ** (-np.arange(quarter, dtype=np.float64) / quarter)
    t = np.arange(SEQ)
    row, col = t // GRID_W, t % GRID_W
    ang = np.concatenate([row[:, None] * inv[None], row[:, None] * inv[None],
                          col[:, None] * inv[None], col[:, None] * inv[None]], axis=1)
    sign = np.tile(np.concatenate([-np.ones(quarter), np.ones(quarter)]), 2)[None]
    return jnp.asarray(np.cos(ang), F32), jnp.asarray(np.sin(ang) * sign, F32)


def _rope(v, cos, sin_signed):
    quarter = NA_HEAD_DIM // 4
    lane = lax.broadcasted_iota(jnp.int32, v.shape, 1)
    first = (lane % (2 * quarter)) < quarter
    partner = jnp.where(first, pltpu.roll(v, NA_HEAD_DIM - quarter, axis=1), pltpu.roll(v, quarter, axis=1))
    return v * cos + partner * sin_signed


def _na_body(rows_per_step, q_ref, k_ref, v_ref, kc_ref, vc_ref, cosq_ref, sinq_ref, cos_ref, sin_ref, bias_ref,
             o_ref, kr_ref, vb_ref):
    rb = pl.program_id(2)
    n_rows = SEQ // GRID_W
    scale = NA_HEAD_DIM ** -0.5

    @pl.when(rb == 0)
    def _():
        kr_ref[...] = _rope(k_ref[...], cos_ref[...], sin_ref[...]).astype(BF16)
        vb_ref[...] = v_ref[...].astype(BF16)

    qr = (_rope(q_ref[...], cosq_ref[...], sinq_ref[...]) * scale).astype(BF16)
    kc = kc_ref[...].astype(BF16)
    vc = vc_ref[...].astype(BF16)
    starts, s_rows = [], []
    for lr in range(rows_per_step):
        r = rb * rows_per_step + lr
        r0 = jnp.clip(r - NA_WIN_R // 2, 0, n_rows - NA_WIN_R)
        start = pl.multiple_of(r0 * GRID_W, GRID_W)
        dr0 = r0 - r + NA_WIN_R - 1
        bias = jnp.concatenate([bias_ref[dr0 + 2 * p] for p in range(NA_WIN_R // 2)], axis=1)
        k_win = kr_ref[pl.ds(start, NA_WIN_R * GRID_W), :]
        s_rows.append(_dot_nt(qr[lr * GRID_W:(lr + 1) * GRID_W], k_win) + bias)
        starts.append(start)
    s_loc = jnp.concatenate(s_rows, axis=0)
    s_ctx = _dot_nt(qr, kc)
    m = jnp.maximum(jnp.max(s_loc, axis=-1, keepdims=True), jnp.max(s_ctx, axis=-1, keepdims=True))
    p_loc = jnp.exp(s_loc - m)
    p_ctx = jnp.exp(s_ctx - m)
    denom = jnp.sum(p_loc, axis=-1, keepdims=True) + jnp.sum(p_ctx, axis=-1, keepdims=True)
    p_loc = p_loc.astype(BF16)
    o_loc = jnp.concatenate(
        [_dot(p_loc[lr * GRID_W:(lr + 1) * GRID_W], vb_ref[pl.ds(starts[lr], NA_WIN_R * GRID_W), :])
         for lr in range(rows_per_step)], axis=0)
    o = o_loc + _dot(p_ctx.astype(BF16), vc)
    o_ref[...] = (o / denom).astype(o_ref.dtype)


def _na_attention(qkv, bias_tbl):
    rows_per_step = 16
    tq = rows_per_step * GRID_W
    steps = SEQ // tq
    cos, sin = _rope_tables()
    hd = NA_HEAD_DIM
    ctx_blk0 = T_LAT // CTX_LEN
    return pl.pallas_call(
        functools.partial(_na_body, rows_per_step),
        grid=(BATCH, NA_HEADS, steps),
        in_specs=[
            pl.BlockSpec((tq, hd), lambda b, h, r: (b * steps + r, h)),
            pl.BlockSpec((SEQ, hd), lambda b, h, r: (b, NA_HEADS + h)),
            pl.BlockSpec((SEQ, hd), lambda b, h, r: (b, 2 * NA_HEADS + h)),
            pl.BlockSpec((CTX_LEN, hd), lambda b, h, r: (ctx_blk0 + b, NA_HEADS + h)),
            pl.BlockSpec((CTX_LEN, hd), lambda b, h, r: (ctx_blk0 + b, 2 * NA_HEADS + h)),
            pl.BlockSpec((tq, hd), lambda b, h, r: (r, 0)),
            pl.BlockSpec((tq, hd), lambda b, h, r: (r, 0)),
            pl.BlockSpec((SEQ, hd), lambda b, h, r: (0, 0)),
            pl.BlockSpec((SEQ, hd), lambda b, h, r: (0, 0)),
            pl.BlockSpec((None,) + bias_tbl.shape[1:], lambda b, h, r: (h, 0, 0, 0)),
        ],
        out_specs=pl.BlockSpec((tq, hd), lambda b, h, r: (b * steps + r, h)),
        out_shape=jax.ShapeDtypeStruct((T_LAT, D_MODEL), BF16),
        scratch_shapes=[pltpu.VMEM((SEQ, hd), BF16), pltpu.VMEM((SEQ, hd), BF16)],
        compiler_params=_cparams(("arbitrary", "arbitrary", "arbitrary"), 48),
        name="na_attention",
    )(qkv, qkv, qkv, qkv, qkv, cos, sin, cos, sin, bias_tbl)


def _ctx_attn_body(q_ref, k_ref, v_ref, o_ref):
    scale = NA_HEAD_DIM ** -0.5
    s = _dot_nt((q_ref[...] * scale).astype(BF16), k_ref[...].astype(BF16))
    p = jnp.exp(s - jnp.max(s, axis=-1, keepdims=True))
    o = _dot(p.astype(BF16), v_ref[...].astype(BF16))
    o_ref[...] = (o / jnp.sum(p, axis=-1, keepdims=True)).astype(o_ref.dtype)


def _ctx_attention(qkv):
    hd = NA_HEAD_DIM
    ctx_blk0 = T_LAT // CTX_LEN
    return pl.pallas_call(
        _ctx_attn_body,
        grid=(BATCH, NA_HEADS),
        in_specs=[
            pl.BlockSpec((CTX_LEN, hd), lambda b, h: (ctx_blk0 + b, h)),
            pl.BlockSpec((CTX_LEN, hd), lambda b, h: (ctx_blk0 + b, NA_HEADS + h)),
            pl.BlockSpec((CTX_LEN, hd), lambda b, h: (ctx_blk0 + b, 2 * NA_HEADS + h)),
        ],
        out_specs=pl.BlockSpec((CTX_LEN, hd), lambda b, h: (b, h)),
        out_shape=jax.ShapeDtypeStruct((T_CTX, D_MODEL), BF16),
        compiler_params=_cparams(("arbitrary", "arbitrary"), 32),
        name="ctx_attention",
    )(qkv, qkv, qkv)


def kernel(x, c, ctx, c_ctx, w_mod, b_mod, norm_g, ffn_w_up, ffn_conv_w, ffn_conv_b, ffn_w_down, ssm_w_in, ssm_conv_w, ssm_conv_b, ssm_dt_bias, ssm_a_log, ssm_d, ssm_norm_g, ssm_w_out, hg_w_in, hg_lb, hg_norm_g, hg_w_out, na_w_qkv, na_rpb, na_w_out):
    xs = jnp.concatenate([x.reshape(T_LAT, D_MODEL), ctx.reshape(T_CTX, D_MODEL)], axis=0)
    cvec8 = jnp.concatenate([c, c_ctx[None], jnp.zeros((SUBLANES - BATCH - 1, D_MODEL), F32)], axis=0)
    mod = _adaln(cvec8, w_mod, b_mod)
    ffn_up_bf, ffn_down_bf = ffn_w_up.astype(BF16), ffn_w_down.astype(BF16)
    ssm_out_bf, hg_out_bf, na_out_bf = ssm_w_out.astype(BF16), hg_w_out.astype(BF16), na_w_out.astype(BF16)

    h = _prep(xs, norm_g[0, 0], mod[0])
    for i in range(DEPTH):
        last = i == DEPTH - 1
        kind, slot = i % N_MIXERS, i // N_MIXERS
        mod_l = mod[i]
        n_rows = T_LAT if last else T_ALL
        op = dict(xs=xs, g_res=norm_g[i, 1], g_ffn=norm_g[i, 2], mod_l=mod_l, n_rows=n_rows, slot=slot)
        if kind == 0:
            proj = _in_proj(h, ssm_w_in, slot, 1152, "ssd_in")
            xbc = _ssd_conv(proj, ssm_conv_w[slot], ssm_conv_b[slot])
            y2 = _ssd_scan(xbc, proj, ssm_dt_bias[slot], ssm_a_log[slot], ssm_d[slot])
            xs, hf = _out_proj(_ssd_gate_prologue, [((y2, 0), 0), ((y2, 1), 0), (proj, 0)],
                               ssm_norm_g[slot].reshape(1, SSM_INNER), ssm_out_bf, tk=1024, row_norm=True,
                               name="ssd_out", **op)
        elif kind == 1:
            proj = _in_proj(h, hg_w_in, slot, 1024, "hg_in")
            o2 = _gla_scan(proj, hg_lb, i)
            tk = 1024
            xs, hf = _out_proj(_hg_readout_prologue, [((o2, 0), 0), ((o2, 1), 0), (proj, 4 * D_MODEL // tk)],
                               hg_norm_g[slot].reshape(1, D_MODEL), hg_out_bf, tk=tk, row_norm=False,
                               name="hg_out", **op)
        else:
            qkv = _in_proj(h, na_w_qkv, slot, 1024, "na_in")
            o = jnp.concatenate([_na_attention(qkv, _na_bias_table(na_rpb[slot])), _ctx_attention(qkv)], axis=0)
            xs, hf = _out_proj(_plain_prologue, [(o, 0)], None, na_out_bf, tk=1024, row_norm=False,
                               name="na_out", **op)
        nxt = min(i + 1, DEPTH - 1)
        xs, h = _ffn(hf, xs, i, ffn_up_bf, ffn_conv_w, ffn_conv_b, ffn_down_bf, norm_g[i, 3], mod_l,
                     norm_g[nxt, 0], mod[nxt], n_rows=n_rows, emit_h=not last)

    return xs.reshape(BATCH, SEQ, D_MODEL)
```

```python
import functools

import numpy as np
import jax
import jax.numpy as jnp
from jax import lax
from jax.experimental import pallas as pl
from jax.experimental.pallas import tpu as pltpu

F32 = jnp.float32
BF16 = jnp.bfloat16

D_MODEL = 2048
BATCH = 2
SEQ = 4096
DEPTH = 4
GRID_W = 64
CTX_LEN = 256
N_MIXERS = 3
RMS_EPS = 1e-6
SSM_INNER = 4096
SSM_HEAD_DIM = 64
SSM_HEADS = 64
SSM_STATE = 128
SSM_GROUPS = 8
SSM_CONV_DIM = SSM_INNER + 2 * SSM_GROUPS * SSM_STATE
SSM_IN_DIM = SSM_INNER + SSM_CONV_DIM + 2 * SSM_HEADS
SSM_CHUNK = 128
HG_HEADS = 16
HG_KEY_DIM = 128
HG_IN_DIM = 5 * D_MODEL
HG_CHUNK = 64
NA_HEADS = 16
NA_HEAD_DIM = 128
NA_WIN_R = 8
NA_WIN_C = 16
ROPE_BASE = 10000.0
FFN_HIDDEN = 5632

T_LAT = BATCH * SEQ
T_CTX = BATCH * CTX_LEN
T_ALL = T_LAT + T_CTX
LANES = 128
SUBLANES = 8
BF16_ROWS = 16
ONES_ROWS = 16
ROW_TILE = 512
FFN_COL_CHUNK = 256
MIB = 1024 * 1024
LOG2E = 1.4426950408889634


def _cparams(sem, vmem_mib):
    return pltpu.CompilerParams(dimension_semantics=sem, vmem_limit_bytes=vmem_mib * MIB)


def _silu(v):
    return v * jax.nn.sigmoid(v)


def _rms(v):
    return v * lax.rsqrt(jnp.mean(v * v, axis=-1, keepdims=True) + RMS_EPS)


def _log1p_unit(e):
    return jnp.log(1.0 + e)


def _softplus(v):
    return jnp.maximum(v, 0.0) + _log1p_unit(jnp.exp(-jnp.abs(v)))


def _dot(a, b):
    return jnp.dot(a, b, preferred_element_type=F32)


def _dot_nt(a, b):
    return lax.dot_general(a, b, (((1,), (1,)), ((), ())), preferred_element_type=F32)


def _dot_tn(a, b):
    return lax.dot_general(a, b, (((0,), (0,)), ((), ())), preferred_element_type=F32)


def _bf16_pieces(v, terms):
    pieces = []
    rem = v
    for _ in range(terms):
        part = rem.astype(BF16)
        rem = rem - part.astype(F32)
        pieces.append(part)
    return pieces


def _dot_sel(m01_rep, v, terms):
    return _dot(m01_rep, jnp.concatenate(_bf16_pieces(v, terms), axis=0))


def _dot_sel_rhs(v, m01_rep, terms):
    return _dot(jnp.concatenate(_bf16_pieces(v, terms), axis=1), m01_rep)


def _mod_row(row0):
    return jnp.where(row0 >= T_LAT, 2, row0 // SEQ)


def _mod_vec(mod_ref, r, idx):
    return mod_ref[pl.ds(r, 1), idx * D_MODEL:(idx + 1) * D_MODEL]


def _norm_mod(x, g, mod_ref, r, shift_idx, scale_idx):
    return _rms(x) * g * (1.0 + _mod_vec(mod_ref, r, scale_idx)) + _mod_vec(mod_ref, r, shift_idx)


def _seg_edges(row0, rows):
    gr = row0 + lax.broadcasted_iota(jnp.int32, (rows, 1), 0)
    pos = jnp.where(gr >= T_LAT, (gr - T_LAT) % CTX_LEN, gr % SEQ)
    length = jnp.where(gr >= T_LAT, CTX_LEN, SEQ)
    return pos == 0, pos == length - 1


def _shift_rows(cur, prev_row, next_row, row0):
    rows = cur.shape[0]
    first, last = _seg_edges(row0, rows)
    ridx = lax.broadcasted_iota(jnp.int32, (rows, 1), 0)
    up = jnp.where(ridx == 0, prev_row, pltpu.roll(cur, 1, axis=0))
    up = jnp.where(first, 0.0, up)
    dn = jnp.where(ridx == rows - 1, next_row, pltpu.roll(cur, rows - 1, axis=0))
    dn = jnp.where(last, 0.0, dn)
    return up, dn


def _token_conv(cur, prev_row, next_row, row0, w_ref, b_ref):
    up, dn = _shift_rows(cur, prev_row, next_row, row0)
    return b_ref[...] + up * w_ref[0:1, :] + cur * w_ref[1:2, :] + dn * w_ref[2:3, :]


def _halo_specs(tm, halo, width, n_rows, col_fn):
    per = tm // halo
    last_blk = n_rows // halo - 1
    prev = pl.BlockSpec((halo, width), lambda i, j: (jnp.maximum(i * per - 1, 0), col_fn(j)))
    nxt = pl.BlockSpec((halo, width), lambda i, j: (jnp.minimum((i + 1) * per, last_blk), col_fn(j)))
    return prev, nxt


def _adaln_body(c_ref, w_ref, b_ref, o_ref):
    s = _silu(c_ref[...]).astype(BF16)
    o_ref[...] = _dot(s, w_ref[...].astype(BF16)) + b_ref[...]


def _adaln(cvec8, w_mod, b_mod):
    tn = 1024
    n_out = 6 * D_MODEL
    return pl.pallas_call(
        _adaln_body,
        grid=(DEPTH, n_out // tn),
        in_specs=[
            pl.BlockSpec((SUBLANES, D_MODEL), lambda l, j: (0, 0)),
            pl.BlockSpec((None, D_MODEL, tn), lambda l, j: (l, 0, j)),
            pl.BlockSpec((None, 1, tn), lambda l, j: (l, 0, j)),
        ],
        out_specs=pl.BlockSpec((None, SUBLANES, tn), lambda l, j: (l, 0, j)),
        out_shape=jax.ShapeDtypeStruct((DEPTH, SUBLANES, n_out), F32),
        compiler_params=_cparams(("arbitrary", "arbitrary"), 40),
        name="adaln",
    )(cvec8, w_mod, b_mod.reshape(DEPTH, 1, n_out))


def _prep_body(x_ref, g_ref, mod_ref, o_ref):
    r = _mod_row(pl.program_id(0) * ROW_TILE)
    o_ref[...] = _norm_mod(x_ref[...], g_ref[...], mod_ref, r, 0, 1).astype(BF16)


def _prep(xs, g_row, mod_l):
    row = pl.BlockSpec((ROW_TILE, D_MODEL), lambda i: (i, 0))
    return pl.pallas_call(
        _prep_body,
        grid=(T_ALL // ROW_TILE,),
        in_specs=[row, pl.BlockSpec((1, D_MODEL), lambda i: (0, 0)), pl.BlockSpec(mod_l.shape, lambda i: (0, 0))],
        out_specs=row,
        out_shape=jax.ShapeDtypeStruct((T_ALL, D_MODEL), BF16),
        compiler_params=_cparams(("arbitrary",), 32),
        name="prep",
    )(xs, g_row.reshape(1, D_MODEL), mod_l)


def _in_proj_body(h_ref, w_ref, o_ref, wb_ref):
    @pl.when(pl.program_id(1) == 0)
    def _():
        wb_ref[...] = w_ref[...].astype(BF16)

    o_ref[...] = _dot(h_ref[...], wb_ref[...])


def _in_proj(h, w_stack, slot, tn, name):
    tm = T_ALL // 8
    n_dim = w_stack.shape[2]
    return pl.pallas_call(
        _in_proj_body,
        grid=(n_dim // tn, T_ALL // tm),
        in_specs=[pl.BlockSpec((tm, D_MODEL), lambda j, i: (i, 0)),
                  pl.BlockSpec((None, D_MODEL, tn), lambda j, i: (slot, 0, j))],
        out_specs=pl.BlockSpec((tm, tn), lambda j, i: (i, j)),
        out_shape=jax.ShapeDtypeStruct((T_ALL, n_dim), F32),
        scratch_shapes=[pltpu.VMEM((D_MODEL, tn), BF16)],
        compiler_params=_cparams(("arbitrary", "arbitrary"), 56),
        name=name,
    )(h, w_stack)


def _ssd_gate_prologue(yf_ref, yb_ref, z_ref, g_ref):
    v = (yf_ref[...].astype(F32) + yb_ref[...].astype(F32)) * _silu(z_ref[...])
    return v * g_ref[...], lax.rsqrt(jnp.mean(v * v, axis=-1, keepdims=True) + RMS_EPS)


def _hg_readout_prologue(of_ref, ob_ref, gate_ref, g_ref):
    o = of_ref[...].astype(F32) + ob_ref[...].astype(F32)
    parts = [_rms(o[:, h * LANES:(h + 1) * LANES]) for h in range(o.shape[1] // LANES)]
    return jnp.concatenate(parts, axis=-1) * g_ref[...] * _silu(gate_ref[...]), None


def _plain_prologue(o_ref):
    return o_ref[...], None


def _out_proj_body(prologue, n_in, tm, *refs):
    ins = refs[:n_in]
    w_ref, x_ref, g1_ref, g2_ref, mod_ref, xo_ref, ho_ref = refs[n_in:]
    part, row_scale = prologue(*ins)
    y = _dot(part.astype(BF16), w_ref[...])
    if row_scale is not None:
        y = y * row_scale
    r = _mod_row(pl.program_id(0) * tm)
    xn = x_ref[...] + _mod_vec(mod_ref, r, 2) * (_rms(y) * g1_ref[...])
    xo_ref[...] = xn
    ho_ref[...] = _norm_mod(xn, g2_ref[...], mod_ref, r, 3, 4).astype(BF16)


def _out_proj(prologue, row_ins, gain, w_stack, slot, xs, g_res, g_ffn, mod_l, *, n_rows, tm, name):
    k_dim = w_stack.shape[1]
    in_specs, args = [], []
    for arr, cb in row_ins:
        if isinstance(arr, tuple):
            in_specs.append(pl.BlockSpec((None, tm, k_dim), functools.partial(
                lambda i, ld, cb: (ld, i, cb), ld=arr[1], cb=cb)))
            args.append(arr[0])
        else:
            in_specs.append(pl.BlockSpec((tm, k_dim), functools.partial(lambda i, cb: (i, cb), cb=cb)))
            args.append(arr)
    if gain is not None:
        in_specs.append(pl.BlockSpec((1, k_dim), lambda i: (0, 0)))
        args.append(gain)
    n_in = len(args)
    row = pl.BlockSpec((tm, D_MODEL), lambda i: (i, 0))
    vec = pl.BlockSpec((1, D_MODEL), lambda i: (0, 0))
    in_specs += [pl.BlockSpec((None, k_dim, D_MODEL), lambda i: (slot, 0, 0), pipeline_mode=pl.Buffered(1)),
                 row, vec, vec, pl.BlockSpec(mod_l.shape, lambda i: (0, 0))]
    return pl.pallas_call(
        functools.partial(_out_proj_body, prologue, n_in, tm),
        grid=(n_rows // tm,),
        in_specs=in_specs,
        out_specs=[row, row],
        out_shape=[jax.ShapeDtypeStruct((n_rows, D_MODEL), F32), jax.ShapeDtypeStruct((n_rows, D_MODEL), BF16)],
        compiler_params=_cparams(("arbitrary",), 60),
        name=name,
    )(*args, w_stack, xs, g_res.reshape(1, D_MODEL), g_ffn.reshape(1, D_MODEL), mod_l)


def _ffn_body(emit_h, has_ctx, h_ref, hp_ref, hn_ref, wa_ref, wv_ref, cwa_ref, cwv_ref, cba_ref, cbv_ref, wd_ref,
              x_ref, g_res_ref, mod_ref, g_next_ref, modn_ref, *out_and_scratch):
    if emit_h:
        xo_ref, ho_ref, hall_ref, acc_ref = out_and_scratch
    else:
        xo_ref, hall_ref, acc_ref = out_and_scratch
    k = pl.program_id(1)
    tm = ROW_TILE
    row0 = pl.program_id(0) * tm
    pad = BF16_ROWS
    rows = tm + 2 * pad
    s8 = SUBLANES

    @pl.when(k == 0)
    def _():
        hall_ref[pl.ds(0, pad), :] = hp_ref[...]
        hall_ref[pl.ds(pad, tm), :] = h_ref[...]
        hall_ref[pl.ds(pad + tm, pad), :] = hn_ref[...]

    hall = hall_ref[...]
    edge_first = row0 % SEQ == 0
    edge_last = ((row0 + tm) % SEQ == 0) | (row0 + tm == T_ALL)
    is_ctx = row0 >= T_LAT
    joints = list(range(CTX_LEN, tm, CTX_LEN)) if has_ctx else []
    sub = lax.broadcasted_iota(jnp.int32, (s8, 1), 0)

    def patch(v, start, cond):
        return jnp.concatenate([v[:start], jnp.where(cond, 0.0, v[start:start + s8]), v[start + s8:]], axis=0)

    def conv(w_ref, cw_ref, cb_ref, cs):
        u = _dot(hall, w_ref[:, cs])
        u = patch(u, pad - s8, edge_first)
        u = patch(u, pad + tm, edge_last)
        up = pltpu.roll(u, 1, axis=0)[pad:pad + tm]
        dn = pltpu.roll(u, rows - 1, axis=0)[pad:pad + tm]
        for j in joints:
            up = patch(up, j, is_ctx & (sub == 0))
            dn = patch(dn, j - s8, is_ctx & (sub == s8 - 1))
        return cb_ref[:, cs] + up * cw_ref[0:1, cs] + u[pad:pad + tm] * cw_ref[1:2, cs] + dn * cw_ref[2:3, cs]

    chunks = [pl.ds(c * FFN_COL_CHUNK, FFN_COL_CHUNK) for c in range(wa_ref.shape[1] // FFN_COL_CHUNK)]
    acts = [(_silu(conv(wa_ref, cwa_ref, cba_ref, cs)) * conv(wv_ref, cwv_ref, cbv_ref, cs)).astype(BF16)
            for cs in chunks]
    contrib = _dot(jnp.concatenate(acts, axis=1), wd_ref[...])

    @pl.when(k == 0)
    def _():
        acc_ref[...] = contrib

    @pl.when(k != 0)
    def _():
        acc_ref[...] += contrib

    @pl.when(k == pl.num_programs(1) - 1)
    def _():
        r = _mod_row(row0)
        xn = x_ref[...] + _mod_vec(mod_ref, r, 5) * (_rms(acc_ref[...]) * g_res_ref[...])
        xo_ref[...] = xn
        if emit_h:
            ho_ref[...] = _norm_mod(xn, g_next_ref[...], modn_ref, r, 0, 1).astype(BF16)


def _ffn(h, xs, layer, w_up, conv_w, conv_b, w_down, g_res, mod_l, g_next, mod_next, *, n_rows, emit_h):
    tm, tk = ROW_TILE, 512
    kt = FFN_HIDDEN // tk
    h_prev, h_next = _halo_specs(tm, BF16_ROWS, D_MODEL, n_rows, lambda k: 0)
    row = pl.BlockSpec((tm, D_MODEL), lambda i, k: (i, 0))
    vec = pl.BlockSpec((1, D_MODEL), lambda i, k: (0, 0))
    tab = pl.BlockSpec(mod_l.shape, lambda i, k: (0, 0))
    out_specs = [row, row] if emit_h else [row]
    out_shape = [jax.ShapeDtypeStruct((n_rows, D_MODEL), F32)]
    if emit_h:
        out_shape.append(jax.ShapeDtypeStruct((n_rows, D_MODEL), BF16))
    conv_b3 = conv_b.reshape(DEPTH, 1, 2 * FFN_HIDDEN)

    res = pl.pallas_call(
        functools.partial(_ffn_body, emit_h, n_rows > T_LAT),
        grid=(n_rows // tm, kt),
        in_specs=[
            row, h_prev, h_next,
            pl.BlockSpec((None, D_MODEL, tk), lambda i, k: (layer, 0, k)),
            pl.BlockSpec((None, D_MODEL, tk), lambda i, k: (layer, 0, k + kt)),
            pl.BlockSpec((None, 3, tk), lambda i, k: (layer, 0, k)),
            pl.BlockSpec((None, 3, tk), lambda i, k: (layer, 0, k + kt)),
            pl.BlockSpec((None, 1, tk), lambda i, k: (layer, 0, k)),
            pl.BlockSpec((None, 1, tk), lambda i, k: (layer, 0, k + kt)),
            pl.BlockSpec((None, tk, D_MODEL), lambda i, k: (layer, k, 0)),
            row, vec, tab, vec, tab,
        ],
        out_specs=out_specs,
        out_shape=out_shape,
        scratch_shapes=[pltpu.VMEM((tm + 2 * BF16_ROWS, D_MODEL), BF16), pltpu.VMEM((tm, D_MODEL), F32)],
        compiler_params=_cparams(("arbitrary", "arbitrary"), 56),
        name="conv_ffn",
    )(h, h, h, w_up, w_up, conv_w, conv_w, conv_b3, conv_b3, w_down, xs, g_res.reshape(1, D_MODEL), mod_l,
      g_next.reshape(1, D_MODEL), mod_next)
    return (res[0], res[1]) if emit_h else (res[0], None)


def _ssd_conv_body(tm, cur_ref, prev_ref, next_ref, w_ref, b_ref, o_ref):
    u = _token_conv(cur_ref[...], prev_ref[SUBLANES - 1:SUBLANES, :], next_ref[0:1, :], pl.program_id(0) * tm,
                    w_ref, b_ref)
    o_ref[...] = _silu(u).astype(o_ref.dtype)


def _ssd_conv(proj, conv_w, conv_b):
    tm, tc = 512, 2048
    col0 = SSM_INNER // tc
    prev, nxt = _halo_specs(tm, SUBLANES, tc, T_ALL, lambda j: j + col0)
    return pl.pallas_call(
        functools.partial(_ssd_conv_body, tm),
        grid=(T_ALL // tm, SSM_CONV_DIM // tc),
        in_specs=[pl.BlockSpec((tm, tc), lambda i, j: (i, j + col0)), prev, nxt,
                  pl.BlockSpec((3, tc), lambda i, j: (0, j)), pl.BlockSpec((1, tc), lambda i, j: (0, j))],
        out_specs=pl.BlockSpec((tm, tc), lambda i, j: (i, j)),
        out_shape=jax.ShapeDtypeStruct((T_ALL, SSM_CONV_DIM), BF16),
        compiler_params=_cparams(("arbitrary", "arbitrary"), 40),
        name="ssd_conv",
    )(proj, proj, proj, conv_w, conv_b.reshape(1, -1))


def _scan_chunk_index(d, b, s, n_lat, n_ctx):
    ctx0 = BATCH * n_lat + b * n_ctx
    fwd = jnp.where(s < n_ctx, ctx0 + s, b * n_lat + s - n_ctx)
    bwd = jnp.where(s < n_ctx, ctx0 + n_ctx - 1 - s, b * n_lat + n_lat - 1 - (s - n_ctx))
    return jnp.where(d == 0, fwd, bwd)


SSD_CUM_TERMS = 3
SSD_DT_TERMS = 2


def _ssd_scan_consts():
    q = SSM_CHUNK
    i = np.arange(q)
    pfx = np.zeros((2, q + ONES_ROWS, q), np.float32)
    pfx[0, :q] = i[:, None] >= i[None, :]
    pfx[1, :q] = i[:, None] <= i[None, :]
    pfx[:, q:] = 1.0
    pfx = np.tile(pfx, (1, 1, SSD_CUM_TERMS))
    expand = np.repeat(np.eye(SSM_HEADS, dtype=np.float32), SSM_HEAD_DIM, axis=1)
    expand = np.tile(expand, (SSD_DT_TERMS, 1))
    return jnp.asarray(pfx, BF16), jnp.asarray(expand, BF16)


def _ssd_scan_body(x_ref, b_ref, c_ref, dtraw_ref, dtb_ref, alog_ref, dskip_ref, pfx_ref, exp_ref, y_ref, s_ref):
    q = SSM_CHUNK
    hp = 2 * SSM_HEAD_DIM
    d = pl.program_id(0)

    @pl.when(pl.program_id(2) == 0)
    def _():
        s_ref[...] = jnp.zeros_like(s_ref)

    def pick(v):
        return jnp.where(d == 0, v[:, :SSM_HEADS], v[:, SSM_HEADS:])

    dt2 = _softplus(dtraw_ref[...] + dtb_ref[...])
    cum2 = _dot_sel(pfx_ref[...], dt2 * (-LOG2E * jnp.exp(alog_ref[...])), SSD_CUM_TERMS)
    acum2 = cum2[:q]
    acum = pick(acum2)
    tot = pick(cum2[q:q + SUBLANES])[0:1]
    acum_t2 = acum2.T
    acum_t = jnp.where(d == 0, acum_t2[:SSM_HEADS], acum_t2[SSM_HEADS:])
    dt_wide = _dot_sel_rhs(pick(dt2), exp_ref[...], SSD_DT_TERMS)

    ii = lax.broadcasted_iota(jnp.int32, (q, q), 0)
    jj = lax.broadcasted_iota(jnp.int32, (q, q), 1)
    causal = (jj - ii) * (1 - 2 * d) <= 0
    low_half = lax.broadcasted_iota(jnp.int32, (q, hp), 1) < SSM_HEAD_DIM

    heads_per_group = SSM_HEADS // SSM_GROUPS
    n_pairs = SSM_HEADS // 2

    b_gs = [b_ref[:, g * SSM_STATE:(g + 1) * SSM_STATE].astype(BF16) for g in range(SSM_GROUPS)]
    c_bf = [c_ref[:, g * SSM_STATE:(g + 1) * SSM_STATE].astype(BF16) for g in range(SSM_GROUPS)]
    c_gs = [c.astype(F32) for c in c_bf]
    cbs = [jnp.where(causal, _dot_nt(c_bf[g], b_gs[g]), 0.0) for g in range(SSM_GROUPS)]

    cols, lhs = [], []
    for h in range(SSM_HEADS):
        g = h // heads_per_group
        col = jnp.broadcast_to(acum[:, h:h + 1], (q, q))
        seg = jnp.where(causal, col - acum_t[h:h + 1, :], 0.0)
        scores = jnp.exp2(seg) * cbs[g]
        c_dec = c_gs[g] * jnp.exp2(col)
        lhs.append(jnp.concatenate([scores.astype(BF16), c_dec.astype(BF16)], axis=1))
        cols.append(col)

    xdts = []
    for p in range(n_pairs):
        lo = p * hp
        x_pair = x_ref[:, lo:lo + hp].astype(F32)
        xdt = x_pair * dt_wide[:, lo:lo + hp]
        rhs = jnp.concatenate([xdt.astype(BF16), s_ref[:, lo:lo + hp].astype(BF16)], axis=0)
        y_pair = jnp.where(low_half, _dot(lhs[2 * p], rhs), _dot(lhs[2 * p + 1], rhs))
        y_ref[:, lo:lo + hp] = (y_pair + dskip_ref[:, lo:lo + hp] * x_pair).astype(y_ref.dtype)
        xdts.append(xdt)

    for p in range(n_pairs):
        lo = p * hp
        g = (2 * p) // heads_per_group
        tot_pair = jnp.where(low_half[0:1], jnp.broadcast_to(tot[:, 2 * p:2 * p + 1], (1, hp)),
                             jnp.broadcast_to(tot[:, 2 * p + 1:2 * p + 2], (1, hp)))
        to_end = jnp.exp2(tot_pair - jnp.where(low_half, cols[2 * p], cols[2 * p + 1]))
        s_ref[:, lo:lo + hp] = (s_ref[:, lo:lo + hp] * jnp.exp2(tot_pair)
                                + _dot_tn(b_gs[g], (xdts[p] * to_end).astype(BF16)))


def _ssd_scan(xbc, proj, dt_bias, a_log, d_skip):
    q = SSM_CHUNK
    n_lat, n_ctx = SEQ // q, CTX_LEN // q
    pfx, expand = _ssd_scan_consts()
    n_heads2 = 2 * SSM_HEADS
    dt_col = (SSM_INNER + SSM_CONV_DIM) // n_heads2
    b_col = SSM_INNER // (SSM_GROUPS * SSM_STATE)
    d_wide = jnp.repeat(d_skip, SSM_HEAD_DIM, axis=1).reshape(2, 1, SSM_INNER)

    def chunk(d, b, s):
        return _scan_chunk_index(d, b, s, n_lat, n_ctx)

    return pl.pallas_call(
        _ssd_scan_body,
        grid=(2, BATCH, n_lat + n_ctx),
        in_specs=[
            pl.BlockSpec((q, SSM_INNER), lambda d, b, s: (chunk(d, b, s), 0)),
            pl.BlockSpec((q, SSM_GROUPS * SSM_STATE), lambda d, b, s: (chunk(d, b, s), b_col)),
            pl.BlockSpec((q, SSM_GROUPS * SSM_STATE), lambda d, b, s: (chunk(d, b, s), b_col + 1)),
            pl.BlockSpec((q, n_heads2), lambda d, b, s: (chunk(d, b, s), dt_col)),
            pl.BlockSpec((1, n_heads2), lambda d, b, s: (0, 0)),
            pl.BlockSpec((1, n_heads2), lambda d, b, s: (0, 0)),
            pl.BlockSpec((None, 1, SSM_INNER), lambda d, b, s: (d, 0, 0)),
            pl.BlockSpec((None,) + pfx.shape[1:], lambda d, b, s: (d, 0, 0)),
            pl.BlockSpec(expand.shape, lambda d, b, s: (0, 0)),
        ],
        out_specs=pl.BlockSpec((None, q, SSM_INNER), lambda d, b, s: (d, chunk(d, b, s), 0)),
        out_shape=jax.ShapeDtypeStruct((2, T_ALL, SSM_INNER), BF16),
        scratch_shapes=[pltpu.VMEM((SSM_STATE, SSM_INNER), F32)],
        compiler_params=_cparams(("arbitrary", "arbitrary", "arbitrary"), 48),
        name="ssd_scan",
    )(xbc, xbc, xbc, proj, dt_bias.reshape(1, n_heads2), a_log.reshape(1, n_heads2), d_wide, pfx, expand)


GLA_SUM_TERMS = 2


def _gla_consts():
    q = HG_CHUNK
    levels = int(np.log2(q))
    i = np.arange(q)
    mats = [(i[:, None] >= i[None, :]).astype(np.float32), (i[None, :] > i[:, None]).astype(np.float32)]
    masks = [np.eye(q, dtype=np.float32)]
    for lv in range(levels):
        s = 1 << lv
        blk, pos = i // (2 * s), i % (2 * s)
        bound = blk * 2 * s + s - 1
        right = pos >= s
        t = i[None, :]
        m_right = right[:, None] & (t > bound[:, None]) & (t <= i[:, None])
        m_left = (~right)[:, None] & (t > i[:, None]) & (t <= bound[:, None])
        mats.append((m_right | m_left).astype(np.float32))
        masks.append((right[:, None] & (~right)[None, :] & (blk[:, None] == blk[None, :])).astype(np.float32))
    mats.append(np.ones((ONES_ROWS, q), np.float32))
    fwd_m, fwd_k = np.concatenate(mats, axis=0), np.stack(masks)
    flip_rows = np.concatenate([np.arange(k * q, (k + 1) * q)[::-1] for k in range(levels + 2)]
                               + [np.arange((levels + 2) * q, (levels + 2) * q + ONES_ROWS)])
    bwd_m = fwd_m[flip_rows][:, ::-1]
    bwd_k = fwd_k[:, ::-1, ::-1]
    mats2 = np.tile(np.stack([fwd_m, bwd_m]), (1, 1, GLA_SUM_TERMS))
    return (jnp.asarray(mats2, BF16), jnp.asarray(np.stack([fwd_k, bwd_k]), F32), levels)


def _gla_body(levels, layer, q_ref, v_ref, f_ref, lbp_ref, mats_ref, masks_ref, o_ref, st_ref):
    q = HG_CHUNK

    @pl.when(pl.program_id(2) == 0)
    def _():
        st_ref[...] = jnp.zeros_like(st_ref)

    lbp = lbp_ref[...]
    e = jnp.exp(lbp - jnp.max(lbp, axis=0, keepdims=True))
    sm = e / jnp.sum(e, axis=0, keepdims=True)
    lb = jnp.sum(sm[1:layer + 1], axis=0, keepdims=True) if layer > 0 else jnp.zeros((1, lbp.shape[1]), F32)

    f = f_ref[...]
    ef = jnp.exp(-jnp.abs(f))
    log_sig = jnp.minimum(f, 0.0) - _log1p_unit(ef)
    la = jnp.log(lb)
    lc = jnp.log1p(-lb) + log_sig
    logf = jnp.maximum(la, lc) + _log1p_unit(jnp.exp(-jnp.abs(la - lc)))
    key = (1.0 - lb) * (jnp.where(f >= 0.0, ef, 1.0) / (1.0 + ef))
    qs = _silu(q_ref[...])

    sums = _dot_sel(mats_ref[...], logf, GLA_SUM_TERMS)
    q_in = (qs * jnp.exp(sums[:q])).astype(BF16)
    k_out = (key * jnp.exp(sums[q:2 * q])).astype(BF16)
    dec = jnp.exp(sums[(levels + 2) * q:(levels + 2) * q + 1])
    qs_bf, key_bf = qs.astype(BF16), key.astype(BF16)
    q_lv, k_lv = [], []
    for lv in range(levels):
        fac = jnp.exp(sums[(lv + 2) * q:(lv + 3) * q])
        q_lv.append((qs * fac).astype(BF16))
        k_lv.append((key * fac).astype(BF16))
    v_bf = v_ref[...].astype(BF16)

    heads = [slice(h * LANES, (h + 1) * LANES) for h in range(HG_HEADS)]
    attn = masks_ref[0][None] * jnp.stack([_dot_nt(qs_bf[:, sl], key_bf[:, sl]) for sl in heads], axis=0)
    for lv in range(levels):
        attn = attn + masks_ref[lv + 1][None] * jnp.stack(
            [_dot_nt(q_lv[lv][:, sl], k_lv[lv][:, sl]) for sl in heads], axis=0)
    attn = attn.astype(BF16)
    st_bf = st_ref[...].astype(BF16)
    o_ref[...] = jnp.concatenate(
        [_dot(attn[h], v_bf[:, sl]) + _dot_nt(q_in[:, sl], st_bf[h]) for h, sl in enumerate(heads)],
        axis=1).astype(o_ref.dtype)
    for h, sl in enumerate(heads):
        st_ref[h] = st_ref[h] * dec[:, sl] + _dot_tn(v_bf[:, sl], k_out[:, sl])


def _gla_scan(proj, hg_lb, layer):
    q = HG_CHUNK
    n_lat, n_ctx = SEQ // q, CTX_LEN // q
    mats, masks, levels = _gla_consts()

    def chunk(d, b, s):
        return _scan_chunk_index(d, b, s, n_lat, n_ctx)

    return pl.pallas_call(
        functools.partial(_gla_body, levels, layer),
        grid=(2, BATCH, n_lat + n_ctx),
        in_specs=[
            pl.BlockSpec((q, D_MODEL), lambda d, b, s: (chunk(d, b, s), 0)),
            pl.BlockSpec((q, D_MODEL), lambda d, b, s: (chunk(d, b, s), 1)),
            pl.BlockSpec((q, D_MODEL), lambda d, b, s: (chunk(d, b, s), 2 + d)),
            pl.BlockSpec((None, DEPTH, D_MODEL), lambda d, b, s: (d, 0, 0)),
            pl.BlockSpec((None,) + mats.shape[1:], lambda d, b, s: (d, 0, 0)),
            pl.BlockSpec((None,) + masks.shape[1:], lambda d, b, s: (d, 0, 0, 0)),
        ],
        out_specs=pl.BlockSpec((None, q, D_MODEL), lambda d, b, s: (d, chunk(d, b, s), 0)),
        out_shape=jax.ShapeDtypeStruct((2, T_ALL, D_MODEL), BF16),
        scratch_shapes=[pltpu.VMEM((HG_HEADS, LANES, HG_KEY_DIM), F32)],
        compiler_params=_cparams(("arbitrary", "arbitrary", "arbitrary"), 48),
        name="gla_scan",
    )(proj, proj, proj, hg_lb, mats, masks)


def _na_bias_body(rpb_ref, o_ref):
    h = pl.program_id(0)
    n_dc = 2 * NA_WIN_C - 1
    qc = lax.broadcasted_iota(jnp.int32, (GRID_W, LANES), 0)
    kc = lax.broadcasted_iota(jnp.int32, (GRID_W, LANES), 1) % GRID_W
    c0 = jnp.clip(qc - NA_WIN_C // 2, 0, GRID_W - NA_WIN_C)
    col_in = (kc >= c0) & (kc < c0 + NA_WIN_C)
    dc = jnp.clip(kc - qc + NA_WIN_C - 1, 0, n_dc - 1)
    low = lax.broadcasted_iota(jnp.int32, (GRID_W, LANES), 1) < GRID_W
    tiles = []
    for dr in range(2 * NA_WIN_R - 1):
        t = jnp.zeros((GRID_W, LANES), F32)
        for k in range(n_dc):
            t = jnp.where(dc == k, rpb_ref[(h * (2 * NA_WIN_R - 1) + dr) * n_dc + k], t)
        tiles.append(jnp.where(col_in, t, -jnp.inf))
    for dr in range(2 * NA_WIN_R - 2):
        o_ref[dr] = jnp.where(low, tiles[dr], tiles[dr + 1])


def _na_bias_table(rpb):
    n_dr = 2 * NA_WIN_R - 2
    return pl.pallas_call(
        _na_bias_body,
        grid=(NA_HEADS,),
        in_specs=[pl.BlockSpec(memory_space=pltpu.SMEM)],
        out_specs=pl.BlockSpec((None, n_dr, GRID_W, LANES), lambda h: (h, 0, 0, 0)),
        out_shape=jax.ShapeDtypeStruct((NA_HEADS, n_dr, GRID_W, LANES), F32),
        compiler_params=_cparams(("arbitrary",), 32),
        name="na_bias",
    )(rpb.reshape(-1))


def _rope_tables():
    quarter = NA_HEAD_DIM // 4
    inv = ROPE_BASE ** (-np.arange(quarter, dtype=np.float64) / quarter)
    t = np.arange(SEQ)
    row, col = t // GRID_W, t % GRID_W
    ang = np.concatenate([row[:, None] * inv[None], row[:, None] * inv[None],
                          col[:, None] * inv[None], col[:, None] * inv[None]], axis=1)
    sign = np.tile(np.concatenate([-np.ones(quarter), np.ones(quarter)]), 2)[None]
    return jnp.asarray(np.cos(ang), F32), jnp.asarray(np.sin(ang) * sign, F32)


def _rope(v, cos, sin_signed):
    quarter = NA_HEAD_DIM // 4
    lane = lax.broadcasted_iota(jnp.int32, v.shape, 1)
    first = (lane % (2 * quarter)) < quarter
    partner = jnp.where(first, pltpu.roll(v, NA_HEAD_DIM - quarter, axis=1), pltpu.roll(v, quarter, axis=1))
    return v * cos + partner * sin_signed


def _na_body(rows_per_step, q_ref, k_ref, v_ref, kc_ref, vc_ref, cosq_ref, sinq_ref, cos_ref, sin_ref, bias_ref,
             o_ref, kr_ref, vb_ref):
    rb = pl.program_id(2)
    n_rows = SEQ // GRID_W
    scale = NA_HEAD_DIM ** -0.5

    @pl.when(rb == 0)
    def _():
        kr_ref[...] = _rope(k_ref[...], cos_ref[...], sin_ref[...]).astype(BF16)
        vb_ref[...] = v_ref[...].astype(BF16)

    qr = (_rope(q_ref[...], cosq_ref[...], sinq_ref[...]) * scale).astype(BF16)
    kc = kc_ref[...].astype(BF16)
    vc = vc_ref[...].astype(BF16)
    starts, s_rows = [], []
    for lr in range(rows_per_step):
        r = rb * rows_per_step + lr
        r0 = jnp.clip(r - NA_WIN_R // 2, 0, n_rows - NA_WIN_R)
        start = pl.multiple_of(r0 * GRID_W, GRID_W)
        dr0 = r0 - r + NA_WIN_R - 1
        bias = jnp.concatenate([bias_ref[dr0 + 2 * p] for p in range(NA_WIN_R // 2)], axis=1)
        k_win = kr_ref[pl.ds(start, NA_WIN_R * GRID_W), :]
        s_rows.append(_dot_nt(qr[lr * GRID_W:(lr + 1) * GRID_W], k_win) + bias)
        starts.append(start)
    s_loc = jnp.concatenate(s_rows, axis=0)
    s_ctx = _dot_nt(qr, kc)
    m = jnp.maximum(jnp.max(s_loc, axis=-1, keepdims=True), jnp.max(s_ctx, axis=-1, keepdims=True))
    p_loc = jnp.exp(s_loc - m)
    p_ctx = jnp.exp(s_ctx - m)
    denom = jnp.sum(p_loc, axis=-1, keepdims=True) + jnp.sum(p_ctx, axis=-1, keepdims=True)
    p_loc = p_loc.astype(BF16)
    o_loc = jnp.concatenate(
        [_dot(p_loc[lr * GRID_W:(lr + 1) * GRID_W], vb_ref[pl.ds(starts[lr], NA_WIN_R * GRID_W), :])
         for lr in range(rows_per_step)], axis=0)
    o = o_loc + _dot(p_ctx.astype(BF16), vc)
    o_ref[...] = (o / denom).astype(o_ref.dtype)


def _na_attention(qkv, bias_tbl):
    rows_per_step = 16
    tq = rows_per_step * GRID_W
    steps = SEQ // tq
    cos, sin = _rope_tables()
    hd = NA_HEAD_DIM
    ctx_blk0 = T_LAT // CTX_LEN
    return pl.pallas_call(
        functools.partial(_na_body, rows_per_step),
        grid=(BATCH, NA_HEADS, steps),
        in_specs=[
            pl.BlockSpec((tq, hd), lambda b, h, r: (b * steps + r, h)),
            pl.BlockSpec((SEQ, hd), lambda b, h, r: (b, NA_HEADS + h)),
            pl.BlockSpec((SEQ, hd), lambda b, h, r: (b, 2 * NA_HEADS + h)),
            pl.BlockSpec((CTX_LEN, hd), lambda b, h, r: (ctx_blk0 + b, NA_HEADS + h)),
            pl.BlockSpec((CTX_LEN, hd), lambda b, h, r: (ctx_blk0 + b, 2 * NA_HEADS + h)),
            pl.BlockSpec((tq, hd), lambda b, h, r: (r, 0)),
            pl.BlockSpec((tq, hd), lambda b, h, r: (r, 0)),
            pl.BlockSpec((SEQ, hd), lambda b, h, r: (0, 0)),
            pl.BlockSpec((SEQ, hd), lambda b, h, r: (0, 0)),
            pl.BlockSpec((None,) + bias_tbl.shape[1:], lambda b, h, r: (h, 0, 0, 0)),
        ],
        out_specs=pl.BlockSpec((tq, hd), lambda b, h, r: (b * steps + r, h)),
        out_shape=jax.ShapeDtypeStruct((T_LAT, D_MODEL), BF16),
        scratch_shapes=[pltpu.VMEM((SEQ, hd), BF16), pltpu.VMEM((SEQ, hd), BF16)],
        compiler_params=_cparams(("arbitrary", "arbitrary", "arbitrary"), 48),
        name="na_attention",
    )(qkv, qkv, qkv, qkv, qkv, cos, sin, cos, sin, bias_tbl)


def _ctx_attn_body(q_ref, k_ref, v_ref, o_ref):
    scale = NA_HEAD_DIM ** -0.5
    s = _dot_nt((q_ref[...] * scale).astype(BF16), k_ref[...].astype(BF16))
    p = jnp.exp(s - jnp.max(s, axis=-1, keepdims=True))
    o = _dot(p.astype(BF16), v_ref[...].astype(BF16))
    o_ref[...] = (o / jnp.sum(p, axis=-1, keepdims=True)).astype(o_ref.dtype)


def _ctx_attention(qkv):
    hd = NA_HEAD_DIM
    ctx_blk0 = T_LAT // CTX_LEN
    return pl.pallas_call(
        _ctx_attn_body,
        grid=(BATCH, NA_HEADS),
        in_specs=[
            pl.BlockSpec((CTX_LEN, hd), lambda b, h: (ctx_blk0 + b, h)),
            pl.BlockSpec((CTX_LEN, hd), lambda b, h: (ctx_blk0 + b, NA_HEADS + h)),
            pl.BlockSpec((CTX_LEN, hd), lambda b, h: (ctx_blk0 + b, 2 * NA_HEADS + h)),
        ],
        out_specs=pl.BlockSpec((CTX_LEN, hd), lambda b, h: (b, h)),
        out_shape=jax.ShapeDtypeStruct((T_CTX, D_MODEL), BF16),
        compiler_params=_cparams(("arbitrary", "arbitrary"), 32),
        name="ctx_attention",
    )(qkv, qkv, qkv)


def kernel(x, c, ctx, c_ctx, w_mod, b_mod, norm_g, ffn_w_up, ffn_conv_w, ffn_conv_b, ffn_w_down, ssm_w_in, ssm_conv_w, ssm_conv_b, ssm_dt_bias, ssm_a_log, ssm_d, ssm_norm_g, ssm_w_out, hg_w_in, hg_lb, hg_norm_g, hg_w_out, na_w_qkv, na_rpb, na_w_out):
    xs = jnp.concatenate([x.reshape(T_LAT, D_MODEL), ctx.reshape(T_CTX, D_MODEL)], axis=0)
    cvec8 = jnp.concatenate([c, c_ctx[None], jnp.zeros((SUBLANES - BATCH - 1, D_MODEL), F32)], axis=0)
    mod = _adaln(cvec8, w_mod, b_mod)
    ffn_up_bf, ffn_down_bf = ffn_w_up.astype(BF16), ffn_w_down.astype(BF16)
    ssm_out_bf, hg_out_bf, na_out_bf = ssm_w_out.astype(BF16), hg_w_out.astype(BF16), na_w_out.astype(BF16)

    h = _prep(xs, norm_g[0, 0], mod[0])
    for i in range(DEPTH):
        last = i == DEPTH - 1
        kind, slot = i % N_MIXERS, i // N_MIXERS
        mod_l = mod[i]
        n_rows = T_LAT if last else T_ALL
        op = dict(xs=xs, g_res=norm_g[i, 1], g_ffn=norm_g[i, 2], mod_l=mod_l, n_rows=n_rows, slot=slot)
        if kind == 0:
            proj = _in_proj(h, ssm_w_in, slot, 1152, "ssd_in")
            xbc = _ssd_conv(proj, ssm_conv_w[slot], ssm_conv_b[slot])
            y2 = _ssd_scan(xbc, proj, ssm_dt_bias[slot], ssm_a_log[slot], ssm_d[slot])
            xs, hf = _out_proj(_ssd_gate_prologue, [((y2, 0), 0), ((y2, 1), 0), (proj, 0)],
                               ssm_norm_g[slot].reshape(1, SSM_INNER), ssm_out_bf, tm=256, name="ssd_out", **op)
        elif kind == 1:
            proj = _in_proj(h, hg_w_in, slot, 1024, "hg_in")
            o2 = _gla_scan(proj, hg_lb, i)
            xs, hf = _out_proj(_hg_readout_prologue, [((o2, 0), 0), ((o2, 1), 0), (proj, 4)],
                               hg_norm_g[slot].reshape(1, D_MODEL), hg_out_bf, tm=ROW_TILE, name="hg_out", **op)
        else:
            qkv = _in_proj(h, na_w_qkv, slot, 1024, "na_in")
            o = jnp.concatenate([_na_attention(qkv, _na_bias_table(na_rpb[slot])), _ctx_attention(qkv)], axis=0)
            xs, hf = _out_proj(_plain_prologue, [(o, 0)], None, na_out_bf, tm=ROW_TILE, name="na_out", **op)
        nxt = min(i + 1, DEPTH - 1)
        xs, h = _ffn(hf, xs, i, ffn_up_bf, ffn_conv_w, ffn_conv_b, ffn_down_bf, norm_g[i, 3], mod_l,
                     norm_g[nxt, 0], mod[nxt], n_rows=n_rows, emit_h=not last)

    return xs.reshape(BATCH, SEQ, D_MODEL)
```

```python
import functools

import numpy as np
import jax
import jax.numpy as jnp
from jax import lax
from jax.experimental import pallas as pl
from jax.experimental.pallas import tpu as pltpu

F32 = jnp.float32
BF16 = jnp.bfloat16

D_MODEL = 2048
BATCH = 2
SEQ = 4096
DEPTH = 4
GRID_W = 64
CTX_LEN = 256
N_MIXERS = 3
RMS_EPS = 1e-6
SSM_INNER = 4096
SSM_HEAD_DIM = 64
SSM_HEADS = 64
SSM_STATE = 128
SSM_GROUPS = 8
SSM_CONV_DIM = SSM_INNER + 2 * SSM_GROUPS * SSM_STATE
SSM_IN_DIM = SSM_INNER + SSM_CONV_DIM + 2 * SSM_HEADS
SSM_CHUNK = 128
HG_HEADS = 16
HG_KEY_DIM = 128
HG_IN_DIM = 5 * D_MODEL
HG_CHUNK = 64
NA_HEADS = 16
NA_HEAD_DIM = 128
NA_WIN_R = 8
NA_WIN_C = 16
ROPE_BASE = 10000.0
FFN_HIDDEN = 5632

T_LAT = BATCH * SEQ
T_CTX = BATCH * CTX_LEN
T_ALL = T_LAT + T_CTX
LANES = 128
SUBLANES = 8
BF16_ROWS = 16
ONES_ROWS = 16
ROW_TILE = 512
FFN_COL_CHUNK = 512
MIB = 1024 * 1024
LOG2E = 1.4426950408889634


def _cparams(sem, vmem_mib):
    return pltpu.CompilerParams(dimension_semantics=sem, vmem_limit_bytes=vmem_mib * MIB)


def _silu(v):
    return v * jax.nn.sigmoid(v)


def _rms(v):
    return v * lax.rsqrt(jnp.mean(v * v, axis=-1, keepdims=True) + RMS_EPS)


def _log1p_unit(e):
    return jnp.log(1.0 + e)


def _softplus(v):
    return jnp.maximum(v, 0.0) + _log1p_unit(jnp.exp(-jnp.abs(v)))


def _dot(a, b):
    return jnp.dot(a, b, preferred_element_type=F32)


def _dot_nt(a, b):
    return lax.dot_general(a, b, (((1,), (1,)), ((), ())), preferred_element_type=F32)


def _dot_tn(a, b):
    return lax.dot_general(a, b, (((0,), (0,)), ((), ())), preferred_element_type=F32)


def _bf16_pieces(v, terms):
    pieces = []
    rem = v
    for _ in range(terms):
        part = rem.astype(BF16)
        rem = rem - part.astype(F32)
        pieces.append(part)
    return pieces


def _dot_sel(m01_rep, v, terms):
    return _dot(m01_rep, jnp.concatenate(_bf16_pieces(v, terms), axis=0))


def _dot_sel_rhs(v, m01_rep, terms):
    return _dot(jnp.concatenate(_bf16_pieces(v, terms), axis=1), m01_rep)


def _mod_row(row0):
    return jnp.where(row0 >= T_LAT, 2, row0 // SEQ)


def _mod_vec(mod_ref, r, idx):
    return mod_ref[pl.ds(r, 1), idx * D_MODEL:(idx + 1) * D_MODEL]


def _norm_mod(x, g, mod_ref, r, shift_idx, scale_idx):
    return _rms(x) * g * (1.0 + _mod_vec(mod_ref, r, scale_idx)) + _mod_vec(mod_ref, r, shift_idx)


def _seg_edges(row0, rows):
    gr = row0 + lax.broadcasted_iota(jnp.int32, (rows, 1), 0)
    pos = jnp.where(gr >= T_LAT, (gr - T_LAT) % CTX_LEN, gr % SEQ)
    length = jnp.where(gr >= T_LAT, CTX_LEN, SEQ)
    return pos == 0, pos == length - 1


def _shift_rows(cur, prev_row, next_row, row0):
    rows = cur.shape[0]
    first, last = _seg_edges(row0, rows)
    ridx = lax.broadcasted_iota(jnp.int32, (rows, 1), 0)
    up = jnp.where(ridx == 0, prev_row, pltpu.roll(cur, 1, axis=0))
    up = jnp.where(first, 0.0, up)
    dn = jnp.where(ridx == rows - 1, next_row, pltpu.roll(cur, rows - 1, axis=0))
    dn = jnp.where(last, 0.0, dn)
    return up, dn


def _token_conv(cur, prev_row, next_row, row0, w_ref, b_ref):
    up, dn = _shift_rows(cur, prev_row, next_row, row0)
    return b_ref[...] + up * w_ref[0:1, :] + cur * w_ref[1:2, :] + dn * w_ref[2:3, :]


def _halo_specs(tm, halo, width, n_rows, col_fn):
    per = tm // halo
    last_blk = n_rows // halo - 1
    prev = pl.BlockSpec((halo, width), lambda i, j: (jnp.maximum(i * per - 1, 0), col_fn(j)))
    nxt = pl.BlockSpec((halo, width), lambda i, j: (jnp.minimum((i + 1) * per, last_blk), col_fn(j)))
    return prev, nxt


def _adaln_body(c_ref, w_ref, b_ref, o_ref):
    s = _silu(c_ref[...]).astype(BF16)
    o_ref[...] = _dot(s, w_ref[...].astype(BF16)) + b_ref[...]


def _adaln(cvec8, w_mod, b_mod):
    tn = 1024
    n_out = 6 * D_MODEL
    return pl.pallas_call(
        _adaln_body,
        grid=(DEPTH, n_out // tn),
        in_specs=[
            pl.BlockSpec((SUBLANES, D_MODEL), lambda l, j: (0, 0)),
            pl.BlockSpec((None, D_MODEL, tn), lambda l, j: (l, 0, j)),
            pl.BlockSpec((None, 1, tn), lambda l, j: (l, 0, j)),
        ],
        out_specs=pl.BlockSpec((None, SUBLANES, tn), lambda l, j: (l, 0, j)),
        out_shape=jax.ShapeDtypeStruct((DEPTH, SUBLANES, n_out), F32),
        compiler_params=_cparams(("arbitrary", "arbitrary"), 40),
        name="adaln",
    )(cvec8, w_mod, b_mod.reshape(DEPTH, 1, n_out))


def _prep_body(lat_ref, ctx_ref, g_ref, mod_ref, xs_ref, h_ref):
    row0 = pl.program_id(0) * ROW_TILE

    def emit(src_ref):
        xv = src_ref[...]
        xs_ref[...] = xv
        h_ref[...] = _norm_mod(xv, g_ref[...], mod_ref, _mod_row(row0), 0, 1).astype(BF16)

    @pl.when(row0 < T_LAT)
    def _():
        emit(lat_ref)

    @pl.when(row0 >= T_LAT)
    def _():
        emit(ctx_ref)


def _prep(x_lat, x_ctx, g_row, mod_l):
    assert T_CTX == ROW_TILE
    n_lat = T_LAT // ROW_TILE
    row = pl.BlockSpec((ROW_TILE, D_MODEL), lambda i: (i, 0))
    return pl.pallas_call(
        _prep_body,
        grid=(T_ALL // ROW_TILE,),
        in_specs=[pl.BlockSpec((ROW_TILE, D_MODEL), lambda i: (jnp.minimum(i, n_lat - 1), 0)),
                  pl.BlockSpec((ROW_TILE, D_MODEL), lambda i: (0, 0)),
                  pl.BlockSpec((1, D_MODEL), lambda i: (0, 0)), pl.BlockSpec(mod_l.shape, lambda i: (0, 0))],
        out_specs=[row, row],
        out_shape=[jax.ShapeDtypeStruct((T_ALL, D_MODEL), F32), jax.ShapeDtypeStruct((T_ALL, D_MODEL), BF16)],
        compiler_params=_cparams(("arbitrary",), 40),
        name="prep",
    )(x_lat, x_ctx, g_row.reshape(1, D_MODEL), mod_l)


def _in_proj_body(h_ref, w_ref, o_ref, wb_ref):
    @pl.when(pl.program_id(1) == 0)
    def _():
        wb_ref[...] = w_ref[...].astype(BF16)

    o_ref[...] = _dot(h_ref[...], wb_ref[...])


def _in_proj(h, w_stack, slot, tn, name):
    tm = T_ALL // 8
    n_dim = w_stack.shape[2]
    return pl.pallas_call(
        _in_proj_body,
        grid=(n_dim // tn, T_ALL // tm),
        in_specs=[pl.BlockSpec((tm, D_MODEL), lambda j, i: (i, 0)),
                  pl.BlockSpec((None, D_MODEL, tn), lambda j, i: (slot, 0, j))],
        out_specs=pl.BlockSpec((tm, tn), lambda j, i: (i, j)),
        out_shape=jax.ShapeDtypeStruct((T_ALL, n_dim), F32),
        scratch_shapes=[pltpu.VMEM((D_MODEL, tn), BF16)],
        compiler_params=_cparams(("arbitrary", "arbitrary"), 56),
        name=name,
    )(h, w_stack)


def _ssd_gate_prologue(yf_ref, yb_ref, z_ref, g_ref):
    v = (yf_ref[...].astype(F32) + yb_ref[...].astype(F32)) * _silu(z_ref[...])
    return v * g_ref[...], lax.rsqrt(jnp.mean(v * v, axis=-1, keepdims=True) + RMS_EPS)


def _hg_readout_prologue(of_ref, ob_ref, gate_ref, g_ref):
    o = of_ref[...].astype(F32) + ob_ref[...].astype(F32)
    parts = [_rms(o[:, h * LANES:(h + 1) * LANES]) for h in range(o.shape[1] // LANES)]
    return jnp.concatenate(parts, axis=-1) * g_ref[...] * _silu(gate_ref[...]), None


def _plain_prologue(o_ref):
    return o_ref[...], None


def _out_proj_body(prologue, n_in, tm, *refs):
    ins = refs[:n_in]
    w_ref, x_ref, g1_ref, g2_ref, mod_ref, xo_ref, ho_ref = refs[n_in:]
    part, row_scale = prologue(*ins)
    y = _dot(part.astype(BF16), w_ref[...])
    if row_scale is not None:
        y = y * row_scale
    r = _mod_row(pl.program_id(0) * tm)
    xn = x_ref[...] + _mod_vec(mod_ref, r, 2) * (_rms(y) * g1_ref[...])
    xo_ref[...] = xn
    ho_ref[...] = _norm_mod(xn, g2_ref[...], mod_ref, r, 3, 4).astype(BF16)


def _out_proj(prologue, row_ins, gain, w_stack, slot, xs, g_res, g_ffn, mod_l, *, n_rows, tm, name):
    k_dim = w_stack.shape[1]
    in_specs, args = [], []
    for arr, cb in row_ins:
        if isinstance(arr, tuple):
            in_specs.append(pl.BlockSpec((None, tm, k_dim), functools.partial(
                lambda i, ld, cb: (ld, i, cb), ld=arr[1], cb=cb)))
            args.append(arr[0])
        else:
            in_specs.append(pl.BlockSpec((tm, k_dim), functools.partial(lambda i, cb: (i, cb), cb=cb)))
            args.append(arr)
    if gain is not None:
        in_specs.append(pl.BlockSpec((1, k_dim), lambda i: (0, 0)))
        args.append(gain)
    n_in = len(args)
    row = pl.BlockSpec((tm, D_MODEL), lambda i: (i, 0))
    vec = pl.BlockSpec((1, D_MODEL), lambda i: (0, 0))
    in_specs += [pl.BlockSpec((None, k_dim, D_MODEL), lambda i: (slot, 0, 0), pipeline_mode=pl.Buffered(1)),
                 row, vec, vec, pl.BlockSpec(mod_l.shape, lambda i: (0, 0))]
    return pl.pallas_call(
        functools.partial(_out_proj_body, prologue, n_in, tm),
        grid=(n_rows // tm,),
        in_specs=in_specs,
        out_specs=[row, row],
        out_shape=[jax.ShapeDtypeStruct((n_rows, D_MODEL), F32), jax.ShapeDtypeStruct((n_rows, D_MODEL), BF16)],
        compiler_params=_cparams(("arbitrary",), 60),
        name=name,
    )(*args, w_stack, xs, g_res.reshape(1, D_MODEL), g_ffn.reshape(1, D_MODEL), mod_l)


def _ffn_body(emit_h, has_ctx, h_ref, hp_ref, hn_ref, wa_ref, wv_ref, cwa_ref, cwv_ref, cba_ref, cbv_ref, wd_ref,
              x_ref, g_res_ref, mod_ref, g_next_ref, modn_ref, *out_and_scratch):
    if emit_h:
        xo_ref, ho_ref, hall_ref, acc_ref = out_and_scratch
    else:
        xo_ref, hall_ref, acc_ref = out_and_scratch
    k = pl.program_id(1)
    tm = ROW_TILE
    row0 = pl.program_id(0) * tm
    pad = BF16_ROWS
    rows = tm + 2 * pad
    s8 = SUBLANES

    @pl.when(k == 0)
    def _():
        hall_ref[pl.ds(0, pad), :] = hp_ref[...]
        hall_ref[pl.ds(pad, tm), :] = h_ref[...]
        hall_ref[pl.ds(pad + tm, pad), :] = hn_ref[...]

    hall = hall_ref[...]
    edge_first = row0 % SEQ == 0
    edge_last = ((row0 + tm) % SEQ == 0) | (row0 + tm == T_ALL)
    is_ctx = row0 >= T_LAT
    joints = list(range(CTX_LEN, tm, CTX_LEN)) if has_ctx else []
    sub = lax.broadcasted_iota(jnp.int32, (s8, 1), 0)

    def patch(v, start, cond):
        return jnp.concatenate([v[:start], jnp.where(cond, 0.0, v[start:start + s8]), v[start + s8:]], axis=0)

    def conv(w_ref, cw_ref, cb_ref, cs):
        u = _dot(hall, w_ref[:, cs])
        u = patch(u, pad - s8, edge_first)
        u = patch(u, pad + tm, edge_last)
        up = pltpu.roll(u, 1, axis=0)[pad:pad + tm]
        dn = pltpu.roll(u, rows - 1, axis=0)[pad:pad + tm]
        for j in joints:
            up = patch(up, j, is_ctx & (sub == 0))
            dn = patch(dn, j - s8, is_ctx & (sub == s8 - 1))
        return cb_ref[:, cs] + up * cw_ref[0:1, cs] + u[pad:pad + tm] * cw_ref[1:2, cs] + dn * cw_ref[2:3, cs]

    chunks = [pl.ds(c * FFN_COL_CHUNK, FFN_COL_CHUNK) for c in range(wa_ref.shape[1] // FFN_COL_CHUNK)]
    acts = [(_silu(conv(wa_ref, cwa_ref, cba_ref, cs)) * conv(wv_ref, cwv_ref, cbv_ref, cs)).astype(BF16)
            for cs in chunks]
    contrib = _dot(jnp.concatenate(acts, axis=1), wd_ref[...])

    @pl.when(k == 0)
    def _():
        acc_ref[...] = contrib

    @pl.when(k != 0)
    def _():
        acc_ref[...] += contrib

    @pl.when(k == pl.num_programs(1) - 1)
    def _():
        r = _mod_row(row0)
        xn = x_ref[...] + _mod_vec(mod_ref, r, 5) * (_rms(acc_ref[...]) * g_res_ref[...])
        xo_ref[...] = xn
        if emit_h:
            ho_ref[...] = _norm_mod(xn, g_next_ref[...], modn_ref, r, 0, 1).astype(BF16)


def _ffn(h, xs, layer, w_up, conv_w, conv_b, w_down, g_res, mod_l, g_next, mod_next, *, n_rows, emit_h):
    tm, tk = ROW_TILE, 512
    kt = FFN_HIDDEN // tk
    h_prev, h_next = _halo_specs(tm, BF16_ROWS, D_MODEL, n_rows, lambda k: 0)
    row = pl.BlockSpec((tm, D_MODEL), lambda i, k: (i, 0))
    vec = pl.BlockSpec((1, D_MODEL), lambda i, k: (0, 0))
    tab = pl.BlockSpec(mod_l.shape, lambda i, k: (0, 0))
    out_specs = [row, row] if emit_h else [row]
    out_shape = [jax.ShapeDtypeStruct((n_rows, D_MODEL), F32)]
    if emit_h:
        out_shape.append(jax.ShapeDtypeStruct((n_rows, D_MODEL), BF16))
    conv_b3 = conv_b.reshape(DEPTH, 1, 2 * FFN_HIDDEN)

    res = pl.pallas_call(
        functools.partial(_ffn_body, emit_h, n_rows > T_LAT),
        grid=(n_rows // tm, kt),
        in_specs=[
            row, h_prev, h_next,
            pl.BlockSpec((None, D_MODEL, tk), lambda i, k: (layer, 0, k)),
            pl.BlockSpec((None, D_MODEL, tk), lambda i, k: (layer, 0, k + kt)),
            pl.BlockSpec((None, 3, tk), lambda i, k: (layer, 0, k)),
            pl.BlockSpec((None, 3, tk), lambda i, k: (layer, 0, k + kt)),
            pl.BlockSpec((None, 1, tk), lambda i, k: (layer, 0, k)),
            pl.BlockSpec((None, 1, tk), lambda i, k: (layer, 0, k + kt)),
            pl.BlockSpec((None, tk, D_MODEL), lambda i, k: (layer, k, 0)),
            row, vec, tab, vec, tab,
        ],
        out_specs=out_specs,
        out_shape=out_shape,
        scratch_shapes=[pltpu.VMEM((tm + 2 * BF16_ROWS, D_MODEL), BF16), pltpu.VMEM((tm, D_MODEL), F32)],
        compiler_params=_cparams(("arbitrary", "arbitrary"), 56),
        name="conv_ffn",
    )(h, h, h, w_up, w_up, conv_w, conv_w, conv_b3, conv_b3, w_down, xs, g_res.reshape(1, D_MODEL), mod_l,
      g_next.reshape(1, D_MODEL), mod_next)
    return (res[0], res[1]) if emit_h else (res[0], None)


def _ssd_conv_body(tm, cur_ref, prev_ref, next_ref, w_ref, b_ref, o_ref):
    u = _token_conv(cur_ref[...], prev_ref[SUBLANES - 1:SUBLANES, :], next_ref[0:1, :], pl.program_id(0) * tm,
                    w_ref, b_ref)
    o_ref[...] = _silu(u).astype(o_ref.dtype)


def _ssd_conv(proj, conv_w, conv_b):
    tm, tc = 512, 2048
    col0 = SSM_INNER // tc
    prev, nxt = _halo_specs(tm, SUBLANES, tc, T_ALL, lambda j: j + col0)
    return pl.pallas_call(
        functools.partial(_ssd_conv_body, tm),
        grid=(T_ALL // tm, SSM_CONV_DIM // tc),
        in_specs=[pl.BlockSpec((tm, tc), lambda i, j: (i, j + col0)), prev, nxt,
                  pl.BlockSpec((3, tc), lambda i, j: (0, j)), pl.BlockSpec((1, tc), lambda i, j: (0, j))],
        out_specs=pl.BlockSpec((tm, tc), lambda i, j: (i, j)),
        out_shape=jax.ShapeDtypeStruct((T_ALL, SSM_CONV_DIM), BF16),
        compiler_params=_cparams(("arbitrary", "arbitrary"), 40),
        name="ssd_conv",
    )(proj, proj, proj, conv_w, conv_b.reshape(1, -1))


def _scan_chunk_index(d, b, s, n_lat, n_ctx):
    ctx0 = BATCH * n_lat + b * n_ctx
    fwd = jnp.where(s < n_ctx, ctx0 + s, b * n_lat + s - n_ctx)
    bwd = jnp.where(s < n_ctx, ctx0 + n_ctx - 1 - s, b * n_lat + n_lat - 1 - (s - n_ctx))
    return jnp.where(d == 0, fwd, bwd)


SSD_CUM_TERMS = 3
SSD_DT_TERMS = 2


def _ssd_scan_consts():
    q = SSM_CHUNK
    i = np.arange(q)
    pfx = np.zeros((2, q + ONES_ROWS, q), np.float32)
    pfx[0, :q] = i[:, None] >= i[None, :]
    pfx[1, :q] = i[:, None] <= i[None, :]
    pfx[:, q:] = 1.0
    pfx = np.tile(pfx, (1, 1, SSD_CUM_TERMS))
    expand = np.repeat(np.eye(SSM_HEADS, dtype=np.float32), SSM_HEAD_DIM, axis=1)
    expand = np.tile(expand, (SSD_DT_TERMS, 1))
    return jnp.asarray(pfx, BF16), jnp.asarray(expand, BF16)


def _ssd_scan_body(x_ref, b_ref, c_ref, dtraw_ref, dtb_ref, alog_ref, dskip_ref, pfx_ref, exp_ref, y_ref, s_ref):
    q = SSM_CHUNK
    hp = 2 * SSM_HEAD_DIM
    d = pl.program_id(0)

    @pl.when(pl.program_id(2) == 0)
    def _():
        s_ref[...] = jnp.zeros_like(s_ref)

    def pick(v):
        return jnp.where(d == 0, v[:, :SSM_HEADS], v[:, SSM_HEADS:])

    dt2 = _softplus(dtraw_ref[...] + dtb_ref[...])
    cum2 = _dot_sel(pfx_ref[...], dt2 * (-LOG2E * jnp.exp(alog_ref[...])), SSD_CUM_TERMS)
    acum2 = cum2[:q]
    acum = pick(acum2)
    tot = pick(cum2[q:q + SUBLANES])[0:1]
    acum_t2 = acum2.T
    acum_t = jnp.where(d == 0, acum_t2[:SSM_HEADS], acum_t2[SSM_HEADS:])
    dt_wide = _dot_sel_rhs(pick(dt2), exp_ref[...], SSD_DT_TERMS)

    ii = lax.broadcasted_iota(jnp.int32, (q, q), 0)
    jj = lax.broadcasted_iota(jnp.int32, (q, q), 1)
    causal = (jj - ii) * (1 - 2 * d) <= 0
    low_half = lax.broadcasted_iota(jnp.int32, (q, hp), 1) < SSM_HEAD_DIM

    heads_per_group = SSM_HEADS // SSM_GROUPS
    n_pairs = SSM_HEADS // 2

    b_gs = [b_ref[:, g * SSM_STATE:(g + 1) * SSM_STATE].astype(BF16) for g in range(SSM_GROUPS)]
    c_bf = [c_ref[:, g * SSM_STATE:(g + 1) * SSM_STATE].astype(BF16) for g in range(SSM_GROUPS)]
    c_gs = [c.astype(F32) for c in c_bf]
    cbs = [jnp.where(causal, _dot_nt(c_bf[g], b_gs[g]), 0.0) for g in range(SSM_GROUPS)]

    cols, lhs = [], []
    for h in range(SSM_HEADS):
        g = h // heads_per_group
        col = jnp.broadcast_to(acum[:, h:h + 1], (q, q))
        seg = jnp.where(causal, col - acum_t[h:h + 1, :], 0.0)
        scores = jnp.exp2(seg) * cbs[g]
        c_dec = c_gs[g] * jnp.exp2(col)
        lhs.append(jnp.concatenate([scores.astype(BF16), c_dec.astype(BF16)], axis=1))
        cols.append(col)

    xdts = []
    for p in range(n_pairs):
        lo = p * hp
        x_pair = x_ref[:, lo:lo + hp].astype(F32)
        xdt = x_pair * dt_wide[:, lo:lo + hp]
        rhs = jnp.concatenate([xdt.astype(BF16), s_ref[:, lo:lo + hp].astype(BF16)], axis=0)
        y_pair = jnp.where(low_half, _dot(lhs[2 * p], rhs), _dot(lhs[2 * p + 1], rhs))
        y_ref[:, lo:lo + hp] = (y_pair + dskip_ref[:, lo:lo + hp] * x_pair).astype(y_ref.dtype)
        xdts.append(xdt)

    for p in range(n_pairs):
        lo = p * hp
        g = (2 * p) // heads_per_group
        tot_pair = jnp.where(low_half[0:1], jnp.broadcast_to(tot[:, 2 * p:2 * p + 1], (1, hp)),
                             jnp.broadcast_to(tot[:, 2 * p + 1:2 * p + 2], (1, hp)))
        to_end = jnp.exp2(tot_pair - jnp.where(low_half, cols[2 * p], cols[2 * p + 1]))
        s_ref[:, lo:lo + hp] = (s_ref[:, lo:lo + hp] * jnp.exp2(tot_pair)
                                + _dot_tn(b_gs[g], (xdts[p] * to_end).astype(BF16)))


def _ssd_scan(xbc, proj, dt_bias, a_log, d_skip):
    q = SSM_CHUNK
    n_lat, n_ctx = SEQ // q, CTX_LEN // q
    pfx, expand = _ssd_scan_consts()
    n_heads2 = 2 * SSM_HEADS
    dt_col = (SSM_INNER + SSM_CONV_DIM) // n_heads2
    b_col = SSM_INNER // (SSM_GROUPS * SSM_STATE)
    d_wide = jnp.repeat(d_skip, SSM_HEAD_DIM, axis=1).reshape(2, 1, SSM_INNER)

    def chunk(d, b, s):
        return _scan_chunk_index(d, b, s, n_lat, n_ctx)

    return pl.pallas_call(
        _ssd_scan_body,
        grid=(2, BATCH, n_lat + n_ctx),
        in_specs=[
            pl.BlockSpec((q, SSM_INNER), lambda d, b, s: (chunk(d, b, s), 0)),
            pl.BlockSpec((q, SSM_GROUPS * SSM_STATE), lambda d, b, s: (chunk(d, b, s), b_col)),
            pl.BlockSpec((q, SSM_GROUPS * SSM_STATE), lambda d, b, s: (chunk(d, b, s), b_col + 1)),
            pl.BlockSpec((q, n_heads2), lambda d, b, s: (chunk(d, b, s), dt_col)),
            pl.BlockSpec((1, n_heads2), lambda d, b, s: (0, 0)),
            pl.BlockSpec((1, n_heads2), lambda d, b, s: (0, 0)),
            pl.BlockSpec((None, 1, SSM_INNER), lambda d, b, s: (d, 0, 0)),
            pl.BlockSpec((None,) + pfx.shape[1:], lambda d, b, s: (d, 0, 0)),
            pl.BlockSpec(expand.shape, lambda d, b, s: (0, 0)),
        ],
        out_specs=pl.BlockSpec((None, q, SSM_INNER), lambda d, b, s: (d, chunk(d, b, s), 0)),
        out_shape=jax.ShapeDtypeStruct((2, T_ALL, SSM_INNER), BF16),
        scratch_shapes=[pltpu.VMEM((SSM_STATE, SSM_INNER), F32)],
        compiler_params=_cparams(("arbitrary", "arbitrary", "arbitrary"), 48),
        name="ssd_scan",
    )(xbc, xbc, xbc, proj, dt_bias.reshape(1, n_heads2), a_log.reshape(1, n_heads2), d_wide, pfx, expand)


GLA_SUM_TERMS = 2


def _gla_consts():
    q = HG_CHUNK
    levels = int(np.log2(q))
    i = np.arange(q)
    mats = [(i[:, None] >= i[None, :]).astype(np.float32), (i[None, :] > i[:, None]).astype(np.float32)]
    masks = [np.eye(q, dtype=np.float32)]
    for lv in range(levels):
        s = 1 << lv
        blk, pos = i // (2 * s), i % (2 * s)
        bound = blk * 2 * s + s - 1
        right = pos >= s
        t = i[None, :]
        m_right = right[:, None] & (t > bound[:, None]) & (t <= i[:, None])
        m_left = (~right)[:, None] & (t > i[:, None]) & (t <= bound[:, None])
        mats.append((m_right | m_left).astype(np.float32))
        masks.append((right[:, None] & (~right)[None, :] & (blk[:, None] == blk[None, :])).astype(np.float32))
    mats.append(np.ones((ONES_ROWS, q), np.float32))
    fwd_m, fwd_k = np.concatenate(mats, axis=0), np.stack(masks)
    flip_rows = np.concatenate([np.arange(k * q, (k + 1) * q)[::-1] for k in range(levels + 2)]
                               + [np.arange((levels + 2) * q, (levels + 2) * q + ONES_ROWS)])
    bwd_m = fwd_m[flip_rows][:, ::-1]
    bwd_k = fwd_k[:, ::-1, ::-1]
    mats2 = np.tile(np.stack([fwd_m, bwd_m]), (1, 1, GLA_SUM_TERMS))
    return (jnp.asarray(mats2, BF16), jnp.asarray(np.stack([fwd_k, bwd_k]), F32), levels)


def _gla_body(levels, layer, q_ref, v_ref, f_ref, lbp_ref, mats_ref, masks_ref, o_ref, st_ref):
    q = HG_CHUNK

    @pl.when(pl.program_id(2) == 0)
    def _():
        st_ref[...] = jnp.zeros_like(st_ref)

    lbp = lbp_ref[...]
    e = jnp.exp(lbp - jnp.max(lbp, axis=0, keepdims=True))
    sm = e / jnp.sum(e, axis=0, keepdims=True)
    lb = jnp.sum(sm[1:layer + 1], axis=0, keepdims=True) if layer > 0 else jnp.zeros((1, lbp.shape[1]), F32)

    f = f_ref[...]
    ef = jnp.exp(-jnp.abs(f))
    log_sig = jnp.minimum(f, 0.0) - _log1p_unit(ef)
    la = jnp.log(lb)
    lc = jnp.log1p(-lb) + log_sig
    logf = jnp.maximum(la, lc) + _log1p_unit(jnp.exp(-jnp.abs(la - lc)))
    key = (1.0 - lb) * (jnp.where(f >= 0.0, ef, 1.0) / (1.0 + ef))
    qs = _silu(q_ref[...])

    sums = _dot_sel(mats_ref[...], logf, GLA_SUM_TERMS)
    q_in = (qs * jnp.exp(sums[:q])).astype(BF16)
    k_out = (key * jnp.exp(sums[q:2 * q])).astype(BF16)
    dec = jnp.exp(sums[(levels + 2) * q:(levels + 2) * q + 1])
    qs_bf, key_bf = qs.astype(BF16), key.astype(BF16)
    q_lv, k_lv = [], []
    for lv in range(levels):
        fac = jnp.exp(sums[(lv + 2) * q:(lv + 3) * q])
        q_lv.append((qs * fac).astype(BF16))
        k_lv.append((key * fac).astype(BF16))
    v_bf = v_ref[...].astype(BF16)

    heads = [slice(h * LANES, (h + 1) * LANES) for h in range(HG_HEADS)]
    attn = masks_ref[0][None] * jnp.stack([_dot_nt(qs_bf[:, sl], key_bf[:, sl]) for sl in heads], axis=0)
    for lv in range(levels):
        attn = attn + masks_ref[lv + 1][None] * jnp.stack(
            [_dot_nt(q_lv[lv][:, sl], k_lv[lv][:, sl]) for sl in heads], axis=0)
    attn = attn.astype(BF16)
    st_bf = st_ref[...].astype(BF16)
    o_ref[...] = jnp.concatenate(
        [_dot(attn[h], v_bf[:, sl]) + _dot_nt(q_in[:, sl], st_bf[h]) for h, sl in enumerate(heads)],
        axis=1).astype(o_ref.dtype)
    for h, sl in enumerate(heads):
        st_ref[h] = st_ref[h] * dec[:, sl] + _dot_tn(v_bf[:, sl], k_out[:, sl])


def _gla_scan(proj, hg_lb, layer):
    q = HG_CHUNK
    n_lat, n_ctx = SEQ // q, CTX_LEN // q
    mats, masks, levels = _gla_consts()

    def chunk(d, b, s):
        return _scan_chunk_index(d, b, s, n_lat, n_ctx)

    return pl.pallas_call(
        functools.partial(_gla_body, levels, layer),
        grid=(2, BATCH, n_lat + n_ctx),
        in_specs=[
            pl.BlockSpec((q, D_MODEL), lambda d, b, s: (chunk(d, b, s), 0)),
            pl.BlockSpec((q, D_MODEL), lambda d, b, s: (chunk(d, b, s), 1)),
            pl.BlockSpec((q, D_MODEL), lambda d, b, s: (chunk(d, b, s), 2 + d)),
            pl.BlockSpec((None, DEPTH, D_MODEL), lambda d, b, s: (d, 0, 0)),
            pl.BlockSpec((None,) + mats.shape[1:], lambda d, b, s: (d, 0, 0)),
            pl.BlockSpec((None,) + masks.shape[1:], lambda d, b, s: (d, 0, 0, 0)),
        ],
        out_specs=pl.BlockSpec((None, q, D_MODEL), lambda d, b, s: (d, chunk(d, b, s), 0)),
        out_shape=jax.ShapeDtypeStruct((2, T_ALL, D_MODEL), BF16),
        scratch_shapes=[pltpu.VMEM((HG_HEADS, LANES, HG_KEY_DIM), F32)],
        compiler_params=_cparams(("arbitrary", "arbitrary", "arbitrary"), 48),
        name="gla_scan",
    )(proj, proj, proj, hg_lb, mats, masks)


def _na_bias_body(rpb_ref, o_ref):
    h = pl.program_id(0)
    n_dc = 2 * NA_WIN_C - 1
    qc = lax.broadcasted_iota(jnp.int32, (GRID_W, LANES), 0)
    kc = lax.broadcasted_iota(jnp.int32, (GRID_W, LANES), 1) % GRID_W
    c0 = jnp.clip(qc - NA_WIN_C // 2, 0, GRID_W - NA_WIN_C)
    col_in = (kc >= c0) & (kc < c0 + NA_WIN_C)
    dc = jnp.clip(kc - qc + NA_WIN_C - 1, 0, n_dc - 1)
    low = lax.broadcasted_iota(jnp.int32, (GRID_W, LANES), 1) < GRID_W
    tiles = []
    for dr in range(2 * NA_WIN_R - 1):
        t = jnp.zeros((GRID_W, LANES), F32)
        for k in range(n_dc):
            t = jnp.where(dc == k, rpb_ref[(h * (2 * NA_WIN_R - 1) + dr) * n_dc + k], t)
        tiles.append(jnp.where(col_in, t, -jnp.inf))
    for dr in range(2 * NA_WIN_R - 2):
        o_ref[dr] = jnp.where(low, tiles[dr], tiles[dr + 1])


def _na_bias_table(rpb):
    n_dr = 2 * NA_WIN_R - 2
    return pl.pallas_call(
        _na_bias_body,
        grid=(NA_HEADS,),
        in_specs=[pl.BlockSpec(memory_space=pltpu.SMEM)],
        out_specs=pl.BlockSpec((None, n_dr, GRID_W, LANES), lambda h: (h, 0, 0, 0)),
        out_shape=jax.ShapeDtypeStruct((NA_HEADS, n_dr, GRID_W, LANES), F32),
        compiler_params=_cparams(("arbitrary",), 32),
        name="na_bias",
    )(rpb.reshape(-1))


def _rope_tables():
    quarter = NA_HEAD_DIM // 4
    inv = ROPE_BASE ** (-np.arange(quarter, dtype=np.float64) / quarter)
    t = np.arange(SEQ)
    row, col = t // GRID_W, t % GRID_W
    ang = np.concatenate([row[:, None] * inv[None], row[:, None] * inv[None],
                          col[:, None] * inv[None], col[:, None] * inv[None]], axis=1)
    sign = np.tile(np.concatenate([-np.ones(quarter), np.ones(quarter)]), 2)[None]
    return jnp.asarray(np.cos(ang), F32), jnp.asarray(np.sin(ang) * sign, F32)


def _rope(v, cos, sin_signed):
    quarter = NA_HEAD_DIM // 4
    lane = lax.broadcasted_iota(jnp.int32, v.shape, 1)
    first = (lane % (2 * quarter)) < quarter
    partner = jnp.where(first, pltpu.roll(v, NA_HEAD_DIM - quarter, axis=1), pltpu.roll(v, quarter, axis=1))
    return v * cos + partner * sin_signed


def _na_body(rows_per_step, q_ref, k_ref, v_ref, kc_ref, vc_ref, cosq_ref, sinq_ref, cos_ref, sin_ref, bias_ref,
             o_ref, kr_ref, vb_ref):
    rb = pl.program_id(2)
    n_rows = SEQ // GRID_W
    scale = NA_HEAD_DIM ** -0.5

    @pl.when(rb == 0)
    def _():
        kr_ref[...] = _rope(k_ref[...], cos_ref[...], sin_ref[...]).astype(BF16)
        vb_ref[...] = v_ref[...].astype(BF16)

    qr = (_rope(q_ref[...], cosq_ref[...], sinq_ref[...]) * scale).astype(BF16)
    kc = kc_ref[...].astype(BF16)
    vc = vc_ref[...].astype(BF16)
    starts, s_rows = [], []
    for lr in range(rows_per_step):
        r = rb * rows_per_step + lr
        r0 = jnp.clip(r - NA_WIN_R // 2, 0, n_rows - NA_WIN_R)
        start = pl.multiple_of(r0 * GRID_W, GRID_W)
        dr0 = r0 - r + NA_WIN_R - 1
        bias = jnp.concatenate([bias_ref[dr0 + 2 * p] for p in range(NA_WIN_R // 2)], axis=1)
        k_win = kr_ref[pl.ds(start, NA_WIN_R * GRID_W), :]
        s_rows.append(_dot_nt(qr[lr * GRID_W:(lr + 1) * GRID_W], k_win) + bias)
        starts.append(start)
    s_loc = jnp.concatenate(s_rows, axis=0)
    s_ctx = _dot_nt(qr, kc)
    m = jnp.maximum(jnp.max(s_loc, axis=-1, keepdims=True), jnp.max(s_ctx, axis=-1, keepdims=True))
    p_loc = jnp.exp(s_loc - m)
    p_ctx = jnp.exp(s_ctx - m)
    denom = jnp.sum(p_loc, axis=-1, keepdims=True) + jnp.sum(p_ctx, axis=-1, keepdims=True)
    p_loc = p_loc.astype(BF16)
    o_loc = jnp.concatenate(
        [_dot(p_loc[lr * GRID_W:(lr + 1) * GRID_W], vb_ref[pl.ds(starts[lr], NA_WIN_R * GRID_W), :])
         for lr in range(rows_per_step)], axis=0)
    o = o_loc + _dot(p_ctx.astype(BF16), vc)
    o_ref[...] = (o / denom).astype(o_ref.dtype)


def _na_attention(qkv, bias_tbl):
    rows_per_step = 32
    tq = rows_per_step * GRID_W
    steps = SEQ // tq
    cos, sin = _rope_tables()
    hd = NA_HEAD_DIM
    ctx_blk0 = T_LAT // CTX_LEN
    return pl.pallas_call(
        functools.partial(_na_body, rows_per_step),
        grid=(BATCH, NA_HEADS, steps),
        in_specs=[
            pl.BlockSpec((tq, hd), lambda b, h, r: (b * steps + r, h)),
            pl.BlockSpec((SEQ, hd), lambda b, h, r: (b, NA_HEADS + h)),
            pl.BlockSpec((SEQ, hd), lambda b, h, r: (b, 2 * NA_HEADS + h)),
            pl.BlockSpec((CTX_LEN, hd), lambda b, h, r: (ctx_blk0 + b, NA_HEADS + h)),
            pl.BlockSpec((CTX_LEN, hd), lambda b, h, r: (ctx_blk0 + b, 2 * NA_HEADS + h)),
            pl.BlockSpec((tq, hd), lambda b, h, r: (r, 0)),
            pl.BlockSpec((tq, hd), lambda b, h, r: (r, 0)),
            pl.BlockSpec((SEQ, hd), lambda b, h, r: (0, 0)),
            pl.BlockSpec((SEQ, hd), lambda b, h, r: (0, 0)),
            pl.BlockSpec((None,) + bias_tbl.shape[1:], lambda b, h, r: (h, 0, 0, 0)),
        ],
        out_specs=pl.BlockSpec((tq, hd), lambda b, h, r: (b * steps + r, h)),
        out_shape=jax.ShapeDtypeStruct((T_ALL, D_MODEL), BF16),
        scratch_shapes=[pltpu.VMEM((SEQ, hd), BF16), pltpu.VMEM((SEQ, hd), BF16)],
        compiler_params=_cparams(("arbitrary", "arbitrary", "arbitrary"), 48),
        name="na_attention",
    )(qkv, qkv, qkv, qkv, qkv, cos, sin, cos, sin, bias_tbl)


def _ctx_attn_body(q_ref, k_ref, v_ref, lat_out_ref, o_ref):
    del lat_out_ref
    scale = NA_HEAD_DIM ** -0.5
    s = _dot_nt((q_ref[...] * scale).astype(BF16), k_ref[...].astype(BF16))
    p = jnp.exp(s - jnp.max(s, axis=-1, keepdims=True))
    o = _dot(p.astype(BF16), v_ref[...].astype(BF16))
    o_ref[...] = (o / jnp.sum(p, axis=-1, keepdims=True)).astype(o_ref.dtype)


def _ctx_attention(qkv, o_all):
    hd = NA_HEAD_DIM
    ctx_blk0 = T_LAT // CTX_LEN
    return pl.pallas_call(
        _ctx_attn_body,
        grid=(BATCH, NA_HEADS),
        in_specs=[
            pl.BlockSpec((CTX_LEN, hd), lambda b, h: (ctx_blk0 + b, h)),
            pl.BlockSpec((CTX_LEN, hd), lambda b, h: (ctx_blk0 + b, NA_HEADS + h)),
            pl.BlockSpec((CTX_LEN, hd), lambda b, h: (ctx_blk0 + b, 2 * NA_HEADS + h)),
            pl.BlockSpec(memory_space=pl.ANY),
        ],
        out_specs=pl.BlockSpec((CTX_LEN, hd), lambda b, h: (ctx_blk0 + b, h)),
        out_shape=jax.ShapeDtypeStruct((T_ALL, D_MODEL), BF16),
        input_output_aliases={3: 0},
        compiler_params=_cparams(("arbitrary", "arbitrary"), 32),
        name="ctx_attention",
    )(qkv, qkv, qkv, o_all)


def kernel(x, c, ctx, c_ctx, w_mod, b_mod, norm_g, ffn_w_up, ffn_conv_w, ffn_conv_b, ffn_w_down, ssm_w_in, ssm_conv_w, ssm_conv_b, ssm_dt_bias, ssm_a_log, ssm_d, ssm_norm_g, ssm_w_out, hg_w_in, hg_lb, hg_norm_g, hg_w_out, na_w_qkv, na_rpb, na_w_out):
    cvec8 = jnp.concatenate([c, c_ctx[None], jnp.zeros((SUBLANES - BATCH - 1, D_MODEL), F32)], axis=0)
    mod = _adaln(cvec8, w_mod, b_mod)
    ffn_up_bf, ffn_down_bf = ffn_w_up.astype(BF16), ffn_w_down.astype(BF16)
    ssm_out_bf, hg_out_bf, na_out_bf = ssm_w_out.astype(BF16), hg_w_out.astype(BF16), na_w_out.astype(BF16)

    xs, h = _prep(x.reshape(T_LAT, D_MODEL), ctx.reshape(T_CTX, D_MODEL), norm_g[0, 0], mod[0])
    for i in range(DEPTH):
        last = i == DEPTH - 1
        kind, slot = i % N_MIXERS, i // N_MIXERS
        mod_l = mod[i]
        n_rows = T_LAT if last else T_ALL
        op = dict(xs=xs, g_res=norm_g[i, 1], g_ffn=norm_g[i, 2], mod_l=mod_l, n_rows=n_rows, slot=slot)
        if kind == 0:
            proj = _in_proj(h, ssm_w_in, slot, 1152, "ssd_in")
            xbc = _ssd_conv(proj, ssm_conv_w[slot], ssm_conv_b[slot])
            y2 = _ssd_scan(xbc, proj, ssm_dt_bias[slot], ssm_a_log[slot], ssm_d[slot])
            xs, hf = _out_proj(_ssd_gate_prologue, [((y2, 0), 0), ((y2, 1), 0), (proj, 0)],
                               ssm_norm_g[slot].reshape(1, SSM_INNER), ssm_out_bf, tm=256, name="ssd_out", **op)
        elif kind == 1:
            proj = _in_proj(h, hg_w_in, slot, 1024, "hg_in")
            o2 = _gla_scan(proj, hg_lb, i)
            xs, hf = _out_proj(_hg_readout_prologue, [((o2, 0), 0), ((o2, 1), 0), (proj, 4)],
                               hg_norm_g[slot].reshape(1, D_MODEL), hg_out_bf, tm=ROW_TILE, name="hg_out", **op)
        else:
            qkv = _in_proj(h, na_w_qkv, slot, 1024, "na_in")
            o = _ctx_attention(qkv, _na_attention(qkv, _na_bias_table(na_rpb[slot])))
            xs, hf = _out_proj(_plain_prologue, [(o, 0)], None, na_out_bf, tm=ROW_TILE, name="na_out", **op)
        nxt = min(i + 1, DEPTH - 1)
        xs, h = _ffn(hf, xs, i, ffn_up_bf, ffn_conv_w, ffn_conv_b, ffn_down_bf, norm_g[i, 3], mod_l,
                     norm_g[nxt, 0], mod[nxt], n_rows=n_rows, emit_h=not last)

    return xs.reshape(BATCH, SEQ, D_MODEL)
```

```python
import functools

import numpy as np
import jax
import jax.numpy as jnp
from jax import lax
from jax.experimental import pallas as pl
from jax.experimental.pallas import tpu as pltpu

F32 = jnp.float32
BF16 = jnp.bfloat16

D_MODEL = 2048
BATCH = 2
SEQ = 4096
DEPTH = 4
GRID_W = 64
CTX_LEN = 256
N_MIXERS = 3
RMS_EPS = 1e-6
SSM_INNER = 4096
SSM_HEAD_DIM = 64
SSM_HEADS = 64
SSM_STATE = 128
SSM_GROUPS = 8
SSM_CONV_DIM = SSM_INNER + 2 * SSM_GROUPS * SSM_STATE
SSM_IN_DIM = SSM_INNER + SSM_CONV_DIM + 2 * SSM_HEADS
SSM_CHUNK = 128
HG_HEADS = 16
HG_KEY_DIM = 128
HG_IN_DIM = 5 * D_MODEL
HG_CHUNK = 64
NA_HEADS = 16
NA_HEAD_DIM = 128
NA_WIN_R = 8
NA_WIN_C = 16
ROPE_BASE = 10000.0
FFN_HIDDEN = 5632

T_LAT = BATCH * SEQ
T_CTX = BATCH * CTX_LEN
T_ALL = T_LAT + T_CTX
LANES = 128
SUBLANES = 8
BF16_ROWS = 16
ONES_ROWS = 16
ROW_TILE = 512
FFN_COL_CHUNK = 512
MIB = 1024 * 1024
LOG2E = 1.4426950408889634


def _cparams(sem, vmem_mib):
    return pltpu.CompilerParams(dimension_semantics=sem, vmem_limit_bytes=vmem_mib * MIB)


def _silu(v):
    return v * jax.nn.sigmoid(v)


def _rms(v):
    return v * lax.rsqrt(jnp.mean(v * v, axis=-1, keepdims=True) + RMS_EPS)


def _log1p_unit(e):
    return jnp.log(1.0 + e)


def _softplus(v):
    return jnp.maximum(v, 0.0) + _log1p_unit(jnp.exp(-jnp.abs(v)))


def _dot(a, b):
    return jnp.dot(a, b, preferred_element_type=F32)


def _dot_nt(a, b):
    return lax.dot_general(a, b, (((1,), (1,)), ((), ())), preferred_element_type=F32)


def _dot_tn(a, b):
    return lax.dot_general(a, b, (((0,), (0,)), ((), ())), preferred_element_type=F32)


def _bf16_pieces(v, terms):
    pieces = []
    rem = v
    for _ in range(terms):
        part = rem.astype(BF16)
        rem = rem - part.astype(F32)
        pieces.append(part)
    return pieces


def _dot_sel(m01_rep, v, terms):
    return _dot(m01_rep, jnp.concatenate(_bf16_pieces(v, terms), axis=0))


def _dot_sel_rhs(v, m01_rep, terms):
    return _dot(jnp.concatenate(_bf16_pieces(v, terms), axis=1), m01_rep)


def _mod_row(row0):
    return jnp.where(row0 >= T_LAT, 2, row0 // SEQ)


def _mod_vec(mod_ref, r, idx):
    return mod_ref[pl.ds(r, 1), idx * D_MODEL:(idx + 1) * D_MODEL]


def _norm_mod(x, g, mod_ref, r, shift_idx, scale_idx):
    return _rms(x) * g * (1.0 + _mod_vec(mod_ref, r, scale_idx)) + _mod_vec(mod_ref, r, shift_idx)


def _seg_edges(row0, rows):
    gr = row0 + lax.broadcasted_iota(jnp.int32, (rows, 1), 0)
    pos = jnp.where(gr >= T_LAT, (gr - T_LAT) % CTX_LEN, gr % SEQ)
    length = jnp.where(gr >= T_LAT, CTX_LEN, SEQ)
    return pos == 0, pos == length - 1


def _shift_rows(cur, prev_row, next_row, row0):
    rows = cur.shape[0]
    first, last = _seg_edges(row0, rows)
    ridx = lax.broadcasted_iota(jnp.int32, (rows, 1), 0)
    up = jnp.where(ridx == 0, prev_row, pltpu.roll(cur, 1, axis=0))
    up = jnp.where(first, 0.0, up)
    dn = jnp.where(ridx == rows - 1, next_row, pltpu.roll(cur, rows - 1, axis=0))
    dn = jnp.where(last, 0.0, dn)
    return up, dn


def _token_conv(cur, prev_row, next_row, row0, w_ref, b_ref):
    up, dn = _shift_rows(cur, prev_row, next_row, row0)
    return b_ref[...] + up * w_ref[0:1, :] + cur * w_ref[1:2, :] + dn * w_ref[2:3, :]


def _halo_specs(tm, halo, width, n_rows, col_fn):
    per = tm // halo
    last_blk = n_rows // halo - 1
    prev = pl.BlockSpec((halo, width), lambda i, j: (jnp.maximum(i * per - 1, 0), col_fn(j)))
    nxt = pl.BlockSpec((halo, width), lambda i, j: (jnp.minimum((i + 1) * per, last_blk), col_fn(j)))
    return prev, nxt


def _adaln_body(c_ref, w_ref, b_ref, o_ref):
    s = _silu(c_ref[...]).astype(BF16)
    o_ref[...] = _dot(s, w_ref[...].astype(BF16)) + b_ref[...]


def _adaln(cvec8, w_mod, b_mod):
    tn = 1024
    n_out = 6 * D_MODEL
    return pl.pallas_call(
        _adaln_body,
        grid=(DEPTH, n_out // tn),
        in_specs=[
            pl.BlockSpec((SUBLANES, D_MODEL), lambda l, j: (0, 0)),
            pl.BlockSpec((None, D_MODEL, tn), lambda l, j: (l, 0, j)),
            pl.BlockSpec((None, 1, tn), lambda l, j: (l, 0, j)),
        ],
        out_specs=pl.BlockSpec((None, SUBLANES, tn), lambda l, j: (l, 0, j)),
        out_shape=jax.ShapeDtypeStruct((DEPTH, SUBLANES, n_out), F32),
        compiler_params=_cparams(("arbitrary", "arbitrary"), 40),
        name="adaln",
    )(cvec8, w_mod, b_mod.reshape(DEPTH, 1, n_out))


def _prep_body(lat_ref, ctx_ref, g_ref, mod_ref, xs_ref, h_ref):
    row0 = pl.program_id(0) * ROW_TILE

    def emit(src_ref):
        xv = src_ref[...]
        xs_ref[...] = xv
        h_ref[...] = _norm_mod(xv, g_ref[...], mod_ref, _mod_row(row0), 0, 1).astype(BF16)

    @pl.when(row0 < T_LAT)
    def _():
        emit(lat_ref)

    @pl.when(row0 >= T_LAT)
    def _():
        emit(ctx_ref)


def _prep(x_lat, x_ctx, g_row, mod_l):
    assert T_CTX == ROW_TILE
    n_lat = T_LAT // ROW_TILE
    row = pl.BlockSpec((ROW_TILE, D_MODEL), lambda i: (i, 0))
    return pl.pallas_call(
        _prep_body,
        grid=(T_ALL // ROW_TILE,),
        in_specs=[pl.BlockSpec((ROW_TILE, D_MODEL), lambda i: (jnp.minimum(i, n_lat - 1), 0)),
                  pl.BlockSpec((ROW_TILE, D_MODEL), lambda i: (0, 0)),
                  pl.BlockSpec((1, D_MODEL), lambda i: (0, 0)), pl.BlockSpec(mod_l.shape, lambda i: (0, 0))],
        out_specs=[row, row],
        out_shape=[jax.ShapeDtypeStruct((T_ALL, D_MODEL), F32), jax.ShapeDtypeStruct((T_ALL, D_MODEL), BF16)],
        compiler_params=_cparams(("arbitrary",), 40),
        name="prep",
    )(x_lat, x_ctx, g_row.reshape(1, D_MODEL), mod_l)


def _in_proj_body(h_ref, w_ref, o_ref, wb_ref):
    @pl.when(pl.program_id(1) == 0)
    def _():
        wb_ref[...] = w_ref[...].astype(BF16)

    o_ref[...] = _dot(h_ref[...], wb_ref[...])


def _in_proj(h, w_stack, slot, tn, name, n_dim=None, first_tile=0):
    tm = T_ALL // 8
    n_dim = w_stack.shape[2] if n_dim is None else n_dim
    return pl.pallas_call(
        _in_proj_body,
        grid=(n_dim // tn, T_ALL // tm),
        in_specs=[pl.BlockSpec((tm, D_MODEL), lambda j, i: (i, 0)),
                  pl.BlockSpec((None, D_MODEL, tn), lambda j, i: (slot, 0, first_tile + j))],
        out_specs=pl.BlockSpec((tm, tn), lambda j, i: (i, j)),
        out_shape=jax.ShapeDtypeStruct((T_ALL, n_dim), F32),
        scratch_shapes=[pltpu.VMEM((D_MODEL, tn), BF16)],
        compiler_params=_cparams(("arbitrary", "arbitrary"), 56),
        name=name,
    )(h, w_stack)


def _ssd_gate_prologue(yf_ref, yb_ref, z_ref, g_ref):
    v = (yf_ref[...].astype(F32) + yb_ref[...].astype(F32)) * _silu(z_ref[...])
    return v * g_ref[...], lax.rsqrt(jnp.mean(v * v, axis=-1, keepdims=True) + RMS_EPS)


def _hg_readout_prologue(of_ref, ob_ref, gate_ref, g_ref):
    o = of_ref[...].astype(F32) + ob_ref[...].astype(F32)
    parts = [_rms(o[:, h * LANES:(h + 1) * LANES]) for h in range(o.shape[1] // LANES)]
    return jnp.concatenate(parts, axis=-1) * g_ref[...] * _silu(gate_ref[...]), None


def _plain_prologue(o_ref):
    return o_ref[...], None


def _out_proj_body(prologue, n_in, tm, *refs):
    ins = refs[:n_in]
    w_ref, x_ref, g1_ref, g2_ref, mod_ref, xo_ref, ho_ref = refs[n_in:]
    part, row_scale = prologue(*ins)
    y = _dot(part.astype(BF16), w_ref[...])
    if row_scale is not None:
        y = y * row_scale
    r = _mod_row(pl.program_id(0) * tm)
    xn = x_ref[...] + _mod_vec(mod_ref, r, 2) * (_rms(y) * g1_ref[...])
    xo_ref[...] = xn
    ho_ref[...] = _norm_mod(xn, g2_ref[...], mod_ref, r, 3, 4).astype(BF16)


def _out_proj(prologue, row_ins, gain, w_stack, slot, xs, g_res, g_ffn, mod_l, *, n_rows, tm, name):
    k_dim = w_stack.shape[1]
    in_specs, args = [], []
    for arr, cb in row_ins:
        if isinstance(arr, tuple):
            in_specs.append(pl.BlockSpec((None, tm, k_dim), functools.partial(
                lambda i, ld, cb: (ld, i, cb), ld=arr[1], cb=cb)))
            args.append(arr[0])
        else:
            in_specs.append(pl.BlockSpec((tm, k_dim), functools.partial(lambda i, cb: (i, cb), cb=cb)))
            args.append(arr)
    if gain is not None:
        in_specs.append(pl.BlockSpec((1, k_dim), lambda i: (0, 0)))
        args.append(gain)
    n_in = len(args)
    row = pl.BlockSpec((tm, D_MODEL), lambda i: (i, 0))
    vec = pl.BlockSpec((1, D_MODEL), lambda i: (0, 0))
    in_specs += [pl.BlockSpec((None, k_dim, D_MODEL), lambda i: (slot, 0, 0), pipeline_mode=pl.Buffered(1)),
                 row, vec, vec, pl.BlockSpec(mod_l.shape, lambda i: (0, 0))]
    return pl.pallas_call(
        functools.partial(_out_proj_body, prologue, n_in, tm),
        grid=(n_rows // tm,),
        in_specs=in_specs,
        out_specs=[row, row],
        out_shape=[jax.ShapeDtypeStruct((n_rows, D_MODEL), F32), jax.ShapeDtypeStruct((n_rows, D_MODEL), BF16)],
        compiler_params=_cparams(("arbitrary",), 60),
        name=name,
    )(*args, w_stack, xs, g_res.reshape(1, D_MODEL), g_ffn.reshape(1, D_MODEL), mod_l)


def _ffn_body(emit_h, has_ctx, h_ref, hp_ref, hn_ref, wa_ref, wv_ref, cwa_ref, cwv_ref, cba_ref, cbv_ref, wd_ref,
              x_ref, g_res_ref, mod_ref, g_next_ref, modn_ref, *out_and_scratch):
    if emit_h:
        xo_ref, ho_ref, hall_ref, acc_ref = out_and_scratch
    else:
        xo_ref, hall_ref, acc_ref = out_and_scratch
    k = pl.program_id(1)
    tm = ROW_TILE
    row0 = pl.program_id(0) * tm
    pad = BF16_ROWS
    rows = tm + 2 * pad
    s8 = SUBLANES

    @pl.when(k == 0)
    def _():
        hall_ref[pl.ds(0, pad), :] = hp_ref[...]
        hall_ref[pl.ds(pad, tm), :] = h_ref[...]
        hall_ref[pl.ds(pad + tm, pad), :] = hn_ref[...]
        acc_ref[...] = jnp.zeros_like(acc_ref)

    hall = hall_ref[...]
    edge_first = row0 % SEQ == 0
    edge_last = ((row0 + tm) % SEQ == 0) | (row0 + tm == T_ALL)
    is_ctx = row0 >= T_LAT
    joints = list(range(CTX_LEN, tm, CTX_LEN)) if has_ctx else []
    sub = lax.broadcasted_iota(jnp.int32, (s8, 1), 0)

    def patch(v, start, cond):
        return jnp.concatenate([v[:start], jnp.where(cond, 0.0, v[start:start + s8]), v[start + s8:]], axis=0)

    def conv(w_ref, cw_ref, cb_ref, cs):
        u = _dot(hall, w_ref[:, cs])
        u = patch(u, pad - s8, edge_first)
        u = patch(u, pad + tm, edge_last)
        up = pltpu.roll(u, 1, axis=0)[pad:pad + tm]
        dn = pltpu.roll(u, rows - 1, axis=0)[pad:pad + tm]
        for j in joints:
            up = patch(up, j, is_ctx & (sub == 0))
            dn = patch(dn, j - s8, is_ctx & (sub == s8 - 1))
        return cb_ref[:, cs] + up * cw_ref[0:1, cs] + u[pad:pad + tm] * cw_ref[1:2, cs] + dn * cw_ref[2:3, cs]

    chunks = [pl.ds(c * FFN_COL_CHUNK, FFN_COL_CHUNK) for c in range(wa_ref.shape[1] // FFN_COL_CHUNK)]
    acts = [(_silu(conv(wa_ref, cwa_ref, cba_ref, cs)) * conv(wv_ref, cwv_ref, cbv_ref, cs)).astype(BF16)
            for cs in chunks]
    acc_ref[...] += _dot(jnp.concatenate(acts, axis=1), wd_ref[...])

    @pl.when(k == pl.num_programs(1) - 1)
    def _():
        r = _mod_row(row0)
        xn = x_ref[...] + _mod_vec(mod_ref, r, 5) * (_rms(acc_ref[...]) * g_res_ref[...])
        xo_ref[...] = xn
        if emit_h:
            ho_ref[...] = _norm_mod(xn, g_next_ref[...], modn_ref, r, 0, 1).astype(BF16)


def _ffn(h, xs, layer, w_up, conv_w, conv_b, w_down, g_res, mod_l, g_next, mod_next, *, n_rows, emit_h):
    tm, tk = ROW_TILE, 512
    kt = FFN_HIDDEN // tk
    h_prev, h_next = _halo_specs(tm, BF16_ROWS, D_MODEL, n_rows, lambda k: 0)
    row = pl.BlockSpec((tm, D_MODEL), lambda i, k: (i, 0))
    vec = pl.BlockSpec((1, D_MODEL), lambda i, k: (0, 0))
    tab = pl.BlockSpec(mod_l.shape, lambda i, k: (0, 0))
    out_specs = [row, row] if emit_h else [row]
    out_shape = [jax.ShapeDtypeStruct((n_rows, D_MODEL), F32)]
    if emit_h:
        out_shape.append(jax.ShapeDtypeStruct((n_rows, D_MODEL), BF16))
    conv_b3 = conv_b.reshape(DEPTH, 1, 2 * FFN_HIDDEN)

    res = pl.pallas_call(
        functools.partial(_ffn_body, emit_h, n_rows > T_LAT),
        grid=(n_rows // tm, kt),
        in_specs=[
            row, h_prev, h_next,
            pl.BlockSpec((None, D_MODEL, tk), lambda i, k: (layer, 0, k)),
            pl.BlockSpec((None, D_MODEL, tk), lambda i, k: (layer, 0, k + kt)),
            pl.BlockSpec((None, 3, tk), lambda i, k: (layer, 0, k)),
            pl.BlockSpec((None, 3, tk), lambda i, k: (layer, 0, k + kt)),
            pl.BlockSpec((None, 1, tk), lambda i, k: (layer, 0, k)),
            pl.BlockSpec((None, 1, tk), lambda i, k: (layer, 0, k + kt)),
            pl.BlockSpec((None, tk, D_MODEL), lambda i, k: (layer, k, 0)),
            row, vec, tab, vec, tab,
        ],
        out_specs=out_specs,
        out_shape=out_shape,
        scratch_shapes=[pltpu.VMEM((tm + 2 * BF16_ROWS, D_MODEL), BF16), pltpu.VMEM((tm, D_MODEL), F32)],
        compiler_params=_cparams(("arbitrary", "arbitrary"), 56),
        name="conv_ffn",
    )(h, h, h, w_up, w_up, conv_w, conv_w, conv_b3, conv_b3, w_down, xs, g_res.reshape(1, D_MODEL), mod_l,
      g_next.reshape(1, D_MODEL), mod_next)
    return (res[0], res[1]) if emit_h else (res[0], None)


def _ssd_conv_body(tm, cur_ref, prev_ref, next_ref, w_ref, b_ref, o_ref):
    u = _token_conv(cur_ref[...], prev_ref[SUBLANES - 1:SUBLANES, :], next_ref[0:1, :], pl.program_id(0) * tm,
                    w_ref, b_ref)
    o_ref[...] = _silu(u).astype(o_ref.dtype)


def _ssd_conv(proj, conv_w, conv_b):
    tm, tc = 512, 2048
    col0 = SSM_INNER // tc
    prev, nxt = _halo_specs(tm, SUBLANES, tc, T_ALL, lambda j: j + col0)
    return pl.pallas_call(
        functools.partial(_ssd_conv_body, tm),
        grid=(T_ALL // tm, SSM_CONV_DIM // tc),
        in_specs=[pl.BlockSpec((tm, tc), lambda i, j: (i, j + col0)), prev, nxt,
                  pl.BlockSpec((3, tc), lambda i, j: (0, j)), pl.BlockSpec((1, tc), lambda i, j: (0, j))],
        out_specs=pl.BlockSpec((tm, tc), lambda i, j: (i, j)),
        out_shape=jax.ShapeDtypeStruct((T_ALL, SSM_CONV_DIM), BF16),
        compiler_params=_cparams(("arbitrary", "arbitrary"), 40),
        name="ssd_conv",
    )(proj, proj, proj, conv_w, conv_b.reshape(1, -1))


def _scan_chunk_index(d, b, s, n_lat, n_ctx):
    ctx0 = BATCH * n_lat + b * n_ctx
    fwd = jnp.where(s < n_ctx, ctx0 + s, b * n_lat + s - n_ctx)
    bwd = jnp.where(s < n_ctx, ctx0 + n_ctx - 1 - s, b * n_lat + n_lat - 1 - (s - n_ctx))
    return jnp.where(d == 0, fwd, bwd)


SSD_CUM_TERMS = 3
SSD_DT_TERMS = 2


def _ssd_scan_consts():
    q = SSM_CHUNK
    i = np.arange(q)
    pfx = np.zeros((2, q + ONES_ROWS, q), np.float32)
    pfx[0, :q] = i[:, None] >= i[None, :]
    pfx[1, :q] = i[:, None] <= i[None, :]
    pfx[:, q:] = 1.0
    pfx = np.tile(pfx, (1, 1, SSD_CUM_TERMS))
    expand = np.repeat(np.eye(SSM_HEADS, dtype=np.float32), SSM_HEAD_DIM, axis=1)
    expand = np.tile(expand, (SSD_DT_TERMS, 1))
    return jnp.asarray(pfx, BF16), jnp.asarray(expand, BF16)


def _ssd_scan_body(x_ref, b_ref, c_ref, dtraw_ref, dtb_ref, alog_ref, dskip_ref, pfx_ref, exp_ref, y_ref, s_ref):
    q = SSM_CHUNK
    hp = 2 * SSM_HEAD_DIM
    d = pl.program_id(0)

    @pl.when(pl.program_id(2) == 0)
    def _():
        s_ref[...] = jnp.zeros_like(s_ref)

    def pick(v):
        return jnp.where(d == 0, v[:, :SSM_HEADS], v[:, SSM_HEADS:])

    dt2 = _softplus(dtraw_ref[...] + dtb_ref[...])
    cum2 = _dot_sel(pfx_ref[...], dt2 * (-LOG2E * jnp.exp(alog_ref[...])), SSD_CUM_TERMS)
    acum2 = cum2[:q]
    acum = pick(acum2)
    tot = pick(cum2[q:q + SUBLANES])[0:1]
    acum_t2 = acum2.T
    acum_t = jnp.where(d == 0, acum_t2[:SSM_HEADS], acum_t2[SSM_HEADS:])
    dt_wide = _dot_sel_rhs(pick(dt2), exp_ref[...], SSD_DT_TERMS)

    ii = lax.broadcasted_iota(jnp.int32, (q, q), 0)
    jj = lax.broadcasted_iota(jnp.int32, (q, q), 1)
    causal = (jj - ii) * (1 - 2 * d) <= 0
    low_half = lax.broadcasted_iota(jnp.int32, (q, hp), 1) < SSM_HEAD_DIM

    heads_per_group = SSM_HEADS // SSM_GROUPS
    n_pairs = SSM_HEADS // 2

    b_gs = [b_ref[:, g * SSM_STATE:(g + 1) * SSM_STATE].astype(BF16) for g in range(SSM_GROUPS)]
    c_bf = [c_ref[:, g * SSM_STATE:(g + 1) * SSM_STATE].astype(BF16) for g in range(SSM_GROUPS)]
    c_gs = [c.astype(F32) for c in c_bf]
    cbs = [jnp.where(causal, _dot_nt(c_bf[g], b_gs[g]), 0.0) for g in range(SSM_GROUPS)]

    cols, lhs = [], []
    for h in range(SSM_HEADS):
        g = h // heads_per_group
        col = jnp.broadcast_to(acum[:, h:h + 1], (q, q))
        seg = jnp.where(causal, col - acum_t[h:h + 1, :], 0.0)
        scores = jnp.exp2(seg) * cbs[g]
        c_dec = c_gs[g] * jnp.exp2(col)
        lhs.append(jnp.concatenate([scores.astype(BF16), c_dec.astype(BF16)], axis=1))
        cols.append(col)

    xdts = []
    for p in range(n_pairs):
        lo = p * hp
        x_pair = x_ref[:, lo:lo + hp].astype(F32)
        xdt = x_pair * dt_wide[:, lo:lo + hp]
        rhs = jnp.concatenate([xdt.astype(BF16), s_ref[:, lo:lo + hp].astype(BF16)], axis=0)
        y_pair = jnp.where(low_half, _dot(lhs[2 * p], rhs), _dot(lhs[2 * p + 1], rhs))
        y_ref[:, lo:lo + hp] = (y_pair + dskip_ref[:, lo:lo + hp] * x_pair).astype(y_ref.dtype)
        xdts.append(xdt)

    for p in range(n_pairs):
        lo = p * hp
        g = (2 * p) // heads_per_group
        tot_pair = jnp.where(low_half[0:1], jnp.broadcast_to(tot[:, 2 * p:2 * p + 1], (1, hp)),
                             jnp.broadcast_to(tot[:, 2 * p + 1:2 * p + 2], (1, hp)))
        to_end = jnp.exp2(tot_pair - jnp.where(low_half, cols[2 * p], cols[2 * p + 1]))
        s_ref[:, lo:lo + hp] = (s_ref[:, lo:lo + hp] * jnp.exp2(tot_pair)
                                + _dot_tn(b_gs[g], (xdts[p] * to_end).astype(BF16)))


def _ssd_scan(xbc, dt_raw, dt_bias, a_log, d_skip):
    q = SSM_CHUNK
    n_lat, n_ctx = SEQ // q, CTX_LEN // q
    pfx, expand = _ssd_scan_consts()
    n_heads2 = 2 * SSM_HEADS
    b_col = SSM_INNER // (SSM_GROUPS * SSM_STATE)
    d_wide = jnp.repeat(d_skip, SSM_HEAD_DIM, axis=1).reshape(2, 1, SSM_INNER)

    def chunk(d, b, s):
        return _scan_chunk_index(d, b, s, n_lat, n_ctx)

    return pl.pallas_call(
        _ssd_scan_body,
        grid=(2, BATCH, n_lat + n_ctx),
        in_specs=[
            pl.BlockSpec((q, SSM_INNER), lambda d, b, s: (chunk(d, b, s), 0)),
            pl.BlockSpec((q, SSM_GROUPS * SSM_STATE), lambda d, b, s: (chunk(d, b, s), b_col)),
            pl.BlockSpec((q, SSM_GROUPS * SSM_STATE), lambda d, b, s: (chunk(d, b, s), b_col + 1)),
            pl.BlockSpec((q, n_heads2), lambda d, b, s: (chunk(d, b, s), 0)),
            pl.BlockSpec((1, n_heads2), lambda d, b, s: (0, 0)),
            pl.BlockSpec((1, n_heads2), lambda d, b, s: (0, 0)),
            pl.BlockSpec((None, 1, SSM_INNER), lambda d, b, s: (d, 0, 0)),
            pl.BlockSpec((None,) + pfx.shape[1:], lambda d, b, s: (d, 0, 0)),
            pl.BlockSpec(expand.shape, lambda d, b, s: (0, 0)),
        ],
        out_specs=pl.BlockSpec((None, q, SSM_INNER), lambda d, b, s: (d, chunk(d, b, s), 0)),
        out_shape=jax.ShapeDtypeStruct((2, T_ALL, SSM_INNER), BF16),
        scratch_shapes=[pltpu.VMEM((SSM_STATE, SSM_INNER), F32)],
        compiler_params=_cparams(("arbitrary", "arbitrary", "arbitrary"), 48),
        name="ssd_scan",
    )(xbc, xbc, xbc, dt_raw, dt_bias.reshape(1, n_heads2), a_log.reshape(1, n_heads2), d_wide, pfx, expand)


GLA_SUM_TERMS = 2


def _gla_consts():
    q = HG_CHUNK
    levels = int(np.log2(q))
    i = np.arange(q)
    mats = [(i[:, None] >= i[None, :]).astype(np.float32), (i[None, :] > i[:, None]).astype(np.float32)]
    masks = [np.eye(q, dtype=np.float32)]
    for lv in range(levels):
        s = 1 << lv
        blk, pos = i // (2 * s), i % (2 * s)
        bound = blk * 2 * s + s - 1
        right = pos >= s
        t = i[None, :]
        m_right = right[:, None] & (t > bound[:, None]) & (t <= i[:, None])
        m_left = (~right)[:, None] & (t > i[:, None]) & (t <= bound[:, None])
        mats.append((m_right | m_left).astype(np.float32))
        masks.append((right[:, None] & (~right)[None, :] & (blk[:, None] == blk[None, :])).astype(np.float32))
    mats.append(np.ones((ONES_ROWS, q), np.float32))
    fwd_m, fwd_k = np.concatenate(mats, axis=0), np.stack(masks)
    flip_rows = np.concatenate([np.arange(k * q, (k + 1) * q)[::-1] for k in range(levels + 2)]
                               + [np.arange((levels + 2) * q, (levels + 2) * q + ONES_ROWS)])
    bwd_m = fwd_m[flip_rows][:, ::-1]
    bwd_k = fwd_k[:, ::-1, ::-1]
    mats2 = np.tile(np.stack([fwd_m, bwd_m]), (1, 1, GLA_SUM_TERMS))
    return (jnp.asarray(mats2, BF16), jnp.asarray(np.stack([fwd_k, bwd_k]), F32), levels)


def _gla_body(levels, layer, q_ref, v_ref, f_ref, lbp_ref, mats_ref, masks_ref, o_ref, st_ref):
    q = HG_CHUNK

    @pl.when(pl.program_id(2) == 0)
    def _():
        st_ref[...] = jnp.zeros_like(st_ref)

    lbp = lbp_ref[...]
    e = jnp.exp(lbp - jnp.max(lbp, axis=0, keepdims=True))
    sm = e / jnp.sum(e, axis=0, keepdims=True)
    lb = jnp.sum(sm[1:layer + 1], axis=0, keepdims=True) if layer > 0 else jnp.zeros((1, lbp.shape[1]), F32)

    f = f_ref[...]
    ef = jnp.exp(-jnp.abs(f))
    log_sig = jnp.minimum(f, 0.0) - _log1p_unit(ef)
    la = jnp.log(lb)
    lc = jnp.log1p(-lb) + log_sig
    logf = jnp.maximum(la, lc) + _log1p_unit(jnp.exp(-jnp.abs(la - lc)))
    key = (1.0 - lb) * (jnp.where(f >= 0.0, ef, 1.0) / (1.0 + ef))
    qs = _silu(q_ref[...])

    sums = _dot_sel(mats_ref[...], logf, GLA_SUM_TERMS)
    q_in = (qs * jnp.exp(sums[:q])).astype(BF16)
    k_out = (key * jnp.exp(sums[q:2 * q])).astype(BF16)
    dec = jnp.exp(sums[(levels + 2) * q:(levels + 2) * q + 1])
    qs_bf, key_bf = qs.astype(BF16), key.astype(BF16)
    q_lv, k_lv = [], []
    for lv in range(levels):
        fac = jnp.exp(sums[(lv + 2) * q:(lv + 3) * q])
        q_lv.append((qs * fac).astype(BF16))
        k_lv.append((key * fac).astype(BF16))
    v_bf = v_ref[...].astype(BF16)

    heads = [slice(h * LANES, (h + 1) * LANES) for h in range(HG_HEADS)]
    attn = masks_ref[0][None] * jnp.stack([_dot_nt(qs_bf[:, sl], key_bf[:, sl]) for sl in heads], axis=0)
    for lv in range(levels):
        attn = attn + masks_ref[lv + 1][None] * jnp.stack(
            [_dot_nt(q_lv[lv][:, sl], k_lv[lv][:, sl]) for sl in heads], axis=0)
    attn = attn.astype(BF16)
    st_bf = st_ref[...].astype(BF16)
    o_ref[...] = jnp.concatenate(
        [_dot(attn[h], v_bf[:, sl]) + _dot_nt(q_in[:, sl], st_bf[h]) for h, sl in enumerate(heads)],
        axis=1).astype(o_ref.dtype)
    for h, sl in enumerate(heads):
        st_ref[h] = st_ref[h] * dec[:, sl] + _dot_tn(v_bf[:, sl], k_out[:, sl])


def _gla_scan(proj, hg_lb, layer):
    q = HG_CHUNK
    n_lat, n_ctx = SEQ // q, CTX_LEN // q
    mats, masks, levels = _gla_consts()

    def chunk(d, b, s):
        return _scan_chunk_index(d, b, s, n_lat, n_ctx)

    return pl.pallas_call(
        functools.partial(_gla_body, levels, layer),
        grid=(2, BATCH, n_lat + n_ctx),
        in_specs=[
            pl.BlockSpec((q, D_MODEL), lambda d, b, s: (chunk(d, b, s), 0)),
            pl.BlockSpec((q, D_MODEL), lambda d, b, s: (chunk(d, b, s), 1)),
            pl.BlockSpec((q, D_MODEL), lambda d, b, s: (chunk(d, b, s), 2 + d)),
            pl.BlockSpec((None, DEPTH, D_MODEL), lambda d, b, s: (d, 0, 0)),
            pl.BlockSpec((None,) + mats.shape[1:], lambda d, b, s: (d, 0, 0)),
            pl.BlockSpec((None,) + masks.shape[1:], lambda d, b, s: (d, 0, 0, 0)),
        ],
        out_specs=pl.BlockSpec((None, q, D_MODEL), lambda d, b, s: (d, chunk(d, b, s), 0)),
        out_shape=jax.ShapeDtypeStruct((2, T_ALL, D_MODEL), BF16),
        scratch_shapes=[pltpu.VMEM((HG_HEADS, LANES, HG_KEY_DIM), F32)],
        compiler_params=_cparams(("arbitrary", "arbitrary", "arbitrary"), 48),
        name="gla_scan",
    )(proj, proj, proj, hg_lb, mats, masks)


def _na_bias_body(rpb_ref, o_ref):
    h = pl.program_id(0)
    n_dc = 2 * NA_WIN_C - 1
    qc = lax.broadcasted_iota(jnp.int32, (GRID_W, LANES), 0)
    kc = lax.broadcasted_iota(jnp.int32, (GRID_W, LANES), 1) % GRID_W
    c0 = jnp.clip(qc - NA_WIN_C // 2, 0, GRID_W - NA_WIN_C)
    col_in = (kc >= c0) & (kc < c0 + NA_WIN_C)
    dc = jnp.clip(kc - qc + NA_WIN_C - 1, 0, n_dc - 1)
    low = lax.broadcasted_iota(jnp.int32, (GRID_W, LANES), 1) < GRID_W
    tiles = []
    for dr in range(2 * NA_WIN_R - 1):
        t = jnp.zeros((GRID_W, LANES), F32)
        for k in range(n_dc):
            t = jnp.where(dc == k, rpb_ref[(h * (2 * NA_WIN_R - 1) + dr) * n_dc + k], t)
        tiles.append(jnp.where(col_in, t, -jnp.inf))
    for dr in range(2 * NA_WIN_R - 2):
        o_ref[dr] = jnp.where(low, tiles[dr], tiles[dr + 1])


def _na_bias_table(rpb):
    n_dr = 2 * NA_WIN_R - 2
    return pl.pallas_call(
        _na_bias_body,
        grid=(NA_HEADS,),
        in_specs=[pl.BlockSpec(memory_space=pltpu.SMEM)],
        out_specs=pl.BlockSpec((None, n_dr, GRID_W, LANES), lambda h: (h, 0, 0, 0)),
        out_shape=jax.ShapeDtypeStruct((NA_HEADS, n_dr, GRID_W, LANES), F32),
        compiler_params=_cparams(("arbitrary",), 32),
        name="na_bias",
    )(rpb.reshape(-1))


def _rope_tables():
    quarter = NA_HEAD_DIM // 4
    inv = ROPE_BASE ** (-np.arange(quarter, dtype=np.float64) / quarter)
    t = np.arange(SEQ)
    row, col = t // GRID_W, t % GRID_W
    ang = np.concatenate([row[:, None] * inv[None], row[:, None] * inv[None],
                          col[:, None] * inv[None], col[:, None] * inv[None]], axis=1)
    sign = np.tile(np.concatenate([-np.ones(quarter), np.ones(quarter)]), 2)[None]
    return jnp.asarray(np.cos(ang), F32), jnp.asarray(np.sin(ang) * sign, F32)


def _rope(v, cos, sin_signed):
    quarter = NA_HEAD_DIM // 4
    lane = lax.broadcasted_iota(jnp.int32, v.shape, 1)
    first = (lane % (2 * quarter)) < quarter
    partner = jnp.where(first, pltpu.roll(v, NA_HEAD_DIM - quarter, axis=1), pltpu.roll(v, quarter, axis=1))
    return v * cos + partner * sin_signed


def _na_body(rows_per_step, q_ref, k_ref, v_ref, kc_ref, vc_ref, cosq_ref, sinq_ref, cos_ref, sin_ref, bias_ref,
             o_ref, kr_ref, vb_ref):
    rb = pl.program_id(2)
    n_rows = SEQ // GRID_W
    scale = NA_HEAD_DIM ** -0.5

    @pl.when(rb == 0)
    def _():
        kr_ref[...] = _rope(k_ref[...], cos_ref[...], sin_ref[...]).astype(BF16)
        vb_ref[...] = v_ref[...].astype(BF16)

    qr = (_rope(q_ref[...], cosq_ref[...], sinq_ref[...]) * scale).astype(BF16)
    kc = kc_ref[...].astype(BF16)
    vc = vc_ref[...].astype(BF16)
    starts, s_rows = [], []
    for lr in range(rows_per_step):
        r = rb * rows_per_step + lr
        r0 = jnp.clip(r - NA_WIN_R // 2, 0, n_rows - NA_WIN_R)
        start = pl.multiple_of(r0 * GRID_W, GRID_W)
        dr0 = r0 - r + NA_WIN_R - 1
        bias = jnp.concatenate([bias_ref[dr0 + 2 * p] for p in range(NA_WIN_R // 2)], axis=1)
        k_win = kr_ref[pl.ds(start, NA_WIN_R * GRID_W), :]
        s_rows.append(_dot_nt(qr[lr * GRID_W:(lr + 1) * GRID_W], k_win) + bias)
        starts.append(start)
    s_loc = jnp.concatenate(s_rows, axis=0)
    s_ctx = _dot_nt(qr, kc)
    m = jnp.maximum(jnp.max(s_loc, axis=-1, keepdims=True), jnp.max(s_ctx, axis=-1, keepdims=True))
    p_loc = jnp.exp(s_loc - m)
    p_ctx = jnp.exp(s_ctx - m)
    denom = jnp.sum(p_loc, axis=-1, keepdims=True) + jnp.sum(p_ctx, axis=-1, keepdims=True)
    p_loc = p_loc.astype(BF16)
    o_loc = jnp.concatenate(
        [_dot(p_loc[lr * GRID_W:(lr + 1) * GRID_W], vb_ref[pl.ds(starts[lr], NA_WIN_R * GRID_W), :])
         for lr in range(rows_per_step)], axis=0)
    o = o_loc + _dot(p_ctx.astype(BF16), vc)
    o_ref[...] = (o / denom).astype(o_ref.dtype)


def _na_attention(qkv, bias_tbl):
    rows_per_step = 32
    tq = rows_per_step * GRID_W
    steps = SEQ // tq
    cos, sin = _rope_tables()
    hd = NA_HEAD_DIM
    ctx_blk0 = T_LAT // CTX_LEN
    return pl.pallas_call(
        functools.partial(_na_body, rows_per_step),
        grid=(BATCH, NA_HEADS, steps),
        in_specs=[
            pl.BlockSpec((tq, hd), lambda b, h, r: (b * steps + r, h)),
            pl.BlockSpec((SEQ, hd), lambda b, h, r: (b, NA_HEADS + h)),
            pl.BlockSpec((SEQ, hd), lambda b, h, r: (b, 2 * NA_HEADS + h)),
            pl.BlockSpec((CTX_LEN, hd), lambda b, h, r: (ctx_blk0 + b, NA_HEADS + h)),
            pl.BlockSpec((CTX_LEN, hd), lambda b, h, r: (ctx_blk0 + b, 2 * NA_HEADS + h)),
            pl.BlockSpec((tq, hd), lambda b, h, r: (r, 0)),
            pl.BlockSpec((tq, hd), lambda b, h, r: (r, 0)),
            pl.BlockSpec((SEQ, hd), lambda b, h, r: (0, 0)),
            pl.BlockSpec((SEQ, hd), lambda b, h, r: (0, 0)),
            pl.BlockSpec((None,) + bias_tbl.shape[1:], lambda b, h, r: (h, 0, 0, 0)),
        ],
        out_specs=pl.BlockSpec((tq, hd), lambda b, h, r: (b * steps + r, h)),
        out_shape=jax.ShapeDtypeStruct((T_ALL, D_MODEL), BF16),
        scratch_shapes=[pltpu.VMEM((SEQ, hd), BF16), pltpu.VMEM((SEQ, hd), BF16)],
        compiler_params=_cparams(("arbitrary", "arbitrary", "arbitrary"), 48),
        name="na_attention",
    )(qkv, qkv, qkv, qkv, qkv, cos, sin, cos, sin, bias_tbl)


def _ctx_attn_body(q_ref, k_ref, v_ref, lat_out_ref, o_ref):
    del lat_out_ref
    scale = NA_HEAD_DIM ** -0.5
    s = _dot_nt((q_ref[...] * scale).astype(BF16), k_ref[...].astype(BF16))
    p = jnp.exp(s - jnp.max(s, axis=-1, keepdims=True))
    o = _dot(p.astype(BF16), v_ref[...].astype(BF16))
    o_ref[...] = (o / jnp.sum(p, axis=-1, keepdims=True)).astype(o_ref.dtype)


def _ctx_attention(qkv, o_all):
    hd = NA_HEAD_DIM
    ctx_blk0 = T_LAT // CTX_LEN
    return pl.pallas_call(
        _ctx_attn_body,
        grid=(BATCH, NA_HEADS),
        in_specs=[
            pl.BlockSpec((CTX_LEN, hd), lambda b, h: (ctx_blk0 + b, h)),
            pl.BlockSpec((CTX_LEN, hd), lambda b, h: (ctx_blk0 + b, NA_HEADS + h)),
            pl.BlockSpec((CTX_LEN, hd), lambda b, h: (ctx_blk0 + b, 2 * NA_HEADS + h)),
            pl.BlockSpec(memory_space=pl.ANY),
        ],
        out_specs=pl.BlockSpec((CTX_LEN, hd), lambda b, h: (ctx_blk0 + b, h)),
        out_shape=jax.ShapeDtypeStruct((T_ALL, D_MODEL), BF16),
        input_output_aliases={3: 0},
        compiler_params=_cparams(("arbitrary", "arbitrary"), 32),
        name="ctx_attention",
    )(qkv, qkv, qkv, o_all)


def kernel(x, c, ctx, c_ctx, w_mod, b_mod, norm_g, ffn_w_up, ffn_conv_w, ffn_conv_b, ffn_w_down, ssm_w_in, ssm_conv_w, ssm_conv_b, ssm_dt_bias, ssm_a_log, ssm_d, ssm_norm_g, ssm_w_out, hg_w_in, hg_lb, hg_norm_g, hg_w_out, na_w_qkv, na_rpb, na_w_out):
    cvec8 = jnp.concatenate([c, c_ctx[None], jnp.zeros((SUBLANES - BATCH - 1, D_MODEL), F32)], axis=0)
    mod = _adaln(cvec8, w_mod, b_mod)
    ffn_up_bf, ffn_down_bf = ffn_w_up.astype(BF16), ffn_w_down.astype(BF16)
    ssm_out_bf, hg_out_bf, na_out_bf = ssm_w_out.astype(BF16), hg_w_out.astype(BF16), na_w_out.astype(BF16)

    xs, h = _prep(x.reshape(T_LAT, D_MODEL), ctx.reshape(T_CTX, D_MODEL), norm_g[0, 0], mod[0])
    for i in range(DEPTH):
        last = i == DEPTH - 1
        kind, slot = i % N_MIXERS, i // N_MIXERS
        mod_l = mod[i]
        n_rows = T_LAT if last else T_ALL
        op = dict(xs=xs, g_res=norm_g[i, 1], g_ffn=norm_g[i, 2], mod_l=mod_l, n_rows=n_rows, slot=slot)
        if kind == 0:
            n_main = SSM_INNER + SSM_CONV_DIM
            proj = _in_proj(h, ssm_w_in, slot, 1024, "ssd_in", n_dim=n_main)
            dt_raw = _in_proj(h, ssm_w_in, slot, 2 * SSM_HEADS, "ssd_dt", n_dim=2 * SSM_HEADS,
                              first_tile=n_main // (2 * SSM_HEADS))
            xbc = _ssd_conv(proj, ssm_conv_w[slot], ssm_conv_b[slot])
            y2 = _ssd_scan(xbc, dt_raw, ssm_dt_bias[slot], ssm_a_log[slot], ssm_d[slot])
            xs, hf = _out_proj(_ssd_gate_prologue, [((y2, 0), 0), ((y2, 1), 0), (proj, 0)],
                               ssm_norm_g[slot].reshape(1, SSM_INNER), ssm_out_bf, tm=256, name="ssd_out", **op)
        elif kind == 1:
            proj = _in_proj(h, hg_w_in, slot, 1024, "hg_in")
            o2 = _gla_scan(proj, hg_lb, i)
            xs, hf = _out_proj(_hg_readout_prologue, [((o2, 0), 0), ((o2, 1), 0), (proj, 4)],
                               hg_norm_g[slot].reshape(1, D_MODEL), hg_out_bf, tm=ROW_TILE, name="hg_out", **op)
        else:
            qkv = _in_proj(h, na_w_qkv, slot, 1024, "na_in")
            o = _ctx_attention(qkv, _na_attention(qkv, _na_bias_table(na_rpb[slot])))
            xs, hf = _out_proj(_plain_prologue, [(o, 0)], None, na_out_bf, tm=ROW_TILE, name="na_out", **op)
        nxt = min(i + 1, DEPTH - 1)
        xs, h = _ffn(hf, xs, i, ffn_up_bf, ffn_conv_w, ffn_conv_b, ffn_down_bf, norm_g[i, 3], mod_l,
                     norm_g[nxt, 0], mod[nxt], n_rows=n_rows, emit_h=not last)

    return xs.reshape(BATCH, SEQ, D_MODEL)
```

```python
import functools

import numpy as np
import jax
import jax.numpy as jnp
from jax import lax
from jax.experimental import pallas as pl
from jax.experimental.pallas import tpu as pltpu

F32 = jnp.float32
BF16 = jnp.bfloat16

D_MODEL = 2048
BATCH = 2
SEQ = 4096
DEPTH = 4
GRID_W = 64
CTX_LEN = 256
N_MIXERS = 3
RMS_EPS = 1e-6
SSM_INNER = 4096
SSM_HEAD_DIM = 64
SSM_HEADS = 64
SSM_STATE = 128
SSM_GROUPS = 8
SSM_CONV_DIM = SSM_INNER + 2 * SSM_GROUPS * SSM_STATE
SSM_IN_DIM = SSM_INNER + SSM_CONV_DIM + 2 * SSM_HEADS
SSM_CHUNK = 128
HG_HEADS = 16
HG_KEY_DIM = 128
HG_IN_DIM = 5 * D_MODEL
HG_CHUNK = 64
NA_HEADS = 16
NA_HEAD_DIM = 128
NA_WIN_R = 8
NA_WIN_C = 16
ROPE_BASE = 10000.0
FFN_HIDDEN = 5632

T_LAT = BATCH * SEQ
T_CTX = BATCH * CTX_LEN
T_ALL = T_LAT + T_CTX
LANES = 128
SUBLANES = 8
BF16_ROWS = 16
ONES_ROWS = 16
ROW_TILE = 512
FFN_COL_CHUNK = 512
MIB = 1024 * 1024
LOG2E = 1.4426950408889634


def _cparams(sem, vmem_mib):
    return pltpu.CompilerParams(dimension_semantics=sem, vmem_limit_bytes=vmem_mib * MIB)


def _silu(v):
    return v * jax.nn.sigmoid(v)


def _rms(v):
    return v * lax.rsqrt(jnp.mean(v * v, axis=-1, keepdims=True) + RMS_EPS)


def _log1p_unit(e):
    return jnp.log(1.0 + e)


def _softplus(v):
    return jnp.maximum(v, 0.0) + _log1p_unit(jnp.exp(-jnp.abs(v)))


def _dot(a, b):
    return jnp.dot(a, b, preferred_element_type=F32)


def _dot_nt(a, b):
    return lax.dot_general(a, b, (((1,), (1,)), ((), ())), preferred_element_type=F32)


def _dot_tn(a, b):
    return lax.dot_general(a, b, (((0,), (0,)), ((), ())), preferred_element_type=F32)


def _bf16_pieces(v, terms):
    pieces = []
    rem = v
    for _ in range(terms):
        part = rem.astype(BF16)
        rem = rem - part.astype(F32)
        pieces.append(part)
    return pieces


def _dot_sel(m01_rep, v, terms):
    return _dot(m01_rep, jnp.concatenate(_bf16_pieces(v, terms), axis=0))


def _dot_sel_rhs(v, m01_rep, terms):
    return _dot(jnp.concatenate(_bf16_pieces(v, terms), axis=1), m01_rep)


def _mod_row(row0):
    return jnp.where(row0 >= T_LAT, 2, row0 // SEQ)


def _mod_vec(mod_ref, r, idx):
    return mod_ref[pl.ds(r, 1), idx * D_MODEL:(idx + 1) * D_MODEL]


def _norm_mod(x, g, mod_ref, r, shift_idx, scale_idx):
    return _rms(x) * g * (1.0 + _mod_vec(mod_ref, r, scale_idx)) + _mod_vec(mod_ref, r, shift_idx)


def _seg_edges(row0, rows):
    gr = row0 + lax.broadcasted_iota(jnp.int32, (rows, 1), 0)
    pos = jnp.where(gr >= T_LAT, (gr - T_LAT) % CTX_LEN, gr % SEQ)
    length = jnp.where(gr >= T_LAT, CTX_LEN, SEQ)
    return pos == 0, pos == length - 1


def _shift_rows(cur, prev_row, next_row, row0):
    rows = cur.shape[0]
    first, last = _seg_edges(row0, rows)
    ridx = lax.broadcasted_iota(jnp.int32, (rows, 1), 0)
    up = jnp.where(ridx == 0, prev_row, pltpu.roll(cur, 1, axis=0))
    up = jnp.where(first, 0.0, up)
    dn = jnp.where(ridx == rows - 1, next_row, pltpu.roll(cur, rows - 1, axis=0))
    dn = jnp.where(last, 0.0, dn)
    return up, dn


def _token_conv(cur, prev_row, next_row, row0, w_ref, b_ref):
    up, dn = _shift_rows(cur, prev_row, next_row, row0)
    return b_ref[...] + up * w_ref[0:1, :] + cur * w_ref[1:2, :] + dn * w_ref[2:3, :]


def _halo_specs(tm, halo, width, n_rows, col_fn):
    per = tm // halo
    last_blk = n_rows // halo - 1
    prev = pl.BlockSpec((halo, width), lambda i, j: (jnp.maximum(i * per - 1, 0), col_fn(j)))
    nxt = pl.BlockSpec((halo, width), lambda i, j: (jnp.minimum((i + 1) * per, last_blk), col_fn(j)))
    return prev, nxt


def _adaln_body(c_ref, w_ref, b_ref, o_ref):
    s = _silu(c_ref[...]).astype(BF16)
    o_ref[...] = _dot(s, w_ref[...].astype(BF16)) + b_ref[...]


def _adaln(cvec8, w_mod, b_mod):
    tn = 1024
    n_out = 6 * D_MODEL
    return pl.pallas_call(
        _adaln_body,
        grid=(DEPTH, n_out // tn),
        in_specs=[
            pl.BlockSpec((SUBLANES, D_MODEL), lambda l, j: (0, 0)),
            pl.BlockSpec((None, D_MODEL, tn), lambda l, j: (l, 0, j)),
            pl.BlockSpec((None, 1, tn), lambda l, j: (l, 0, j)),
        ],
        out_specs=pl.BlockSpec((None, SUBLANES, tn), lambda l, j: (l, 0, j)),
        out_shape=jax.ShapeDtypeStruct((DEPTH, SUBLANES, n_out), F32),
        compiler_params=_cparams(("arbitrary", "arbitrary"), 40),
        name="adaln",
    )(cvec8, w_mod, b_mod.reshape(DEPTH, 1, n_out))


def _prep_body(lat_ref, ctx_ref, g_ref, mod_ref, xs_ref, h_ref):
    row0 = pl.program_id(0) * ROW_TILE

    def emit(src_ref):
        xv = src_ref[...]
        xs_ref[...] = xv
        h_ref[...] = _norm_mod(xv, g_ref[...], mod_ref, _mod_row(row0), 0, 1).astype(BF16)

    @pl.when(row0 < T_LAT)
    def _():
        emit(lat_ref)

    @pl.when(row0 >= T_LAT)
    def _():
        emit(ctx_ref)


def _prep(x_lat, x_ctx, g_row, mod_l):
    assert T_CTX == ROW_TILE
    n_lat = T_LAT // ROW_TILE
    row = pl.BlockSpec((ROW_TILE, D_MODEL), lambda i: (i, 0))
    return pl.pallas_call(
        _prep_body,
        grid=(T_ALL // ROW_TILE,),
        in_specs=[pl.BlockSpec((ROW_TILE, D_MODEL), lambda i: (jnp.minimum(i, n_lat - 1), 0)),
                  pl.BlockSpec((ROW_TILE, D_MODEL), lambda i: (0, 0)),
                  pl.BlockSpec((1, D_MODEL), lambda i: (0, 0)), pl.BlockSpec(mod_l.shape, lambda i: (0, 0))],
        out_specs=[row, row],
        out_shape=[jax.ShapeDtypeStruct((T_ALL, D_MODEL), F32), jax.ShapeDtypeStruct((T_ALL, D_MODEL), BF16)],
        compiler_params=_cparams(("arbitrary",), 40),
        name="prep",
    )(x_lat, x_ctx, g_row.reshape(1, D_MODEL), mod_l)


def _in_proj_body(h_ref, w_ref, o_ref, wb_ref):
    @pl.when(pl.program_id(1) == 0)
    def _():
        wb_ref[...] = w_ref[...].astype(BF16)

    o_ref[...] = _dot(h_ref[...], wb_ref[...])


def _in_proj(h, w_stack, slot, tn, name, n_dim=None, first_tile=0):
    tm = T_ALL // 8
    n_dim = w_stack.shape[2] if n_dim is None else n_dim
    return pl.pallas_call(
        _in_proj_body,
        grid=(n_dim // tn, T_ALL // tm),
        in_specs=[pl.BlockSpec((tm, D_MODEL), lambda j, i: (i, 0)),
                  pl.BlockSpec((None, D_MODEL, tn), lambda j, i: (slot, 0, first_tile + j))],
        out_specs=pl.BlockSpec((tm, tn), lambda j, i: (i, j)),
        out_shape=jax.ShapeDtypeStruct((T_ALL, n_dim), F32),
        scratch_shapes=[pltpu.VMEM((D_MODEL, tn), BF16)],
        compiler_params=_cparams(("arbitrary", "arbitrary"), 56),
        name=name,
    )(h, w_stack)


def _ssd_gate_prologue(yf_ref, yb_ref, z_ref, g_ref):
    v = (yf_ref[...].astype(F32) + yb_ref[...].astype(F32)) * _silu(z_ref[...])
    return v * g_ref[...], lax.rsqrt(jnp.mean(v * v, axis=-1, keepdims=True) + RMS_EPS)


def _hg_readout_prologue(of_ref, ob_ref, gate_ref, g_ref):
    o = of_ref[...].astype(F32) + ob_ref[...].astype(F32)
    parts = [_rms(o[:, h * LANES:(h + 1) * LANES]) for h in range(o.shape[1] // LANES)]
    return jnp.concatenate(parts, axis=-1) * g_ref[...] * _silu(gate_ref[...]), None


def _plain_prologue(o_ref):
    return o_ref[...], None


def _out_proj_body(prologue, n_in, tm, *refs):
    ins = refs[:n_in]
    w_ref, x_ref, g1_ref, g2_ref, mod_ref, xo_ref, ho_ref = refs[n_in:]
    part, row_scale = prologue(*ins)
    y = _dot(part.astype(BF16), w_ref[...])
    if row_scale is not None:
        y = y * row_scale
    r = _mod_row(pl.program_id(0) * tm)
    xn = x_ref[...] + _mod_vec(mod_ref, r, 2) * (_rms(y) * g1_ref[...])
    xo_ref[...] = xn
    ho_ref[...] = _norm_mod(xn, g2_ref[...], mod_ref, r, 3, 4).astype(BF16)


def _out_proj(prologue, row_ins, gain, w_stack, slot, xs, g_res, g_ffn, mod_l, *, n_rows, tm, name):
    k_dim = w_stack.shape[1]
    in_specs, args = [], []
    for arr, cb in row_ins:
        if isinstance(arr, tuple):
            in_specs.append(pl.BlockSpec((None, tm, k_dim), functools.partial(
                lambda i, ld, cb: (ld, i, cb), ld=arr[1], cb=cb)))
            args.append(arr[0])
        else:
            in_specs.append(pl.BlockSpec((tm, k_dim), functools.partial(lambda i, cb: (i, cb), cb=cb)))
            args.append(arr)
    if gain is not None:
        in_specs.append(pl.BlockSpec((1, k_dim), lambda i: (0, 0)))
        args.append(gain)
    n_in = len(args)
    row = pl.BlockSpec((tm, D_MODEL), lambda i: (i, 0))
    vec = pl.BlockSpec((1, D_MODEL), lambda i: (0, 0))
    in_specs += [pl.BlockSpec((None, k_dim, D_MODEL), lambda i: (slot, 0, 0), pipeline_mode=pl.Buffered(1)),
                 row, vec, vec, pl.BlockSpec(mod_l.shape, lambda i: (0, 0))]
    return pl.pallas_call(
        functools.partial(_out_proj_body, prologue, n_in, tm),
        grid=(n_rows // tm,),
        in_specs=in_specs,
        out_specs=[row, row],
        out_shape=[jax.ShapeDtypeStruct((n_rows, D_MODEL), F32), jax.ShapeDtypeStruct((n_rows, D_MODEL), BF16)],
        compiler_params=_cparams(("arbitrary",), 60),
        name=name,
    )(*args, w_stack, xs, g_res.reshape(1, D_MODEL), g_ffn.reshape(1, D_MODEL), mod_l)


def _ffn_body(emit_h, has_ctx, h_ref, hp_ref, hn_ref, wa_ref, wv_ref, cwa_ref, cwv_ref, cba_ref, cbv_ref, wd_ref,
              x_ref, g_res_ref, mod_ref, g_next_ref, modn_ref, *out_and_scratch):
    if emit_h:
        xo_ref, ho_ref, hall_ref, acc_ref = out_and_scratch
    else:
        xo_ref, hall_ref, acc_ref = out_and_scratch
    k = pl.program_id(1)
    tm = ROW_TILE
    row0 = pl.program_id(0) * tm
    pad = BF16_ROWS
    rows = tm + 2 * pad
    s8 = SUBLANES

    @pl.when(k == 0)
    def _():
        hall_ref[pl.ds(0, pad), :] = hp_ref[...]
        hall_ref[pl.ds(pad, tm), :] = h_ref[...]
        hall_ref[pl.ds(pad + tm, pad), :] = hn_ref[...]
        acc_ref[...] = jnp.zeros_like(acc_ref)

    hall = hall_ref[...]
    edge_first = row0 % SEQ == 0
    edge_last = ((row0 + tm) % SEQ == 0) | (row0 + tm == T_ALL)
    is_ctx = row0 >= T_LAT
    joints = list(range(CTX_LEN, tm, CTX_LEN)) if has_ctx else []
    sub = lax.broadcasted_iota(jnp.int32, (s8, 1), 0)

    def patch(v, start, cond):
        return jnp.concatenate([v[:start], jnp.where(cond, 0.0, v[start:start + s8]), v[start + s8:]], axis=0)

    def conv(w_ref, cw_ref, cb_ref, cs):
        u = _dot(hall, w_ref[:, cs])
        u = patch(u, pad - s8, edge_first)
        u = patch(u, pad + tm, edge_last)
        up = pltpu.roll(u, 1, axis=0)[pad:pad + tm]
        dn = pltpu.roll(u, rows - 1, axis=0)[pad:pad + tm]
        for j in joints:
            up = patch(up, j, is_ctx & (sub == 0))
            dn = patch(dn, j - s8, is_ctx & (sub == s8 - 1))
        return cb_ref[:, cs] + up * cw_ref[0:1, cs] + u[pad:pad + tm] * cw_ref[1:2, cs] + dn * cw_ref[2:3, cs]

    chunks = [pl.ds(c * FFN_COL_CHUNK, FFN_COL_CHUNK) for c in range(wa_ref.shape[1] // FFN_COL_CHUNK)]
    acts = [(_silu(conv(wa_ref, cwa_ref, cba_ref, cs)) * conv(wv_ref, cwv_ref, cbv_ref, cs)).astype(BF16)
            for cs in chunks]
    acc_ref[...] += _dot(jnp.concatenate(acts, axis=1), wd_ref[...])

    @pl.when(k == pl.num_programs(1) - 1)
    def _():
        r = _mod_row(row0)
        xn = x_ref[...] + _mod_vec(mod_ref, r, 5) * (_rms(acc_ref[...]) * g_res_ref[...])
        xo_ref[...] = xn
        if emit_h:
            ho_ref[...] = _norm_mod(xn, g_next_ref[...], modn_ref, r, 0, 1).astype(BF16)


def _ffn(h, xs, layer, w_up, conv_w, conv_b, w_down, g_res, mod_l, g_next, mod_next, *, n_rows, emit_h):
    tm, tk = ROW_TILE, 512
    kt = FFN_HIDDEN // tk
    h_prev, h_next = _halo_specs(tm, BF16_ROWS, D_MODEL, n_rows, lambda k: 0)
    row = pl.BlockSpec((tm, D_MODEL), lambda i, k: (i, 0))
    vec = pl.BlockSpec((1, D_MODEL), lambda i, k: (0, 0))
    tab = pl.BlockSpec(mod_l.shape, lambda i, k: (0, 0))
    out_specs = [row, row] if emit_h else [row]
    out_shape = [jax.ShapeDtypeStruct((n_rows, D_MODEL), F32)]
    if emit_h:
        out_shape.append(jax.ShapeDtypeStruct((n_rows, D_MODEL), BF16))
    conv_b3 = conv_b.reshape(DEPTH, 1, 2 * FFN_HIDDEN)

    res = pl.pallas_call(
        functools.partial(_ffn_body, emit_h, n_rows > T_LAT),
        grid=(n_rows // tm, kt),
        in_specs=[
            row, h_prev, h_next,
            pl.BlockSpec((None, D_MODEL, tk), lambda i, k: (layer, 0, k)),
            pl.BlockSpec((None, D_MODEL, tk), lambda i, k: (layer, 0, k + kt)),
            pl.BlockSpec((None, 3, tk), lambda i, k: (layer, 0, k)),
            pl.BlockSpec((None, 3, tk), lambda i, k: (layer, 0, k + kt)),
            pl.BlockSpec((None, 1, tk), lambda i, k: (layer, 0, k)),
            pl.BlockSpec((None, 1, tk), lambda i, k: (layer, 0, k + kt)),
            pl.BlockSpec((None, tk, D_MODEL), lambda i, k: (layer, k, 0)),
            row, vec, tab, vec, tab,
        ],
        out_specs=out_specs,
        out_shape=out_shape,
        scratch_shapes=[pltpu.VMEM((tm + 2 * BF16_ROWS, D_MODEL), BF16), pltpu.VMEM((tm, D_MODEL), F32)],
        compiler_params=_cparams(("arbitrary", "arbitrary"), 56),
        name="conv_ffn",
    )(h, h, h, w_up, w_up, conv_w, conv_w, conv_b3, conv_b3, w_down, xs, g_res.reshape(1, D_MODEL), mod_l,
      g_next.reshape(1, D_MODEL), mod_next)
    return (res[0], res[1]) if emit_h else (res[0], None)


def _in_proj_conv_body(tm, h_ref, hp_ref, hn_ref, w_ref, cw_ref, cb_ref, o_ref, wb_ref):
    @pl.when(pl.program_id(1) == 0)
    def _():
        wb_ref[...] = w_ref[...].astype(BF16)

    w = wb_ref[...]
    u = _dot(h_ref[...], w)
    uh = _dot(jnp.concatenate([hp_ref[...], hn_ref[...]], axis=0), w)
    conv = _token_conv(u, uh[BF16_ROWS - 1:BF16_ROWS], uh[BF16_ROWS:BF16_ROWS + 1], pl.program_id(1) * tm,
                       cw_ref, cb_ref)
    o_ref[...] = _silu(conv).astype(o_ref.dtype)


def _in_proj_conv(h, w_stack, slot, conv_w, conv_b, tn, first_tile, name):
    tm = T_ALL // 8
    n_dim = conv_w.shape[1]
    per = tm // BF16_ROWS
    last_blk = T_ALL // BF16_ROWS - 1
    return pl.pallas_call(
        functools.partial(_in_proj_conv_body, tm),
        grid=(n_dim // tn, T_ALL // tm),
        in_specs=[pl.BlockSpec((tm, D_MODEL), lambda j, i: (i, 0)),
                  pl.BlockSpec((BF16_ROWS, D_MODEL), lambda j, i: (jnp.maximum(i * per - 1, 0), 0)),
                  pl.BlockSpec((BF16_ROWS, D_MODEL), lambda j, i: (jnp.minimum((i + 1) * per, last_blk), 0)),
                  pl.BlockSpec((None, D_MODEL, tn), lambda j, i: (slot, 0, first_tile + j)),
                  pl.BlockSpec((3, tn), lambda j, i: (0, j)),
                  pl.BlockSpec((1, tn), lambda j, i: (0, j))],
        out_specs=pl.BlockSpec((tm, tn), lambda j, i: (i, j)),
        out_shape=jax.ShapeDtypeStruct((T_ALL, n_dim), BF16),
        scratch_shapes=[pltpu.VMEM((D_MODEL, tn), BF16)],
        compiler_params=_cparams(("arbitrary", "arbitrary"), 56),
        name=name,
    )(h, h, h, w_stack, conv_w, conv_b.reshape(1, -1))


def _scan_chunk_index(d, b, s, n_lat, n_ctx):
    ctx0 = BATCH * n_lat + b * n_ctx
    fwd = jnp.where(s < n_ctx, ctx0 + s, b * n_lat + s - n_ctx)
    bwd = jnp.where(s < n_ctx, ctx0 + n_ctx - 1 - s, b * n_lat + n_lat - 1 - (s - n_ctx))
    return jnp.where(d == 0, fwd, bwd)


SSD_CUM_TERMS = 3
SSD_DT_TERMS = 2


def _ssd_scan_consts():
    q = SSM_CHUNK
    i = np.arange(q)
    pfx = np.zeros((2, q + ONES_ROWS, q), np.float32)
    pfx[0, :q] = i[:, None] >= i[None, :]
    pfx[1, :q] = i[:, None] <= i[None, :]
    pfx[:, q:] = 1.0
    pfx = np.tile(pfx, (1, 1, SSD_CUM_TERMS))
    expand = np.repeat(np.eye(SSM_HEADS, dtype=np.float32), SSM_HEAD_DIM, axis=1)
    expand = np.tile(expand, (SSD_DT_TERMS, 1))
    return jnp.asarray(pfx, BF16), jnp.asarray(expand, BF16)


def _ssd_scan_body(x_ref, b_ref, c_ref, dtraw_ref, dtb_ref, alog_ref, dskip_ref, pfx_ref, exp_ref, y_ref, s_ref):
    q = SSM_CHUNK
    hp = 2 * SSM_HEAD_DIM
    d = pl.program_id(0)

    @pl.when(pl.program_id(2) == 0)
    def _():
        s_ref[...] = jnp.zeros_like(s_ref)

    def pick(v):
        return jnp.where(d == 0, v[:, :SSM_HEADS], v[:, SSM_HEADS:])

    dt2 = _softplus(dtraw_ref[...] + dtb_ref[...])
    cum2 = _dot_sel(pfx_ref[...], dt2 * (-LOG2E * jnp.exp(alog_ref[...])), SSD_CUM_TERMS)
    acum2 = cum2[:q]
    acum = pick(acum2)
    tot = pick(cum2[q:q + SUBLANES])[0:1]
    acum_t2 = acum2.T
    acum_t = jnp.where(d == 0, acum_t2[:SSM_HEADS], acum_t2[SSM_HEADS:])
    dt_wide = _dot_sel_rhs(pick(dt2), exp_ref[...], SSD_DT_TERMS)

    ii = lax.broadcasted_iota(jnp.int32, (q, q), 0)
    jj = lax.broadcasted_iota(jnp.int32, (q, q), 1)
    causal = (jj - ii) * (1 - 2 * d) <= 0
    low_half = lax.broadcasted_iota(jnp.int32, (q, hp), 1) < SSM_HEAD_DIM

    heads_per_group = SSM_HEADS // SSM_GROUPS
    n_pairs = SSM_HEADS // 2

    b_gs = [b_ref[:, g * SSM_STATE:(g + 1) * SSM_STATE].astype(BF16) for g in range(SSM_GROUPS)]
    c_bf = [c_ref[:, g * SSM_STATE:(g + 1) * SSM_STATE].astype(BF16) for g in range(SSM_GROUPS)]
    c_gs = [c.astype(F32) for c in c_bf]
    cbs = [jnp.where(causal, _dot_nt(c_bf[g], b_gs[g]), 0.0) for g in range(SSM_GROUPS)]

    cols, lhs = [], []
    for h in range(SSM_HEADS):
        g = h // heads_per_group
        col = jnp.broadcast_to(acum[:, h:h + 1], (q, q))
        seg = jnp.where(causal, col - acum_t[h:h + 1, :], 0.0)
        scores = jnp.exp2(seg) * cbs[g]
        c_dec = c_gs[g] * jnp.exp2(col)
        lhs.append(jnp.concatenate([scores.astype(BF16), c_dec.astype(BF16)], axis=1))
        cols.append(col)

    xdts = []
    for p in range(n_pairs):
        lo = p * hp
        x_pair = x_ref[:, lo:lo + hp].astype(F32)
        xdt = x_pair * dt_wide[:, lo:lo + hp]
        rhs = jnp.concatenate([xdt.astype(BF16), s_ref[:, lo:lo + hp].astype(BF16)], axis=0)
        y_pair = jnp.where(low_half, _dot(lhs[2 * p], rhs), _dot(lhs[2 * p + 1], rhs))
        y_ref[:, lo:lo + hp] = (y_pair + dskip_ref[:, lo:lo + hp] * x_pair).astype(y_ref.dtype)
        xdts.append(xdt)

    for p in range(n_pairs):
        lo = p * hp
        g = (2 * p) // heads_per_group
        tot_pair = jnp.where(low_half[0:1], jnp.broadcast_to(tot[:, 2 * p:2 * p + 1], (1, hp)),
                             jnp.broadcast_to(tot[:, 2 * p + 1:2 * p + 2], (1, hp)))
        to_end = jnp.exp2(tot_pair - jnp.where(low_half, cols[2 * p], cols[2 * p + 1]))
        s_ref[:, lo:lo + hp] = (s_ref[:, lo:lo + hp] * jnp.exp2(tot_pair)
                                + _dot_tn(b_gs[g], (xdts[p] * to_end).astype(BF16)))


def _ssd_scan(xbc, dt_raw, dt_bias, a_log, d_skip):
    q = SSM_CHUNK
    n_lat, n_ctx = SEQ // q, CTX_LEN // q
    pfx, expand = _ssd_scan_consts()
    n_heads2 = 2 * SSM_HEADS
    b_col = SSM_INNER // (SSM_GROUPS * SSM_STATE)
    d_wide = jnp.repeat(d_skip, SSM_HEAD_DIM, axis=1).reshape(2, 1, SSM_INNER)

    def chunk(d, b, s):
        return _scan_chunk_index(d, b, s, n_lat, n_ctx)

    return pl.pallas_call(
        _ssd_scan_body,
        grid=(2, BATCH, n_lat + n_ctx),
        in_specs=[
            pl.BlockSpec((q, SSM_INNER), lambda d, b, s: (chunk(d, b, s), 0)),
            pl.BlockSpec((q, SSM_GROUPS * SSM_STATE), lambda d, b, s: (chunk(d, b, s), b_col)),
            pl.BlockSpec((q, SSM_GROUPS * SSM_STATE), lambda d, b, s: (chunk(d, b, s), b_col + 1)),
            pl.BlockSpec((q, n_heads2), lambda d, b, s: (chunk(d, b, s), 0)),
            pl.BlockSpec((1, n_heads2), lambda d, b, s: (0, 0)),
            pl.BlockSpec((1, n_heads2), lambda d, b, s: (0, 0)),
            pl.BlockSpec((None, 1, SSM_INNER), lambda d, b, s: (d, 0, 0)),
            pl.BlockSpec((None,) + pfx.shape[1:], lambda d, b, s: (d, 0, 0)),
            pl.BlockSpec(expand.shape, lambda d, b, s: (0, 0)),
        ],
        out_specs=pl.BlockSpec((None, q, SSM_INNER), lambda d, b, s: (d, chunk(d, b, s), 0)),
        out_shape=jax.ShapeDtypeStruct((2, T_ALL, SSM_INNER), BF16),
        scratch_shapes=[pltpu.VMEM((SSM_STATE, SSM_INNER), F32)],
        compiler_params=_cparams(("arbitrary", "arbitrary", "arbitrary"), 48),
        name="ssd_scan",
    )(xbc, xbc, xbc, dt_raw, dt_bias.reshape(1, n_heads2), a_log.reshape(1, n_heads2), d_wide, pfx, expand)


GLA_SUM_TERMS = 2


def _gla_consts():
    q = HG_CHUNK
    levels = int(np.log2(q))
    i = np.arange(q)
    mats = [(i[:, None] >= i[None, :]).astype(np.float32), (i[None, :] > i[:, None]).astype(np.float32)]
    masks = [np.eye(q, dtype=np.float32)]
    for lv in range(levels):
        s = 1 << lv
        blk, pos = i // (2 * s), i % (2 * s)
        bound = blk * 2 * s + s - 1
        right = pos >= s
        t = i[None, :]
        m_right = right[:, None] & (t > bound[:, None]) & (t <= i[:, None])
        m_left = (~right)[:, None] & (t > i[:, None]) & (t <= bound[:, None])
        mats.append((m_right | m_left).astype(np.float32))
        masks.append((right[:, None] & (~right)[None, :] & (blk[:, None] == blk[None, :])).astype(np.float32))
    mats.append(np.ones((ONES_ROWS, q), np.float32))
    fwd_m, fwd_k = np.concatenate(mats, axis=0), np.stack(masks)
    flip_rows = np.concatenate([np.arange(k * q, (k + 1) * q)[::-1] for k in range(levels + 2)]
                               + [np.arange((levels + 2) * q, (levels + 2) * q + ONES_ROWS)])
    bwd_m = fwd_m[flip_rows][:, ::-1]
    bwd_k = fwd_k[:, ::-1, ::-1]
    mats2 = np.tile(np.stack([fwd_m, bwd_m]), (1, 1, GLA_SUM_TERMS))
    return (jnp.asarray(mats2, BF16), jnp.asarray(np.stack([fwd_k, bwd_k]), F32), levels)


def _gla_body(levels, layer, q_ref, v_ref, f_ref, lbp_ref, mats_ref, masks_ref, o_ref, st_ref):
    q = HG_CHUNK

    @pl.when(pl.program_id(2) == 0)
    def _():
        st_ref[...] = jnp.zeros_like(st_ref)

    lbp = lbp_ref[...]
    e = jnp.exp(lbp - jnp.max(lbp, axis=0, keepdims=True))
    sm = e / jnp.sum(e, axis=0, keepdims=True)
    lb = jnp.sum(sm[1:layer + 1], axis=0, keepdims=True) if layer > 0 else jnp.zeros((1, lbp.shape[1]), F32)

    f = f_ref[...]
    ef = jnp.exp(-jnp.abs(f))
    log_sig = jnp.minimum(f, 0.0) - _log1p_unit(ef)
    la = jnp.log(lb)
    lc = jnp.log1p(-lb) + log_sig
    logf = jnp.maximum(la, lc) + _log1p_unit(jnp.exp(-jnp.abs(la - lc)))
    key = (1.0 - lb) * (jnp.where(f >= 0.0, ef, 1.0) / (1.0 + ef))
    qs = _silu(q_ref[...])

    sums = _dot_sel(mats_ref[...], logf, GLA_SUM_TERMS)
    q_in = (qs * jnp.exp(sums[:q])).astype(BF16)
    k_out = (key * jnp.exp(sums[q:2 * q])).astype(BF16)
    dec = jnp.exp(sums[(levels + 2) * q:(levels + 2) * q + 1])
    qs_bf, key_bf = qs.astype(BF16), key.astype(BF16)
    q_lv, k_lv = [], []
    for lv in range(levels):
        fac = jnp.exp(sums[(lv + 2) * q:(lv + 3) * q])
        q_lv.append((qs * fac).astype(BF16))
        k_lv.append((key * fac).astype(BF16))
    v_bf = v_ref[...].astype(BF16)

    heads = [slice(h * LANES, (h + 1) * LANES) for h in range(HG_HEADS)]
    attn = masks_ref[0][None] * jnp.stack([_dot_nt(qs_bf[:, sl], key_bf[:, sl]) for sl in heads], axis=0)
    for lv in range(levels):
        attn = attn + masks_ref[lv + 1][None] * jnp.stack(
            [_dot_nt(q_lv[lv][:, sl], k_lv[lv][:, sl]) for sl in heads], axis=0)
    attn = attn.astype(BF16)
    st_bf = st_ref[...].astype(BF16)
    o_ref[...] = jnp.concatenate(
        [_dot(attn[h], v_bf[:, sl]) + _dot_nt(q_in[:, sl], st_bf[h]) for h, sl in enumerate(heads)],
        axis=1).astype(o_ref.dtype)
    for h, sl in enumerate(heads):
        st_ref[h] = st_ref[h] * dec[:, sl] + _dot_tn(v_bf[:, sl], k_out[:, sl])


def _gla_scan(proj, hg_lb, layer):
    q = HG_CHUNK
    n_lat, n_ctx = SEQ // q, CTX_LEN // q
    mats, masks, levels = _gla_consts()

    def chunk(d, b, s):
        return _scan_chunk_index(d, b, s, n_lat, n_ctx)

    return pl.pallas_call(
        functools.partial(_gla_body, levels, layer),
        grid=(2, BATCH, n_lat + n_ctx),
        in_specs=[
            pl.BlockSpec((q, D_MODEL), lambda d, b, s: (chunk(d, b, s), 0)),
            pl.BlockSpec((q, D_MODEL), lambda d, b, s: (chunk(d, b, s), 1)),
            pl.BlockSpec((q, D_MODEL), lambda d, b, s: (chunk(d, b, s), 2 + d)),
            pl.BlockSpec((None, DEPTH, D_MODEL), lambda d, b, s: (d, 0, 0)),
            pl.BlockSpec((None,) + mats.shape[1:], lambda d, b, s: (d, 0, 0)),
            pl.BlockSpec((None,) + masks.shape[1:], lambda d, b, s: (d, 0, 0, 0)),
        ],
        out_specs=pl.BlockSpec((None, q, D_MODEL), lambda d, b, s: (d, chunk(d, b, s), 0)),
        out_shape=jax.ShapeDtypeStruct((2, T_ALL, D_MODEL), BF16),
        scratch_shapes=[pltpu.VMEM((HG_HEADS, LANES, HG_KEY_DIM), F32)],
        compiler_params=_cparams(("arbitrary", "arbitrary", "arbitrary"), 48),
        name="gla_scan",
    )(proj, proj, proj, hg_lb, mats, masks)


def _na_bias_body(rpb_ref, o_ref):
    h = pl.program_id(0)
    n_dc = 2 * NA_WIN_C - 1
    qc = lax.broadcasted_iota(jnp.int32, (GRID_W, LANES), 0)
    kc = lax.broadcasted_iota(jnp.int32, (GRID_W, LANES), 1) % GRID_W
    c0 = jnp.clip(qc - NA_WIN_C // 2, 0, GRID_W - NA_WIN_C)
    col_in = (kc >= c0) & (kc < c0 + NA_WIN_C)
    dc = jnp.clip(kc - qc + NA_WIN_C - 1, 0, n_dc - 1)
    low = lax.broadcasted_iota(jnp.int32, (GRID_W, LANES), 1) < GRID_W
    tiles = []
    for dr in range(2 * NA_WIN_R - 1):
        t = jnp.zeros((GRID_W, LANES), F32)
        for k in range(n_dc):
            t = jnp.where(dc == k, rpb_ref[(h * (2 * NA_WIN_R - 1) + dr) * n_dc + k], t)
        tiles.append(jnp.where(col_in, t, -jnp.inf))
    for dr in range(2 * NA_WIN_R - 2):
        o_ref[dr] = jnp.where(low, tiles[dr], tiles[dr + 1])


def _na_bias_table(rpb):
    n_dr = 2 * NA_WIN_R - 2
    return pl.pallas_call(
        _na_bias_body,
        grid=(NA_HEADS,),
        in_specs=[pl.BlockSpec(memory_space=pltpu.SMEM)],
        out_specs=pl.BlockSpec((None, n_dr, GRID_W, LANES), lambda h: (h, 0, 0, 0)),
        out_shape=jax.ShapeDtypeStruct((NA_HEADS, n_dr, GRID_W, LANES), F32),
        compiler_params=_cparams(("arbitrary",), 32),
        name="na_bias",
    )(rpb.reshape(-1))


def _rope_tables():
    quarter = NA_HEAD_DIM // 4
    inv = ROPE_BASE ** (-np.arange(quarter, dtype=np.float64) / quarter)
    t = np.arange(SEQ)
    row, col = t // GRID_W, t % GRID_W
    ang = np.concatenate([row[:, None] * inv[None], row[:, None] * inv[None],
                          col[:, None] * inv[None], col[:, None] * inv[None]], axis=1)
    sign = np.tile(np.concatenate([-np.ones(quarter), np.ones(quarter)]), 2)[None]
    return jnp.asarray(np.cos(ang), F32), jnp.asarray(np.sin(ang) * sign, F32)


def _rope(v, cos, sin_signed):
    quarter = NA_HEAD_DIM // 4
    lane = lax.broadcasted_iota(jnp.int32, v.shape, 1)
    first = (lane % (2 * quarter)) < quarter
    partner = jnp.where(first, pltpu.roll(v, NA_HEAD_DIM - quarter, axis=1), pltpu.roll(v, quarter, axis=1))
    return v * cos + partner * sin_signed


def _na_body(rows_per_step, q_ref, k_ref, v_ref, kc_ref, vc_ref, cosq_ref, sinq_ref, cos_ref, sin_ref, bias_ref,
             o_ref, kr_ref, vb_ref):
    rb = pl.program_id(2)
    n_rows = SEQ // GRID_W
    scale = NA_HEAD_DIM ** -0.5

    @pl.when(rb == 0)
    def _():
        kr_ref[...] = _rope(k_ref[...], cos_ref[...], sin_ref[...]).astype(BF16)
        vb_ref[...] = v_ref[...].astype(BF16)

    qr = (_rope(q_ref[...], cosq_ref[...], sinq_ref[...]) * scale).astype(BF16)
    kc = kc_ref[...].astype(BF16)
    vc = vc_ref[...].astype(BF16)
    starts, s_rows = [], []
    for lr in range(rows_per_step):
        r = rb * rows_per_step + lr
        r0 = jnp.clip(r - NA_WIN_R // 2, 0, n_rows - NA_WIN_R)
        start = pl.multiple_of(r0 * GRID_W, GRID_W)
        dr0 = r0 - r + NA_WIN_R - 1
        bias = jnp.concatenate([bias_ref[dr0 + 2 * p] for p in range(NA_WIN_R // 2)], axis=1)
        k_win = kr_ref[pl.ds(start, NA_WIN_R * GRID_W), :]
        s_rows.append(_dot_nt(qr[lr * GRID_W:(lr + 1) * GRID_W], k_win) + bias)
        starts.append(start)
    s_loc = jnp.concatenate(s_rows, axis=0)
    s_ctx = _dot_nt(qr, kc)
    m = jnp.maximum(jnp.max(s_loc, axis=-1, keepdims=True), jnp.max(s_ctx, axis=-1, keepdims=True))
    p_loc = jnp.exp(s_loc - m)
    p_ctx = jnp.exp(s_ctx - m)
    denom = jnp.sum(p_loc, axis=-1, keepdims=True) + jnp.sum(p_ctx, axis=-1, keepdims=True)
    p_loc = p_loc.astype(BF16)
    o_loc = jnp.concatenate(
        [_dot(p_loc[lr * GRID_W:(lr + 1) * GRID_W], vb_ref[pl.ds(starts[lr], NA_WIN_R * GRID_W), :])
         for lr in range(rows_per_step)], axis=0)
    o = o_loc + _dot(p_ctx.astype(BF16), vc)
    o_ref[...] = (o / denom).astype(o_ref.dtype)


def _na_attention(qkv, bias_tbl):
    rows_per_step = 32
    tq = rows_per_step * GRID_W
    steps = SEQ // tq
    cos, sin = _rope_tables()
    hd = NA_HEAD_DIM
    ctx_blk0 = T_LAT // CTX_LEN
    return pl.pallas_call(
        functools.partial(_na_body, rows_per_step),
        grid=(BATCH, NA_HEADS, steps),
        in_specs=[
            pl.BlockSpec((tq, hd), lambda b, h, r: (b * steps + r, h)),
            pl.BlockSpec((SEQ, hd), lambda b, h, r: (b, NA_HEADS + h)),
            pl.BlockSpec((SEQ, hd), lambda b, h, r: (b, 2 * NA_HEADS + h)),
            pl.BlockSpec((CTX_LEN, hd), lambda b, h, r: (ctx_blk0 + b, NA_HEADS + h)),
            pl.BlockSpec((CTX_LEN, hd), lambda b, h, r: (ctx_blk0 + b, 2 * NA_HEADS + h)),
            pl.BlockSpec((tq, hd), lambda b, h, r: (r, 0)),
            pl.BlockSpec((tq, hd), lambda b, h, r: (r, 0)),
            pl.BlockSpec((SEQ, hd), lambda b, h, r: (0, 0)),
            pl.BlockSpec((SEQ, hd), lambda b, h, r: (0, 0)),
            pl.BlockSpec((None,) + bias_tbl.shape[1:], lambda b, h, r: (h, 0, 0, 0)),
        ],
        out_specs=pl.BlockSpec((tq, hd), lambda b, h, r: (b * steps + r, h)),
        out_shape=jax.ShapeDtypeStruct((T_ALL, D_MODEL), BF16),
        scratch_shapes=[pltpu.VMEM((SEQ, hd), BF16), pltpu.VMEM((SEQ, hd), BF16)],
        compiler_params=_cparams(("arbitrary", "arbitrary", "arbitrary"), 48),
        name="na_attention",
    )(qkv, qkv, qkv, qkv, qkv, cos, sin, cos, sin, bias_tbl)


def _ctx_attn_body(q_ref, k_ref, v_ref, lat_out_ref, o_ref):
    del lat_out_ref
    scale = NA_HEAD_DIM ** -0.5
    s = _dot_nt((q_ref[...] * scale).astype(BF16), k_ref[...].astype(BF16))
    p = jnp.exp(s - jnp.max(s, axis=-1, keepdims=True))
    o = _dot(p.astype(BF16), v_ref[...].astype(BF16))
    o_ref[...] = (o / jnp.sum(p, axis=-1, keepdims=True)).astype(o_ref.dtype)


def _ctx_attention(qkv, o_all):
    hd = NA_HEAD_DIM
    ctx_blk0 = T_LAT // CTX_LEN
    return pl.pallas_call(
        _ctx_attn_body,
        grid=(BATCH, NA_HEADS),
        in_specs=[
            pl.BlockSpec((CTX_LEN, hd), lambda b, h: (ctx_blk0 + b, h)),
            pl.BlockSpec((CTX_LEN, hd), lambda b, h: (ctx_blk0 + b, NA_HEADS + h)),
            pl.BlockSpec((CTX_LEN, hd), lambda b, h: (ctx_blk0 + b, 2 * NA_HEADS + h)),
            pl.BlockSpec(memory_space=pl.ANY),
        ],
        out_specs=pl.BlockSpec((CTX_LEN, hd), lambda b, h: (ctx_blk0 + b, h)),
        out_shape=jax.ShapeDtypeStruct((T_ALL, D_MODEL), BF16),
        input_output_aliases={3: 0},
        compiler_params=_cparams(("arbitrary", "arbitrary"), 32),
        name="ctx_attention",
    )(qkv, qkv, qkv, o_all)


def kernel(x, c, ctx, c_ctx, w_mod, b_mod, norm_g, ffn_w_up, ffn_conv_w, ffn_conv_b, ffn_w_down, ssm_w_in, ssm_conv_w, ssm_conv_b, ssm_dt_bias, ssm_a_log, ssm_d, ssm_norm_g, ssm_w_out, hg_w_in, hg_lb, hg_norm_g, hg_w_out, na_w_qkv, na_rpb, na_w_out):
    cvec8 = jnp.concatenate([c, c_ctx[None], jnp.zeros((SUBLANES - BATCH - 1, D_MODEL), F32)], axis=0)
    mod = _adaln(cvec8, w_mod, b_mod)
    ffn_up_bf, ffn_down_bf = ffn_w_up.astype(BF16), ffn_w_down.astype(BF16)
    ssm_out_bf, hg_out_bf, na_out_bf = ssm_w_out.astype(BF16), hg_w_out.astype(BF16), na_w_out.astype(BF16)

    xs, h = _prep(x.reshape(T_LAT, D_MODEL), ctx.reshape(T_CTX, D_MODEL), norm_g[0, 0], mod[0])
    for i in range(DEPTH):
        last = i == DEPTH - 1
        kind, slot = i % N_MIXERS, i // N_MIXERS
        mod_l = mod[i]
        n_rows = T_LAT if last else T_ALL
        op = dict(xs=xs, g_res=norm_g[i, 1], g_ffn=norm_g[i, 2], mod_l=mod_l, n_rows=n_rows, slot=slot)
        if kind == 0:
            tn = 1024
            z = _in_proj(h, ssm_w_in, slot, tn, "ssd_in_z", n_dim=SSM_INNER)
            xbc = _in_proj_conv(h, ssm_w_in, slot, ssm_conv_w[slot], ssm_conv_b[slot], tn, SSM_INNER // tn,
                                "ssd_in_xbc")
            dt_raw = _in_proj(h, ssm_w_in, slot, 2 * SSM_HEADS, "ssd_dt", n_dim=2 * SSM_HEADS,
                              first_tile=(SSM_INNER + SSM_CONV_DIM) // (2 * SSM_HEADS))
            y2 = _ssd_scan(xbc, dt_raw, ssm_dt_bias[slot], ssm_a_log[slot], ssm_d[slot])
            xs, hf = _out_proj(_ssd_gate_prologue, [((y2, 0), 0), ((y2, 1), 0), (z, 0)],
                               ssm_norm_g[slot].reshape(1, SSM_INNER), ssm_out_bf, tm=256, name="ssd_out", **op)
        elif kind == 1:
            proj = _in_proj(h, hg_w_in, slot, 1024, "hg_in")
            o2 = _gla_scan(proj, hg_lb, i)
            xs, hf = _out_proj(_hg_readout_prologue, [((o2, 0), 0), ((o2, 1), 0), (proj, 4)],
                               hg_norm_g[slot].reshape(1, D_MODEL), hg_out_bf, tm=ROW_TILE, name="hg_out", **op)
        else:
            qkv = _in_proj(h, na_w_qkv, slot, 1024, "na_in")
            o = _ctx_attention(qkv, _na_attention(qkv, _na_bias_table(na_rpb[slot])))
            xs, hf = _out_proj(_plain_prologue, [(o, 0)], None, na_out_bf, tm=ROW_TILE, name="na_out", **op)
        nxt = min(i + 1, DEPTH - 1)
        xs, h = _ffn(hf, xs, i, ffn_up_bf, ffn_conv_w, ffn_conv_b, ffn_down_bf, norm_g[i, 3], mod_l,
                     norm_g[nxt, 0], mod[nxt], n_rows=n_rows, emit_h=not last)

    return xs.reshape(BATCH, SEQ, D_MODEL)
```

```python
import functools

import numpy as np
import jax
import jax.numpy as jnp
from jax import lax
from jax.experimental import pallas as pl
from jax.experimental.pallas import tpu as pltpu

F32 = jnp.float32
BF16 = jnp.bfloat16

D_MODEL = 2048
BATCH = 2
SEQ = 4096
DEPTH = 4
GRID_W = 64
CTX_LEN = 256
N_MIXERS = 3
RMS_EPS = 1e-6
SSM_INNER = 4096
SSM_HEAD_DIM = 64
SSM_HEADS = 64
SSM_STATE = 128
SSM_GROUPS = 8
SSM_CONV_DIM = SSM_INNER + 2 * SSM_GROUPS * SSM_STATE
SSM_IN_DIM = SSM_INNER + SSM_CONV_DIM + 2 * SSM_HEADS
SSM_CHUNK = 128
HG_HEADS = 16
HG_KEY_DIM = 128
HG_IN_DIM = 5 * D_MODEL
HG_CHUNK = 64
NA_HEADS = 16
NA_HEAD_DIM = 128
NA_WIN_R = 8
NA_WIN_C = 16
ROPE_BASE = 10000.0
FFN_HIDDEN = 5632

T_LAT = BATCH * SEQ
T_CTX = BATCH * CTX_LEN
T_ALL = T_LAT + T_CTX
LANES = 128
SUBLANES = 8
BF16_ROWS = 16
ONES_ROWS = 16
ROW_TILE = 512
FFN_COL_CHUNK = 512
MIB = 1024 * 1024
LOG2E = 1.4426950408889634


def _cparams(sem, vmem_mib):
    return pltpu.CompilerParams(dimension_semantics=sem, vmem_limit_bytes=vmem_mib * MIB)


def _silu(v):
    return v * jax.nn.sigmoid(v)


def _rms(v):
    return v * lax.rsqrt(jnp.mean(v * v, axis=-1, keepdims=True) + RMS_EPS)


def _log1p_unit(e):
    return jnp.log(1.0 + e)


def _softplus(v):
    return jnp.maximum(v, 0.0) + _log1p_unit(jnp.exp(-jnp.abs(v)))


def _dot(a, b):
    return jnp.dot(a, b, preferred_element_type=F32)


def _dot_nt(a, b):
    return lax.dot_general(a, b, (((1,), (1,)), ((), ())), preferred_element_type=F32)


def _dot_tn(a, b):
    return lax.dot_general(a, b, (((0,), (0,)), ((), ())), preferred_element_type=F32)


def _bf16_pieces(v, terms):
    pieces = []
    rem = v
    for _ in range(terms):
        part = rem.astype(BF16)
        rem = rem - part.astype(F32)
        pieces.append(part)
    return pieces


def _dot_sel(m01_rep, v, terms):
    return _dot(m01_rep, jnp.concatenate(_bf16_pieces(v, terms), axis=0))


def _dot_sel_rhs(v, m01_rep, terms):
    return _dot(jnp.concatenate(_bf16_pieces(v, terms), axis=1), m01_rep)


def _mod_row(row0):
    return jnp.where(row0 >= T_LAT, 2, row0 // SEQ)


def _mod_vec(mod_ref, r, idx):
    return mod_ref[pl.ds(r, 1), idx * D_MODEL:(idx + 1) * D_MODEL]


def _norm_mod(x, g, mod_ref, r, shift_idx, scale_idx):
    return _rms(x) * g * (1.0 + _mod_vec(mod_ref, r, scale_idx)) + _mod_vec(mod_ref, r, shift_idx)


def _seg_edges(row0, rows):
    gr = row0 + lax.broadcasted_iota(jnp.int32, (rows, 1), 0)
    pos = jnp.where(gr >= T_LAT, (gr - T_LAT) % CTX_LEN, gr % SEQ)
    length = jnp.where(gr >= T_LAT, CTX_LEN, SEQ)
    return pos == 0, pos == length - 1


def _shift_rows(cur, prev_row, next_row, row0):
    rows = cur.shape[0]
    first, last = _seg_edges(row0, rows)
    ridx = lax.broadcasted_iota(jnp.int32, (rows, 1), 0)
    up = jnp.where(ridx == 0, prev_row, pltpu.roll(cur, 1, axis=0))
    up = jnp.where(first, 0.0, up)
    dn = jnp.where(ridx == rows - 1, next_row, pltpu.roll(cur, rows - 1, axis=0))
    dn = jnp.where(last, 0.0, dn)
    return up, dn


def _token_conv(cur, prev_row, next_row, row0, w_ref, b_ref):
    up, dn = _shift_rows(cur, prev_row, next_row, row0)
    return b_ref[...] + up * w_ref[0:1, :] + cur * w_ref[1:2, :] + dn * w_ref[2:3, :]


def _halo_specs(tm, halo, width, n_rows, col_fn):
    per = tm // halo
    last_blk = n_rows // halo - 1
    prev = pl.BlockSpec((halo, width), lambda i, j: (jnp.maximum(i * per - 1, 0), col_fn(j)))
    nxt = pl.BlockSpec((halo, width), lambda i, j: (jnp.minimum((i + 1) * per, last_blk), col_fn(j)))
    return prev, nxt


def _adaln_body(c_ref, w_ref, b_ref, o_ref):
    s = _silu(c_ref[...]).astype(BF16)
    o_ref[...] = _dot(s, w_ref[...].astype(BF16)) + b_ref[...]


def _adaln(cvec8, w_mod, b_mod):
    tn = 1024
    n_out = 6 * D_MODEL
    return pl.pallas_call(
        _adaln_body,
        grid=(DEPTH, n_out // tn),
        in_specs=[
            pl.BlockSpec((SUBLANES, D_MODEL), lambda l, j: (0, 0)),
            pl.BlockSpec((None, D_MODEL, tn), lambda l, j: (l, 0, j)),
            pl.BlockSpec((None, 1, tn), lambda l, j: (l, 0, j)),
        ],
        out_specs=pl.BlockSpec((None, SUBLANES, tn), lambda l, j: (l, 0, j)),
        out_shape=jax.ShapeDtypeStruct((DEPTH, SUBLANES, n_out), F32),
        compiler_params=_cparams(("arbitrary", "arbitrary"), 40),
        name="adaln",
    )(cvec8, w_mod, b_mod.reshape(DEPTH, 1, n_out))


def _prep_body(lat_ref, ctx_ref, g_ref, mod_ref, xs_ref, h_ref):
    row0 = pl.program_id(0) * ROW_TILE

    def emit(src_ref):
        xv = src_ref[...]
        xs_ref[...] = xv
        h_ref[...] = _norm_mod(xv, g_ref[...], mod_ref, _mod_row(row0), 0, 1).astype(BF16)

    @pl.when(row0 < T_LAT)
    def _():
        emit(lat_ref)

    @pl.when(row0 >= T_LAT)
    def _():
        emit(ctx_ref)


def _prep(x_lat, x_ctx, g_row, mod_l):
    assert T_CTX == ROW_TILE
    n_lat = T_LAT // ROW_TILE
    row = pl.BlockSpec((ROW_TILE, D_MODEL), lambda i: (i, 0))
    return pl.pallas_call(
        _prep_body,
        grid=(T_ALL // ROW_TILE,),
        in_specs=[pl.BlockSpec((ROW_TILE, D_MODEL), lambda i: (jnp.minimum(i, n_lat - 1), 0)),
                  pl.BlockSpec((ROW_TILE, D_MODEL), lambda i: (0, 0)),
                  pl.BlockSpec((1, D_MODEL), lambda i: (0, 0)), pl.BlockSpec(mod_l.shape, lambda i: (0, 0))],
        out_specs=[row, row],
        out_shape=[jax.ShapeDtypeStruct((T_ALL, D_MODEL), F32), jax.ShapeDtypeStruct((T_ALL, D_MODEL), BF16)],
        compiler_params=_cparams(("arbitrary",), 40),
        name="prep",
    )(x_lat, x_ctx, g_row.reshape(1, D_MODEL), mod_l)


def _in_proj_body(epilogue, n_aux, h_ref, w_ref, *rest):
    aux, outs, wb_ref = rest[:n_aux], rest[n_aux:-1], rest[-1]

    @pl.when(pl.program_id(1) == 0)
    def _():
        wb_ref[...] = w_ref[...].astype(BF16)

    u = _dot(h_ref[...], wb_ref[...])
    vals = (u,) if epilogue is None else epilogue(u, *aux)
    for o_ref, val in zip(outs, vals):
        o_ref[...] = val.astype(o_ref.dtype)


def _in_proj(h, w_stack, slot, tn, name, n_dim=None, first_tile=0, epilogue=None, aux=(), out_dtypes=(F32,)):
    tm = T_ALL // 8
    n_dim = w_stack.shape[2] if n_dim is None else n_dim
    res = pl.pallas_call(
        functools.partial(_in_proj_body, epilogue, len(aux)),
        grid=(n_dim // tn, T_ALL // tm),
        in_specs=[pl.BlockSpec((tm, D_MODEL), lambda j, i: (i, 0)),
                  pl.BlockSpec((None, D_MODEL, tn), lambda j, i: (slot, 0, first_tile + j))]
        + [spec for _, spec in aux],
        out_specs=[pl.BlockSpec((tm, tn), lambda j, i: (i, j)) for _ in out_dtypes],
        out_shape=[jax.ShapeDtypeStruct((T_ALL, n_dim), dt) for dt in out_dtypes],
        scratch_shapes=[pltpu.VMEM((D_MODEL, tn), BF16)],
        compiler_params=_cparams(("arbitrary", "arbitrary"), 56),
        name=name,
    )(h, w_stack, *[a for a, _ in aux])
    return res[0] if len(out_dtypes) == 1 else tuple(res)


def _ssd_gate_prologue(yf_ref, yb_ref, z_ref, g_ref):
    v = (yf_ref[...].astype(F32) + yb_ref[...].astype(F32)) * _silu(z_ref[...])
    return v * g_ref[...], lax.rsqrt(jnp.mean(v * v, axis=-1, keepdims=True) + RMS_EPS)


def _hg_readout_prologue(of_ref, ob_ref, gate_ref, g_ref):
    o = of_ref[...].astype(F32) + ob_ref[...].astype(F32)
    parts = [_rms(o[:, h * LANES:(h + 1) * LANES]) for h in range(o.shape[1] // LANES)]
    return jnp.concatenate(parts, axis=-1) * g_ref[...] * _silu(gate_ref[...]), None


def _plain_prologue(o_ref):
    return o_ref[...], None


def _out_proj_body(prologue, n_in, tm, *refs):
    ins = refs[:n_in]
    w_ref, x_ref, g1_ref, g2_ref, mod_ref, xo_ref, ho_ref = refs[n_in:]
    part, row_scale = prologue(*ins)
    y = _dot(part.astype(BF16), w_ref[...])
    if row_scale is not None:
        y = y * row_scale
    r = _mod_row(pl.program_id(0) * tm)
    xn = x_ref[...] + _mod_vec(mod_ref, r, 2) * (_rms(y) * g1_ref[...])
    xo_ref[...] = xn
    ho_ref[...] = _norm_mod(xn, g2_ref[...], mod_ref, r, 3, 4).astype(BF16)


def _out_proj(prologue, row_ins, gain, w_stack, slot, xs, g_res, g_ffn, mod_l, *, n_rows, tm, name):
    k_dim = w_stack.shape[1]
    in_specs, args = [], []
    for arr, cb in row_ins:
        if isinstance(arr, tuple):
            in_specs.append(pl.BlockSpec((None, tm, k_dim), functools.partial(
                lambda i, ld, cb: (ld, i, cb), ld=arr[1], cb=cb)))
            args.append(arr[0])
        else:
            in_specs.append(pl.BlockSpec((tm, k_dim), functools.partial(lambda i, cb: (i, cb), cb=cb)))
            args.append(arr)
    if gain is not None:
        in_specs.append(pl.BlockSpec((1, k_dim), lambda i: (0, 0)))
        args.append(gain)
    n_in = len(args)
    row = pl.BlockSpec((tm, D_MODEL), lambda i: (i, 0))
    vec = pl.BlockSpec((1, D_MODEL), lambda i: (0, 0))
    in_specs += [pl.BlockSpec((None, k_dim, D_MODEL), lambda i: (slot, 0, 0), pipeline_mode=pl.Buffered(1)),
                 row, vec, vec, pl.BlockSpec(mod_l.shape, lambda i: (0, 0))]
    return pl.pallas_call(
        functools.partial(_out_proj_body, prologue, n_in, tm),
        grid=(n_rows // tm,),
        in_specs=in_specs,
        out_specs=[row, row],
        out_shape=[jax.ShapeDtypeStruct((n_rows, D_MODEL), F32), jax.ShapeDtypeStruct((n_rows, D_MODEL), BF16)],
        compiler_params=_cparams(("arbitrary",), 60),
        name=name,
    )(*args, w_stack, xs, g_res.reshape(1, D_MODEL), g_ffn.reshape(1, D_MODEL), mod_l)


def _ffn_body(emit_h, has_ctx, h_ref, hp_ref, hn_ref, wa_ref, wv_ref, cwa_ref, cwv_ref, cba_ref, cbv_ref, wd_ref,
              x_ref, g_res_ref, mod_ref, g_next_ref, modn_ref, *out_and_scratch):
    if emit_h:
        xo_ref, ho_ref, hall_ref, acc_ref = out_and_scratch
    else:
        xo_ref, hall_ref, acc_ref = out_and_scratch
    k = pl.program_id(1)
    tm = ROW_TILE
    row0 = pl.program_id(0) * tm
    pad = BF16_ROWS
    rows = tm + 2 * pad
    s8 = SUBLANES

    @pl.when(k == 0)
    def _():
        hall_ref[pl.ds(0, pad), :] = hp_ref[...]
        hall_ref[pl.ds(pad, tm), :] = h_ref[...]
        hall_ref[pl.ds(pad + tm, pad), :] = hn_ref[...]
        acc_ref[...] = jnp.zeros_like(acc_ref)

    hall = hall_ref[...]
    edge_first = row0 % SEQ == 0
    edge_last = ((row0 + tm) % SEQ == 0) | (row0 + tm == T_ALL)
    is_ctx = row0 >= T_LAT
    joints = list(range(CTX_LEN, tm, CTX_LEN)) if has_ctx else []
    sub = lax.broadcasted_iota(jnp.int32, (s8, 1), 0)

    def patch(v, start, cond):
        return jnp.concatenate([v[:start], jnp.where(cond, 0.0, v[start:start + s8]), v[start + s8:]], axis=0)

    def conv(w_ref, cw_ref, cb_ref, cs):
        u = _dot(hall, w_ref[:, cs])
        u = patch(u, pad - s8, edge_first)
        u = patch(u, pad + tm, edge_last)
        up = pltpu.roll(u, 1, axis=0)[pad:pad + tm]
        dn = pltpu.roll(u, rows - 1, axis=0)[pad:pad + tm]
        for j in joints:
            up = patch(up, j, is_ctx & (sub == 0))
            dn = patch(dn, j - s8, is_ctx & (sub == s8 - 1))
        return cb_ref[:, cs] + up * cw_ref[0:1, cs] + u[pad:pad + tm] * cw_ref[1:2, cs] + dn * cw_ref[2:3, cs]

    chunks = [pl.ds(c * FFN_COL_CHUNK, FFN_COL_CHUNK) for c in range(wa_ref.shape[1] // FFN_COL_CHUNK)]
    acts = [(_silu(conv(wa_ref, cwa_ref, cba_ref, cs)) * conv(wv_ref, cwv_ref, cbv_ref, cs)).astype(BF16)
            for cs in chunks]
    acc_ref[...] += _dot(jnp.concatenate(acts, axis=1), wd_ref[...])

    @pl.when(k == pl.num_programs(1) - 1)
    def _():
        r = _mod_row(row0)
        xn = x_ref[...] + _mod_vec(mod_ref, r, 5) * (_rms(acc_ref[...]) * g_res_ref[...])
        xo_ref[...] = xn
        if emit_h:
            ho_ref[...] = _norm_mod(xn, g_next_ref[...], modn_ref, r, 0, 1).astype(BF16)


def _ffn(h, xs, layer, w_up, conv_w, conv_b, w_down, g_res, mod_l, g_next, mod_next, *, n_rows, emit_h):
    tm, tk = ROW_TILE, 512
    kt = FFN_HIDDEN // tk
    h_prev, h_next = _halo_specs(tm, BF16_ROWS, D_MODEL, n_rows, lambda k: 0)
    row = pl.BlockSpec((tm, D_MODEL), lambda i, k: (i, 0))
    vec = pl.BlockSpec((1, D_MODEL), lambda i, k: (0, 0))
    tab = pl.BlockSpec(mod_l.shape, lambda i, k: (0, 0))
    out_specs = [row, row] if emit_h else [row]
    out_shape = [jax.ShapeDtypeStruct((n_rows, D_MODEL), F32)]
    if emit_h:
        out_shape.append(jax.ShapeDtypeStruct((n_rows, D_MODEL), BF16))
    conv_b3 = conv_b.reshape(DEPTH, 1, 2 * FFN_HIDDEN)

    res = pl.pallas_call(
        functools.partial(_ffn_body, emit_h, n_rows > T_LAT),
        grid=(n_rows // tm, kt),
        in_specs=[
            row, h_prev, h_next,
            pl.BlockSpec((None, D_MODEL, tk), lambda i, k: (layer, 0, k)),
            pl.BlockSpec((None, D_MODEL, tk), lambda i, k: (layer, 0, k + kt)),
            pl.BlockSpec((None, 3, tk), lambda i, k: (layer, 0, k)),
            pl.BlockSpec((None, 3, tk), lambda i, k: (layer, 0, k + kt)),
            pl.BlockSpec((None, 1, tk), lambda i, k: (layer, 0, k)),
            pl.BlockSpec((None, 1, tk), lambda i, k: (layer, 0, k + kt)),
            pl.BlockSpec((None, tk, D_MODEL), lambda i, k: (layer, k, 0)),
            row, vec, tab, vec, tab,
        ],
        out_specs=out_specs,
        out_shape=out_shape,
        scratch_shapes=[pltpu.VMEM((tm + 2 * BF16_ROWS, D_MODEL), BF16), pltpu.VMEM((tm, D_MODEL), F32)],
        compiler_params=_cparams(("arbitrary", "arbitrary"), 56),
        name="conv_ffn",
    )(h, h, h, w_up, w_up, conv_w, conv_w, conv_b3, conv_b3, w_down, xs, g_res.reshape(1, D_MODEL), mod_l,
      g_next.reshape(1, D_MODEL), mod_next)
    return (res[0], res[1]) if emit_h else (res[0], None)


def _in_proj_conv_body(tm, h_ref, hp_ref, hn_ref, w_ref, cw_ref, cb_ref, o_ref, wb_ref):
    @pl.when(pl.program_id(1) == 0)
    def _():
        wb_ref[...] = w_ref[...].astype(BF16)

    w = wb_ref[...]
    u = _dot(h_ref[...], w)
    uh = _dot(jnp.concatenate([hp_ref[...], hn_ref[...]], axis=0), w)
    conv = _token_conv(u, uh[BF16_ROWS - 1:BF16_ROWS], uh[BF16_ROWS:BF16_ROWS + 1], pl.program_id(1) * tm,
                       cw_ref, cb_ref)
    o_ref[...] = _silu(conv).astype(o_ref.dtype)


def _in_proj_conv(h, w_stack, slot, conv_w, conv_b, tn, first_tile, name):
    tm = T_ALL // 8
    n_dim = conv_w.shape[1]
    per = tm // BF16_ROWS
    last_blk = T_ALL // BF16_ROWS - 1
    return pl.pallas_call(
        functools.partial(_in_proj_conv_body, tm),
        grid=(n_dim // tn, T_ALL // tm),
        in_specs=[pl.BlockSpec((tm, D_MODEL), lambda j, i: (i, 0)),
                  pl.BlockSpec((BF16_ROWS, D_MODEL), lambda j, i: (jnp.maximum(i * per - 1, 0), 0)),
                  pl.BlockSpec((BF16_ROWS, D_MODEL), lambda j, i: (jnp.minimum((i + 1) * per, last_blk), 0)),
                  pl.BlockSpec((None, D_MODEL, tn), lambda j, i: (slot, 0, first_tile + j)),
                  pl.BlockSpec((3, tn), lambda j, i: (0, j)),
                  pl.BlockSpec((1, tn), lambda j, i: (0, j))],
        out_specs=pl.BlockSpec((tm, tn), lambda j, i: (i, j)),
        out_shape=jax.ShapeDtypeStruct((T_ALL, n_dim), BF16),
        scratch_shapes=[pltpu.VMEM((D_MODEL, tn), BF16)],
        compiler_params=_cparams(("arbitrary", "arbitrary"), 56),
        name=name,
    )(h, h, h, w_stack, conv_w, conv_b.reshape(1, -1))


def _scan_chunk_index(d, b, s, n_lat, n_ctx):
    ctx0 = BATCH * n_lat + b * n_ctx
    fwd = jnp.where(s < n_ctx, ctx0 + s, b * n_lat + s - n_ctx)
    bwd = jnp.where(s < n_ctx, ctx0 + n_ctx - 1 - s, b * n_lat + n_lat - 1 - (s - n_ctx))
    return jnp.where(d == 0, fwd, bwd)


SSD_CUM_TERMS = 3
SSD_DT_TERMS = 2


def _ssd_scan_consts():
    q = SSM_CHUNK
    i = np.arange(q)
    pfx = np.zeros((2, q + ONES_ROWS, q), np.float32)
    pfx[0, :q] = i[:, None] >= i[None, :]
    pfx[1, :q] = i[:, None] <= i[None, :]
    pfx[:, q:] = 1.0
    pfx = np.tile(pfx, (1, 1, SSD_CUM_TERMS))
    expand = np.repeat(np.eye(SSM_HEADS, dtype=np.float32), SSM_HEAD_DIM, axis=1)
    expand = np.tile(expand, (SSD_DT_TERMS, 1))
    return jnp.asarray(pfx, BF16), jnp.asarray(expand, BF16)


def _ssd_scan_body(x_ref, b_ref, c_ref, dtraw_ref, dtb_ref, alog_ref, dskip_ref, pfx_ref, exp_ref, y_ref, s_ref):
    q = SSM_CHUNK
    hp = 2 * SSM_HEAD_DIM
    d = pl.program_id(0)

    @pl.when(pl.program_id(2) == 0)
    def _():
        s_ref[...] = jnp.zeros_like(s_ref)

    def pick(v):
        return jnp.where(d == 0, v[:, :SSM_HEADS], v[:, SSM_HEADS:])

    dt2 = _softplus(dtraw_ref[...] + dtb_ref[...])
    cum2 = _dot_sel(pfx_ref[...], dt2 * (-LOG2E * jnp.exp(alog_ref[...])), SSD_CUM_TERMS)
    acum2 = cum2[:q]
    acum = pick(acum2)
    tot = pick(cum2[q:q + SUBLANES])[0:1]
    acum_t2 = acum2.T
    acum_t = jnp.where(d == 0, acum_t2[:SSM_HEADS], acum_t2[SSM_HEADS:])
    dt_wide = _dot_sel_rhs(pick(dt2), exp_ref[...], SSD_DT_TERMS)

    ii = lax.broadcasted_iota(jnp.int32, (q, q), 0)
    jj = lax.broadcasted_iota(jnp.int32, (q, q), 1)
    causal = (jj - ii) * (1 - 2 * d) <= 0
    low_half = lax.broadcasted_iota(jnp.int32, (q, hp), 1) < SSM_HEAD_DIM

    heads_per_group = SSM_HEADS // SSM_GROUPS
    n_pairs = SSM_HEADS // 2

    b_gs = [b_ref[:, g * SSM_STATE:(g + 1) * SSM_STATE].astype(BF16) for g in range(SSM_GROUPS)]
    c_bf = [c_ref[:, g * SSM_STATE:(g + 1) * SSM_STATE].astype(BF16) for g in range(SSM_GROUPS)]
    c_gs = [c.astype(F32) for c in c_bf]
    cbs = [jnp.where(causal, _dot_nt(c_bf[g], b_gs[g]), 0.0) for g in range(SSM_GROUPS)]

    cols, lhs = [], []
    for h in range(SSM_HEADS):
        g = h // heads_per_group
        col = jnp.broadcast_to(acum[:, h:h + 1], (q, q))
        seg = jnp.where(causal, col - acum_t[h:h + 1, :], 0.0)
        scores = jnp.exp2(seg) * cbs[g]
        c_dec = c_gs[g] * jnp.exp2(col)
        lhs.append(jnp.concatenate([scores.astype(BF16), c_dec.astype(BF16)], axis=1))
        cols.append(col)

    xdts = []
    for p in range(n_pairs):
        lo = p * hp
        x_pair = x_ref[:, lo:lo + hp].astype(F32)
        xdt = x_pair * dt_wide[:, lo:lo + hp]
        rhs = jnp.concatenate([xdt.astype(BF16), s_ref[:, lo:lo + hp].astype(BF16)], axis=0)
        y_pair = jnp.where(low_half, _dot(lhs[2 * p], rhs), _dot(lhs[2 * p + 1], rhs))
        y_ref[:, lo:lo + hp] = (y_pair + dskip_ref[:, lo:lo + hp] * x_pair).astype(y_ref.dtype)
        xdts.append(xdt)

    for p in range(n_pairs):
        lo = p * hp
        g = (2 * p) // heads_per_group
        tot_pair = jnp.where(low_half[0:1], jnp.broadcast_to(tot[:, 2 * p:2 * p + 1], (1, hp)),
                             jnp.broadcast_to(tot[:, 2 * p + 1:2 * p + 2], (1, hp)))
        to_end = jnp.exp2(tot_pair - jnp.where(low_half, cols[2 * p], cols[2 * p + 1]))
        s_ref[:, lo:lo + hp] = (s_ref[:, lo:lo + hp] * jnp.exp2(tot_pair)
                                + _dot_tn(b_gs[g], (xdts[p] * to_end).astype(BF16)))


def _ssd_scan(xbc, dt_raw, dt_bias, a_log, d_skip):
    q = SSM_CHUNK
    n_lat, n_ctx = SEQ // q, CTX_LEN // q
    pfx, expand = _ssd_scan_consts()
    n_heads2 = 2 * SSM_HEADS
    b_col = SSM_INNER // (SSM_GROUPS * SSM_STATE)
    d_wide = jnp.repeat(d_skip, SSM_HEAD_DIM, axis=1).reshape(2, 1, SSM_INNER)

    def chunk(d, b, s):
        return _scan_chunk_index(d, b, s, n_lat, n_ctx)

    return pl.pallas_call(
        _ssd_scan_body,
        grid=(2, BATCH, n_lat + n_ctx),
        in_specs=[
            pl.BlockSpec((q, SSM_INNER), lambda d, b, s: (chunk(d, b, s), 0)),
            pl.BlockSpec((q, SSM_GROUPS * SSM_STATE), lambda d, b, s: (chunk(d, b, s), b_col)),
            pl.BlockSpec((q, SSM_GROUPS * SSM_STATE), lambda d, b, s: (chunk(d, b, s), b_col + 1)),
            pl.BlockSpec((q, n_heads2), lambda d, b, s: (chunk(d, b, s), 0)),
            pl.BlockSpec((1, n_heads2), lambda d, b, s: (0, 0)),
            pl.BlockSpec((1, n_heads2), lambda d, b, s: (0, 0)),
            pl.BlockSpec((None, 1, SSM_INNER), lambda d, b, s: (d, 0, 0)),
            pl.BlockSpec((None,) + pfx.shape[1:], lambda d, b, s: (d, 0, 0)),
            pl.BlockSpec(expand.shape, lambda d, b, s: (0, 0)),
        ],
        out_specs=pl.BlockSpec((None, q, SSM_INNER), lambda d, b, s: (d, chunk(d, b, s), 0)),
        out_shape=jax.ShapeDtypeStruct((2, T_ALL, SSM_INNER), BF16),
        scratch_shapes=[pltpu.VMEM((SSM_STATE, SSM_INNER), F32)],
        compiler_params=_cparams(("arbitrary", "arbitrary", "arbitrary"), 48),
        name="ssd_scan",
    )(xbc, xbc, xbc, dt_raw, dt_bias.reshape(1, n_heads2), a_log.reshape(1, n_heads2), d_wide, pfx, expand)


GLA_SUM_TERMS = 2


def _gla_consts():
    q = HG_CHUNK
    levels = int(np.log2(q))
    i = np.arange(q)
    mats = [(i[:, None] >= i[None, :]).astype(np.float32), (i[None, :] > i[:, None]).astype(np.float32)]
    masks = [np.eye(q, dtype=np.float32)]
    for lv in range(levels):
        s = 1 << lv
        blk, pos = i // (2 * s), i % (2 * s)
        bound = blk * 2 * s + s - 1
        right = pos >= s
        t = i[None, :]
        m_right = right[:, None] & (t > bound[:, None]) & (t <= i[:, None])
        m_left = (~right)[:, None] & (t > i[:, None]) & (t <= bound[:, None])
        mats.append((m_right | m_left).astype(np.float32))
        masks.append((right[:, None] & (~right)[None, :] & (blk[:, None] == blk[None, :])).astype(np.float32))
    mats.append(np.ones((ONES_ROWS, q), np.float32))
    fwd_m, fwd_k = np.concatenate(mats, axis=0), np.stack(masks)
    flip_rows = np.concatenate([np.arange(k * q, (k + 1) * q)[::-1] for k in range(levels + 2)]
                               + [np.arange((levels + 2) * q, (levels + 2) * q + ONES_ROWS)])
    bwd_m = fwd_m[flip_rows][:, ::-1]
    bwd_k = fwd_k[:, ::-1, ::-1]
    mats2 = np.tile(np.stack([fwd_m, bwd_m]), (1, 1, GLA_SUM_TERMS))
    return (jnp.asarray(mats2, BF16), jnp.asarray(np.stack([fwd_k, bwd_k]), F32), levels)


def _silu_epilogue(u):
    return (_silu(u),)


def _hg_gate_epilogue(layer, u, lbp_ref):
    lbp = lbp_ref[...]
    e = jnp.exp(lbp - jnp.max(lbp, axis=0, keepdims=True))
    sm = e / jnp.sum(e, axis=0, keepdims=True)
    lb = jnp.sum(sm[1:layer + 1], axis=0, keepdims=True) if layer > 0 else jnp.zeros((1, lbp.shape[1]), F32)
    ef = jnp.exp(-jnp.abs(u))
    log_sig = jnp.minimum(u, 0.0) - _log1p_unit(ef)
    la = jnp.log(lb)
    lc = jnp.log1p(-lb) + log_sig
    logf = jnp.maximum(la, lc) + _log1p_unit(jnp.exp(-jnp.abs(la - lc)))
    key = (1.0 - lb) * (jnp.where(u >= 0.0, ef, 1.0) / (1.0 + ef))
    return logf, key


def _gla_body(levels, q_ref, v_ref, logf_ref, key_ref, mats_ref, masks_ref, o_ref, st_ref):
    q = HG_CHUNK

    @pl.when(pl.program_id(2) == 0)
    def _():
        st_ref[...] = jnp.zeros_like(st_ref)

    qs_bf, key_bf, v_bf = q_ref[...], key_ref[...], v_ref[...]
    qs, key = qs_bf.astype(F32), key_bf.astype(F32)

    sums = _dot_sel(mats_ref[...], logf_ref[...], GLA_SUM_TERMS)
    q_in = (qs * jnp.exp(sums[:q])).astype(BF16)
    k_out = (key * jnp.exp(sums[q:2 * q])).astype(BF16)
    dec = jnp.exp(sums[(levels + 2) * q:(levels + 2) * q + 1])
    q_lv, k_lv = [], []
    for lv in range(levels):
        fac = jnp.exp(sums[(lv + 2) * q:(lv + 3) * q])
        q_lv.append((qs * fac).astype(BF16))
        k_lv.append((key * fac).astype(BF16))

    heads = [slice(h * LANES, (h + 1) * LANES) for h in range(HG_HEADS)]
    attn = masks_ref[0][None] * jnp.stack([_dot_nt(qs_bf[:, sl], key_bf[:, sl]) for sl in heads], axis=0)
    for lv in range(levels):
        attn = attn + masks_ref[lv + 1][None] * jnp.stack(
            [_dot_nt(q_lv[lv][:, sl], k_lv[lv][:, sl]) for sl in heads], axis=0)
    attn = attn.astype(BF16)
    st_bf = st_ref[...].astype(BF16)
    o_ref[...] = jnp.concatenate(
        [_dot(attn[h], v_bf[:, sl]) + _dot_nt(q_in[:, sl], st_bf[h]) for h, sl in enumerate(heads)],
        axis=1).astype(o_ref.dtype)
    for h, sl in enumerate(heads):
        st_ref[h] = st_ref[h] * dec[:, sl] + _dot_tn(v_bf[:, sl], k_out[:, sl])


def _gla_scan(qs, v, logf, key):
    q = HG_CHUNK
    n_lat, n_ctx = SEQ // q, CTX_LEN // q
    mats, masks, levels = _gla_consts()

    def chunk(d, b, s):
        return _scan_chunk_index(d, b, s, n_lat, n_ctx)

    return pl.pallas_call(
        functools.partial(_gla_body, levels),
        grid=(2, BATCH, n_lat + n_ctx),
        in_specs=[
            pl.BlockSpec((q, D_MODEL), lambda d, b, s: (chunk(d, b, s), 0)),
            pl.BlockSpec((q, D_MODEL), lambda d, b, s: (chunk(d, b, s), 0)),
            pl.BlockSpec((q, D_MODEL), lambda d, b, s: (chunk(d, b, s), d)),
            pl.BlockSpec((q, D_MODEL), lambda d, b, s: (chunk(d, b, s), d)),
            pl.BlockSpec((None,) + mats.shape[1:], lambda d, b, s: (d, 0, 0)),
            pl.BlockSpec((None,) + masks.shape[1:], lambda d, b, s: (d, 0, 0, 0)),
        ],
        out_specs=pl.BlockSpec((None, q, D_MODEL), lambda d, b, s: (d, chunk(d, b, s), 0)),
        out_shape=jax.ShapeDtypeStruct((2, T_ALL, D_MODEL), BF16),
        scratch_shapes=[pltpu.VMEM((HG_HEADS, LANES, HG_KEY_DIM), F32)],
        compiler_params=_cparams(("arbitrary", "arbitrary", "arbitrary"), 48),
        name="gla_scan",
    )(qs, v, logf, key, mats, masks)


def _na_bias_body(rpb_ref, o_ref):
    h = pl.program_id(0)
    n_dc = 2 * NA_WIN_C - 1
    qc = lax.broadcasted_iota(jnp.int32, (GRID_W, LANES), 0)
    kc = lax.broadcasted_iota(jnp.int32, (GRID_W, LANES), 1) % GRID_W
    c0 = jnp.clip(qc - NA_WIN_C // 2, 0, GRID_W - NA_WIN_C)
    col_in = (kc >= c0) & (kc < c0 + NA_WIN_C)
    dc = jnp.clip(kc - qc + NA_WIN_C - 1, 0, n_dc - 1)
    low = lax.broadcasted_iota(jnp.int32, (GRID_W, LANES), 1) < GRID_W
    tiles = []
    for dr in range(2 * NA_WIN_R - 1):
        t = jnp.zeros((GRID_W, LANES), F32)
        for k in range(n_dc):
            t = jnp.where(dc == k, rpb_ref[(h * (2 * NA_WIN_R - 1) + dr) * n_dc + k], t)
        tiles.append(jnp.where(col_in, t, -jnp.inf))
    for dr in range(2 * NA_WIN_R - 2):
        o_ref[dr] = jnp.where(low, tiles[dr], tiles[dr + 1])


def _na_bias_table(rpb):
    n_dr = 2 * NA_WIN_R - 2
    return pl.pallas_call(
        _na_bias_body,
        grid=(NA_HEADS,),
        in_specs=[pl.BlockSpec(memory_space=pltpu.SMEM)],
        out_specs=pl.BlockSpec((None, n_dr, GRID_W, LANES), lambda h: (h, 0, 0, 0)),
        out_shape=jax.ShapeDtypeStruct((NA_HEADS, n_dr, GRID_W, LANES), F32),
        compiler_params=_cparams(("arbitrary",), 32),
        name="na_bias",
    )(rpb.reshape(-1))


def _rope_tables():
    quarter = NA_HEAD_DIM // 4
    inv = ROPE_BASE ** (-np.arange(quarter, dtype=np.float64) / quarter)
    t = np.arange(SEQ)
    row, col = t // GRID_W, t % GRID_W
    ang = np.concatenate([row[:, None] * inv[None], row[:, None] * inv[None],
                          col[:, None] * inv[None], col[:, None] * inv[None]], axis=1)
    sign = np.tile(np.concatenate([-np.ones(quarter), np.ones(quarter)]), 2)[None]
    return jnp.asarray(np.cos(ang), F32), jnp.asarray(np.sin(ang) * sign, F32)


def _rope(v, cos, sin_signed):
    quarter = NA_HEAD_DIM // 4
    lane = lax.broadcasted_iota(jnp.int32, v.shape, 1)
    first = (lane % (2 * quarter)) < quarter
    partner = jnp.where(first, pltpu.roll(v, NA_HEAD_DIM - quarter, axis=1), pltpu.roll(v, quarter, axis=1))
    return v * cos + partner * sin_signed


def _na_body(rows_per_step, q_ref, k_ref, v_ref, kc_ref, vc_ref, cosq_ref, sinq_ref, cos_ref, sin_ref, bias_ref,
             o_ref, kr_ref, vb_ref):
    rb = pl.program_id(2)
    n_rows = SEQ // GRID_W
    scale = NA_HEAD_DIM ** -0.5

    @pl.when(rb == 0)
    def _():
        kr_ref[...] = _rope(k_ref[...], cos_ref[...], sin_ref[...]).astype(BF16)
        vb_ref[...] = v_ref[...].astype(BF16)

    qr = (_rope(q_ref[...], cosq_ref[...], sinq_ref[...]) * scale).astype(BF16)
    kc = kc_ref[...].astype(BF16)
    vc = vc_ref[...].astype(BF16)
    starts, s_rows = [], []
    for lr in range(rows_per_step):
        r = rb * rows_per_step + lr
        r0 = jnp.clip(r - NA_WIN_R // 2, 0, n_rows - NA_WIN_R)
        start = pl.multiple_of(r0 * GRID_W, GRID_W)
        dr0 = r0 - r + NA_WIN_R - 1
        bias = jnp.concatenate([bias_ref[dr0 + 2 * p] for p in range(NA_WIN_R // 2)], axis=1)
        k_win = kr_ref[pl.ds(start, NA_WIN_R * GRID_W), :]
        s_rows.append(_dot_nt(qr[lr * GRID_W:(lr + 1) * GRID_W], k_win) + bias)
        starts.append(start)
    s_loc = jnp.concatenate(s_rows, axis=0)
    s_ctx = _dot_nt(qr, kc)
    m = jnp.maximum(jnp.max(s_loc, axis=-1, keepdims=True), jnp.max(s_ctx, axis=-1, keepdims=True))
    p_loc = jnp.exp(s_loc - m)
    p_ctx = jnp.exp(s_ctx - m)
    denom = jnp.sum(p_loc, axis=-1, keepdims=True) + jnp.sum(p_ctx, axis=-1, keepdims=True)
    p_loc = p_loc.astype(BF16)
    o_loc = jnp.concatenate(
        [_dot(p_loc[lr * GRID_W:(lr + 1) * GRID_W], vb_ref[pl.ds(starts[lr], NA_WIN_R * GRID_W), :])
         for lr in range(rows_per_step)], axis=0)
    o = o_loc + _dot(p_ctx.astype(BF16), vc)
    o_ref[...] = (o / denom).astype(o_ref.dtype)


def _na_attention(qkv, bias_tbl):
    rows_per_step = 32
    tq = rows_per_step * GRID_W
    steps = SEQ // tq
    cos, sin = _rope_tables()
    hd = NA_HEAD_DIM
    ctx_blk0 = T_LAT // CTX_LEN
    return pl.pallas_call(
        functools.partial(_na_body, rows_per_step),
        grid=(BATCH, NA_HEADS, steps),
        in_specs=[
            pl.BlockSpec((tq, hd), lambda b, h, r: (b * steps + r, h)),
            pl.BlockSpec((SEQ, hd), lambda b, h, r: (b, NA_HEADS + h)),
            pl.BlockSpec((SEQ, hd), lambda b, h, r: (b, 2 * NA_HEADS + h)),
            pl.BlockSpec((CTX_LEN, hd), lambda b, h, r: (ctx_blk0 + b, NA_HEADS + h)),
            pl.BlockSpec((CTX_LEN, hd), lambda b, h, r: (ctx_blk0 + b, 2 * NA_HEADS + h)),
            pl.BlockSpec((tq, hd), lambda b, h, r: (r, 0)),
            pl.BlockSpec((tq, hd), lambda b, h, r: (r, 0)),
            pl.BlockSpec((SEQ, hd), lambda b, h, r: (0, 0)),
            pl.BlockSpec((SEQ, hd), lambda b, h, r: (0, 0)),
            pl.BlockSpec((None,) + bias_tbl.shape[1:], lambda b, h, r: (h, 0, 0, 0)),
        ],
        out_specs=pl.BlockSpec((tq, hd), lambda b, h, r: (b * steps + r, h)),
        out_shape=jax.ShapeDtypeStruct((T_ALL, D_MODEL), BF16),
        scratch_shapes=[pltpu.VMEM((SEQ, hd), BF16), pltpu.VMEM((SEQ, hd), BF16)],
        compiler_params=_cparams(("arbitrary", "arbitrary", "arbitrary"), 48),
        name="na_attention",
    )(qkv, qkv, qkv, qkv, qkv, cos, sin, cos, sin, bias_tbl)


def _ctx_attn_body(q_ref, k_ref, v_ref, lat_out_ref, o_ref):
    del lat_out_ref
    scale = NA_HEAD_DIM ** -0.5
    s = _dot_nt((q_ref[...] * scale).astype(BF16), k_ref[...].astype(BF16))
    p = jnp.exp(s - jnp.max(s, axis=-1, keepdims=True))
    o = _dot(p.astype(BF16), v_ref[...].astype(BF16))
    o_ref[...] = (o / jnp.sum(p, axis=-1, keepdims=True)).astype(o_ref.dtype)


def _ctx_attention(qkv, o_all):
    hd = NA_HEAD_DIM
    ctx_blk0 = T_LAT // CTX_LEN
    return pl.pallas_call(
        _ctx_attn_body,
        grid=(BATCH, NA_HEADS),
        in_specs=[
            pl.BlockSpec((CTX_LEN, hd), lambda b, h: (ctx_blk0 + b, h)),
            pl.BlockSpec((CTX_LEN, hd), lambda b, h: (ctx_blk0 + b, NA_HEADS + h)),
            pl.BlockSpec((CTX_LEN, hd), lambda b, h: (ctx_blk0 + b, 2 * NA_HEADS + h)),
            pl.BlockSpec(memory_space=pl.ANY),
        ],
        out_specs=pl.BlockSpec((CTX_LEN, hd), lambda b, h: (ctx_blk0 + b, h)),
        out_shape=jax.ShapeDtypeStruct((T_ALL, D_MODEL), BF16),
        input_output_aliases={3: 0},
        compiler_params=_cparams(("arbitrary", "arbitrary"), 32),
        name="ctx_attention",
    )(qkv, qkv, qkv, o_all)


def kernel(x, c, ctx, c_ctx, w_mod, b_mod, norm_g, ffn_w_up, ffn_conv_w, ffn_conv_b, ffn_w_down, ssm_w_in, ssm_conv_w, ssm_conv_b, ssm_dt_bias, ssm_a_log, ssm_d, ssm_norm_g, ssm_w_out, hg_w_in, hg_lb, hg_norm_g, hg_w_out, na_w_qkv, na_rpb, na_w_out):
    cvec8 = jnp.concatenate([c, c_ctx[None], jnp.zeros((SUBLANES - BATCH - 1, D_MODEL), F32)], axis=0)
    mod = _adaln(cvec8, w_mod, b_mod)
    ffn_up_bf, ffn_down_bf = ffn_w_up.astype(BF16), ffn_w_down.astype(BF16)
    ssm_out_bf, hg_out_bf, na_out_bf = ssm_w_out.astype(BF16), hg_w_out.astype(BF16), na_w_out.astype(BF16)

    xs, h = _prep(x.reshape(T_LAT, D_MODEL), ctx.reshape(T_CTX, D_MODEL), norm_g[0, 0], mod[0])
    for i in range(DEPTH):
        last = i == DEPTH - 1
        kind, slot = i % N_MIXERS, i // N_MIXERS
        mod_l = mod[i]
        n_rows = T_LAT if last else T_ALL
        op = dict(xs=xs, g_res=norm_g[i, 1], g_ffn=norm_g[i, 2], mod_l=mod_l, n_rows=n_rows, slot=slot)
        if kind == 0:
            tn = 1024
            z = _in_proj(h, ssm_w_in, slot, tn, "ssd_in_z", n_dim=SSM_INNER)
            xbc = _in_proj_conv(h, ssm_w_in, slot, ssm_conv_w[slot], ssm_conv_b[slot], tn, SSM_INNER // tn,
                                "ssd_in_xbc")
            dt_raw = _in_proj(h, ssm_w_in, slot, 2 * SSM_HEADS, "ssd_dt", n_dim=2 * SSM_HEADS,
                              first_tile=(SSM_INNER + SSM_CONV_DIM) // (2 * SSM_HEADS))
            y2 = _ssd_scan(xbc, dt_raw, ssm_dt_bias[slot], ssm_a_log[slot], ssm_d[slot])
            xs, hf = _out_proj(_ssd_gate_prologue, [((y2, 0), 0), ((y2, 1), 0), (z, 0)],
                               ssm_norm_g[slot].reshape(1, SSM_INNER), ssm_out_bf, tm=256, name="ssd_out", **op)
        elif kind == 1:
            tn = 1024
            per = D_MODEL // tn
            qs = _in_proj(h, hg_w_in, slot, tn, "hg_in_q", n_dim=D_MODEL, epilogue=_silu_epilogue,
                          out_dtypes=(BF16,))
            v = _in_proj(h, hg_w_in, slot, tn, "hg_in_v", n_dim=D_MODEL, first_tile=per, out_dtypes=(BF16,))
            lb_spec = pl.BlockSpec((None, DEPTH, tn), lambda j, r: (j // per, 0, j % per))
            logf, key = _in_proj(h, hg_w_in, slot, tn, "hg_in_f", n_dim=2 * D_MODEL, first_tile=2 * per,
                                 epilogue=functools.partial(_hg_gate_epilogue, i), aux=[(hg_lb, lb_spec)],
                                 out_dtypes=(F32, BF16))
            gate = _in_proj(h, hg_w_in, slot, tn, "hg_in_g", n_dim=D_MODEL, first_tile=4 * per)
            o2 = _gla_scan(qs, v, logf, key)
            xs, hf = _out_proj(_hg_readout_prologue, [((o2, 0), 0), ((o2, 1), 0), (gate, 0)],
                               hg_norm_g[slot].reshape(1, D_MODEL), hg_out_bf, tm=ROW_TILE, name="hg_out", **op)
        else:
            qkv = _in_proj(h, na_w_qkv, slot, 1024, "na_in")
            o = _ctx_attention(qkv, _na_attention(qkv, _na_bias_table(na_rpb[slot])))
            xs, hf = _out_proj(_plain_prologue, [(o, 0)], None, na_out_bf, tm=ROW_TILE, name="na_out", **op)
        nxt = min(i + 1, DEPTH - 1)
        xs, h = _ffn(hf, xs, i, ffn_up_bf, ffn_conv_w, ffn_conv_b, ffn_down_bf, norm_g[i, 3], mod_l,
                     norm_g[nxt, 0], mod[nxt], n_rows=n_rows, emit_h=not last)

    return xs.reshape(BATCH, SEQ, D_MODEL)
```

```python
import functools

import numpy as np
import jax
import jax.numpy as jnp
from jax import lax
from jax.experimental import pallas as pl
from jax.experimental.pallas import tpu as pltpu

F32 = jnp.float32
BF16 = jnp.bfloat16

D_MODEL = 2048
BATCH = 2
SEQ = 4096
DEPTH = 4
GRID_W = 64
CTX_LEN = 256
N_MIXERS = 3
RMS_EPS = 1e-6
SSM_INNER = 4096
SSM_HEAD_DIM = 64
SSM_HEADS = 64
SSM_STATE = 128
SSM_GROUPS = 8
SSM_CONV_DIM = SSM_INNER + 2 * SSM_GROUPS * SSM_STATE
SSM_IN_DIM = SSM_INNER + SSM_CONV_DIM + 2 * SSM_HEADS
SSM_CHUNK = 128
HG_HEADS = 16
HG_KEY_DIM = 128
HG_IN_DIM = 5 * D_MODEL
HG_CHUNK = 64
NA_HEADS = 16
NA_HEAD_DIM = 128
NA_WIN_R = 8
NA_WIN_C = 16
ROPE_BASE = 10000.0
FFN_HIDDEN = 5632

T_LAT = BATCH * SEQ
T_CTX = BATCH * CTX_LEN
T_ALL = T_LAT + T_CTX
LANES = 128
SUBLANES = 8
BF16_ROWS = 16
ONES_ROWS = 16
ROW_TILE = 512
FFN_COL_CHUNK = 512
MIB = 1024 * 1024
LOG2E = 1.4426950408889634


def _cparams(sem, vmem_mib):
    return pltpu.CompilerParams(dimension_semantics=sem, vmem_limit_bytes=vmem_mib * MIB)


def _silu(v):
    return v * jax.nn.sigmoid(v)


def _rms(v):
    return v * lax.rsqrt(jnp.mean(v * v, axis=-1, keepdims=True) + RMS_EPS)


def _log1p_unit(e):
    return jnp.log(1.0 + e)


def _softplus(v):
    return jnp.maximum(v, 0.0) + _log1p_unit(jnp.exp(-jnp.abs(v)))


def _dot(a, b):
    return jnp.dot(a, b, preferred_element_type=F32)


def _dot_nt(a, b):
    return lax.dot_general(a, b, (((1,), (1,)), ((), ())), preferred_element_type=F32)


def _dot_tn(a, b):
    return lax.dot_general(a, b, (((0,), (0,)), ((), ())), preferred_element_type=F32)


def _bf16_pieces(v, terms):
    pieces = []
    rem = v
    for _ in range(terms):
        part = rem.astype(BF16)
        rem = rem - part.astype(F32)
        pieces.append(part)
    return pieces


def _dot_sel(m01_rep, v, terms):
    return _dot(m01_rep, jnp.concatenate(_bf16_pieces(v, terms), axis=0))


def _dot_sel_rhs(v, m01_rep, terms):
    return _dot(jnp.concatenate(_bf16_pieces(v, terms), axis=1), m01_rep)


def _mod_row(row0):
    return jnp.where(row0 >= T_LAT, 2, row0 // SEQ)


def _mod_vec(mod_ref, r, idx):
    return mod_ref[pl.ds(r, 1), idx * D_MODEL:(idx + 1) * D_MODEL]


def _norm_mod(x, g, mod_ref, r, shift_idx, scale_idx):
    return _rms(x) * g * (1.0 + _mod_vec(mod_ref, r, scale_idx)) + _mod_vec(mod_ref, r, shift_idx)


def _seg_edges(row0, rows):
    gr = row0 + lax.broadcasted_iota(jnp.int32, (rows, 1), 0)
    pos = jnp.where(gr >= T_LAT, (gr - T_LAT) % CTX_LEN, gr % SEQ)
    length = jnp.where(gr >= T_LAT, CTX_LEN, SEQ)
    return pos == 0, pos == length - 1


def _shift_rows(cur, prev_row, next_row, row0):
    rows = cur.shape[0]
    first, last = _seg_edges(row0, rows)
    sub = lax.broadcasted_iota(jnp.int32, (SUBLANES, 1), 0)
    up = pltpu.roll(cur, 1, axis=0)
    up = jnp.concatenate([jnp.where(sub == 0, prev_row, up[:SUBLANES]), up[SUBLANES:]], axis=0)
    up = jnp.where(first, 0.0, up)
    dn = pltpu.roll(cur, rows - 1, axis=0)
    dn = jnp.concatenate([dn[:rows - SUBLANES], jnp.where(sub == SUBLANES - 1, next_row, dn[rows - SUBLANES:])],
                         axis=0)
    dn = jnp.where(last, 0.0, dn)
    return up, dn


def _token_conv(cur, prev_row, next_row, row0, w_ref, b_ref):
    up, dn = _shift_rows(cur, prev_row, next_row, row0)
    return b_ref[...] + up * w_ref[0:1, :] + cur * w_ref[1:2, :] + dn * w_ref[2:3, :]


def _halo_specs(tm, halo, width, n_rows, col_fn):
    per = tm // halo
    last_blk = n_rows // halo - 1
    prev = pl.BlockSpec((halo, width), lambda i, j: (jnp.maximum(i * per - 1, 0), col_fn(j)))
    nxt = pl.BlockSpec((halo, width), lambda i, j: (jnp.minimum((i + 1) * per, last_blk), col_fn(j)))
    return prev, nxt


def _adaln_body(c_ref, w_ref, b_ref, o_ref):
    s = _silu(c_ref[...]).astype(BF16)
    o_ref[...] = _dot(s, w_ref[...].astype(BF16)) + b_ref[...]


def _adaln(cvec8, w_mod, b_mod):
    tn = 1024
    n_out = 6 * D_MODEL
    return pl.pallas_call(
        _adaln_body,
        grid=(DEPTH, n_out // tn),
        in_specs=[
            pl.BlockSpec((SUBLANES, D_MODEL), lambda l, j: (0, 0)),
            pl.BlockSpec((None, D_MODEL, tn), lambda l, j: (l, 0, j)),
            pl.BlockSpec((None, 1, tn), lambda l, j: (l, 0, j)),
        ],
        out_specs=pl.BlockSpec((None, SUBLANES, tn), lambda l, j: (l, 0, j)),
        out_shape=jax.ShapeDtypeStruct((DEPTH, SUBLANES, n_out), F32),
        compiler_params=_cparams(("arbitrary", "arbitrary"), 40),
        name="adaln",
    )(cvec8, w_mod, b_mod.reshape(DEPTH, 1, n_out))


def _prep_body(lat_ref, ctx_ref, g_ref, mod_ref, xs_ref, h_ref):
    row0 = pl.program_id(0) * ROW_TILE

    def emit(src_ref):
        xv = src_ref[...]
        xs_ref[...] = xv
        h_ref[...] = _norm_mod(xv, g_ref[...], mod_ref, _mod_row(row0), 0, 1).astype(BF16)

    @pl.when(row0 < T_LAT)
    def _():
        emit(lat_ref)

    @pl.when(row0 >= T_LAT)
    def _():
        emit(ctx_ref)


def _prep(x_lat, x_ctx, g_row, mod_l):
    assert T_CTX == ROW_TILE
    n_lat = T_LAT // ROW_TILE
    row = pl.BlockSpec((ROW_TILE, D_MODEL), lambda i: (i, 0))
    return pl.pallas_call(
        _prep_body,
        grid=(T_ALL // ROW_TILE,),
        in_specs=[pl.BlockSpec((ROW_TILE, D_MODEL), lambda i: (jnp.minimum(i, n_lat - 1), 0)),
                  pl.BlockSpec((ROW_TILE, D_MODEL), lambda i: (0, 0)),
                  pl.BlockSpec((1, D_MODEL), lambda i: (0, 0)), pl.BlockSpec(mod_l.shape, lambda i: (0, 0))],
        out_specs=[row, row],
        out_shape=[jax.ShapeDtypeStruct((T_ALL, D_MODEL), F32), jax.ShapeDtypeStruct((T_ALL, D_MODEL), BF16)],
        compiler_params=_cparams(("arbitrary",), 40),
        name="prep",
    )(x_lat, x_ctx, g_row.reshape(1, D_MODEL), mod_l)


def _in_proj_body(epilogue, n_aux, h_ref, w_ref, *rest):
    aux, outs, wb_ref = rest[:n_aux], rest[n_aux:-1], rest[-1]

    @pl.when(pl.program_id(1) == 0)
    def _():
        wb_ref[...] = w_ref[...].astype(BF16)

    u = _dot(h_ref[...], wb_ref[...])
    vals = (u,) if epilogue is None else epilogue(u, *aux)
    for o_ref, val in zip(outs, vals):
        o_ref[...] = val.astype(o_ref.dtype)


def _in_proj(h, w_stack, slot, tn, name, n_dim=None, first_tile=0, epilogue=None, aux=(), out_dtypes=(F32,)):
    tm = T_ALL // 8
    n_dim = w_stack.shape[2] if n_dim is None else n_dim
    res = pl.pallas_call(
        functools.partial(_in_proj_body, epilogue, len(aux)),
        grid=(n_dim // tn, T_ALL // tm),
        in_specs=[pl.BlockSpec((tm, D_MODEL), lambda j, i: (i, 0)),
                  pl.BlockSpec((None, D_MODEL, tn), lambda j, i: (slot, 0, first_tile + j))]
        + [spec for _, spec in aux],
        out_specs=[pl.BlockSpec((tm, tn), lambda j, i: (i, j)) for _ in out_dtypes],
        out_shape=[jax.ShapeDtypeStruct((T_ALL, n_dim), dt) for dt in out_dtypes],
        scratch_shapes=[pltpu.VMEM((D_MODEL, tn), BF16)],
        compiler_params=_cparams(("arbitrary", "arbitrary"), 56),
        name=name,
    )(h, w_stack, *[a for a, _ in aux])
    return res[0] if len(out_dtypes) == 1 else tuple(res)


def _ssd_gate_prologue(row0, yf_ref, yb_ref, z_ref, g_ref):
    v = (yf_ref[...].astype(F32) + yb_ref[...].astype(F32)) * _silu(z_ref[...])
    return v * g_ref[...], lax.rsqrt(jnp.mean(v * v, axis=-1, keepdims=True) + RMS_EPS)


def _hg_readout_prologue(row0, of_ref, ob_ref, gate_ref, g_ref):
    o = of_ref[...].astype(F32) + ob_ref[...].astype(F32)
    parts = [_rms(o[:, h * LANES:(h + 1) * LANES]) for h in range(o.shape[1] // LANES)]
    return jnp.concatenate(parts, axis=-1) * g_ref[...] * _silu(gate_ref[...]), None


def _na_prologue(row0, o_lat_ref, o_ctx_ref):
    return jnp.where(row0 >= T_LAT, o_ctx_ref[...], o_lat_ref[...]), None


def _out_proj_body(prologue, n_in, tm, *refs):
    ins = refs[:n_in]
    w_ref, x_ref, g1_ref, g2_ref, mod_ref, xo_ref, ho_ref = refs[n_in:]
    part, row_scale = prologue(pl.program_id(0) * tm, *ins)
    y = _dot(part.astype(BF16), w_ref[...])
    if row_scale is not None:
        y = y * row_scale
    r = _mod_row(pl.program_id(0) * tm)
    xn = x_ref[...] + _mod_vec(mod_ref, r, 2) * (_rms(y) * g1_ref[...])
    xo_ref[...] = xn
    ho_ref[...] = _norm_mod(xn, g2_ref[...], mod_ref, r, 3, 4).astype(BF16)


def _out_proj(prologue, row_ins, gain, w_stack, slot, xs, g_res, g_ffn, mod_l, *, n_rows, tm, name):
    k_dim = w_stack.shape[1]
    in_specs, args = [], []
    for arr, cb, *row_fn in row_ins:
        row_fn = row_fn[0] if row_fn else (lambda i: i)
        if isinstance(arr, tuple):
            in_specs.append(pl.BlockSpec((None, tm, k_dim), functools.partial(
                lambda i, ld, cb, row_fn: (ld, row_fn(i), cb), ld=arr[1], cb=cb, row_fn=row_fn)))
            args.append(arr[0])
        else:
            in_specs.append(pl.BlockSpec((tm, k_dim), functools.partial(
                lambda i, cb, row_fn: (row_fn(i), cb), cb=cb, row_fn=row_fn)))
            args.append(arr)
    if gain is not None:
        in_specs.append(pl.BlockSpec((1, k_dim), lambda i: (0, 0)))
        args.append(gain)
    n_in = len(args)
    row = pl.BlockSpec((tm, D_MODEL), lambda i: (i, 0))
    vec = pl.BlockSpec((1, D_MODEL), lambda i: (0, 0))
    in_specs += [pl.BlockSpec((None, k_dim, D_MODEL), lambda i: (slot, 0, 0), pipeline_mode=pl.Buffered(1)),
                 row, vec, vec, pl.BlockSpec(mod_l.shape, lambda i: (0, 0))]
    return pl.pallas_call(
        functools.partial(_out_proj_body, prologue, n_in, tm),
        grid=(n_rows // tm,),
        in_specs=in_specs,
        out_specs=[row, row],
        out_shape=[jax.ShapeDtypeStruct((n_rows, D_MODEL), F32), jax.ShapeDtypeStruct((n_rows, D_MODEL), BF16)],
        compiler_params=_cparams(("arbitrary",), 60),
        name=name,
    )(*args, w_stack, xs, g_res.reshape(1, D_MODEL), g_ffn.reshape(1, D_MODEL), mod_l)


def _ffn_body(emit_h, has_ctx, h_ref, hp_ref, hn_ref, wa_ref, wv_ref, cwa_ref, cwv_ref, cba_ref, cbv_ref, wd_ref,
              x_ref, g_res_ref, mod_ref, g_next_ref, modn_ref, *out_and_scratch):
    if emit_h:
        xo_ref, ho_ref, hall_ref, acc_ref = out_and_scratch
    else:
        xo_ref, hall_ref, acc_ref = out_and_scratch
    k = pl.program_id(1)
    tm = ROW_TILE
    row0 = pl.program_id(0) * tm
    pad = BF16_ROWS
    rows = tm + 2 * pad
    s8 = SUBLANES

    @pl.when(k == 0)
    def _():
        hall_ref[pl.ds(0, pad), :] = hp_ref[...]
        hall_ref[pl.ds(pad, tm), :] = h_ref[...]
        hall_ref[pl.ds(pad + tm, pad), :] = hn_ref[...]
        acc_ref[...] = jnp.zeros_like(acc_ref)

    hall = hall_ref[...]
    edge_first = row0 % SEQ == 0
    edge_last = ((row0 + tm) % SEQ == 0) | (row0 + tm == T_ALL)
    is_ctx = row0 >= T_LAT
    joints = list(range(CTX_LEN, tm, CTX_LEN)) if has_ctx else []
    sub = lax.broadcasted_iota(jnp.int32, (s8, 1), 0)

    def patch(v, start, cond):
        return jnp.concatenate([v[:start], jnp.where(cond, 0.0, v[start:start + s8]), v[start + s8:]], axis=0)

    def conv(w_ref, cw_ref, cb_ref, cs):
        u = _dot(hall, w_ref[:, cs])
        u = patch(u, pad - s8, edge_first)
        u = patch(u, pad + tm, edge_last)
        up = pltpu.roll(u, 1, axis=0)[pad:pad + tm]
        dn = pltpu.roll(u, rows - 1, axis=0)[pad:pad + tm]
        for j in joints:
            up = patch(up, j, is_ctx & (sub == 0))
            dn = patch(dn, j - s8, is_ctx & (sub == s8 - 1))
        return cb_ref[:, cs] + up * cw_ref[0:1, cs] + u[pad:pad + tm] * cw_ref[1:2, cs] + dn * cw_ref[2:3, cs]

    chunks = [pl.ds(c * FFN_COL_CHUNK, FFN_COL_CHUNK) for c in range(wa_ref.shape[1] // FFN_COL_CHUNK)]
    acts = [(_silu(conv(wa_ref, cwa_ref, cba_ref, cs)) * conv(wv_ref, cwv_ref, cbv_ref, cs)).astype(BF16)
            for cs in chunks]
    acc_ref[...] += _dot(jnp.concatenate(acts, axis=1), wd_ref[...])

    @pl.when(k == pl.num_programs(1) - 1)
    def _():
        r = _mod_row(row0)
        xn = x_ref[...] + _mod_vec(mod_ref, r, 5) * (_rms(acc_ref[...]) * g_res_ref[...])
        xo_ref[...] = xn
        if emit_h:
            ho_ref[...] = _norm_mod(xn, g_next_ref[...], modn_ref, r, 0, 1).astype(BF16)


def _ffn(h, xs, layer, w_up, conv_w, conv_b, w_down, g_res, mod_l, g_next, mod_next, *, n_rows, emit_h):
    tm, tk = ROW_TILE, 512
    kt = FFN_HIDDEN // tk
    h_prev, h_next = _halo_specs(tm, BF16_ROWS, D_MODEL, n_rows, lambda k: 0)
    row = pl.BlockSpec((tm, D_MODEL), lambda i, k: (i, 0))
    vec = pl.BlockSpec((1, D_MODEL), lambda i, k: (0, 0))
    tab = pl.BlockSpec(mod_l.shape, lambda i, k: (0, 0))
    out_specs = [row, row] if emit_h else [row]
    out_shape = [jax.ShapeDtypeStruct((n_rows, D_MODEL), F32)]
    if emit_h:
        out_shape.append(jax.ShapeDtypeStruct((n_rows, D_MODEL), BF16))
    conv_b3 = conv_b.reshape(DEPTH, 1, 2 * FFN_HIDDEN)

    res = pl.pallas_call(
        functools.partial(_ffn_body, emit_h, n_rows > T_LAT),
        grid=(n_rows // tm, kt),
        in_specs=[
            row, h_prev, h_next,
            pl.BlockSpec((None, D_MODEL, tk), lambda i, k: (layer, 0, k)),
            pl.BlockSpec((None, D_MODEL, tk), lambda i, k: (layer, 0, k + kt)),
            pl.BlockSpec((None, 3, tk), lambda i, k: (layer, 0, k)),
            pl.BlockSpec((None, 3, tk), lambda i, k: (layer, 0, k + kt)),
            pl.BlockSpec((None, 1, tk), lambda i, k: (layer, 0, k)),
            pl.BlockSpec((None, 1, tk), lambda i, k: (layer, 0, k + kt)),
            pl.BlockSpec((None, tk, D_MODEL), lambda i, k: (layer, k, 0)),
            row, vec, tab, vec, tab,
        ],
        out_specs=out_specs,
        out_shape=out_shape,
        scratch_shapes=[pltpu.VMEM((tm + 2 * BF16_ROWS, D_MODEL), BF16), pltpu.VMEM((tm, D_MODEL), F32)],
        compiler_params=_cparams(("arbitrary", "arbitrary"), 56),
        name="conv_ffn",
    )(h, h, h, w_up, w_up, conv_w, conv_w, conv_b3, conv_b3, w_down, xs, g_res.reshape(1, D_MODEL), mod_l,
      g_next.reshape(1, D_MODEL), mod_next)
    return (res[0], res[1]) if emit_h else (res[0], None)


def _in_proj_conv_body(tm, h_ref, hp_ref, hn_ref, w_ref, cw_ref, cb_ref, o_ref, wb_ref):
    @pl.when(pl.program_id(1) == 0)
    def _():
        wb_ref[...] = w_ref[...].astype(BF16)

    w = wb_ref[...]
    u = _dot(h_ref[...], w)
    uh = _dot(jnp.concatenate([hp_ref[...], hn_ref[...]], axis=0), w)
    conv = _token_conv(u, uh[BF16_ROWS - 1:BF16_ROWS], uh[BF16_ROWS:BF16_ROWS + 1], pl.program_id(1) * tm,
                       cw_ref, cb_ref)
    o_ref[...] = _silu(conv).astype(o_ref.dtype)


def _in_proj_conv(h, w_stack, slot, conv_w, conv_b, tn, first_tile, name):
    tm = T_ALL // 8
    n_dim = conv_w.shape[1]
    per = tm // BF16_ROWS
    last_blk = T_ALL // BF16_ROWS - 1
    return pl.pallas_call(
        functools.partial(_in_proj_conv_body, tm),
        grid=(n_dim // tn, T_ALL // tm),
        in_specs=[pl.BlockSpec((tm, D_MODEL), lambda j, i: (i, 0)),
                  pl.BlockSpec((BF16_ROWS, D_MODEL), lambda j, i: (jnp.maximum(i * per - 1, 0), 0)),
                  pl.BlockSpec((BF16_ROWS, D_MODEL), lambda j, i: (jnp.minimum((i + 1) * per, last_blk), 0)),
                  pl.BlockSpec((None, D_MODEL, tn), lambda j, i: (slot, 0, first_tile + j)),
                  pl.BlockSpec((3, tn), lambda j, i: (0, j)),
                  pl.BlockSpec((1, tn), lambda j, i: (0, j))],
        out_specs=pl.BlockSpec((tm, tn), lambda j, i: (i, j)),
        out_shape=jax.ShapeDtypeStruct((T_ALL, n_dim), BF16),
        scratch_shapes=[pltpu.VMEM((D_MODEL, tn), BF16)],
        compiler_params=_cparams(("arbitrary", "arbitrary"), 56),
        name=name,
    )(h, h, h, w_stack, conv_w, conv_b.reshape(1, -1))


def _scan_chunk_index(d, b, s, n_lat, n_ctx):
    ctx0 = BATCH * n_lat + b * n_ctx
    fwd = jnp.where(s < n_ctx, ctx0 + s, b * n_lat + s - n_ctx)
    bwd = jnp.where(s < n_ctx, ctx0 + n_ctx - 1 - s, b * n_lat + n_lat - 1 - (s - n_ctx))
    return jnp.where(d == 0, fwd, bwd)


SSD_CUM_TERMS = 3
SSD_DT_TERMS = 2


def _ssd_scan_consts():
    q = SSM_CHUNK
    i = np.arange(q)
    pfx = np.zeros((2, q + ONES_ROWS, q), np.float32)
    pfx[0, :q] = i[:, None] >= i[None, :]
    pfx[1, :q] = i[:, None] <= i[None, :]
    pfx[:, q:] = 1.0
    pfx = np.tile(pfx, (1, 1, SSD_CUM_TERMS))
    expand = np.repeat(np.eye(SSM_HEADS, dtype=np.float32), SSM_HEAD_DIM, axis=1)
    expand = np.tile(expand, (SSD_DT_TERMS, 1))
    return jnp.asarray(pfx, BF16), jnp.asarray(expand, BF16)


def _ssd_scan_body(x_ref, b_ref, c_ref, dtraw_ref, dtb_ref, alog_ref, dskip_ref, pfx_ref, exp_ref, y_ref, s_ref):
    q = SSM_CHUNK
    hp = 2 * SSM_HEAD_DIM
    d = pl.program_id(0)

    @pl.when(pl.program_id(2) == 0)
    def _():
        s_ref[...] = jnp.zeros_like(s_ref)

    def pick(v):
        return jnp.where(d == 0, v[:, :SSM_HEADS], v[:, SSM_HEADS:])

    dt2 = _softplus(dtraw_ref[...] + dtb_ref[...])
    cum2 = _dot_sel(pfx_ref[...], dt2 * (-LOG2E * jnp.exp(alog_ref[...])), SSD_CUM_TERMS)
    acum2 = cum2[:q]
    acum = pick(acum2)
    tot = pick(cum2[q:q + SUBLANES])[0:1]
    acum_t2 = acum2.T
    acum_t = jnp.where(d == 0, acum_t2[:SSM_HEADS], acum_t2[SSM_HEADS:])
    dt_wide = _dot_sel_rhs(pick(dt2), exp_ref[...], SSD_DT_TERMS)

    ii = lax.broadcasted_iota(jnp.int32, (q, q), 0)
    jj = lax.broadcasted_iota(jnp.int32, (q, q), 1)
    causal = (jj - ii) * (1 - 2 * d) <= 0
    low_half = lax.broadcasted_iota(jnp.int32, (q, hp), 1) < SSM_HEAD_DIM

    heads_per_group = SSM_HEADS // SSM_GROUPS
    n_pairs = SSM_HEADS // 2

    b_gs = [b_ref[:, g * SSM_STATE:(g + 1) * SSM_STATE].astype(BF16) for g in range(SSM_GROUPS)]
    c_bf = [c_ref[:, g * SSM_STATE:(g + 1) * SSM_STATE].astype(BF16) for g in range(SSM_GROUPS)]
    c_gs = [c.astype(F32) for c in c_bf]
    cbs = [jnp.where(causal, _dot_nt(c_bf[g], b_gs[g]), 0.0) for g in range(SSM_GROUPS)]

    cols, lhs = [], []
    for h in range(SSM_HEADS):
        g = h // heads_per_group
        col = jnp.broadcast_to(acum[:, h:h + 1], (q, q))
        seg = jnp.where(causal, col - acum_t[h:h + 1, :], 0.0)
        scores = jnp.exp2(seg) * cbs[g]
        c_dec = c_gs[g] * jnp.exp2(col)
        lhs.append(jnp.concatenate([scores.astype(BF16), c_dec.astype(BF16)], axis=1))
        cols.append(col)

    xdts = []
    for p in range(n_pairs):
        lo = p * hp
        x_pair = x_ref[:, lo:lo + hp].astype(F32)
        xdt = x_pair * dt_wide[:, lo:lo + hp]
        rhs = jnp.concatenate([xdt.astype(BF16), s_ref[:, lo:lo + hp].astype(BF16)], axis=0)
        y_pair = jnp.where(low_half, _dot(lhs[2 * p], rhs), _dot(lhs[2 * p + 1], rhs))
        y_ref[:, lo:lo + hp] = (y_pair + dskip_ref[:, lo:lo + hp] * x_pair).astype(y_ref.dtype)
        xdts.append(xdt)

    for p in range(n_pairs):
        lo = p * hp
        g = (2 * p) // heads_per_group
        tot_pair = jnp.where(low_half[0:1], jnp.broadcast_to(tot[:, 2 * p:2 * p + 1], (1, hp)),
                             jnp.broadcast_to(tot[:, 2 * p + 1:2 * p + 2], (1, hp)))
        to_end = jnp.exp2(tot_pair - jnp.where(low_half, cols[2 * p], cols[2 * p + 1]))
        s_ref[:, lo:lo + hp] = (s_ref[:, lo:lo + hp] * jnp.exp2(tot_pair)
                                + _dot_tn(b_gs[g], (xdts[p] * to_end).astype(BF16)))


def _ssd_scan(xbc, dt_raw, dt_bias, a_log, d_skip):
    q = SSM_CHUNK
    n_lat, n_ctx = SEQ // q, CTX_LEN // q
    pfx, expand = _ssd_scan_consts()
    n_heads2 = 2 * SSM_HEADS
    b_col = SSM_INNER // (SSM_GROUPS * SSM_STATE)
    d_wide = jnp.repeat(d_skip, SSM_HEAD_DIM, axis=1).reshape(2, 1, SSM_INNER)

    def chunk(d, b, s):
        return _scan_chunk_index(d, b, s, n_lat, n_ctx)

    return pl.pallas_call(
        _ssd_scan_body,
        grid=(2, BATCH, n_lat + n_ctx),
        in_specs=[
            pl.BlockSpec((q, SSM_INNER), lambda d, b, s: (chunk(d, b, s), 0)),
            pl.BlockSpec((q, SSM_GROUPS * SSM_STATE), lambda d, b, s: (chunk(d, b, s), b_col)),
            pl.BlockSpec((q, SSM_GROUPS * SSM_STATE), lambda d, b, s: (chunk(d, b, s), b_col + 1)),
            pl.BlockSpec((q, n_heads2), lambda d, b, s: (chunk(d, b, s), 0)),
            pl.BlockSpec((1, n_heads2), lambda d, b, s: (0, 0)),
            pl.BlockSpec((1, n_heads2), lambda d, b, s: (0, 0)),
            pl.BlockSpec((None, 1, SSM_INNER), lambda d, b, s: (d, 0, 0)),
            pl.BlockSpec((None,) + pfx.shape[1:], lambda d, b, s: (d, 0, 0)),
            pl.BlockSpec(expand.shape, lambda d, b, s: (0, 0)),
        ],
        out_specs=pl.BlockSpec((None, q, SSM_INNER), lambda d, b, s: (d, chunk(d, b, s), 0)),
        out_shape=jax.ShapeDtypeStruct((2, T_ALL, SSM_INNER), BF16),
        scratch_shapes=[pltpu.VMEM((SSM_STATE, SSM_INNER), F32)],
        compiler_params=_cparams(("arbitrary", "arbitrary", "arbitrary"), 48),
        name="ssd_scan",
    )(xbc, xbc, xbc, dt_raw, dt_bias.reshape(1, n_heads2), a_log.reshape(1, n_heads2), d_wide, pfx, expand)


GLA_SUM_TERMS = 2


def _gla_consts():
    q = HG_CHUNK
    levels = int(np.log2(q))
    i = np.arange(q)
    mats = [(i[:, None] >= i[None, :]).astype(np.float32), (i[None, :] > i[:, None]).astype(np.float32)]
    masks = [np.eye(q, dtype=np.float32)]
    for lv in range(levels):
        s = 1 << lv
        blk, pos = i // (2 * s), i % (2 * s)
        bound = blk * 2 * s + s - 1
        right = pos >= s
        t = i[None, :]
        m_right = right[:, None] & (t > bound[:, None]) & (t <= i[:, None])
        m_left = (~right)[:, None] & (t > i[:, None]) & (t <= bound[:, None])
        mats.append((m_right | m_left).astype(np.float32))
        masks.append((right[:, None] & (~right)[None, :] & (blk[:, None] == blk[None, :])).astype(np.float32))
    mats.append(np.ones((ONES_ROWS, q), np.float32))
    fwd_m, fwd_k = np.concatenate(mats, axis=0), np.stack(masks)
    flip_rows = np.concatenate([np.arange(k * q, (k + 1) * q)[::-1] for k in range(levels + 2)]
                               + [np.arange((levels + 2) * q, (levels + 2) * q + ONES_ROWS)])
    bwd_m = fwd_m[flip_rows][:, ::-1]
    bwd_k = fwd_k[:, ::-1, ::-1]
    mats2 = np.tile(np.stack([fwd_m, bwd_m]), (1, 1, GLA_SUM_TERMS))
    return (jnp.asarray(mats2, BF16), jnp.asarray(np.stack([fwd_k, bwd_k]), F32), levels)


def _silu_epilogue(u):
    return (_silu(u),)


def _hg_gate_epilogue(layer, u, lbp_ref):
    lbp = lbp_ref[...]
    e = jnp.exp(lbp - jnp.max(lbp, axis=0, keepdims=True))
    sm = e / jnp.sum(e, axis=0, keepdims=True)
    lb = jnp.sum(sm[1:layer + 1], axis=0, keepdims=True) if layer > 0 else jnp.zeros((1, lbp.shape[1]), F32)
    ef = jnp.exp(-jnp.abs(u))
    log_sig = jnp.minimum(u, 0.0) - _log1p_unit(ef)
    la = jnp.log(lb)
    lc = jnp.log1p(-lb) + log_sig
    logf = jnp.maximum(la, lc) + _log1p_unit(jnp.exp(-jnp.abs(la - lc)))
    key = (1.0 - lb) * (jnp.where(u >= 0.0, ef, 1.0) / (1.0 + ef))
    return logf, key


def _gla_body(levels, q_ref, v_ref, logf_ref, key_ref, mats_ref, masks_ref, o_ref, st_ref):
    q = HG_CHUNK

    @pl.when(pl.program_id(2) == 0)
    def _():
        st_ref[...] = jnp.zeros_like(st_ref)

    qs_bf, key_bf, v_bf = q_ref[...], key_ref[...], v_ref[...]
    qs, key = qs_bf.astype(F32), key_bf.astype(F32)

    sums = _dot_sel(mats_ref[...], logf_ref[...], GLA_SUM_TERMS)
    q_in = (qs * jnp.exp(sums[:q])).astype(BF16)
    k_out = (key * jnp.exp(sums[q:2 * q])).astype(BF16)
    dec = jnp.exp(sums[(levels + 2) * q:(levels + 2) * q + 1])
    q_lv, k_lv = [], []
    for lv in range(levels):
        fac = jnp.exp(sums[(lv + 2) * q:(lv + 3) * q])
        q_lv.append((qs * fac).astype(BF16))
        k_lv.append((key * fac).astype(BF16))

    heads = [slice(h * LANES, (h + 1) * LANES) for h in range(HG_HEADS)]
    attn = masks_ref[0][None] * jnp.stack([_dot_nt(qs_bf[:, sl], key_bf[:, sl]) for sl in heads], axis=0)
    for lv in range(levels):
        attn = attn + masks_ref[lv + 1][None] * jnp.stack(
            [_dot_nt(q_lv[lv][:, sl], k_lv[lv][:, sl]) for sl in heads], axis=0)
    attn = attn.astype(BF16)
    st_bf = st_ref[...].astype(BF16)
    o_ref[...] = jnp.concatenate(
        [_dot(attn[h], v_bf[:, sl]) + _dot_nt(q_in[:, sl], st_bf[h]) for h, sl in enumerate(heads)],
        axis=1).astype(o_ref.dtype)
    for h, sl in enumerate(heads):
        st_ref[h] = st_ref[h] * dec[:, sl] + _dot_tn(v_bf[:, sl], k_out[:, sl])


def _gla_scan(qs, v, logf, key):
    q = HG_CHUNK
    n_lat, n_ctx = SEQ // q, CTX_LEN // q
    mats, masks, levels = _gla_consts()

    def chunk(d, b, s):
        return _scan_chunk_index(d, b, s, n_lat, n_ctx)

    return pl.pallas_call(
        functools.partial(_gla_body, levels),
        grid=(2, BATCH, n_lat + n_ctx),
        in_specs=[
            pl.BlockSpec((q, D_MODEL), lambda d, b, s: (chunk(d, b, s), 0)),
            pl.BlockSpec((q, D_MODEL), lambda d, b, s: (chunk(d, b, s), 0)),
            pl.BlockSpec((q, D_MODEL), lambda d, b, s: (chunk(d, b, s), d)),
            pl.BlockSpec((q, D_MODEL), lambda d, b, s: (chunk(d, b, s), d)),
            pl.BlockSpec((None,) + mats.shape[1:], lambda d, b, s: (d, 0, 0)),
            pl.BlockSpec((None,) + masks.shape[1:], lambda d, b, s: (d, 0, 0, 0)),
        ],
        out_specs=pl.BlockSpec((None, q, D_MODEL), lambda d, b, s: (d, chunk(d, b, s), 0)),
        out_shape=jax.ShapeDtypeStruct((2, T_ALL, D_MODEL), BF16),
        scratch_shapes=[pltpu.VMEM((HG_HEADS, LANES, HG_KEY_DIM), F32)],
        compiler_params=_cparams(("arbitrary", "arbitrary", "arbitrary"), 48),
        name="gla_scan",
    )(qs, v, logf, key, mats, masks)


def _na_bias_body(rpb_ref, o_ref):
    h = pl.program_id(0)
    n_dc = 2 * NA_WIN_C - 1
    qc = lax.broadcasted_iota(jnp.int32, (GRID_W, LANES), 0)
    kc = lax.broadcasted_iota(jnp.int32, (GRID_W, LANES), 1) % GRID_W
    c0 = jnp.clip(qc - NA_WIN_C // 2, 0, GRID_W - NA_WIN_C)
    col_in = (kc >= c0) & (kc < c0 + NA_WIN_C)
    dc = jnp.clip(kc - qc + NA_WIN_C - 1, 0, n_dc - 1)
    low = lax.broadcasted_iota(jnp.int32, (GRID_W, LANES), 1) < GRID_W
    tiles = []
    for dr in range(2 * NA_WIN_R - 1):
        t = jnp.zeros((GRID_W, LANES), F32)
        for k in range(n_dc):
            t = jnp.where(dc == k, rpb_ref[(h * (2 * NA_WIN_R - 1) + dr) * n_dc + k], t)
        tiles.append(jnp.where(col_in, t, -jnp.inf))
    for dr in range(2 * NA_WIN_R - 2):
        o_ref[dr] = jnp.where(low, tiles[dr], tiles[dr + 1])


def _na_bias_table(rpb):
    n_dr = 2 * NA_WIN_R - 2
    return pl.pallas_call(
        _na_bias_body,
        grid=(NA_HEADS,),
        in_specs=[pl.BlockSpec(memory_space=pltpu.SMEM)],
        out_specs=pl.BlockSpec((None, n_dr, GRID_W, LANES), lambda h: (h, 0, 0, 0)),
        out_shape=jax.ShapeDtypeStruct((NA_HEADS, n_dr, GRID_W, LANES), F32),
        compiler_params=_cparams(("arbitrary",), 32),
        name="na_bias",
    )(rpb.reshape(-1))


def _rope_tables():
    quarter = NA_HEAD_DIM // 4
    inv = ROPE_BASE ** (-np.arange(quarter, dtype=np.float64) / quarter)
    t = np.arange(SEQ)
    row, col = t // GRID_W, t % GRID_W
    ang = np.concatenate([row[:, None] * inv[None], row[:, None] * inv[None],
                          col[:, None] * inv[None], col[:, None] * inv[None]], axis=1)
    sign = np.tile(np.concatenate([-np.ones(quarter), np.ones(quarter)]), 2)[None]
    return jnp.asarray(np.cos(ang), F32), jnp.asarray(np.sin(ang) * sign, F32)


def _rope(v, cos, sin_signed):
    quarter = NA_HEAD_DIM // 4
    lane = lax.broadcasted_iota(jnp.int32, v.shape, 1)
    first = (lane % (2 * quarter)) < quarter
    partner = jnp.where(first, pltpu.roll(v, NA_HEAD_DIM - quarter, axis=1), pltpu.roll(v, quarter, axis=1))
    return v * cos + partner * sin_signed


def _na_body(rows_per_step, q_ref, k_ref, v_ref, kc_ref, vc_ref, cosq_ref, sinq_ref, cos_ref, sin_ref, bias_ref,
             o_ref, kr_ref, vb_ref):
    rb = pl.program_id(2)
    n_rows = SEQ // GRID_W
    scale = NA_HEAD_DIM ** -0.5

    @pl.when(rb == 0)
    def _():
        kr_ref[...] = _rope(k_ref[...], cos_ref[...], sin_ref[...]).astype(BF16)
        vb_ref[...] = v_ref[...].astype(BF16)

    qr = (_rope(q_ref[...], cosq_ref[...], sinq_ref[...]) * scale).astype(BF16)
    kc = kc_ref[...].astype(BF16)
    vc = vc_ref[...].astype(BF16)
    starts, s_rows = [], []
    for lr in range(rows_per_step):
        r = rb * rows_per_step + lr
        r0 = jnp.clip(r - NA_WIN_R // 2, 0, n_rows - NA_WIN_R)
        start = pl.multiple_of(r0 * GRID_W, GRID_W)
        dr0 = r0 - r + NA_WIN_R - 1
        bias = jnp.concatenate([bias_ref[dr0 + 2 * p] for p in range(NA_WIN_R // 2)], axis=1)
        k_win = kr_ref[pl.ds(start, NA_WIN_R * GRID_W), :]
        s_rows.append(_dot_nt(qr[lr * GRID_W:(lr + 1) * GRID_W], k_win) + bias)
        starts.append(start)
    s_loc = jnp.concatenate(s_rows, axis=0)
    s_ctx = _dot_nt(qr, kc)
    m = jnp.maximum(jnp.max(s_loc, axis=-1, keepdims=True), jnp.max(s_ctx, axis=-1, keepdims=True))
    p_loc = jnp.exp(s_loc - m)
    p_ctx = jnp.exp(s_ctx - m)
    denom = jnp.sum(p_loc, axis=-1, keepdims=True) + jnp.sum(p_ctx, axis=-1, keepdims=True)
    p_loc = p_loc.astype(BF16)
    o_loc = jnp.concatenate(
        [_dot(p_loc[lr * GRID_W:(lr + 1) * GRID_W], vb_ref[pl.ds(starts[lr], NA_WIN_R * GRID_W), :])
         for lr in range(rows_per_step)], axis=0)
    o = o_loc + _dot(p_ctx.astype(BF16), vc)
    o_ref[...] = (o / denom).astype(o_ref.dtype)


def _na_attention(qkv, bias_tbl):
    rows_per_step = 32
    tq = rows_per_step * GRID_W
    steps = SEQ // tq
    cos, sin = _rope_tables()
    hd = NA_HEAD_DIM
    ctx_blk0 = T_LAT // CTX_LEN
    return pl.pallas_call(
        functools.partial(_na_body, rows_per_step),
        grid=(BATCH, NA_HEADS, steps),
        in_specs=[
            pl.BlockSpec((tq, hd), lambda b, h, r: (b * steps + r, h)),
            pl.BlockSpec((SEQ, hd), lambda b, h, r: (b, NA_HEADS + h)),
            pl.BlockSpec((SEQ, hd), lambda b, h, r: (b, 2 * NA_HEADS + h)),
            pl.BlockSpec((CTX_LEN, hd), lambda b, h, r: (ctx_blk0 + b, NA_HEADS + h)),
            pl.BlockSpec((CTX_LEN, hd), lambda b, h, r: (ctx_blk0 + b, 2 * NA_HEADS + h)),
            pl.BlockSpec((tq, hd), lambda b, h, r: (r, 0)),
            pl.BlockSpec((tq, hd), lambda b, h, r: (r, 0)),
            pl.BlockSpec((SEQ, hd), lambda b, h, r: (0, 0)),
            pl.BlockSpec((SEQ, hd), lambda b, h, r: (0, 0)),
            pl.BlockSpec((None,) + bias_tbl.shape[1:], lambda b, h, r: (h, 0, 0, 0)),
        ],
        out_specs=pl.BlockSpec((tq, hd), lambda b, h, r: (b * steps + r, h)),
        out_shape=jax.ShapeDtypeStruct((T_LAT, D_MODEL), BF16),
        scratch_shapes=[pltpu.VMEM((SEQ, hd), BF16), pltpu.VMEM((SEQ, hd), BF16)],
        compiler_params=_cparams(("arbitrary", "arbitrary", "arbitrary"), 48),
        name="na_attention",
    )(qkv, qkv, qkv, qkv, qkv, cos, sin, cos, sin, bias_tbl)


def _ctx_attn_body(q_ref, k_ref, v_ref, o_ref):
    scale = NA_HEAD_DIM ** -0.5
    s = _dot_nt((q_ref[...] * scale).astype(BF16), k_ref[...].astype(BF16))
    p = jnp.exp(s - jnp.max(s, axis=-1, keepdims=True))
    o = _dot(p.astype(BF16), v_ref[...].astype(BF16))
    o_ref[...] = (o / jnp.sum(p, axis=-1, keepdims=True)).astype(o_ref.dtype)


def _ctx_attention(qkv):
    hd = NA_HEAD_DIM
    ctx_blk0 = T_LAT // CTX_LEN
    return pl.pallas_call(
        _ctx_attn_body,
        grid=(BATCH, NA_HEADS),
        in_specs=[
            pl.BlockSpec((CTX_LEN, hd), lambda b, h: (ctx_blk0 + b, h)),
            pl.BlockSpec((CTX_LEN, hd), lambda b, h: (ctx_blk0 + b, NA_HEADS + h)),
            pl.BlockSpec((CTX_LEN, hd), lambda b, h: (ctx_blk0 + b, 2 * NA_HEADS + h)),
        ],
        out_specs=pl.BlockSpec((CTX_LEN, hd), lambda b, h: (b, h)),
        out_shape=jax.ShapeDtypeStruct((T_CTX, D_MODEL), BF16),
        compiler_params=_cparams(("arbitrary", "arbitrary"), 32),
        name="ctx_attention",
    )(qkv, qkv, qkv)


def kernel(x, c, ctx, c_ctx, w_mod, b_mod, norm_g, ffn_w_up, ffn_conv_w, ffn_conv_b, ffn_w_down, ssm_w_in, ssm_conv_w, ssm_conv_b, ssm_dt_bias, ssm_a_log, ssm_d, ssm_norm_g, ssm_w_out, hg_w_in, hg_lb, hg_norm_g, hg_w_out, na_w_qkv, na_rpb, na_w_out):
    cvec8 = jnp.concatenate([c, c_ctx[None], jnp.zeros((SUBLANES - BATCH - 1, D_MODEL), F32)], axis=0)
    mod = _adaln(cvec8, w_mod, b_mod)
    ffn_up_bf, ffn_down_bf = ffn_w_up.astype(BF16), ffn_w_down.astype(BF16)
    ssm_out_bf, hg_out_bf, na_out_bf = ssm_w_out.astype(BF16), hg_w_out.astype(BF16), na_w_out.astype(BF16)

    xs, h = _prep(x.reshape(T_LAT, D_MODEL), ctx.reshape(T_CTX, D_MODEL), norm_g[0, 0], mod[0])
    for i in range(DEPTH):
        last = i == DEPTH - 1
        kind, slot = i % N_MIXERS, i // N_MIXERS
        mod_l = mod[i]
        n_rows = T_LAT if last else T_ALL
        op = dict(xs=xs, g_res=norm_g[i, 1], g_ffn=norm_g[i, 2], mod_l=mod_l, n_rows=n_rows, slot=slot)
        if kind == 0:
            tn = 1024
            z = _in_proj(h, ssm_w_in, slot, tn, "ssd_in_z", n_dim=SSM_INNER)
            xbc = _in_proj_conv(h, ssm_w_in, slot, ssm_conv_w[slot], ssm_conv_b[slot], tn, SSM_INNER // tn,
                                "ssd_in_xbc")
            dt_raw = _in_proj(h, ssm_w_in, slot, 2 * SSM_HEADS, "ssd_dt", n_dim=2 * SSM_HEADS,
                              first_tile=(SSM_INNER + SSM_CONV_DIM) // (2 * SSM_HEADS))
            y2 = _ssd_scan(xbc, dt_raw, ssm_dt_bias[slot], ssm_a_log[slot], ssm_d[slot])
            xs, hf = _out_proj(_ssd_gate_prologue, [((y2, 0), 0), ((y2, 1), 0), (z, 0)],
                               ssm_norm_g[slot].reshape(1, SSM_INNER), ssm_out_bf, tm=256, name="ssd_out", **op)
        elif kind == 1:
            tn = 1024
            per = D_MODEL // tn
            qs = _in_proj(h, hg_w_in, slot, tn, "hg_in_q", n_dim=D_MODEL, epilogue=_silu_epilogue,
                          out_dtypes=(BF16,))
            v = _in_proj(h, hg_w_in, slot, tn, "hg_in_v", n_dim=D_MODEL, first_tile=per, out_dtypes=(BF16,))
            lb_spec = pl.BlockSpec((None, DEPTH, tn), lambda j, r: (j // per, 0, j % per))
            logf, key = _in_proj(h, hg_w_in, slot, tn, "hg_in_f", n_dim=2 * D_MODEL, first_tile=2 * per,
                                 epilogue=functools.partial(_hg_gate_epilogue, i), aux=[(hg_lb, lb_spec)],
                                 out_dtypes=(F32, BF16))
            gate = _in_proj(h, hg_w_in, slot, tn, "hg_in_g", n_dim=D_MODEL, first_tile=4 * per)
            o2 = _gla_scan(qs, v, logf, key)
            xs, hf = _out_proj(_hg_readout_prologue, [((o2, 0), 0), ((o2, 1), 0), (gate, 0)],
                               hg_norm_g[slot].reshape(1, D_MODEL), hg_out_bf, tm=ROW_TILE, name="hg_out", **op)
        else:
            qkv = _in_proj(h, na_w_qkv, slot, 1024, "na_in")
            o_lat = _na_attention(qkv, _na_bias_table(na_rpb[slot]))
            o_ctx = _ctx_attention(qkv)
            assert T_CTX == ROW_TILE
            last_lat = T_LAT // ROW_TILE - 1
            xs, hf = _out_proj(_na_prologue, [(o_lat, 0, functools.partial(jnp.minimum, last_lat)),
                                              (o_ctx, 0, lambda t: 0)],
                               None, na_out_bf, tm=ROW_TILE, name="na_out", **op)
        nxt = min(i + 1, DEPTH - 1)
        xs, h = _ffn(hf, xs, i, ffn_up_bf, ffn_conv_w, ffn_conv_b, ffn_down_bf, norm_g[i, 3], mod_l,
                     norm_g[nxt, 0], mod[nxt], n_rows=n_rows, emit_h=not last)

    return xs.reshape(BATCH, SEQ, D_MODEL)
```

```python
import functools

import numpy as np
import jax
import jax.numpy as jnp
from jax import lax
from jax.experimental import pallas as pl
from jax.experimental.pallas import tpu as pltpu

F32 = jnp.float32
BF16 = jnp.bfloat16

D_MODEL = 2048
BATCH = 2
SEQ = 4096
DEPTH = 4
GRID_W = 64
CTX_LEN = 256
N_MIXERS = 3
RMS_EPS = 1e-6
SSM_INNER = 4096
SSM_HEAD_DIM = 64
SSM_HEADS = 64
SSM_STATE = 128
SSM_GROUPS = 8
SSM_CONV_DIM = SSM_INNER + 2 * SSM_GROUPS * SSM_STATE
SSM_IN_DIM = SSM_INNER + SSM_CONV_DIM + 2 * SSM_HEADS
SSM_CHUNK = 128
HG_HEADS = 16
HG_KEY_DIM = 128
HG_IN_DIM = 5 * D_MODEL
HG_CHUNK = 64
NA_HEADS = 16
NA_HEAD_DIM = 128
NA_WIN_R = 8
NA_WIN_C = 16
ROPE_BASE = 10000.0
FFN_HIDDEN = 5632

T_LAT = BATCH * SEQ
T_CTX = BATCH * CTX_LEN
T_ALL = T_LAT + T_CTX
LANES = 128
SUBLANES = 8
BF16_ROWS = 16
ONES_ROWS = 16
ROW_TILE = 512
FFN_COL_CHUNK = 512
MIB = 1024 * 1024
LOG2E = 1.4426950408889634


def _cparams(sem, vmem_mib):
    return pltpu.CompilerParams(dimension_semantics=sem, vmem_limit_bytes=vmem_mib * MIB)


def _silu(v):
    return v * jax.nn.sigmoid(v)


def _rms(v):
    return v * lax.rsqrt(jnp.mean(v * v, axis=-1, keepdims=True) + RMS_EPS)


def _log1p_unit(e):
    return jnp.log(1.0 + e)


def _softplus(v):
    return jnp.maximum(v, 0.0) + _log1p_unit(jnp.exp(-jnp.abs(v)))


def _dot(a, b):
    return jnp.dot(a, b, preferred_element_type=F32)


def _dot_nt(a, b):
    return lax.dot_general(a, b, (((1,), (1,)), ((), ())), preferred_element_type=F32)


def _dot_tn(a, b):
    return lax.dot_general(a, b, (((0,), (0,)), ((), ())), preferred_element_type=F32)


def _bf16_pieces(v, terms):
    pieces = []
    rem = v
    for _ in range(terms):
        part = rem.astype(BF16)
        rem = rem - part.astype(F32)
        pieces.append(part)
    return pieces


def _dot_sel(m01_rep, v, terms):
    return _dot(m01_rep, jnp.concatenate(_bf16_pieces(v, terms), axis=0))


def _dot_sel_rhs(v, m01_rep, terms):
    return _dot(jnp.concatenate(_bf16_pieces(v, terms), axis=1), m01_rep)


def _mod_row(row0):
    return jnp.where(row0 >= T_LAT, 2, row0 // SEQ)


def _mod_vec(mod_ref, r, idx):
    return mod_ref[pl.ds(r, 1), idx * D_MODEL:(idx + 1) * D_MODEL]


def _norm_mod(x, g, mod_ref, r, shift_idx, scale_idx):
    return _rms(x) * g * (1.0 + _mod_vec(mod_ref, r, scale_idx)) + _mod_vec(mod_ref, r, shift_idx)


def _seg_edges(row0, rows):
    gr = row0 + lax.broadcasted_iota(jnp.int32, (rows, 1), 0)
    pos = jnp.where(gr >= T_LAT, (gr - T_LAT) % CTX_LEN, gr % SEQ)
    length = jnp.where(gr >= T_LAT, CTX_LEN, SEQ)
    return pos == 0, pos == length - 1


def _shift_rows(cur, prev_row, next_row, row0):
    rows = cur.shape[0]
    first, last = _seg_edges(row0, rows)
    sub = lax.broadcasted_iota(jnp.int32, (SUBLANES, 1), 0)
    up = pltpu.roll(cur, 1, axis=0)
    up = jnp.concatenate([jnp.where(sub == 0, prev_row, up[:SUBLANES]), up[SUBLANES:]], axis=0)
    up = jnp.where(first, 0.0, up)
    dn = pltpu.roll(cur, rows - 1, axis=0)
    dn = jnp.concatenate([dn[:rows - SUBLANES], jnp.where(sub == SUBLANES - 1, next_row, dn[rows - SUBLANES:])],
                         axis=0)
    dn = jnp.where(last, 0.0, dn)
    return up, dn


def _token_conv(cur, prev_row, next_row, row0, w_ref, b_ref):
    up, dn = _shift_rows(cur, prev_row, next_row, row0)
    return b_ref[...] + up * w_ref[0:1, :] + cur * w_ref[1:2, :] + dn * w_ref[2:3, :]


def _halo_specs(tm, halo, width, n_rows, col_fn):
    per = tm // halo
    last_blk = n_rows // halo - 1
    prev = pl.BlockSpec((halo, width), lambda i, j: (jnp.maximum(i * per - 1, 0), col_fn(j)))
    nxt = pl.BlockSpec((halo, width), lambda i, j: (jnp.minimum((i + 1) * per, last_blk), col_fn(j)))
    return prev, nxt


def _adaln_body(c_ref, w_ref, b_ref, o_ref):
    s = _silu(c_ref[...]).astype(BF16)
    o_ref[...] = _dot(s, w_ref[...].astype(BF16)) + b_ref[...]


def _adaln(cvec8, w_mod, b_mod):
    tn = 1024
    n_out = 6 * D_MODEL
    return pl.pallas_call(
        _adaln_body,
        grid=(DEPTH, n_out // tn),
        in_specs=[
            pl.BlockSpec((SUBLANES, D_MODEL), lambda l, j: (0, 0)),
            pl.BlockSpec((None, D_MODEL, tn), lambda l, j: (l, 0, j)),
            pl.BlockSpec((None, 1, tn), lambda l, j: (l, 0, j)),
        ],
        out_specs=pl.BlockSpec((None, SUBLANES, tn), lambda l, j: (l, 0, j)),
        out_shape=jax.ShapeDtypeStruct((DEPTH, SUBLANES, n_out), F32),
        compiler_params=_cparams(("arbitrary", "arbitrary"), 40),
        name="adaln",
    )(cvec8, w_mod, b_mod.reshape(DEPTH, 1, n_out))


def _prep_body(lat_ref, ctx_ref, g_ref, mod_ref, xs_ref, h_ref):
    row0 = pl.program_id(0) * ROW_TILE

    def emit(src_ref):
        xv = src_ref[...]
        xs_ref[...] = xv
        h_ref[...] = _norm_mod(xv, g_ref[...], mod_ref, _mod_row(row0), 0, 1).astype(BF16)

    @pl.when(row0 < T_LAT)
    def _():
        emit(lat_ref)

    @pl.when(row0 >= T_LAT)
    def _():
        emit(ctx_ref)


def _prep(x_lat, x_ctx, g_row, mod_l):
    assert T_CTX == ROW_TILE
    n_lat = T_LAT // ROW_TILE
    row = pl.BlockSpec((ROW_TILE, D_MODEL), lambda i: (i, 0))
    return pl.pallas_call(
        _prep_body,
        grid=(T_ALL // ROW_TILE,),
        in_specs=[pl.BlockSpec((ROW_TILE, D_MODEL), lambda i: (jnp.minimum(i, n_lat - 1), 0)),
                  pl.BlockSpec((ROW_TILE, D_MODEL), lambda i: (0, 0)),
                  pl.BlockSpec((1, D_MODEL), lambda i: (0, 0)), pl.BlockSpec(mod_l.shape, lambda i: (0, 0))],
        out_specs=[row, row],
        out_shape=[jax.ShapeDtypeStruct((T_ALL, D_MODEL), F32), jax.ShapeDtypeStruct((T_ALL, D_MODEL), BF16)],
        compiler_params=_cparams(("arbitrary",), 40),
        name="prep",
    )(x_lat, x_ctx, g_row.reshape(1, D_MODEL), mod_l)


def _in_proj_body(epilogue, n_aux, h_ref, w_ref, *rest):
    aux, outs, wb_ref = rest[:n_aux], rest[n_aux:-1], rest[-1]

    @pl.when(pl.program_id(1) == 0)
    def _():
        wb_ref[...] = w_ref[...].astype(BF16)

    u = _dot(h_ref[...], wb_ref[...])
    vals = (u,) if epilogue is None else epilogue(u, *aux)
    for o_ref, val in zip(outs, vals):
        o_ref[...] = val.astype(o_ref.dtype)


def _in_proj(h, w_stack, slot, tn, name, n_dim=None, first_tile=0, epilogue=None, aux=(), out_dtypes=(F32,)):
    tm = T_ALL // 8
    n_dim = w_stack.shape[2] if n_dim is None else n_dim
    res = pl.pallas_call(
        functools.partial(_in_proj_body, epilogue, len(aux)),
        grid=(n_dim // tn, T_ALL // tm),
        in_specs=[pl.BlockSpec((tm, D_MODEL), lambda j, i: (i, 0)),
                  pl.BlockSpec((None, D_MODEL, tn), lambda j, i: (slot, 0, first_tile + j))]
        + [spec for _, spec in aux],
        out_specs=[pl.BlockSpec((tm, tn), lambda j, i: (i, j)) for _ in out_dtypes],
        out_shape=[jax.ShapeDtypeStruct((T_ALL, n_dim), dt) for dt in out_dtypes],
        scratch_shapes=[pltpu.VMEM((D_MODEL, tn), BF16)],
        compiler_params=_cparams(("arbitrary", "arbitrary"), 56),
        name=name,
    )(h, w_stack, *[a for a, _ in aux])
    return res[0] if len(out_dtypes) == 1 else tuple(res)


def _ssd_gate_prologue(row0, yf_ref, yb_ref, z_ref, g_ref):
    v = (yf_ref[...].astype(F32) + yb_ref[...].astype(F32)) * _silu(z_ref[...])
    return v * g_ref[...], lax.rsqrt(jnp.mean(v * v, axis=-1, keepdims=True) + RMS_EPS)


def _hg_readout_prologue(row0, of_ref, ob_ref, gate_ref, g_ref):
    o = of_ref[...].astype(F32) + ob_ref[...].astype(F32)
    parts = [_rms(o[:, h * LANES:(h + 1) * LANES]) for h in range(o.shape[1] // LANES)]
    return jnp.concatenate(parts, axis=-1) * g_ref[...] * _silu(gate_ref[...]), None


def _na_prologue(row0, o_lat_ref, o_ctx_ref):
    return jnp.where(row0 >= T_LAT, o_ctx_ref[...], o_lat_ref[...]), None


def _out_proj_body(prologue, n_in, tm, *refs):
    ins = refs[:n_in]
    w_ref, x_ref, g1_ref, g2_ref, mod_ref, xo_ref, ho_ref = refs[n_in:]
    part, row_scale = prologue(pl.program_id(0) * tm, *ins)
    y = _dot(part.astype(BF16), w_ref[...])
    if row_scale is not None:
        y = y * row_scale
    r = _mod_row(pl.program_id(0) * tm)
    xn = x_ref[...] + _mod_vec(mod_ref, r, 2) * (_rms(y) * g1_ref[...])
    xo_ref[...] = xn
    ho_ref[...] = _norm_mod(xn, g2_ref[...], mod_ref, r, 3, 4).astype(BF16)


def _out_proj(prologue, row_ins, gain, w_stack, slot, xs, g_res, g_ffn, mod_l, *, n_rows, tm, name):
    k_dim = w_stack.shape[1]
    in_specs, args = [], []
    for arr, cb, *row_fn in row_ins:
        row_fn = row_fn[0] if row_fn else (lambda i: i)
        if isinstance(arr, tuple):
            in_specs.append(pl.BlockSpec((None, tm, k_dim), functools.partial(
                lambda i, ld, cb, row_fn: (ld, row_fn(i), cb), ld=arr[1], cb=cb, row_fn=row_fn)))
            args.append(arr[0])
        else:
            in_specs.append(pl.BlockSpec((tm, k_dim), functools.partial(
                lambda i, cb, row_fn: (row_fn(i), cb), cb=cb, row_fn=row_fn)))
            args.append(arr)
    if gain is not None:
        in_specs.append(pl.BlockSpec((1, k_dim), lambda i: (0, 0)))
        args.append(gain)
    n_in = len(args)
    row = pl.BlockSpec((tm, D_MODEL), lambda i: (i, 0))
    vec = pl.BlockSpec((1, D_MODEL), lambda i: (0, 0))
    in_specs += [pl.BlockSpec((None, k_dim, D_MODEL), lambda i: (slot, 0, 0), pipeline_mode=pl.Buffered(1)),
                 row, vec, vec, pl.BlockSpec(mod_l.shape, lambda i: (0, 0))]
    return pl.pallas_call(
        functools.partial(_out_proj_body, prologue, n_in, tm),
        grid=(n_rows // tm,),
        in_specs=in_specs,
        out_specs=[row, row],
        out_shape=[jax.ShapeDtypeStruct((n_rows, D_MODEL), F32), jax.ShapeDtypeStruct((n_rows, D_MODEL), BF16)],
        compiler_params=_cparams(("arbitrary",), 60),
        name=name,
    )(*args, w_stack, xs, g_res.reshape(1, D_MODEL), g_ffn.reshape(1, D_MODEL), mod_l)


def _ffn_body(emit_h, has_ctx, h_ref, hp_ref, hn_ref, wa_ref, wv_ref, cwa_ref, cwv_ref, cba_ref, cbv_ref, wd_ref,
              x_ref, g_res_ref, mod_ref, g_next_ref, modn_ref, *out_and_scratch):
    if emit_h:
        xo_ref, ho_ref, hall_ref, acc_ref = out_and_scratch
    else:
        xo_ref, hall_ref, acc_ref = out_and_scratch
    k = pl.program_id(1)
    tm = ROW_TILE
    row0 = pl.program_id(0) * tm
    pad = BF16_ROWS
    rows = tm + 2 * pad
    s8 = SUBLANES

    @pl.when(k == 0)
    def _():
        hall_ref[pl.ds(0, pad), :] = hp_ref[...]
        hall_ref[pl.ds(pad, tm), :] = h_ref[...]
        hall_ref[pl.ds(pad + tm, pad), :] = hn_ref[...]
        acc_ref[...] = jnp.zeros_like(acc_ref)

    hall = hall_ref[...]
    edge_first = row0 % SEQ == 0
    edge_last = ((row0 + tm) % SEQ == 0) | (row0 + tm == T_ALL)
    is_ctx = row0 >= T_LAT
    joints = list(range(CTX_LEN, tm, CTX_LEN)) if has_ctx else []
    sub = lax.broadcasted_iota(jnp.int32, (s8, 1), 0)

    def patch(v, start, cond):
        return jnp.concatenate([v[:start], jnp.where(cond, 0.0, v[start:start + s8]), v[start + s8:]], axis=0)

    def conv(w_ref, cw_ref, cb_ref, cs):
        u = _dot(hall, w_ref[:, cs])
        u = patch(u, pad - s8, edge_first)
        u = patch(u, pad + tm, edge_last)
        up = pltpu.roll(u, 1, axis=0)[pad:pad + tm]
        dn = pltpu.roll(u, rows - 1, axis=0)[pad:pad + tm]
        for j in joints:
            up = patch(up, j, is_ctx & (sub == 0))
            dn = patch(dn, j - s8, is_ctx & (sub == s8 - 1))
        return cb_ref[:, cs] + up * cw_ref[0:1, cs] + u[pad:pad + tm] * cw_ref[1:2, cs] + dn * cw_ref[2:3, cs]

    chunks = [pl.ds(c * FFN_COL_CHUNK, FFN_COL_CHUNK) for c in range(wa_ref.shape[1] // FFN_COL_CHUNK)]
    acts = [(_silu(conv(wa_ref, cwa_ref, cba_ref, cs)) * conv(wv_ref, cwv_ref, cbv_ref, cs)).astype(BF16)
            for cs in chunks]
    acc_ref[...] += _dot(jnp.concatenate(acts, axis=1), wd_ref[...])

    @pl.when(k == pl.num_programs(1) - 1)
    def _():
        r = _mod_row(row0)
        xn = x_ref[...] + _mod_vec(mod_ref, r, 5) * (_rms(acc_ref[...]) * g_res_ref[...])
        xo_ref[...] = xn
        if emit_h:
            ho_ref[...] = _norm_mod(xn, g_next_ref[...], modn_ref, r, 0, 1).astype(BF16)


def _ffn(h, xs, layer, w_up, conv_w, conv_b, w_down, g_res, mod_l, g_next, mod_next, *, n_rows, emit_h):
    tm, tk = ROW_TILE, 512
    kt = FFN_HIDDEN // tk
    h_prev, h_next = _halo_specs(tm, BF16_ROWS, D_MODEL, n_rows, lambda k: 0)
    row = pl.BlockSpec((tm, D_MODEL), lambda i, k: (i, 0))
    vec = pl.BlockSpec((1, D_MODEL), lambda i, k: (0, 0))
    tab = pl.BlockSpec(mod_l.shape, lambda i, k: (0, 0))
    out_specs = [row, row] if emit_h else [row]
    out_shape = [jax.ShapeDtypeStruct((n_rows, D_MODEL), F32)]
    if emit_h:
        out_shape.append(jax.ShapeDtypeStruct((n_rows, D_MODEL), BF16))
    conv_b3 = conv_b.reshape(DEPTH, 1, 2 * FFN_HIDDEN)

    res = pl.pallas_call(
        functools.partial(_ffn_body, emit_h, n_rows > T_LAT),
        grid=(n_rows // tm, kt),
        in_specs=[
            row, h_prev, h_next,
            pl.BlockSpec((None, D_MODEL, tk), lambda i, k: (layer, 0, k)),
            pl.BlockSpec((None, D_MODEL, tk), lambda i, k: (layer, 0, k + kt)),
            pl.BlockSpec((None, 3, tk), lambda i, k: (layer, 0, k)),
            pl.BlockSpec((None, 3, tk), lambda i, k: (layer, 0, k + kt)),
            pl.BlockSpec((None, 1, tk), lambda i, k: (layer, 0, k)),
            pl.BlockSpec((None, 1, tk), lambda i, k: (layer, 0, k + kt)),
            pl.BlockSpec((None, tk, D_MODEL), lambda i, k: (layer, k, 0)),
            row, vec, tab, vec, tab,
        ],
        out_specs=out_specs,
        out_shape=out_shape,
        scratch_shapes=[pltpu.VMEM((tm + 2 * BF16_ROWS, D_MODEL), BF16), pltpu.VMEM((tm, D_MODEL), F32)],
        compiler_params=_cparams(("arbitrary", "arbitrary"), 56),
        name="conv_ffn",
    )(h, h, h, w_up, w_up, conv_w, conv_w, conv_b3, conv_b3, w_down, xs, g_res.reshape(1, D_MODEL), mod_l,
      g_next.reshape(1, D_MODEL), mod_next)
    return (res[0], res[1]) if emit_h else (res[0], None)


def _in_proj_conv_body(tm, h_ref, hp_ref, hn_ref, w_ref, cw_ref, cb_ref, o_ref, wb_ref):
    @pl.when(pl.program_id(1) == 0)
    def _():
        wb_ref[...] = w_ref[...].astype(BF16)

    w = wb_ref[...]
    u = _dot(h_ref[...], w)
    uh = _dot(jnp.concatenate([hp_ref[...], hn_ref[...]], axis=0), w)
    conv = _token_conv(u, uh[BF16_ROWS - 1:BF16_ROWS], uh[BF16_ROWS:BF16_ROWS + 1], pl.program_id(1) * tm,
                       cw_ref, cb_ref)
    o_ref[...] = _silu(conv).astype(o_ref.dtype)


def _in_proj_conv(h, w_stack, slot, conv_w, conv_b, tn, first_tile, name):
    tm = T_ALL // 8
    n_dim = conv_w.shape[1]
    per = tm // BF16_ROWS
    last_blk = T_ALL // BF16_ROWS - 1
    return pl.pallas_call(
        functools.partial(_in_proj_conv_body, tm),
        grid=(n_dim // tn, T_ALL // tm),
        in_specs=[pl.BlockSpec((tm, D_MODEL), lambda j, i: (i, 0)),
                  pl.BlockSpec((BF16_ROWS, D_MODEL), lambda j, i: (jnp.maximum(i * per - 1, 0), 0)),
                  pl.BlockSpec((BF16_ROWS, D_MODEL), lambda j, i: (jnp.minimum((i + 1) * per, last_blk), 0)),
                  pl.BlockSpec((None, D_MODEL, tn), lambda j, i: (slot, 0, first_tile + j)),
                  pl.BlockSpec((3, tn), lambda j, i: (0, j)),
                  pl.BlockSpec((1, tn), lambda j, i: (0, j))],
        out_specs=pl.BlockSpec((tm, tn), lambda j, i: (i, j)),
        out_shape=jax.ShapeDtypeStruct((T_ALL, n_dim), BF16),
        scratch_shapes=[pltpu.VMEM((D_MODEL, tn), BF16)],
        compiler_params=_cparams(("arbitrary", "arbitrary"), 56),
        name=name,
    )(h, h, h, w_stack, conv_w, conv_b.reshape(1, -1))


def _scan_chunk_index(d, b, s, n_lat, n_ctx):
    ctx0 = BATCH * n_lat + b * n_ctx
    fwd = jnp.where(s < n_ctx, ctx0 + s, b * n_lat + s - n_ctx)
    bwd = jnp.where(s < n_ctx, ctx0 + n_ctx - 1 - s, b * n_lat + n_lat - 1 - (s - n_ctx))
    return jnp.where(d == 0, fwd, bwd)


SSD_CUM_TERMS = 3
SSD_DT_TERMS = 2


def _ssd_scan_consts():
    q = SSM_CHUNK
    i = np.arange(q)
    pfx = np.zeros((2, q + ONES_ROWS, q), np.float32)
    pfx[0, :q] = i[:, None] >= i[None, :]
    pfx[1, :q] = i[:, None] <= i[None, :]
    pfx[:, q:] = 1.0
    pfx = np.tile(pfx, (1, 1, SSD_CUM_TERMS))
    expand = np.repeat(np.eye(SSM_HEADS, dtype=np.float32), SSM_HEAD_DIM, axis=1)
    expand = np.tile(expand, (SSD_DT_TERMS, 1))
    return jnp.asarray(pfx, BF16), jnp.asarray(expand, BF16)


def _ssd_scan_body(x_ref, b_ref, c_ref, dtraw_ref, dtb_ref, alog_ref, dskip_ref, pfx_ref, exp_ref, y_ref, s_ref):
    q = SSM_CHUNK
    hp = 2 * SSM_HEAD_DIM
    d = pl.program_id(0)

    @pl.when(pl.program_id(2) == 0)
    def _():
        s_ref[...] = jnp.zeros_like(s_ref)

    def pick(v):
        return jnp.where(d == 0, v[:, :SSM_HEADS], v[:, SSM_HEADS:])

    dt2 = _softplus(dtraw_ref[...] + dtb_ref[...])
    cum2 = _dot_sel(pfx_ref[...], dt2 * (-LOG2E * jnp.exp(alog_ref[...])), SSD_CUM_TERMS)
    acum2 = cum2[:q]
    acum = pick(acum2)
    tot = pick(cum2[q:q + SUBLANES])[0:1]
    acum_t2 = acum2.T
    acum_t = jnp.where(d == 0, acum_t2[:SSM_HEADS], acum_t2[SSM_HEADS:])
    dt_wide = _dot_sel_rhs(pick(dt2), exp_ref[...], SSD_DT_TERMS)

    ii = lax.broadcasted_iota(jnp.int32, (q, q), 0)
    jj = lax.broadcasted_iota(jnp.int32, (q, q), 1)
    causal = (jj - ii) * (1 - 2 * d) <= 0
    low_half = lax.broadcasted_iota(jnp.int32, (q, hp), 1) < SSM_HEAD_DIM

    heads_per_group = SSM_HEADS // SSM_GROUPS
    n_pairs = SSM_HEADS // 2

    b_gs = [b_ref[:, g * SSM_STATE:(g + 1) * SSM_STATE].astype(BF16) for g in range(SSM_GROUPS)]
    c_bf = [c_ref[:, g * SSM_STATE:(g + 1) * SSM_STATE].astype(BF16) for g in range(SSM_GROUPS)]
    c_gs = [c.astype(F32) for c in c_bf]
    cbs = [jnp.where(causal, _dot_nt(c_bf[g], b_gs[g]), 0.0) for g in range(SSM_GROUPS)]

    cols, lhs = [], []
    for h in range(SSM_HEADS):
        g = h // heads_per_group
        col = jnp.broadcast_to(acum[:, h:h + 1], (q, q))
        seg = jnp.where(causal, col - acum_t[h:h + 1, :], 0.0)
        scores = jnp.exp2(seg) * cbs[g]
        c_dec = c_gs[g] * jnp.exp2(col)
        lhs.append(jnp.concatenate([scores.astype(BF16), c_dec.astype(BF16)], axis=1))
        cols.append(col)

    xdts = []
    for p in range(n_pairs):
        lo = p * hp
        x_pair = x_ref[:, lo:lo + hp].astype(F32)
        xdt = x_pair * dt_wide[:, lo:lo + hp]
        rhs = jnp.concatenate([xdt.astype(BF16), s_ref[:, lo:lo + hp].astype(BF16)], axis=0)
        y_pair = jnp.where(low_half, _dot(lhs[2 * p], rhs), _dot(lhs[2 * p + 1], rhs))
        y_ref[:, lo:lo + hp] = (y_pair + dskip_ref[:, lo:lo + hp] * x_pair).astype(y_ref.dtype)
        xdts.append(xdt)

    for p in range(n_pairs):
        lo = p * hp
        g = (2 * p) // heads_per_group
        tot_pair = jnp.where(low_half[0:1], jnp.broadcast_to(tot[:, 2 * p:2 * p + 1], (1, hp)),
                             jnp.broadcast_to(tot[:, 2 * p + 1:2 * p + 2], (1, hp)))
        to_end = jnp.exp2(tot_pair - jnp.where(low_half, cols[2 * p], cols[2 * p + 1]))
        s_ref[:, lo:lo + hp] = (s_ref[:, lo:lo + hp] * jnp.exp2(tot_pair)
                                + _dot_tn(b_gs[g], (xdts[p] * to_end).astype(BF16)))


def _ssd_scan(xbc, dt_raw, dt_bias, a_log, d_skip):
    q = SSM_CHUNK
    n_lat, n_ctx = SEQ // q, CTX_LEN // q
    pfx, expand = _ssd_scan_consts()
    n_heads2 = 2 * SSM_HEADS
    b_col = SSM_INNER // (SSM_GROUPS * SSM_STATE)
    d_wide = jnp.repeat(d_skip, SSM_HEAD_DIM, axis=1).reshape(2, 1, SSM_INNER)

    def chunk(d, b, s):
        return _scan_chunk_index(d, b, s, n_lat, n_ctx)

    return pl.pallas_call(
        _ssd_scan_body,
        grid=(2, BATCH, n_lat + n_ctx),
        in_specs=[
            pl.BlockSpec((q, SSM_INNER), lambda d, b, s: (chunk(d, b, s), 0)),
            pl.BlockSpec((q, SSM_GROUPS * SSM_STATE), lambda d, b, s: (chunk(d, b, s), b_col)),
            pl.BlockSpec((q, SSM_GROUPS * SSM_STATE), lambda d, b, s: (chunk(d, b, s), b_col + 1)),
            pl.BlockSpec((q, n_heads2), lambda d, b, s: (chunk(d, b, s), 0)),
            pl.BlockSpec((1, n_heads2), lambda d, b, s: (0, 0)),
            pl.BlockSpec((1, n_heads2), lambda d, b, s: (0, 0)),
            pl.BlockSpec((None, 1, SSM_INNER), lambda d, b, s: (d, 0, 0)),
            pl.BlockSpec((None,) + pfx.shape[1:], lambda d, b, s: (d, 0, 0)),
            pl.BlockSpec(expand.shape, lambda d, b, s: (0, 0)),
        ],
        out_specs=pl.BlockSpec((None, q, SSM_INNER), lambda d, b, s: (d, chunk(d, b, s), 0)),
        out_shape=jax.ShapeDtypeStruct((2, T_ALL, SSM_INNER), BF16),
        scratch_shapes=[pltpu.VMEM((SSM_STATE, SSM_INNER), F32)],
        compiler_params=_cparams(("arbitrary", "arbitrary", "arbitrary"), 48),
        name="ssd_scan",
    )(xbc, xbc, xbc, dt_raw, dt_bias.reshape(1, n_heads2), a_log.reshape(1, n_heads2), d_wide, pfx, expand)


GLA_SUM_TERMS = 2


def _gla_consts():
    q = HG_CHUNK
    levels = int(np.log2(q))
    i = np.arange(q)
    mats = [(i[:, None] >= i[None, :]).astype(np.float32), (i[None, :] > i[:, None]).astype(np.float32)]
    masks = [np.eye(q, dtype=np.float32)]
    for lv in range(levels):
        s = 1 << lv
        blk, pos = i // (2 * s), i % (2 * s)
        bound = blk * 2 * s + s - 1
        right = pos >= s
        t = i[None, :]
        m_right = right[:, None] & (t > bound[:, None]) & (t <= i[:, None])
        m_left = (~right)[:, None] & (t > i[:, None]) & (t <= bound[:, None])
        mats.append((m_right | m_left).astype(np.float32))
        masks.append((right[:, None] & (~right)[None, :] & (blk[:, None] == blk[None, :])).astype(np.float32))
    mats.append(np.ones((ONES_ROWS, q), np.float32))
    fwd_m, fwd_k = np.concatenate(mats, axis=0), np.stack(masks)
    flip_rows = np.concatenate([np.arange(k * q, (k + 1) * q)[::-1] for k in range(levels + 2)]
                               + [np.arange((levels + 2) * q, (levels + 2) * q + ONES_ROWS)])
    bwd_m = fwd_m[flip_rows][:, ::-1]
    bwd_k = fwd_k[:, ::-1, ::-1]
    mats2 = np.tile(np.stack([fwd_m, bwd_m]), (1, 1, GLA_SUM_TERMS))
    return (jnp.asarray(mats2, BF16), jnp.asarray(np.stack([fwd_k, bwd_k]), F32), levels)


def _silu_epilogue(u):
    return (_silu(u),)


def _hg_gate_epilogue(layer, u, lbp_ref):
    lbp = lbp_ref[...]
    e = jnp.exp(lbp - jnp.max(lbp, axis=0, keepdims=True))
    sm = e / jnp.sum(e, axis=0, keepdims=True)
    lb = jnp.sum(sm[1:layer + 1], axis=0, keepdims=True) if layer > 0 else jnp.zeros((1, lbp.shape[1]), F32)
    ef = jnp.exp(-jnp.abs(u))
    log_sig = jnp.minimum(u, 0.0) - _log1p_unit(ef)
    la = jnp.log(lb)
    lc = jnp.log1p(-lb) + log_sig
    logf = jnp.maximum(la, lc) + _log1p_unit(jnp.exp(-jnp.abs(la - lc)))
    key = (1.0 - lb) * (jnp.where(u >= 0.0, ef, 1.0) / (1.0 + ef))
    return logf, key


def _gla_body(levels, q_ref, v_ref, logf_ref, key_ref, mats_ref, masks_ref, o_ref, st_ref):
    q = HG_CHUNK

    @pl.when(pl.program_id(2) == 0)
    def _():
        st_ref[...] = jnp.zeros_like(st_ref)

    qs_bf, key_bf, v_bf = q_ref[...], key_ref[...], v_ref[...]

    sums = _dot_sel(mats_ref[...], logf_ref[...], GLA_SUM_TERMS)
    q_in = qs_bf * jnp.exp(sums[:q]).astype(BF16)
    k_out = key_bf * jnp.exp(sums[q:2 * q]).astype(BF16)
    dec = jnp.exp(sums[(levels + 2) * q:(levels + 2) * q + 1])
    q_lv, k_lv = [], []
    for lv in range(levels):
        fac = jnp.exp(sums[(lv + 2) * q:(lv + 3) * q]).astype(BF16)
        q_lv.append(qs_bf * fac)
        k_lv.append(key_bf * fac)

    heads = [slice(h * LANES, (h + 1) * LANES) for h in range(HG_HEADS)]
    attn = masks_ref[0][None] * jnp.stack([_dot_nt(qs_bf[:, sl], key_bf[:, sl]) for sl in heads], axis=0)
    for lv in range(levels):
        attn = attn + masks_ref[lv + 1][None] * jnp.stack(
            [_dot_nt(q_lv[lv][:, sl], k_lv[lv][:, sl]) for sl in heads], axis=0)
    attn = attn.astype(BF16)
    st_bf = st_ref[...].astype(BF16)
    o_ref[...] = jnp.concatenate(
        [_dot(attn[h], v_bf[:, sl]) + _dot_nt(q_in[:, sl], st_bf[h]) for h, sl in enumerate(heads)],
        axis=1).astype(o_ref.dtype)
    for h, sl in enumerate(heads):
        st_ref[h] = st_ref[h] * dec[:, sl] + _dot_tn(v_bf[:, sl], k_out[:, sl])


def _gla_scan(qs, v, logf, key):
    q = HG_CHUNK
    n_lat, n_ctx = SEQ // q, CTX_LEN // q
    mats, masks, levels = _gla_consts()

    def chunk(d, b, s):
        return _scan_chunk_index(d, b, s, n_lat, n_ctx)

    return pl.pallas_call(
        functools.partial(_gla_body, levels),
        grid=(2, BATCH, n_lat + n_ctx),
        in_specs=[
            pl.BlockSpec((q, D_MODEL), lambda d, b, s: (chunk(d, b, s), 0)),
            pl.BlockSpec((q, D_MODEL), lambda d, b, s: (chunk(d, b, s), 0)),
            pl.BlockSpec((q, D_MODEL), lambda d, b, s: (chunk(d, b, s), d)),
            pl.BlockSpec((q, D_MODEL), lambda d, b, s: (chunk(d, b, s), d)),
            pl.BlockSpec((None,) + mats.shape[1:], lambda d, b, s: (d, 0, 0)),
            pl.BlockSpec((None,) + masks.shape[1:], lambda d, b, s: (d, 0, 0, 0)),
        ],
        out_specs=pl.BlockSpec((None, q, D_MODEL), lambda d, b, s: (d, chunk(d, b, s), 0)),
        out_shape=jax.ShapeDtypeStruct((2, T_ALL, D_MODEL), BF16),
        scratch_shapes=[pltpu.VMEM((HG_HEADS, LANES, HG_KEY_DIM), F32)],
        compiler_params=_cparams(("arbitrary", "arbitrary", "arbitrary"), 48),
        name="gla_scan",
    )(qs, v, logf, key, mats, masks)


def _na_bias_body(rpb_ref, o_ref):
    h = pl.program_id(0)
    n_dc = 2 * NA_WIN_C - 1
    qc = lax.broadcasted_iota(jnp.int32, (GRID_W, LANES), 0)
    kc = lax.broadcasted_iota(jnp.int32, (GRID_W, LANES), 1) % GRID_W
    c0 = jnp.clip(qc - NA_WIN_C // 2, 0, GRID_W - NA_WIN_C)
    col_in = (kc >= c0) & (kc < c0 + NA_WIN_C)
    dc = jnp.clip(kc - qc + NA_WIN_C - 1, 0, n_dc - 1)
    low = lax.broadcasted_iota(jnp.int32, (GRID_W, LANES), 1) < GRID_W
    tiles = []
    for dr in range(2 * NA_WIN_R - 1):
        t = jnp.zeros((GRID_W, LANES), F32)
        for k in range(n_dc):
            t = jnp.where(dc == k, rpb_ref[(h * (2 * NA_WIN_R - 1) + dr) * n_dc + k], t)
        tiles.append(jnp.where(col_in, t, -jnp.inf))
    for dr in range(2 * NA_WIN_R - 2):
        o_ref[dr] = jnp.where(low, tiles[dr], tiles[dr + 1])


def _na_bias_table(rpb):
    n_dr = 2 * NA_WIN_R - 2
    return pl.pallas_call(
        _na_bias_body,
        grid=(NA_HEADS,),
        in_specs=[pl.BlockSpec(memory_space=pltpu.SMEM)],
        out_specs=pl.BlockSpec((None, n_dr, GRID_W, LANES), lambda h: (h, 0, 0, 0)),
        out_shape=jax.ShapeDtypeStruct((NA_HEADS, n_dr, GRID_W, LANES), F32),
        compiler_params=_cparams(("arbitrary",), 32),
        name="na_bias",
    )(rpb.reshape(-1))


def _rope_tables():
    quarter = NA_HEAD_DIM // 4
    inv = ROPE_BASE ** (-np.arange(quarter, dtype=np.float64) / quarter)
    t = np.arange(SEQ)
    row, col = t // GRID_W, t % GRID_W
    ang = np.concatenate([row[:, None] * inv[None], row[:, None] * inv[None],
                          col[:, None] * inv[None], col[:, None] * inv[None]], axis=1)
    sign = np.tile(np.concatenate([-np.ones(quarter), np.ones(quarter)]), 2)[None]
    return jnp.asarray(np.cos(ang), F32), jnp.asarray(np.sin(ang) * sign, F32)


def _rope(v, cos, sin_signed):
    quarter = NA_HEAD_DIM // 4
    lane = lax.broadcasted_iota(jnp.int32, v.shape, 1)
    first = (lane % (2 * quarter)) < quarter
    partner = jnp.where(first, pltpu.roll(v, NA_HEAD_DIM - quarter, axis=1), pltpu.roll(v, quarter, axis=1))
    return v * cos + partner * sin_signed


def _na_body(rows_per_step, q_ref, k_ref, v_ref, kc_ref, vc_ref, cosq_ref, sinq_ref, cos_ref, sin_ref, bias_ref,
             o_ref, kr_ref, vb_ref):
    rb = pl.program_id(2)
    n_rows = SEQ // GRID_W
    scale = NA_HEAD_DIM ** -0.5

    @pl.when(rb == 0)
    def _():
        kr_ref[...] = _rope(k_ref[...], cos_ref[...], sin_ref[...]).astype(BF16)
        vb_ref[...] = v_ref[...].astype(BF16)

    qr = (_rope(q_ref[...], cosq_ref[...], sinq_ref[...]) * scale).astype(BF16)
    kc = kc_ref[...].astype(BF16)
    vc = vc_ref[...].astype(BF16)
    starts, s_rows = [], []
    for lr in range(rows_per_step):
        r = rb * rows_per_step + lr
        r0 = jnp.clip(r - NA_WIN_R // 2, 0, n_rows - NA_WIN_R)
        start = pl.multiple_of(r0 * GRID_W, GRID_W)
        dr0 = r0 - r + NA_WIN_R - 1
        bias = jnp.concatenate([bias_ref[dr0 + 2 * p] for p in range(NA_WIN_R // 2)], axis=1)
        k_win = kr_ref[pl.ds(start, NA_WIN_R * GRID_W), :]
        s_rows.append(_dot_nt(qr[lr * GRID_W:(lr + 1) * GRID_W], k_win) + bias)
        starts.append(start)
    s_loc = jnp.concatenate(s_rows, axis=0)
    s_ctx = _dot_nt(qr, kc)
    m = jnp.maximum(jnp.max(s_loc, axis=-1, keepdims=True), jnp.max(s_ctx, axis=-1, keepdims=True))
    p_loc = jnp.exp(s_loc - m)
    p_ctx = jnp.exp(s_ctx - m)
    denom = jnp.sum(p_loc, axis=-1, keepdims=True) + jnp.sum(p_ctx, axis=-1, keepdims=True)
    p_loc = p_loc.astype(BF16)
    o_loc = jnp.concatenate(
        [_dot(p_loc[lr * GRID_W:(lr + 1) * GRID_W], vb_ref[pl.ds(starts[lr], NA_WIN_R * GRID_W), :])
         for lr in range(rows_per_step)], axis=0)
    o = o_loc + _dot(p_ctx.astype(BF16), vc)
    o_ref[...] = (o / denom).astype(o_ref.dtype)


def _na_attention(qkv, bias_tbl):
    rows_per_step = 64
    tq = rows_per_step * GRID_W
    steps = SEQ // tq
    cos, sin = _rope_tables()
    hd = NA_HEAD_DIM
    ctx_blk0 = T_LAT // CTX_LEN
    return pl.pallas_call(
        functools.partial(_na_body, rows_per_step),
        grid=(BATCH, NA_HEADS, steps),
        in_specs=[
            pl.BlockSpec((tq, hd), lambda b, h, r: (b * steps + r, h)),
            pl.BlockSpec((SEQ, hd), lambda b, h, r: (b, NA_HEADS + h)),
            pl.BlockSpec((SEQ, hd), lambda b, h, r: (b, 2 * NA_HEADS + h)),
            pl.BlockSpec((CTX_LEN, hd), lambda b, h, r: (ctx_blk0 + b, NA_HEADS + h)),
            pl.BlockSpec((CTX_LEN, hd), lambda b, h, r: (ctx_blk0 + b, 2 * NA_HEADS + h)),
            pl.BlockSpec((tq, hd), lambda b, h, r: (r, 0)),
            pl.BlockSpec((tq, hd), lambda b, h, r: (r, 0)),
            pl.BlockSpec((SEQ, hd), lambda b, h, r: (0, 0)),
            pl.BlockSpec((SEQ, hd), lambda b, h, r: (0, 0)),
            pl.BlockSpec((None,) + bias_tbl.shape[1:], lambda b, h, r: (h, 0, 0, 0)),
        ],
        out_specs=pl.BlockSpec((tq, hd), lambda b, h, r: (b * steps + r, h)),
        out_shape=jax.ShapeDtypeStruct((T_LAT, D_MODEL), BF16),
        scratch_shapes=[pltpu.VMEM((SEQ, hd), BF16), pltpu.VMEM((SEQ, hd), BF16)],
        compiler_params=_cparams(("arbitrary", "arbitrary", "arbitrary"), 48),
        name="na_attention",
    )(qkv, qkv, qkv, qkv, qkv, cos, sin, cos, sin, bias_tbl)


def _ctx_attn_body(q_ref, k_ref, v_ref, o_ref):
    scale = NA_HEAD_DIM ** -0.5
    s = _dot_nt((q_ref[...] * scale).astype(BF16), k_ref[...].astype(BF16))
    p = jnp.exp(s - jnp.max(s, axis=-1, keepdims=True))
    o = _dot(p.astype(BF16), v_ref[...].astype(BF16))
    o_ref[...] = (o / jnp.sum(p, axis=-1, keepdims=True)).astype(o_ref.dtype)


def _ctx_attention(qkv):
    hd = NA_HEAD_DIM
    ctx_blk0 = T_LAT // CTX_LEN
    return pl.pallas_call(
        _ctx_attn_body,
        grid=(BATCH, NA_HEADS),
        in_specs=[
            pl.BlockSpec((CTX_LEN, hd), lambda b, h: (ctx_blk0 + b, h)),
            pl.BlockSpec((CTX_LEN, hd), lambda b, h: (ctx_blk0 + b, NA_HEADS + h)),
            pl.BlockSpec((CTX_LEN, hd), lambda b, h: (ctx_blk0 + b, 2 * NA_HEADS + h)),
        ],
        out_specs=pl.BlockSpec((CTX_LEN, hd), lambda b, h: (b, h)),
        out_shape=jax.ShapeDtypeStruct((T_CTX, D_MODEL), BF16),
        compiler_params=_cparams(("arbitrary", "arbitrary"), 32),
        name="ctx_attention",
    )(qkv, qkv, qkv)


def kernel(x, c, ctx, c_ctx, w_mod, b_mod, norm_g, ffn_w_up, ffn_conv_w, ffn_conv_b, ffn_w_down, ssm_w_in, ssm_conv_w, ssm_conv_b, ssm_dt_bias, ssm_a_log, ssm_d, ssm_norm_g, ssm_w_out, hg_w_in, hg_lb, hg_norm_g, hg_w_out, na_w_qkv, na_rpb, na_w_out):
    cvec8 = jnp.concatenate([c, c_ctx[None], jnp.zeros((SUBLANES - BATCH - 1, D_MODEL), F32)], axis=0)
    mod = _adaln(cvec8, w_mod, b_mod)
    ffn_up_bf, ffn_down_bf = ffn_w_up.astype(BF16), ffn_w_down.astype(BF16)
    ssm_out_bf, hg_out_bf, na_out_bf = ssm_w_out.astype(BF16), hg_w_out.astype(BF16), na_w_out.astype(BF16)

    xs, h = _prep(x.reshape(T_LAT, D_MODEL), ctx.reshape(T_CTX, D_MODEL), norm_g[0, 0], mod[0])
    for i in range(DEPTH):
        last = i == DEPTH - 1
        kind, slot = i % N_MIXERS, i // N_MIXERS
        mod_l = mod[i]
        n_rows = T_LAT if last else T_ALL
        op = dict(xs=xs, g_res=norm_g[i, 1], g_ffn=norm_g[i, 2], mod_l=mod_l, n_rows=n_rows, slot=slot)
        if kind == 0:
            tn = 1024
            z = _in_proj(h, ssm_w_in, slot, tn, "ssd_in_z", n_dim=SSM_INNER)
            xbc = _in_proj_conv(h, ssm_w_in, slot, ssm_conv_w[slot], ssm_conv_b[slot], tn, SSM_INNER // tn,
                                "ssd_in_xbc")
            dt_raw = _in_proj(h, ssm_w_in, slot, 2 * SSM_HEADS, "ssd_dt", n_dim=2 * SSM_HEADS,
                              first_tile=(SSM_INNER + SSM_CONV_DIM) // (2 * SSM_HEADS))
            y2 = _ssd_scan(xbc, dt_raw, ssm_dt_bias[slot], ssm_a_log[slot], ssm_d[slot])
            xs, hf = _out_proj(_ssd_gate_prologue, [((y2, 0), 0), ((y2, 1), 0), (z, 0)],
                               ssm_norm_g[slot].reshape(1, SSM_INNER), ssm_out_bf, tm=256, name="ssd_out", **op)
        elif kind == 1:
            tn = 1024
            per = D_MODEL // tn
            qs = _in_proj(h, hg_w_in, slot, tn, "hg_in_q", n_dim=D_MODEL, epilogue=_silu_epilogue,
                          out_dtypes=(BF16,))
            v = _in_proj(h, hg_w_in, slot, tn, "hg_in_v", n_dim=D_MODEL, first_tile=per, out_dtypes=(BF16,))
            lb_spec = pl.BlockSpec((None, DEPTH, tn), lambda j, r: (j // per, 0, j % per))
            logf, key = _in_proj(h, hg_w_in, slot, tn, "hg_in_f", n_dim=2 * D_MODEL, first_tile=2 * per,
                                 epilogue=functools.partial(_hg_gate_epilogue, i), aux=[(hg_lb, lb_spec)],
                                 out_dtypes=(F32, BF16))
            gate = _in_proj(h, hg_w_in, slot, tn, "hg_in_g", n_dim=D_MODEL, first_tile=4 * per)
            o2 = _gla_scan(qs, v, logf, key)
            xs, hf = _out_proj(_hg_readout_prologue, [((o2, 0), 0), ((o2, 1), 0), (gate, 0)],
                               hg_norm_g[slot].reshape(1, D_MODEL), hg_out_bf, tm=ROW_TILE, name="hg_out", **op)
        else:
            qkv = _in_proj(h, na_w_qkv, slot, 1024, "na_in")
            o_lat = _na_attention(qkv, _na_bias_table(na_rpb[slot]))
            o_ctx = _ctx_attention(qkv)
            assert T_CTX == ROW_TILE
            last_lat = T_LAT // ROW_TILE - 1
            xs, hf = _out_proj(_na_prologue, [(o_lat, 0, functools.partial(jnp.minimum, last_lat)),
                                              (o_ctx, 0, lambda t: 0)],
                               None, na_out_bf, tm=ROW_TILE, name="na_out", **op)
        nxt = min(i + 1, DEPTH - 1)
        xs, h = _ffn(hf, xs, i, ffn_up_bf, ffn_conv_w, ffn_conv_b, ffn_down_bf, norm_g[i, 3], mod_l,
                     norm_g[nxt, 0], mod[nxt], n_rows=n_rows, emit_h=not last)

    return xs.reshape(BATCH, SEQ, D_MODEL)
```

```python
import functools

import numpy as np
import jax
import jax.numpy as jnp
from jax import lax
from jax.experimental import pallas as pl
from jax.experimental.pallas import tpu as pltpu

F32 = jnp.float32
BF16 = jnp.bfloat16

D_MODEL = 2048
BATCH = 2
SEQ = 4096
DEPTH = 4
GRID_W = 64
CTX_LEN = 256
N_MIXERS = 3
RMS_EPS = 1e-6
SSM_INNER = 4096
SSM_HEAD_DIM = 64
SSM_HEADS = 64
SSM_STATE = 128
SSM_GROUPS = 8
SSM_CONV_DIM = SSM_INNER + 2 * SSM_GROUPS * SSM_STATE
SSM_IN_DIM = SSM_INNER + SSM_CONV_DIM + 2 * SSM_HEADS
SSM_CHUNK = 128
HG_HEADS = 16
HG_KEY_DIM = 128
HG_IN_DIM = 5 * D_MODEL
HG_CHUNK = 64
NA_HEADS = 16
NA_HEAD_DIM = 128
NA_WIN_R = 8
NA_WIN_C = 16
ROPE_BASE = 10000.0
FFN_HIDDEN = 5632

T_LAT = BATCH * SEQ
T_CTX = BATCH * CTX_LEN
T_ALL = T_LAT + T_CTX
LANES = 128
SUBLANES = 8
BF16_ROWS = 16
ONES_ROWS = 16
ROW_TILE = 512
FFN_COL_CHUNK = 512
MIB = 1024 * 1024
LOG2E = 1.4426950408889634


def _cparams(sem, vmem_mib):
    return pltpu.CompilerParams(dimension_semantics=sem, vmem_limit_bytes=vmem_mib * MIB)


def _silu(v):
    return v * jax.nn.sigmoid(v)


def _rms(v):
    return v * lax.rsqrt(jnp.mean(v * v, axis=-1, keepdims=True) + RMS_EPS)


def _log1p_unit(e):
    return jnp.log(1.0 + e)


def _softplus(v):
    return jnp.maximum(v, 0.0) + _log1p_unit(jnp.exp(-jnp.abs(v)))


def _dot(a, b):
    return jnp.dot(a, b, preferred_element_type=F32)


def _dot_nt(a, b):
    return lax.dot_general(a, b, (((1,), (1,)), ((), ())), preferred_element_type=F32)


def _dot_tn(a, b):
    return lax.dot_general(a, b, (((0,), (0,)), ((), ())), preferred_element_type=F32)


def _bf16_pieces(v, terms):
    pieces = []
    rem = v
    for _ in range(terms):
        part = rem.astype(BF16)
        rem = rem - part.astype(F32)
        pieces.append(part)
    return pieces


def _dot_sel(m01_rep, v, terms):
    return _dot(m01_rep, jnp.concatenate(_bf16_pieces(v, terms), axis=0))


def _dot_sel_rhs(v, m01_rep, terms):
    return _dot(jnp.concatenate(_bf16_pieces(v, terms), axis=1), m01_rep)


def _mod_row(row0):
    return jnp.where(row0 >= T_LAT, 2, row0 // SEQ)


def _mod_vec(mod_ref, r, idx):
    return mod_ref[pl.ds(r, 1), idx * D_MODEL:(idx + 1) * D_MODEL]


def _norm_mod(x, g, mod_ref, r, shift_idx, scale_idx):
    return _rms(x) * (g * (1.0 + _mod_vec(mod_ref, r, scale_idx))) + _mod_vec(mod_ref, r, shift_idx)


def _seg_edges(row0, rows):
    gr = row0 + lax.broadcasted_iota(jnp.int32, (rows, 1), 0)
    pos = jnp.where(gr >= T_LAT, (gr - T_LAT) % CTX_LEN, gr % SEQ)
    length = jnp.where(gr >= T_LAT, CTX_LEN, SEQ)
    return pos == 0, pos == length - 1


def _shift_rows(cur, prev_row, next_row, row0):
    rows = cur.shape[0]
    first, last = _seg_edges(row0, rows)
    sub = lax.broadcasted_iota(jnp.int32, (SUBLANES, 1), 0)
    up = pltpu.roll(cur, 1, axis=0)
    up = jnp.concatenate([jnp.where(sub == 0, prev_row, up[:SUBLANES]), up[SUBLANES:]], axis=0)
    up = jnp.where(first, 0.0, up)
    dn = pltpu.roll(cur, rows - 1, axis=0)
    dn = jnp.concatenate([dn[:rows - SUBLANES], jnp.where(sub == SUBLANES - 1, next_row, dn[rows - SUBLANES:])],
                         axis=0)
    dn = jnp.where(last, 0.0, dn)
    return up, dn


def _token_conv(cur, prev_row, next_row, row0, w_ref, b_ref):
    up, dn = _shift_rows(cur, prev_row, next_row, row0)
    return b_ref[...] + up * w_ref[0:1, :] + cur * w_ref[1:2, :] + dn * w_ref[2:3, :]


def _halo_specs(tm, halo, width, n_rows, col_fn):
    per = tm // halo
    last_blk = n_rows // halo - 1
    prev = pl.BlockSpec((halo, width), lambda i, j: (jnp.maximum(i * per - 1, 0), col_fn(j)))
    nxt = pl.BlockSpec((halo, width), lambda i, j: (jnp.minimum((i + 1) * per, last_blk), col_fn(j)))
    return prev, nxt


def _adaln_body(c_ref, w_ref, b_ref, o_ref):
    s = _silu(c_ref[...]).astype(BF16)
    o_ref[...] = _dot(s, w_ref[...].astype(BF16)) + b_ref[...]


def _adaln(cvec8, w_mod, b_mod):
    tn = 1024
    n_out = 6 * D_MODEL
    return pl.pallas_call(
        _adaln_body,
        grid=(DEPTH, n_out // tn),
        in_specs=[
            pl.BlockSpec((SUBLANES, D_MODEL), lambda l, j: (0, 0)),
            pl.BlockSpec((None, D_MODEL, tn), lambda l, j: (l, 0, j)),
            pl.BlockSpec((None, 1, tn), lambda l, j: (l, 0, j)),
        ],
        out_specs=pl.BlockSpec((None, SUBLANES, tn), lambda l, j: (l, 0, j)),
        out_shape=jax.ShapeDtypeStruct((DEPTH, SUBLANES, n_out), F32),
        compiler_params=_cparams(("arbitrary", "arbitrary"), 40),
        name="adaln",
    )(cvec8, w_mod, b_mod.reshape(DEPTH, 1, n_out))


def _prep_body(lat_ref, ctx_ref, g_ref, mod_ref, xs_ref, h_ref):
    row0 = pl.program_id(0) * ROW_TILE

    def emit(src_ref):
        xv = src_ref[...]
        xs_ref[...] = xv
        h_ref[...] = _norm_mod(xv, g_ref[...], mod_ref, _mod_row(row0), 0, 1).astype(BF16)

    @pl.when(row0 < T_LAT)
    def _():
        emit(lat_ref)

    @pl.when(row0 >= T_LAT)
    def _():
        emit(ctx_ref)


def _prep(x_lat, x_ctx, g_row, mod_l):
    assert T_CTX == ROW_TILE
    n_lat = T_LAT // ROW_TILE
    row = pl.BlockSpec((ROW_TILE, D_MODEL), lambda i: (i, 0))
    return pl.pallas_call(
        _prep_body,
        grid=(T_ALL // ROW_TILE,),
        in_specs=[pl.BlockSpec((ROW_TILE, D_MODEL), lambda i: (jnp.minimum(i, n_lat - 1), 0)),
                  pl.BlockSpec((ROW_TILE, D_MODEL), lambda i: (0, 0)),
                  pl.BlockSpec((1, D_MODEL), lambda i: (0, 0)), pl.BlockSpec(mod_l.shape, lambda i: (0, 0))],
        out_specs=[row, row],
        out_shape=[jax.ShapeDtypeStruct((T_ALL, D_MODEL), F32), jax.ShapeDtypeStruct((T_ALL, D_MODEL), BF16)],
        compiler_params=_cparams(("arbitrary",), 40),
        name="prep",
    )(x_lat, x_ctx, g_row.reshape(1, D_MODEL), mod_l)


def _in_proj_body(epilogue, n_aux, h_ref, w_ref, *rest):
    aux, outs, wb_ref = rest[:n_aux], rest[n_aux:-1], rest[-1]

    @pl.when(pl.program_id(1) == 0)
    def _():
        wb_ref[...] = w_ref[...].astype(BF16)

    u = _dot(h_ref[...], wb_ref[...])
    vals = (u,) if epilogue is None else epilogue(u, *aux)
    for o_ref, val in zip(outs, vals):
        o_ref[...] = val.astype(o_ref.dtype)


def _in_proj(h, w_stack, slot, tn, name, n_dim=None, first_tile=0, epilogue=None, aux=(), out_dtypes=(F32,)):
    tm = T_ALL // 8
    n_dim = w_stack.shape[2] if n_dim is None else n_dim
    res = pl.pallas_call(
        functools.partial(_in_proj_body, epilogue, len(aux)),
        grid=(n_dim // tn, T_ALL // tm),
        in_specs=[pl.BlockSpec((tm, D_MODEL), lambda j, i: (i, 0)),
                  pl.BlockSpec((None, D_MODEL, tn), lambda j, i: (slot, 0, first_tile + j))]
        + [spec for _, spec in aux],
        out_specs=[pl.BlockSpec((tm, tn), lambda j, i: (i, j)) for _ in out_dtypes],
        out_shape=[jax.ShapeDtypeStruct((T_ALL, n_dim), dt) for dt in out_dtypes],
        scratch_shapes=[pltpu.VMEM((D_MODEL, tn), BF16)],
        compiler_params=_cparams(("arbitrary", "arbitrary"), 56),
        name=name,
    )(h, w_stack, *[a for a, _ in aux])
    return res[0] if len(out_dtypes) == 1 else tuple(res)


def _ssd_gate_prologue(row0, yf_ref, yb_ref, z_ref, g_ref):
    v = (yf_ref[...].astype(F32) + yb_ref[...].astype(F32)) * _silu(z_ref[...])
    return v * g_ref[...], lax.rsqrt(jnp.mean(v * v, axis=-1, keepdims=True) + RMS_EPS)


def _hg_readout_prologue(row0, of_ref, ob_ref, gate_ref, g_ref):
    o = of_ref[...].astype(F32) + ob_ref[...].astype(F32)
    parts = [_rms(o[:, h * LANES:(h + 1) * LANES]) for h in range(o.shape[1] // LANES)]
    return jnp.concatenate(parts, axis=-1) * g_ref[...] * _silu(gate_ref[...]), None


def _na_prologue(row0, o_lat_ref, o_ctx_ref):
    return jnp.where(row0 >= T_LAT, o_ctx_ref[...], o_lat_ref[...]), None


def _out_proj_body(prologue, n_in, tm, *refs):
    ins = refs[:n_in]
    w_ref, x_ref, g1_ref, g2_ref, mod_ref, xo_ref, ho_ref = refs[n_in:]
    part, row_scale = prologue(pl.program_id(0) * tm, *ins)
    y = _dot(part.astype(BF16), w_ref[...])
    if row_scale is not None:
        y = y * row_scale
    r = _mod_row(pl.program_id(0) * tm)
    xn = x_ref[...] + _rms(y) * (_mod_vec(mod_ref, r, 2) * g1_ref[...])
    xo_ref[...] = xn
    ho_ref[...] = _norm_mod(xn, g2_ref[...], mod_ref, r, 3, 4).astype(BF16)


def _out_proj(prologue, row_ins, gain, w_stack, slot, xs, g_res, g_ffn, mod_l, *, n_rows, tm, name):
    k_dim = w_stack.shape[1]
    in_specs, args = [], []
    for arr, cb, *row_fn in row_ins:
        row_fn = row_fn[0] if row_fn else (lambda i: i)
        if isinstance(arr, tuple):
            in_specs.append(pl.BlockSpec((None, tm, k_dim), functools.partial(
                lambda i, ld, cb, row_fn: (ld, row_fn(i), cb), ld=arr[1], cb=cb, row_fn=row_fn)))
            args.append(arr[0])
        else:
            in_specs.append(pl.BlockSpec((tm, k_dim), functools.partial(
                lambda i, cb, row_fn: (row_fn(i), cb), cb=cb, row_fn=row_fn)))
            args.append(arr)
    if gain is not None:
        in_specs.append(pl.BlockSpec((1, k_dim), lambda i: (0, 0)))
        args.append(gain)
    n_in = len(args)
    row = pl.BlockSpec((tm, D_MODEL), lambda i: (i, 0))
    vec = pl.BlockSpec((1, D_MODEL), lambda i: (0, 0))
    in_specs += [pl.BlockSpec((None, k_dim, D_MODEL), lambda i: (slot, 0, 0), pipeline_mode=pl.Buffered(1)),
                 row, vec, vec, pl.BlockSpec(mod_l.shape, lambda i: (0, 0))]
    return pl.pallas_call(
        functools.partial(_out_proj_body, prologue, n_in, tm),
        grid=(n_rows // tm,),
        in_specs=in_specs,
        out_specs=[row, row],
        out_shape=[jax.ShapeDtypeStruct((n_rows, D_MODEL), F32), jax.ShapeDtypeStruct((n_rows, D_MODEL), BF16)],
        compiler_params=_cparams(("arbitrary",), 60),
        name=name,
    )(*args, w_stack, xs, g_res.reshape(1, D_MODEL), g_ffn.reshape(1, D_MODEL), mod_l)


def _ffn_body(emit_h, has_ctx, h_ref, hp_ref, hn_ref, wa_ref, wv_ref, cwa_ref, cwv_ref, cba_ref, cbv_ref, wd_ref,
              x_ref, g_res_ref, mod_ref, g_next_ref, modn_ref, *out_and_scratch):
    if emit_h:
        xo_ref, ho_ref, hall_ref, acc_ref = out_and_scratch
    else:
        xo_ref, hall_ref, acc_ref = out_and_scratch
    k = pl.program_id(1)
    tm = ROW_TILE
    row0 = pl.program_id(0) * tm
    pad = BF16_ROWS
    rows = tm + 2 * pad
    s8 = SUBLANES

    @pl.when(k == 0)
    def _():
        hall_ref[pl.ds(0, pad), :] = hp_ref[...]
        hall_ref[pl.ds(pad, tm), :] = h_ref[...]
        hall_ref[pl.ds(pad + tm, pad), :] = hn_ref[...]
        acc_ref[...] = jnp.zeros_like(acc_ref)

    hall = hall_ref[...]
    edge_first = row0 % SEQ == 0
    edge_last = ((row0 + tm) % SEQ == 0) | (row0 + tm == T_ALL)
    is_ctx = row0 >= T_LAT
    joints = list(range(CTX_LEN, tm, CTX_LEN)) if has_ctx else []
    sub = lax.broadcasted_iota(jnp.int32, (s8, 1), 0)

    def patch(v, start, cond):
        return jnp.concatenate([v[:start], jnp.where(cond, 0.0, v[start:start + s8]), v[start + s8:]], axis=0)

    def conv(w_ref, cw_ref, cb_ref, cs):
        u = _dot(hall, w_ref[:, cs])
        u = patch(u, pad - s8, edge_first)
        u = patch(u, pad + tm, edge_last)
        up = pltpu.roll(u, 1, axis=0)[pad:pad + tm]
        dn = pltpu.roll(u, rows - 1, axis=0)[pad:pad + tm]
        for j in joints:
            up = patch(up, j, is_ctx & (sub == 0))
            dn = patch(dn, j - s8, is_ctx & (sub == s8 - 1))
        return cb_ref[:, cs] + up * cw_ref[0:1, cs] + u[pad:pad + tm] * cw_ref[1:2, cs] + dn * cw_ref[2:3, cs]

    chunks = [pl.ds(c * FFN_COL_CHUNK, FFN_COL_CHUNK) for c in range(wa_ref.shape[1] // FFN_COL_CHUNK)]
    acts = [(_silu(conv(wa_ref, cwa_ref, cba_ref, cs)) * conv(wv_ref, cwv_ref, cbv_ref, cs)).astype(BF16)
            for cs in chunks]
    acc_ref[...] += _dot(jnp.concatenate(acts, axis=1), wd_ref[...])

    @pl.when(k == pl.num_programs(1) - 1)
    def _():
        r = _mod_row(row0)
        xn = x_ref[...] + _rms(acc_ref[...]) * (_mod_vec(mod_ref, r, 5) * g_res_ref[...])
        xo_ref[...] = xn
        if emit_h:
            ho_ref[...] = _norm_mod(xn, g_next_ref[...], modn_ref, r, 0, 1).astype(BF16)


def _ffn(h, xs, layer, w_up, conv_w, conv_b, w_down, g_res, mod_l, g_next, mod_next, *, n_rows, emit_h):
    tm, tk = ROW_TILE, 512
    kt = FFN_HIDDEN // tk
    h_prev, h_next = _halo_specs(tm, BF16_ROWS, D_MODEL, n_rows, lambda k: 0)
    row = pl.BlockSpec((tm, D_MODEL), lambda i, k: (i, 0))
    vec = pl.BlockSpec((1, D_MODEL), lambda i, k: (0, 0))
    tab = pl.BlockSpec(mod_l.shape, lambda i, k: (0, 0))
    out_specs = [row, row] if emit_h else [row]
    out_shape = [jax.ShapeDtypeStruct((n_rows, D_MODEL), F32)]
    if emit_h:
        out_shape.append(jax.ShapeDtypeStruct((n_rows, D_MODEL), BF16))
    conv_b3 = conv_b.reshape(DEPTH, 1, 2 * FFN_HIDDEN)

    res = pl.pallas_call(
        functools.partial(_ffn_body, emit_h, n_rows > T_LAT),
        grid=(n_rows // tm, kt),
        in_specs=[
            row, h_prev, h_next,
            pl.BlockSpec((None, D_MODEL, tk), lambda i, k: (layer, 0, k)),
            pl.BlockSpec((None, D_MODEL, tk), lambda i, k: (layer, 0, k + kt)),
            pl.BlockSpec((None, 3, tk), lambda i, k: (layer, 0, k)),
            pl.BlockSpec((None, 3, tk), lambda i, k: (layer, 0, k + kt)),
            pl.BlockSpec((None, 1, tk), lambda i, k: (layer, 0, k)),
            pl.BlockSpec((None, 1, tk), lambda i, k: (layer, 0, k + kt)),
            pl.BlockSpec((None, tk, D_MODEL), lambda i, k: (layer, k, 0)),
            row, vec, tab, vec, tab,
        ],
        out_specs=out_specs,
        out_shape=out_shape,
        scratch_shapes=[pltpu.VMEM((tm + 2 * BF16_ROWS, D_MODEL), BF16), pltpu.VMEM((tm, D_MODEL), F32)],
        compiler_params=_cparams(("arbitrary", "arbitrary"), 56),
        name="conv_ffn",
    )(h, h, h, w_up, w_up, conv_w, conv_w, conv_b3, conv_b3, w_down, xs, g_res.reshape(1, D_MODEL), mod_l,
      g_next.reshape(1, D_MODEL), mod_next)
    return (res[0], res[1]) if emit_h else (res[0], None)


def _in_proj_conv_body(tm, h_ref, hp_ref, hn_ref, w_ref, cw_ref, cb_ref, o_ref, wb_ref):
    @pl.when(pl.program_id(1) == 0)
    def _():
        wb_ref[...] = w_ref[...].astype(BF16)

    w = wb_ref[...]
    u = _dot(h_ref[...], w)
    uh = _dot(jnp.concatenate([hp_ref[...], hn_ref[...]], axis=0), w)
    conv = _token_conv(u, uh[BF16_ROWS - 1:BF16_ROWS], uh[BF16_ROWS:BF16_ROWS + 1], pl.program_id(1) * tm,
                       cw_ref, cb_ref)
    o_ref[...] = _silu(conv).astype(o_ref.dtype)


def _in_proj_conv(h, w_stack, slot, conv_w, conv_b, tn, first_tile, name):
    tm = T_ALL // 8
    n_dim = conv_w.shape[1]
    per = tm // BF16_ROWS
    last_blk = T_ALL // BF16_ROWS - 1
    return pl.pallas_call(
        functools.partial(_in_proj_conv_body, tm),
        grid=(n_dim // tn, T_ALL // tm),
        in_specs=[pl.BlockSpec((tm, D_MODEL), lambda j, i: (i, 0)),
                  pl.BlockSpec((BF16_ROWS, D_MODEL), lambda j, i: (jnp.maximum(i * per - 1, 0), 0)),
                  pl.BlockSpec((BF16_ROWS, D_MODEL), lambda j, i: (jnp.minimum((i + 1) * per, last_blk), 0)),
                  pl.BlockSpec((None, D_MODEL, tn), lambda j, i: (slot, 0, first_tile + j)),
                  pl.BlockSpec((3, tn), lambda j, i: (0, j)),
                  pl.BlockSpec((1, tn), lambda j, i: (0, j))],
        out_specs=pl.BlockSpec((tm, tn), lambda j, i: (i, j)),
        out_shape=jax.ShapeDtypeStruct((T_ALL, n_dim), BF16),
        scratch_shapes=[pltpu.VMEM((D_MODEL, tn), BF16)],
        compiler_params=_cparams(("arbitrary", "arbitrary"), 56),
        name=name,
    )(h, h, h, w_stack, conv_w, conv_b.reshape(1, -1))


def _scan_chunk_index(d, b, s, n_lat, n_ctx):
    ctx0 = BATCH * n_lat + b * n_ctx
    fwd = jnp.where(s < n_ctx, ctx0 + s, b * n_lat + s - n_ctx)
    bwd = jnp.where(s < n_ctx, ctx0 + n_ctx - 1 - s, b * n_lat + n_lat - 1 - (s - n_ctx))
    return jnp.where(d == 0, fwd, bwd)


SSD_CUM_TERMS = 3
SSD_DT_TERMS = 2


def _ssd_scan_consts():
    q = SSM_CHUNK
    i = np.arange(q)
    pfx = np.zeros((2, q + ONES_ROWS, q), np.float32)
    pfx[0, :q] = i[:, None] >= i[None, :]
    pfx[1, :q] = i[:, None] <= i[None, :]
    pfx[:, q:] = 1.0
    pfx = np.tile(pfx, (1, 1, SSD_CUM_TERMS))
    expand = np.repeat(np.eye(SSM_HEADS, dtype=np.float32), SSM_HEAD_DIM, axis=1)
    expand = np.tile(expand, (SSD_DT_TERMS, 1))
    return jnp.asarray(pfx, BF16), jnp.asarray(expand, BF16)


def _ssd_scan_body(x_ref, b_ref, c_ref, dtraw_ref, dtb_ref, alog_ref, dskip_ref, pfx_ref, exp_ref, y_ref, s_ref):
    q = SSM_CHUNK
    hp = 2 * SSM_HEAD_DIM
    d = pl.program_id(0)

    @pl.when(pl.program_id(2) == 0)
    def _():
        s_ref[...] = jnp.zeros_like(s_ref)

    def pick(v):
        return jnp.where(d == 0, v[:, :SSM_HEADS], v[:, SSM_HEADS:])

    dt2 = _softplus(dtraw_ref[...] + dtb_ref[...])
    cum2 = _dot_sel(pfx_ref[...], dt2 * (-LOG2E * jnp.exp(alog_ref[...])), SSD_CUM_TERMS)
    acum2 = cum2[:q]
    acum = pick(acum2)
    tot = pick(cum2[q:q + SUBLANES])[0:1]
    acum_t2 = acum2.T
    acum_t = jnp.where(d == 0, acum_t2[:SSM_HEADS], acum_t2[SSM_HEADS:])
    dt_wide = _dot_sel_rhs(pick(dt2), exp_ref[...], SSD_DT_TERMS)

    ii = lax.broadcasted_iota(jnp.int32, (q, q), 0)
    jj = lax.broadcasted_iota(jnp.int32, (q, q), 1)
    causal = (jj - ii) * (1 - 2 * d) <= 0
    low_half = lax.broadcasted_iota(jnp.int32, (q, hp), 1) < SSM_HEAD_DIM

    heads_per_group = SSM_HEADS // SSM_GROUPS
    n_pairs = SSM_HEADS // 2

    b_gs = [b_ref[:, g * SSM_STATE:(g + 1) * SSM_STATE].astype(BF16) for g in range(SSM_GROUPS)]
    c_bf = [c_ref[:, g * SSM_STATE:(g + 1) * SSM_STATE].astype(BF16) for g in range(SSM_GROUPS)]
    c_gs = [c.astype(F32) for c in c_bf]
    cbs = [jnp.where(causal, _dot_nt(c_bf[g], b_gs[g]), 0.0) for g in range(SSM_GROUPS)]

    cols, lhs = [], []
    for h in range(SSM_HEADS):
        g = h // heads_per_group
        col = jnp.broadcast_to(acum[:, h:h + 1], (q, q))
        seg = jnp.where(causal, col - acum_t[h:h + 1, :], 0.0)
        scores = jnp.exp2(seg) * cbs[g]
        c_dec = c_gs[g] * jnp.exp2(col)
        lhs.append(jnp.concatenate([scores.astype(BF16), c_dec.astype(BF16)], axis=1))
        cols.append(col)

    xdts = []
    for p in range(n_pairs):
        lo = p * hp
        x_pair = x_ref[:, lo:lo + hp].astype(F32)
        xdt = x_pair * dt_wide[:, lo:lo + hp]
        rhs = jnp.concatenate([xdt.astype(BF16), s_ref[:, lo:lo + hp].astype(BF16)], axis=0)
        y_pair = jnp.where(low_half, _dot(lhs[2 * p], rhs), _dot(lhs[2 * p + 1], rhs))
        y_ref[:, lo:lo + hp] = (y_pair + dskip_ref[:, lo:lo + hp] * x_pair).astype(y_ref.dtype)
        xdts.append(xdt)

    for p in range(n_pairs):
        lo = p * hp
        g = (2 * p) // heads_per_group
        tot_pair = jnp.where(low_half[0:1], jnp.broadcast_to(tot[:, 2 * p:2 * p + 1], (1, hp)),
                             jnp.broadcast_to(tot[:, 2 * p + 1:2 * p + 2], (1, hp)))
        to_end = jnp.exp2(tot_pair - jnp.where(low_half, cols[2 * p], cols[2 * p + 1]))
        s_ref[:, lo:lo + hp] = (s_ref[:, lo:lo + hp] * jnp.exp2(tot_pair)
                                + _dot_tn(b_gs[g], (xdts[p] * to_end).astype(BF16)))


def _ssd_scan(xbc, dt_raw, dt_bias, a_log, d_skip):
    q = SSM_CHUNK
    n_lat, n_ctx = SEQ // q, CTX_LEN // q
    pfx, expand = _ssd_scan_consts()
    n_heads2 = 2 * SSM_HEADS
    b_col = SSM_INNER // (SSM_GROUPS * SSM_STATE)
    d_wide = jnp.repeat(d_skip, SSM_HEAD_DIM, axis=1).reshape(2, 1, SSM_INNER)

    def chunk(d, b, s):
        return _scan_chunk_index(d, b, s, n_lat, n_ctx)

    return pl.pallas_call(
        _ssd_scan_body,
        grid=(2, BATCH, n_lat + n_ctx),
        in_specs=[
            pl.BlockSpec((q, SSM_INNER), lambda d, b, s: (chunk(d, b, s), 0)),
            pl.BlockSpec((q, SSM_GROUPS * SSM_STATE), lambda d, b, s: (chunk(d, b, s), b_col)),
            pl.BlockSpec((q, SSM_GROUPS * SSM_STATE), lambda d, b, s: (chunk(d, b, s), b_col + 1)),
            pl.BlockSpec((q, n_heads2), lambda d, b, s: (chunk(d, b, s), 0)),
            pl.BlockSpec((1, n_heads2), lambda d, b, s: (0, 0)),
            pl.BlockSpec((1, n_heads2), lambda d, b, s: (0, 0)),
            pl.BlockSpec((None, 1, SSM_INNER), lambda d, b, s: (d, 0, 0)),
            pl.BlockSpec((None,) + pfx.shape[1:], lambda d, b, s: (d, 0, 0)),
            pl.BlockSpec(expand.shape, lambda d, b, s: (0, 0)),
        ],
        out_specs=pl.BlockSpec((None, q, SSM_INNER), lambda d, b, s: (d, chunk(d, b, s), 0)),
        out_shape=jax.ShapeDtypeStruct((2, T_ALL, SSM_INNER), BF16),
        scratch_shapes=[pltpu.VMEM((SSM_STATE, SSM_INNER), F32)],
        compiler_params=_cparams(("arbitrary", "arbitrary", "arbitrary"), 48),
        name="ssd_scan",
    )(xbc, xbc, xbc, dt_raw, dt_bias.reshape(1, n_heads2), a_log.reshape(1, n_heads2), d_wide, pfx, expand)


GLA_SUM_TERMS = 2


def _gla_consts():
    q = HG_CHUNK
    levels = int(np.log2(q))
    i = np.arange(q)
    mats = [(i[:, None] >= i[None, :]).astype(np.float32), (i[None, :] > i[:, None]).astype(np.float32)]
    masks = [np.eye(q, dtype=np.float32)]
    for lv in range(levels):
        s = 1 << lv
        blk, pos = i // (2 * s), i % (2 * s)
        bound = blk * 2 * s + s - 1
        right = pos >= s
        t = i[None, :]
        m_right = right[:, None] & (t > bound[:, None]) & (t <= i[:, None])
        m_left = (~right)[:, None] & (t > i[:, None]) & (t <= bound[:, None])
        mats.append((m_right | m_left).astype(np.float32))
        masks.append((right[:, None] & (~right)[None, :] & (blk[:, None] == blk[None, :])).astype(np.float32))
    mats.append(np.ones((ONES_ROWS, q), np.float32))
    fwd_m, fwd_k = np.concatenate(mats, axis=0), np.stack(masks)
    flip_rows = np.concatenate([np.arange(k * q, (k + 1) * q)[::-1] for k in range(levels + 2)]
                               + [np.arange((levels + 2) * q, (levels + 2) * q + ONES_ROWS)])
    bwd_m = fwd_m[flip_rows][:, ::-1]
    bwd_k = fwd_k[:, ::-1, ::-1]
    mats2 = np.tile(np.stack([fwd_m, bwd_m]), (1, 1, GLA_SUM_TERMS))
    return (jnp.asarray(mats2, BF16), jnp.asarray(np.stack([fwd_k, bwd_k]), F32), levels)


def _silu_epilogue(u):
    return (_silu(u),)


def _hg_gate_epilogue(layer, u, lbp_ref):
    lbp = lbp_ref[...]
    e = jnp.exp(lbp - jnp.max(lbp, axis=0, keepdims=True))
    sm = e / jnp.sum(e, axis=0, keepdims=True)
    lb = jnp.sum(sm[1:layer + 1], axis=0, keepdims=True) if layer > 0 else jnp.zeros((1, lbp.shape[1]), F32)
    ef = jnp.exp(-jnp.abs(u))
    log_sig = jnp.minimum(u, 0.0) - _log1p_unit(ef)
    la = jnp.log(lb)
    lc = jnp.log1p(-lb) + log_sig
    logf = jnp.maximum(la, lc) + _log1p_unit(jnp.exp(-jnp.abs(la - lc)))
    key = (1.0 - lb) * (jnp.where(u >= 0.0, ef, 1.0) / (1.0 + ef))
    return logf, key


def _gla_body(levels, q_ref, v_ref, logf_ref, key_ref, mats_ref, masks_ref, o_ref, st_ref):
    q = HG_CHUNK

    @pl.when(pl.program_id(2) == 0)
    def _():
        st_ref[...] = jnp.zeros_like(st_ref)

    qs_bf, key_bf, v_bf = q_ref[...], key_ref[...], v_ref[...]

    sums = _dot_sel(mats_ref[...], logf_ref[...], GLA_SUM_TERMS)
    q_in = qs_bf * jnp.exp(sums[:q]).astype(BF16)
    k_out = key_bf * jnp.exp(sums[q:2 * q]).astype(BF16)
    dec = jnp.exp(sums[(levels + 2) * q:(levels + 2) * q + 1])
    q_lv, k_lv = [], []
    for lv in range(levels):
        fac = jnp.exp(sums[(lv + 2) * q:(lv + 3) * q]).astype(BF16)
        q_lv.append(qs_bf * fac)
        k_lv.append(key_bf * fac)

    heads = [slice(h * LANES, (h + 1) * LANES) for h in range(HG_HEADS)]
    attn = masks_ref[0][None] * jnp.stack([_dot_nt(qs_bf[:, sl], key_bf[:, sl]) for sl in heads], axis=0)
    for lv in range(levels):
        attn = attn + masks_ref[lv + 1][None] * jnp.stack(
            [_dot_nt(q_lv[lv][:, sl], k_lv[lv][:, sl]) for sl in heads], axis=0)
    attn = attn.astype(BF16)
    st_bf = st_ref[...].astype(BF16)
    o_ref[...] = jnp.concatenate(
        [_dot(attn[h], v_bf[:, sl]) + _dot_nt(q_in[:, sl], st_bf[h]) for h, sl in enumerate(heads)],
        axis=1).astype(o_ref.dtype)
    for h, sl in enumerate(heads):
        st_ref[h] = st_ref[h] * dec[:, sl] + _dot_tn(v_bf[:, sl], k_out[:, sl])


def _gla_scan(qs, v, logf, key):
    q = HG_CHUNK
    n_lat, n_ctx = SEQ // q, CTX_LEN // q
    mats, masks, levels = _gla_consts()

    def chunk(d, b, s):
        return _scan_chunk_index(d, b, s, n_lat, n_ctx)

    return pl.pallas_call(
        functools.partial(_gla_body, levels),
        grid=(2, BATCH, n_lat + n_ctx),
        in_specs=[
            pl.BlockSpec((q, D_MODEL), lambda d, b, s: (chunk(d, b, s), 0)),
            pl.BlockSpec((q, D_MODEL), lambda d, b, s: (chunk(d, b, s), 0)),
            pl.BlockSpec((q, D_MODEL), lambda d, b, s: (chunk(d, b, s), d)),
            pl.BlockSpec((q, D_MODEL), lambda d, b, s: (chunk(d, b, s), d)),
            pl.BlockSpec((None,) + mats.shape[1:], lambda d, b, s: (d, 0, 0)),
            pl.BlockSpec((None,) + masks.shape[1:], lambda d, b, s: (d, 0, 0, 0)),
        ],
        out_specs=pl.BlockSpec((None, q, D_MODEL), lambda d, b, s: (d, chunk(d, b, s), 0)),
        out_shape=jax.ShapeDtypeStruct((2, T_ALL, D_MODEL), BF16),
        scratch_shapes=[pltpu.VMEM((HG_HEADS, LANES, HG_KEY_DIM), F32)],
        compiler_params=_cparams(("arbitrary", "arbitrary", "arbitrary"), 48),
        name="gla_scan",
    )(qs, v, logf, key, mats, masks)


def _na_bias_body(rpb_ref, o_ref):
    h = pl.program_id(0)
    n_dc = 2 * NA_WIN_C - 1
    qc = lax.broadcasted_iota(jnp.int32, (GRID_W, LANES), 0)
    kc = lax.broadcasted_iota(jnp.int32, (GRID_W, LANES), 1) % GRID_W
    c0 = jnp.clip(qc - NA_WIN_C // 2, 0, GRID_W - NA_WIN_C)
    col_in = (kc >= c0) & (kc < c0 + NA_WIN_C)
    dc = jnp.clip(kc - qc + NA_WIN_C - 1, 0, n_dc - 1)
    low = lax.broadcasted_iota(jnp.int32, (GRID_W, LANES), 1) < GRID_W
    tiles = []
    for dr in range(2 * NA_WIN_R - 1):
        t = jnp.zeros((GRID_W, LANES), F32)
        for k in range(n_dc):
            t = jnp.where(dc == k, rpb_ref[(h * (2 * NA_WIN_R - 1) + dr) * n_dc + k], t)
        tiles.append(jnp.where(col_in, t, -jnp.inf))
    for dr in range(2 * NA_WIN_R - 2):
        o_ref[dr] = jnp.where(low, tiles[dr], tiles[dr + 1])


def _na_bias_table(rpb):
    n_dr = 2 * NA_WIN_R - 2
    return pl.pallas_call(
        _na_bias_body,
        grid=(NA_HEADS,),
        in_specs=[pl.BlockSpec(memory_space=pltpu.SMEM)],
        out_specs=pl.BlockSpec((None, n_dr, GRID_W, LANES), lambda h: (h, 0, 0, 0)),
        out_shape=jax.ShapeDtypeStruct((NA_HEADS, n_dr, GRID_W, LANES), F32),
        compiler_params=_cparams(("arbitrary",), 32),
        name="na_bias",
    )(rpb.reshape(-1))


def _rope_tables():
    quarter = NA_HEAD_DIM // 4
    inv = ROPE_BASE ** (-np.arange(quarter, dtype=np.float64) / quarter)
    t = np.arange(SEQ)
    row, col = t // GRID_W, t % GRID_W
    ang = np.concatenate([row[:, None] * inv[None], row[:, None] * inv[None],
                          col[:, None] * inv[None], col[:, None] * inv[None]], axis=1)
    sign = np.tile(np.concatenate([-np.ones(quarter), np.ones(quarter)]), 2)[None]
    return jnp.asarray(np.cos(ang), F32), jnp.asarray(np.sin(ang) * sign, F32)


def _rope(v, cos, sin_signed):
    quarter = NA_HEAD_DIM // 4
    lane = lax.broadcasted_iota(jnp.int32, v.shape, 1)
    first = (lane % (2 * quarter)) < quarter
    partner = jnp.where(first, pltpu.roll(v, NA_HEAD_DIM - quarter, axis=1), pltpu.roll(v, quarter, axis=1))
    return v * cos + partner * sin_signed


def _na_body(rows_per_step, q_ref, k_ref, v_ref, kc_ref, vc_ref, cosq_ref, sinq_ref, cos_ref, sin_ref, bias_ref,
             o_ref, kr_ref, vb_ref):
    rb = pl.program_id(2)
    n_rows = SEQ // GRID_W
    scale = NA_HEAD_DIM ** -0.5

    @pl.when(rb == 0)
    def _():
        kr_ref[...] = _rope(k_ref[...], cos_ref[...], sin_ref[...]).astype(BF16)
        vb_ref[...] = v_ref[...].astype(BF16)

    qr = (_rope(q_ref[...], cosq_ref[...], sinq_ref[...]) * scale).astype(BF16)
    kc = kc_ref[...].astype(BF16)
    vc = vc_ref[...].astype(BF16)
    starts, s_rows = [], []
    for lr in range(rows_per_step):
        r = rb * rows_per_step + lr
        r0 = jnp.clip(r - NA_WIN_R // 2, 0, n_rows - NA_WIN_R)
        start = pl.multiple_of(r0 * GRID_W, GRID_W)
        dr0 = r0 - r + NA_WIN_R - 1
        bias = jnp.concatenate([bias_ref[dr0 + 2 * p] for p in range(NA_WIN_R // 2)], axis=1)
        k_win = kr_ref[pl.ds(start, NA_WIN_R * GRID_W), :]
        s_rows.append(_dot_nt(qr[lr * GRID_W:(lr + 1) * GRID_W], k_win) + bias)
        starts.append(start)
    s_loc = jnp.concatenate(s_rows, axis=0)
    s_ctx = _dot_nt(qr, kc)
    m = jnp.maximum(jnp.max(s_loc, axis=-1, keepdims=True), jnp.max(s_ctx, axis=-1, keepdims=True))
    p_loc = jnp.exp(s_loc - m)
    p_ctx = jnp.exp(s_ctx - m)
    denom = jnp.sum(p_loc, axis=-1, keepdims=True) + jnp.sum(p_ctx, axis=-1, keepdims=True)
    p_loc = p_loc.astype(BF16)
    o_loc = jnp.concatenate(
        [_dot(p_loc[lr * GRID_W:(lr + 1) * GRID_W], vb_ref[pl.ds(starts[lr], NA_WIN_R * GRID_W), :])
         for lr in range(rows_per_step)], axis=0)
    o = o_loc + _dot(p_ctx.astype(BF16), vc)
    o_ref[...] = (o / denom).astype(o_ref.dtype)


def _na_attention(qkv, bias_tbl):
    rows_per_step = 64
    tq = rows_per_step * GRID_W
    steps = SEQ // tq
    cos, sin = _rope_tables()
    hd = NA_HEAD_DIM
    ctx_blk0 = T_LAT // CTX_LEN
    return pl.pallas_call(
        functools.partial(_na_body, rows_per_step),
        grid=(BATCH, NA_HEADS, steps),
        in_specs=[
            pl.BlockSpec((tq, hd), lambda b, h, r: (b * steps + r, h)),
            pl.BlockSpec((SEQ, hd), lambda b, h, r: (b, NA_HEADS + h)),
            pl.BlockSpec((SEQ, hd), lambda b, h, r: (b, 2 * NA_HEADS + h)),
            pl.BlockSpec((CTX_LEN, hd), lambda b, h, r: (ctx_blk0 + b, NA_HEADS + h)),
            pl.BlockSpec((CTX_LEN, hd), lambda b, h, r: (ctx_blk0 + b, 2 * NA_HEADS + h)),
            pl.BlockSpec((tq, hd), lambda b, h, r: (r, 0)),
            pl.BlockSpec((tq, hd), lambda b, h, r: (r, 0)),
            pl.BlockSpec((SEQ, hd), lambda b, h, r: (0, 0)),
            pl.BlockSpec((SEQ, hd), lambda b, h, r: (0, 0)),
            pl.BlockSpec((None,) + bias_tbl.shape[1:], lambda b, h, r: (h, 0, 0, 0)),
        ],
        out_specs=pl.BlockSpec((tq, hd), lambda b, h, r: (b * steps + r, h)),
        out_shape=jax.ShapeDtypeStruct((T_LAT, D_MODEL), BF16),
        scratch_shapes=[pltpu.VMEM((SEQ, hd), BF16), pltpu.VMEM((SEQ, hd), BF16)],
        compiler_params=_cparams(("arbitrary", "arbitrary", "arbitrary"), 48),
        name="na_attention",
    )(qkv, qkv, qkv, qkv, qkv, cos, sin, cos, sin, bias_tbl)


def _ctx_attn_body(q_ref, k_ref, v_ref, o_ref):
    scale = NA_HEAD_DIM ** -0.5
    s = _dot_nt((q_ref[...] * scale).astype(BF16), k_ref[...].astype(BF16))
    p = jnp.exp(s - jnp.max(s, axis=-1, keepdims=True))
    o = _dot(p.astype(BF16), v_ref[...].astype(BF16))
    o_ref[...] = (o / jnp.sum(p, axis=-1, keepdims=True)).astype(o_ref.dtype)


def _ctx_attention(qkv):
    hd = NA_HEAD_DIM
    ctx_blk0 = T_LAT // CTX_LEN
    return pl.pallas_call(
        _ctx_attn_body,
        grid=(BATCH, NA_HEADS),
        in_specs=[
            pl.BlockSpec((CTX_LEN, hd), lambda b, h: (ctx_blk0 + b, h)),
            pl.BlockSpec((CTX_LEN, hd), lambda b, h: (ctx_blk0 + b, NA_HEADS + h)),
            pl.BlockSpec((CTX_LEN, hd), lambda b, h: (ctx_blk0 + b, 2 * NA_HEADS + h)),
        ],
        out_specs=pl.BlockSpec((CTX_LEN, hd), lambda b, h: (b, h)),
        out_shape=jax.ShapeDtypeStruct((T_CTX, D_MODEL), BF16),
        compiler_params=_cparams(("arbitrary", "arbitrary"), 32),
        name="ctx_attention",
    )(qkv, qkv, qkv)


def kernel(x, c, ctx, c_ctx, w_mod, b_mod, norm_g, ffn_w_up, ffn_conv_w, ffn_conv_b, ffn_w_down, ssm_w_in, ssm_conv_w, ssm_conv_b, ssm_dt_bias, ssm_a_log, ssm_d, ssm_norm_g, ssm_w_out, hg_w_in, hg_lb, hg_norm_g, hg_w_out, na_w_qkv, na_rpb, na_w_out):
    cvec8 = jnp.concatenate([c, c_ctx[None], jnp.zeros((SUBLANES - BATCH - 1, D_MODEL), F32)], axis=0)
    mod = _adaln(cvec8, w_mod, b_mod)
    ffn_up_bf, ffn_down_bf = ffn_w_up.astype(BF16), ffn_w_down.astype(BF16)
    ssm_out_bf, hg_out_bf, na_out_bf = ssm_w_out.astype(BF16), hg_w_out.astype(BF16), na_w_out.astype(BF16)

    xs, h = _prep(x.reshape(T_LAT, D_MODEL), ctx.reshape(T_CTX, D_MODEL), norm_g[0, 0], mod[0])
    for i in range(DEPTH):
        last = i == DEPTH - 1
        kind, slot = i % N_MIXERS, i // N_MIXERS
        mod_l = mod[i]
        n_rows = T_LAT if last else T_ALL
        op = dict(xs=xs, g_res=norm_g[i, 1], g_ffn=norm_g[i, 2], mod_l=mod_l, n_rows=n_rows, slot=slot)
        if kind == 0:
            tn = 1024
            z = _in_proj(h, ssm_w_in, slot, tn, "ssd_in_z", n_dim=SSM_INNER)
            xbc = _in_proj_conv(h, ssm_w_in, slot, ssm_conv_w[slot], ssm_conv_b[slot], tn, SSM_INNER // tn,
                                "ssd_in_xbc")
            dt_raw = _in_proj(h, ssm_w_in, slot, 2 * SSM_HEADS, "ssd_dt", n_dim=2 * SSM_HEADS,
                              first_tile=(SSM_INNER + SSM_CONV_DIM) // (2 * SSM_HEADS))
            y2 = _ssd_scan(xbc, dt_raw, ssm_dt_bias[slot], ssm_a_log[slot], ssm_d[slot])
            xs, hf = _out_proj(_ssd_gate_prologue, [((y2, 0), 0), ((y2, 1), 0), (z, 0)],
                               ssm_norm_g[slot].reshape(1, SSM_INNER), ssm_out_bf, tm=256, name="ssd_out", **op)
        elif kind == 1:
            tn = 1024
            per = D_MODEL // tn
            qs = _in_proj(h, hg_w_in, slot, tn, "hg_in_q", n_dim=D_MODEL, epilogue=_silu_epilogue,
                          out_dtypes=(BF16,))
            v = _in_proj(h, hg_w_in, slot, tn, "hg_in_v", n_dim=D_MODEL, first_tile=per, out_dtypes=(BF16,))
            lb_spec = pl.BlockSpec((None, DEPTH, tn), lambda j, r: (j // per, 0, j % per))
            logf, key = _in_proj(h, hg_w_in, slot, tn, "hg_in_f", n_dim=2 * D_MODEL, first_tile=2 * per,
                                 epilogue=functools.partial(_hg_gate_epilogue, i), aux=[(hg_lb, lb_spec)],
                                 out_dtypes=(F32, BF16))
            gate = _in_proj(h, hg_w_in, slot, tn, "hg_in_g", n_dim=D_MODEL, first_tile=4 * per)
            o2 = _gla_scan(qs, v, logf, key)
            xs, hf = _out_proj(_hg_readout_prologue, [((o2, 0), 0), ((o2, 1), 0), (gate, 0)],
                               hg_norm_g[slot].reshape(1, D_MODEL), hg_out_bf, tm=ROW_TILE, name="hg_out", **op)
        else:
            qkv = _in_proj(h, na_w_qkv, slot, 1024, "na_in")
            o_lat = _na_attention(qkv, _na_bias_table(na_rpb[slot]))
            o_ctx = _ctx_attention(qkv)
            assert T_CTX == ROW_TILE
            last_lat = T_LAT // ROW_TILE - 1
            xs, hf = _out_proj(_na_prologue, [(o_lat, 0, functools.partial(jnp.minimum, last_lat)),
                                              (o_ctx, 0, lambda t: 0)],
                               None, na_out_bf, tm=ROW_TILE, name="na_out", **op)
        nxt = min(i + 1, DEPTH - 1)
        xs, h = _ffn(hf, xs, i, ffn_up_bf, ffn_conv_w, ffn_conv_b, ffn_down_bf, norm_g[i, 3], mod_l,
                     norm_g[nxt, 0], mod[nxt], n_rows=n_rows, emit_h=not last)

    return xs.reshape(BATCH, SEQ, D_MODEL)
```

```python
import functools

import numpy as np
import jax
import jax.numpy as jnp
from jax import lax
from jax.experimental import pallas as pl
from jax.experimental.pallas import tpu as pltpu

F32 = jnp.float32
BF16 = jnp.bfloat16

D_MODEL = 2048
BATCH = 2
SEQ = 4096
DEPTH = 4
GRID_W = 64
CTX_LEN = 256
N_MIXERS = 3
RMS_EPS = 1e-6
SSM_INNER = 4096
SSM_HEAD_DIM = 64
SSM_HEADS = 64
SSM_STATE = 128
SSM_GROUPS = 8
SSM_CONV_DIM = SSM_INNER + 2 * SSM_GROUPS * SSM_STATE
SSM_IN_DIM = SSM_INNER + SSM_CONV_DIM + 2 * SSM_HEADS
SSM_CHUNK = 128
HG_HEADS = 16
HG_KEY_DIM = 128
HG_IN_DIM = 5 * D_MODEL
HG_CHUNK = 64
NA_HEADS = 16
NA_HEAD_DIM = 128
NA_WIN_R = 8
NA_WIN_C = 16
ROPE_BASE = 10000.0
FFN_HIDDEN = 5632

T_LAT = BATCH * SEQ
T_CTX = BATCH * CTX_LEN
T_ALL = T_LAT + T_CTX
LANES = 128
SUBLANES = 8
BF16_ROWS = 16
ONES_ROWS = 16
ROW_TILE = 512
FFN_COL_CHUNK = 512
MIB = 1024 * 1024
LOG2E = 1.4426950408889634


def _cparams(sem, vmem_mib):
    return pltpu.CompilerParams(dimension_semantics=sem, vmem_limit_bytes=vmem_mib * MIB)


def _silu(v):
    return v * jax.nn.sigmoid(v)


def _rms(v):
    return v * lax.rsqrt(jnp.mean(v * v, axis=-1, keepdims=True) + RMS_EPS)


def _log1p_unit(e):
    return jnp.log(1.0 + e)


def _softplus(v):
    return jnp.maximum(v, 0.0) + _log1p_unit(jnp.exp(-jnp.abs(v)))


def _dot(a, b):
    return jnp.dot(a, b, preferred_element_type=F32)


def _dot_nt(a, b):
    return lax.dot_general(a, b, (((1,), (1,)), ((), ())), preferred_element_type=F32)


def _dot_tn(a, b):
    return lax.dot_general(a, b, (((0,), (0,)), ((), ())), preferred_element_type=F32)


def _bf16_pieces(v, terms):
    pieces = []
    rem = v
    for _ in range(terms):
        part = rem.astype(BF16)
        rem = rem - part.astype(F32)
        pieces.append(part)
    return pieces


def _dot_sel(m01_rep, v, terms):
    return _dot(m01_rep, jnp.concatenate(_bf16_pieces(v, terms), axis=0))


def _dot_sel_rhs(v, m01_rep, terms):
    return _dot(jnp.concatenate(_bf16_pieces(v, terms), axis=1), m01_rep)


def _mod_row(row0):
    return jnp.where(row0 >= T_LAT, 2, row0 // SEQ)


def _mod_vec(mod_ref, r, idx):
    return mod_ref[pl.ds(r, 1), idx * D_MODEL:(idx + 1) * D_MODEL]


def _norm_mod(x, g, mod_ref, r, shift_idx, scale_idx):
    return _rms(x) * (g * (1.0 + _mod_vec(mod_ref, r, scale_idx))) + _mod_vec(mod_ref, r, shift_idx)


def _seg_edges(row0, rows):
    gr = row0 + lax.broadcasted_iota(jnp.int32, (rows, 1), 0)
    pos = jnp.where(gr >= T_LAT, (gr - T_LAT) % CTX_LEN, gr % SEQ)
    length = jnp.where(gr >= T_LAT, CTX_LEN, SEQ)
    return pos == 0, pos == length - 1


def _shift_rows(cur, prev_row, next_row, row0):
    rows = cur.shape[0]
    first, last = _seg_edges(row0, rows)
    sub = lax.broadcasted_iota(jnp.int32, (SUBLANES, 1), 0)
    up = pltpu.roll(cur, 1, axis=0)
    up = jnp.concatenate([jnp.where(sub == 0, prev_row, up[:SUBLANES]), up[SUBLANES:]], axis=0)
    up = jnp.where(first, 0.0, up)
    dn = pltpu.roll(cur, rows - 1, axis=0)
    dn = jnp.concatenate([dn[:rows - SUBLANES], jnp.where(sub == SUBLANES - 1, next_row, dn[rows - SUBLANES:])],
                         axis=0)
    dn = jnp.where(last, 0.0, dn)
    return up, dn


def _token_conv(cur, prev_row, next_row, row0, w_ref, b_ref):
    up, dn = _shift_rows(cur, prev_row, next_row, row0)
    return b_ref[...] + up * w_ref[0:1, :] + cur * w_ref[1:2, :] + dn * w_ref[2:3, :]


def _halo_specs(tm, halo, width, n_rows, col_fn):
    per = tm // halo
    last_blk = n_rows // halo - 1
    prev = pl.BlockSpec((halo, width), lambda i, j: (jnp.maximum(i * per - 1, 0), col_fn(j)))
    nxt = pl.BlockSpec((halo, width), lambda i, j: (jnp.minimum((i + 1) * per, last_blk), col_fn(j)))
    return prev, nxt


def _adaln_body(c_ref, w_ref, b_ref, o_ref):
    s = _silu(c_ref[...]).astype(BF16)
    o_ref[...] = _dot(s, w_ref[...].astype(BF16)) + b_ref[...]


def _adaln(cvec8, w_mod, b_mod):
    tn = 1024
    n_out = 6 * D_MODEL
    return pl.pallas_call(
        _adaln_body,
        grid=(DEPTH, n_out // tn),
        in_specs=[
            pl.BlockSpec((SUBLANES, D_MODEL), lambda l, j: (0, 0)),
            pl.BlockSpec((None, D_MODEL, tn), lambda l, j: (l, 0, j)),
            pl.BlockSpec((None, 1, tn), lambda l, j: (l, 0, j)),
        ],
        out_specs=pl.BlockSpec((None, SUBLANES, tn), lambda l, j: (l, 0, j)),
        out_shape=jax.ShapeDtypeStruct((DEPTH, SUBLANES, n_out), F32),
        compiler_params=_cparams(("arbitrary", "arbitrary"), 40),
        name="adaln",
    )(cvec8, w_mod, b_mod.reshape(DEPTH, 1, n_out))


def _prep_body(lat_ref, ctx_ref, g_ref, mod_ref, xs_ref, h_ref):
    row0 = pl.program_id(0) * ROW_TILE

    def emit(src_ref):
        xv = src_ref[...]
        xs_ref[...] = xv
        h_ref[...] = _norm_mod(xv, g_ref[...], mod_ref, _mod_row(row0), 0, 1).astype(BF16)

    @pl.when(row0 < T_LAT)
    def _():
        emit(lat_ref)

    @pl.when(row0 >= T_LAT)
    def _():
        emit(ctx_ref)


def _prep(x_lat, x_ctx, g_row, mod_l):
    assert T_CTX == ROW_TILE
    n_lat = T_LAT // ROW_TILE
    row = pl.BlockSpec((ROW_TILE, D_MODEL), lambda i: (i, 0))
    return pl.pallas_call(
        _prep_body,
        grid=(T_ALL // ROW_TILE,),
        in_specs=[pl.BlockSpec((ROW_TILE, D_MODEL), lambda i: (jnp.minimum(i, n_lat - 1), 0)),
                  pl.BlockSpec((ROW_TILE, D_MODEL), lambda i: (0, 0)),
                  pl.BlockSpec((1, D_MODEL), lambda i: (0, 0)), pl.BlockSpec(mod_l.shape, lambda i: (0, 0))],
        out_specs=[row, row],
        out_shape=[jax.ShapeDtypeStruct((T_ALL, D_MODEL), F32), jax.ShapeDtypeStruct((T_ALL, D_MODEL), BF16)],
        compiler_params=_cparams(("arbitrary",), 40),
        name="prep",
    )(x_lat, x_ctx, g_row.reshape(1, D_MODEL), mod_l)


def _in_proj_body(epilogue, n_aux, h_ref, w_ref, *rest):
    aux, outs, wb_ref = rest[:n_aux], rest[n_aux:-1], rest[-1]

    @pl.when(pl.program_id(1) == 0)
    def _():
        wb_ref[...] = w_ref[...].astype(BF16)

    u = _dot(h_ref[...], wb_ref[...])
    vals = (u,) if epilogue is None else epilogue(u, *aux)
    for o_ref, val in zip(outs, vals):
        o_ref[...] = val.astype(o_ref.dtype)


def _in_proj(h, w_stack, slot, tn, name, n_dim=None, first_tile=0, epilogue=None, aux=(), out_dtypes=(F32,)):
    tm = T_ALL // 8
    n_dim = w_stack.shape[2] if n_dim is None else n_dim
    res = pl.pallas_call(
        functools.partial(_in_proj_body, epilogue, len(aux)),
        grid=(n_dim // tn, T_ALL // tm),
        in_specs=[pl.BlockSpec((tm, D_MODEL), lambda j, i: (i, 0)),
                  pl.BlockSpec((None, D_MODEL, tn), lambda j, i: (slot, 0, first_tile + j))]
        + [spec for _, spec in aux],
        out_specs=[pl.BlockSpec((tm, tn), lambda j, i: (i, j)) for _ in out_dtypes],
        out_shape=[jax.ShapeDtypeStruct((T_ALL, n_dim), dt) for dt in out_dtypes],
        scratch_shapes=[pltpu.VMEM((D_MODEL, tn), BF16)],
        compiler_params=_cparams(("arbitrary", "arbitrary"), 56),
        name=name,
    )(h, w_stack, *[a for a, _ in aux])
    return res[0] if len(out_dtypes) == 1 else tuple(res)


def _ssd_gate_prologue(row0, yf_ref, yb_ref, z_ref, g_ref):
    v = (yf_ref[...].astype(F32) + yb_ref[...].astype(F32)) * _silu(z_ref[...])
    return v * g_ref[...], lax.rsqrt(jnp.mean(v * v, axis=-1, keepdims=True) + RMS_EPS)


def _hg_readout_prologue(row0, of_ref, ob_ref, gate_ref, g_ref):
    o = of_ref[...].astype(F32) + ob_ref[...].astype(F32)
    parts = [_rms(o[:, h * LANES:(h + 1) * LANES]) for h in range(o.shape[1] // LANES)]
    return jnp.concatenate(parts, axis=-1) * g_ref[...] * _silu(gate_ref[...]), None


def _na_prologue(row0, o_lat_ref, o_ctx_ref):
    return jnp.where(row0 >= T_LAT, o_ctx_ref[...], o_lat_ref[...]), None


def _out_proj_body(prologue, n_in, tm, *refs):
    ins = refs[:n_in]
    w_ref, x_ref, g1_ref, g2_ref, mod_ref, xo_ref, ho_ref = refs[n_in:]
    part, row_scale = prologue(pl.program_id(0) * tm, *ins)
    y = _dot(part.astype(BF16), w_ref[...])
    if row_scale is not None:
        y = y * row_scale
    r = _mod_row(pl.program_id(0) * tm)
    xn = x_ref[...] + _rms(y) * (_mod_vec(mod_ref, r, 2) * g1_ref[...])
    xo_ref[...] = xn
    ho_ref[...] = _norm_mod(xn, g2_ref[...], mod_ref, r, 3, 4).astype(BF16)


def _out_proj(prologue, row_ins, gain, w_stack, slot, xs, g_res, g_ffn, mod_l, *, n_rows, tm, name):
    k_dim = w_stack.shape[1]
    in_specs, args = [], []
    for arr, cb, *row_fn in row_ins:
        row_fn = row_fn[0] if row_fn else (lambda i: i)
        if isinstance(arr, tuple):
            in_specs.append(pl.BlockSpec((None, tm, k_dim), functools.partial(
                lambda i, ld, cb, row_fn: (ld, row_fn(i), cb), ld=arr[1], cb=cb, row_fn=row_fn)))
            args.append(arr[0])
        else:
            in_specs.append(pl.BlockSpec((tm, k_dim), functools.partial(
                lambda i, cb, row_fn: (row_fn(i), cb), cb=cb, row_fn=row_fn)))
            args.append(arr)
    if gain is not None:
        in_specs.append(pl.BlockSpec((1, k_dim), lambda i: (0, 0)))
        args.append(gain)
    n_in = len(args)
    row = pl.BlockSpec((tm, D_MODEL), lambda i: (i, 0))
    vec = pl.BlockSpec((1, D_MODEL), lambda i: (0, 0))
    in_specs += [pl.BlockSpec((None, k_dim, D_MODEL), lambda i: (slot, 0, 0), pipeline_mode=pl.Buffered(1)),
                 row, vec, vec, pl.BlockSpec(mod_l.shape, lambda i: (0, 0))]
    return pl.pallas_call(
        functools.partial(_out_proj_body, prologue, n_in, tm),
        grid=(n_rows // tm,),
        in_specs=in_specs,
        out_specs=[row, row],
        out_shape=[jax.ShapeDtypeStruct((n_rows, D_MODEL), F32), jax.ShapeDtypeStruct((n_rows, D_MODEL), BF16)],
        compiler_params=_cparams(("arbitrary",), 60),
        name=name,
    )(*args, w_stack, xs, g_res.reshape(1, D_MODEL), g_ffn.reshape(1, D_MODEL), mod_l)


def _ffn_body(emit_h, has_ctx, h_ref, hp_ref, hn_ref, wa_ref, wv_ref, cwa_ref, cwv_ref, cba_ref, cbv_ref, wd_ref,
              x_ref, g_res_ref, mod_ref, g_next_ref, modn_ref, *out_and_scratch):
    if emit_h:
        xo_ref, ho_ref, hall_ref, acc_ref = out_and_scratch
    else:
        xo_ref, hall_ref, acc_ref = out_and_scratch
    k = pl.program_id(1)
    tm = ROW_TILE
    row0 = pl.program_id(0) * tm
    pad = BF16_ROWS
    rows = tm + 2 * pad
    s8 = SUBLANES

    @pl.when(k == 0)
    def _():
        hall_ref[pl.ds(0, pad), :] = hp_ref[...]
        hall_ref[pl.ds(pad, tm), :] = h_ref[...]
        hall_ref[pl.ds(pad + tm, pad), :] = hn_ref[...]
        acc_ref[...] = jnp.zeros_like(acc_ref)

    hall = hall_ref[...]
    edge_first = row0 % SEQ == 0
    edge_last = ((row0 + tm) % SEQ == 0) | (row0 + tm == T_ALL)
    is_ctx = row0 >= T_LAT
    joints = list(range(CTX_LEN, tm, CTX_LEN)) if has_ctx else []
    sub = lax.broadcasted_iota(jnp.int32, (s8, 1), 0)

    def patch(v, start, cond):
        return jnp.concatenate([v[:start], jnp.where(cond, 0.0, v[start:start + s8]), v[start + s8:]], axis=0)

    def conv(w_ref, cw_ref, cb_ref, cs):
        u = _dot(hall, w_ref[:, cs])
        u = patch(u, pad - s8, edge_first)
        u = patch(u, pad + tm, edge_last)
        up = pltpu.roll(u, 1, axis=0)[pad:pad + tm]
        dn = pltpu.roll(u, rows - 1, axis=0)[pad:pad + tm]
        for j in joints:
            up = patch(up, j, is_ctx & (sub == 0))
            dn = patch(dn, j - s8, is_ctx & (sub == s8 - 1))
        return cb_ref[:, cs] + up * cw_ref[0:1, cs] + u[pad:pad + tm] * cw_ref[1:2, cs] + dn * cw_ref[2:3, cs]

    chunks = [pl.ds(c * FFN_COL_CHUNK, FFN_COL_CHUNK) for c in range(wa_ref.shape[1] // FFN_COL_CHUNK)]
    acts = [(_silu(conv(wa_ref, cwa_ref, cba_ref, cs)) * conv(wv_ref, cwv_ref, cbv_ref, cs)).astype(BF16)
            for cs in chunks]
    acc_ref[...] += _dot(jnp.concatenate(acts, axis=1), wd_ref[...])

    @pl.when(k == pl.num_programs(1) - 1)
    def _():
        r = _mod_row(row0)
        xn = x_ref[...] + _rms(acc_ref[...]) * (_mod_vec(mod_ref, r, 5) * g_res_ref[...])
        xo_ref[...] = xn
        if emit_h:
            ho_ref[...] = _norm_mod(xn, g_next_ref[...], modn_ref, r, 0, 1).astype(BF16)


def _ffn(h, xs, layer, w_up, conv_w, conv_b, w_down, g_res, mod_l, g_next, mod_next, *, n_rows, emit_h):
    tm, tk = ROW_TILE, 512
    kt = FFN_HIDDEN // tk
    h_prev, h_next = _halo_specs(tm, BF16_ROWS, D_MODEL, n_rows, lambda k: 0)
    row = pl.BlockSpec((tm, D_MODEL), lambda i, k: (i, 0))
    vec = pl.BlockSpec((1, D_MODEL), lambda i, k: (0, 0))
    tab = pl.BlockSpec(mod_l.shape, lambda i, k: (0, 0))
    out_specs = [row, row] if emit_h else [row]
    out_shape = [jax.ShapeDtypeStruct((n_rows, D_MODEL), F32)]
    if emit_h:
        out_shape.append(jax.ShapeDtypeStruct((n_rows, D_MODEL), BF16))
    conv_b3 = conv_b.reshape(DEPTH, 1, 2 * FFN_HIDDEN)

    res = pl.pallas_call(
        functools.partial(_ffn_body, emit_h, n_rows > T_LAT),
        grid=(n_rows // tm, kt),
        in_specs=[
            row, h_prev, h_next,
            pl.BlockSpec((None, D_MODEL, tk), lambda i, k: (layer, 0, k)),
            pl.BlockSpec((None, D_MODEL, tk), lambda i, k: (layer, 0, k + kt)),
            pl.BlockSpec((None, 3, tk), lambda i, k: (layer, 0, k)),
            pl.BlockSpec((None, 3, tk), lambda i, k: (layer, 0, k + kt)),
            pl.BlockSpec((None, 1, tk), lambda i, k: (layer, 0, k)),
            pl.BlockSpec((None, 1, tk), lambda i, k: (layer, 0, k + kt)),
            pl.BlockSpec((None, tk, D_MODEL), lambda i, k: (layer, k, 0)),
            row, vec, tab, vec, tab,
        ],
        out_specs=out_specs,
        out_shape=out_shape,
        scratch_shapes=[pltpu.VMEM((tm + 2 * BF16_ROWS, D_MODEL), BF16), pltpu.VMEM((tm, D_MODEL), F32)],
        compiler_params=_cparams(("arbitrary", "arbitrary"), 56),
        name="conv_ffn",
    )(h, h, h, w_up, w_up, conv_w, conv_w, conv_b3, conv_b3, w_down, xs, g_res.reshape(1, D_MODEL), mod_l,
      g_next.reshape(1, D_MODEL), mod_next)
    return (res[0], res[1]) if emit_h else (res[0], None)


def _in_proj_conv_body(tm, h_ref, hp_ref, hn_ref, w_ref, cw_ref, cb_ref, o_ref, wb_ref):
    @pl.when(pl.program_id(1) == 0)
    def _():
        wb_ref[...] = w_ref[...].astype(BF16)

    w = wb_ref[...]
    u = _dot(h_ref[...], w)
    uh = _dot(jnp.concatenate([hp_ref[...], hn_ref[...]], axis=0), w)
    conv = _token_conv(u, uh[BF16_ROWS - 1:BF16_ROWS], uh[BF16_ROWS:BF16_ROWS + 1], pl.program_id(1) * tm,
                       cw_ref, cb_ref)
    o_ref[...] = _silu(conv).astype(o_ref.dtype)


def _in_proj_conv(h, w_stack, slot, conv_w, conv_b, tn, first_tile, name):
    tm = T_ALL // 8
    n_dim = conv_w.shape[1]
    per = tm // BF16_ROWS
    last_blk = T_ALL // BF16_ROWS - 1
    return pl.pallas_call(
        functools.partial(_in_proj_conv_body, tm),
        grid=(n_dim // tn, T_ALL // tm),
        in_specs=[pl.BlockSpec((tm, D_MODEL), lambda j, i: (i, 0)),
                  pl.BlockSpec((BF16_ROWS, D_MODEL), lambda j, i: (jnp.maximum(i * per - 1, 0), 0)),
                  pl.BlockSpec((BF16_ROWS, D_MODEL), lambda j, i: (jnp.minimum((i + 1) * per, last_blk), 0)),
                  pl.BlockSpec((None, D_MODEL, tn), lambda j, i: (slot, 0, first_tile + j)),
                  pl.BlockSpec((3, tn), lambda j, i: (0, j)),
                  pl.BlockSpec((1, tn), lambda j, i: (0, j))],
        out_specs=pl.BlockSpec((tm, tn), lambda j, i: (i, j)),
        out_shape=jax.ShapeDtypeStruct((T_ALL, n_dim), BF16),
        scratch_shapes=[pltpu.VMEM((D_MODEL, tn), BF16)],
        compiler_params=_cparams(("arbitrary", "arbitrary"), 56),
        name=name,
    )(h, h, h, w_stack, conv_w, conv_b.reshape(1, -1))


def _scan_chunk_index(d, b, s, n_lat, n_ctx):
    ctx0 = BATCH * n_lat + b * n_ctx
    fwd = jnp.where(s < n_ctx, ctx0 + s, b * n_lat + s - n_ctx)
    bwd = jnp.where(s < n_ctx, ctx0 + n_ctx - 1 - s, b * n_lat + n_lat - 1 - (s - n_ctx))
    return jnp.where(d == 0, fwd, bwd)


SSD_CUM_TERMS = 3
SSD_DT_TERMS = 2


def _ssd_scan_consts():
    q = SSM_CHUNK
    i = np.arange(q)
    pfx = np.zeros((2, q + ONES_ROWS, q), np.float32)
    pfx[0, :q] = i[:, None] >= i[None, :]
    pfx[1, :q] = i[:, None] <= i[None, :]
    pfx[:, q:] = 1.0
    pfx = np.tile(pfx, (1, 1, SSD_CUM_TERMS))
    expand = np.repeat(np.eye(SSM_HEADS, dtype=np.float32), SSM_HEAD_DIM, axis=1)
    expand = np.tile(expand, (SSD_DT_TERMS, 1))
    return jnp.asarray(pfx, BF16), jnp.asarray(expand, BF16)


def _ssd_scan_body(x_ref, b_ref, c_ref, dtraw_ref, dtb_ref, alog_ref, dskip_ref, pfx_ref, exp_ref, y_ref, s_ref):
    q = SSM_CHUNK
    hp = 2 * SSM_HEAD_DIM
    d = pl.program_id(0)

    @pl.when(pl.program_id(2) == 0)
    def _():
        s_ref[...] = jnp.zeros_like(s_ref)

    def pick(v):
        return jnp.where(d == 0, v[:, :SSM_HEADS], v[:, SSM_HEADS:])

    dt2 = _softplus(dtraw_ref[...] + dtb_ref[...])
    cum2 = _dot_sel(pfx_ref[...], dt2 * (-LOG2E * jnp.exp(alog_ref[...])), SSD_CUM_TERMS)
    acum2 = cum2[:q]
    acum = pick(acum2)
    tot = pick(cum2[q:q + SUBLANES])[0:1]
    acum_t2 = acum2.T
    acum_t = jnp.where(d == 0, acum_t2[:SSM_HEADS], acum_t2[SSM_HEADS:])
    dt_wide = _dot_sel_rhs(pick(dt2), exp_ref[...], SSD_DT_TERMS)

    ii = lax.broadcasted_iota(jnp.int32, (q, q), 0)
    jj = lax.broadcasted_iota(jnp.int32, (q, q), 1)
    causal = (jj - ii) * (1 - 2 * d) <= 0
    low_half = lax.broadcasted_iota(jnp.int32, (q, hp), 1) < SSM_HEAD_DIM

    heads_per_group = SSM_HEADS // SSM_GROUPS
    n_pairs = SSM_HEADS // 2

    b_gs = [b_ref[:, g * SSM_STATE:(g + 1) * SSM_STATE].astype(BF16) for g in range(SSM_GROUPS)]
    c_bf = [c_ref[:, g * SSM_STATE:(g + 1) * SSM_STATE].astype(BF16) for g in range(SSM_GROUPS)]
    c_gs = [c.astype(F32) for c in c_bf]
    cbs = [jnp.where(causal, _dot_nt(c_bf[g], b_gs[g]), 0.0) for g in range(SSM_GROUPS)]

    cols, lhs = [], []
    for h in range(SSM_HEADS):
        g = h // heads_per_group
        col = jnp.broadcast_to(acum[:, h:h + 1], (q, q))
        seg = jnp.where(causal, col - acum_t[h:h + 1, :], 0.0)
        scores = jnp.exp2(seg) * cbs[g]
        c_dec = c_gs[g] * jnp.exp2(col)
        lhs.append(jnp.concatenate([scores.astype(BF16), c_dec.astype(BF16)], axis=1))
        cols.append(col)

    xdts = []
    for p in range(n_pairs):
        lo = p * hp
        x_pair = x_ref[:, lo:lo + hp].astype(F32)
        xdt = x_pair * dt_wide[:, lo:lo + hp]
        rhs = jnp.concatenate([xdt.astype(BF16), s_ref[:, lo:lo + hp].astype(BF16)], axis=0)
        y_pair = jnp.where(low_half, _dot(lhs[2 * p], rhs), _dot(lhs[2 * p + 1], rhs))
        y_ref[:, lo:lo + hp] = (y_pair + dskip_ref[:, lo:lo + hp] * x_pair).astype(y_ref.dtype)
        xdts.append(xdt)

    for p in range(n_pairs):
        lo = p * hp
        g = (2 * p) // heads_per_group
        tot_pair = jnp.where(low_half[0:1], jnp.broadcast_to(tot[:, 2 * p:2 * p + 1], (1, hp)),
                             jnp.broadcast_to(tot[:, 2 * p + 1:2 * p + 2], (1, hp)))
        to_end = jnp.exp2(tot_pair - jnp.where(low_half, cols[2 * p], cols[2 * p + 1]))
        s_ref[:, lo:lo + hp] = (s_ref[:, lo:lo + hp] * jnp.exp2(tot_pair)
                                + _dot_tn(b_gs[g], (xdts[p] * to_end).astype(BF16)))


def _ssd_scan(xbc, dt_raw, dt_bias, a_log, d_skip):
    q = SSM_CHUNK
    n_lat, n_ctx = SEQ // q, CTX_LEN // q
    pfx, expand = _ssd_scan_consts()
    n_heads2 = 2 * SSM_HEADS
    b_col = SSM_INNER // (SSM_GROUPS * SSM_STATE)
    d_wide = jnp.repeat(d_skip, SSM_HEAD_DIM, axis=1).reshape(2, 1, SSM_INNER)

    def chunk(d, b, s):
        return _scan_chunk_index(d, b, s, n_lat, n_ctx)

    return pl.pallas_call(
        _ssd_scan_body,
        grid=(2, BATCH, n_lat + n_ctx),
        in_specs=[
            pl.BlockSpec((q, SSM_INNER), lambda d, b, s: (chunk(d, b, s), 0)),
            pl.BlockSpec((q, SSM_GROUPS * SSM_STATE), lambda d, b, s: (chunk(d, b, s), b_col)),
            pl.BlockSpec((q, SSM_GROUPS * SSM_STATE), lambda d, b, s: (chunk(d, b, s), b_col + 1)),
            pl.BlockSpec((q, n_heads2), lambda d, b, s: (chunk(d, b, s), 0)),
            pl.BlockSpec((1, n_heads2), lambda d, b, s: (0, 0)),
            pl.BlockSpec((1, n_heads2), lambda d, b, s: (0, 0)),
            pl.BlockSpec((None, 1, SSM_INNER), lambda d, b, s: (d, 0, 0)),
            pl.BlockSpec((None,) + pfx.shape[1:], lambda d, b, s: (d, 0, 0)),
            pl.BlockSpec(expand.shape, lambda d, b, s: (0, 0)),
        ],
        out_specs=pl.BlockSpec((None, q, SSM_INNER), lambda d, b, s: (d, chunk(d, b, s), 0)),
        out_shape=jax.ShapeDtypeStruct((2, T_ALL, SSM_INNER), BF16),
        scratch_shapes=[pltpu.VMEM((SSM_STATE, SSM_INNER), F32)],
        compiler_params=_cparams(("arbitrary", "arbitrary", "arbitrary"), 48),
        name="ssd_scan",
    )(xbc, xbc, xbc, dt_raw, dt_bias.reshape(1, n_heads2), a_log.reshape(1, n_heads2), d_wide, pfx, expand)


GLA_SUM_TERMS = 2


def _gla_consts():
    q = HG_CHUNK
    levels = int(np.log2(q))
    i = np.arange(q)
    mats = [(i[:, None] >= i[None, :]).astype(np.float32), (i[None, :] > i[:, None]).astype(np.float32)]
    masks = [np.eye(q, dtype=np.float32)]
    for lv in range(levels):
        s = 1 << lv
        blk, pos = i // (2 * s), i % (2 * s)
        bound = blk * 2 * s + s - 1
        right = pos >= s
        t = i[None, :]
        m_right = right[:, None] & (t > bound[:, None]) & (t <= i[:, None])
        m_left = (~right)[:, None] & (t > i[:, None]) & (t <= bound[:, None])
        mats.append((m_right | m_left).astype(np.float32))
        masks.append((right[:, None] & (~right)[None, :] & (blk[:, None] == blk[None, :])).astype(np.float32))
    mats.append(np.ones((ONES_ROWS, q), np.float32))
    fwd_m, fwd_k = np.concatenate(mats, axis=0), np.stack(masks)
    flip_rows = np.concatenate([np.arange(k * q, (k + 1) * q)[::-1] for k in range(levels + 2)]
                               + [np.arange((levels + 2) * q, (levels + 2) * q + ONES_ROWS)])
    bwd_m = fwd_m[flip_rows][:, ::-1]
    bwd_k = fwd_k[:, ::-1, ::-1]
    mats2 = np.tile(np.stack([fwd_m, bwd_m]), (1, 1, GLA_SUM_TERMS))
    return (jnp.asarray(mats2, BF16), jnp.asarray(np.stack([fwd_k, bwd_k]), F32), levels)


def _silu_epilogue(u):
    return (_silu(u),)


def _hg_gate_epilogue(layer, u, lbp_ref):
    lbp = lbp_ref[...]
    e = jnp.exp(lbp - jnp.max(lbp, axis=0, keepdims=True))
    sm = e / jnp.sum(e, axis=0, keepdims=True)
    lb = jnp.sum(sm[1:layer + 1], axis=0, keepdims=True) if layer > 0 else jnp.zeros((1, lbp.shape[1]), F32)
    ef = jnp.exp(-jnp.abs(u))
    log_sig = jnp.minimum(u, 0.0) - _log1p_unit(ef)
    la = jnp.log(lb)
    lc = jnp.log1p(-lb) + log_sig
    logf = jnp.maximum(la, lc) + _log1p_unit(jnp.exp(-jnp.abs(la - lc)))
    return logf, 1.0 - jnp.exp(logf)


def _gla_body(levels, q_ref, v_ref, logf_ref, key_ref, mats_ref, masks_ref, o_ref, st_ref):
    q = HG_CHUNK

    @pl.when(pl.program_id(2) == 0)
    def _():
        st_ref[...] = jnp.zeros_like(st_ref)

    qs_bf, key_bf, v_bf = q_ref[...], key_ref[...], v_ref[...]

    sums = _dot_sel(mats_ref[...], logf_ref[...], GLA_SUM_TERMS)
    q_in = qs_bf * jnp.exp(sums[:q]).astype(BF16)
    k_out = key_bf * jnp.exp(sums[q:2 * q]).astype(BF16)
    dec = jnp.exp(sums[(levels + 2) * q:(levels + 2) * q + 1])
    q_lv, k_lv = [], []
    for lv in range(levels):
        fac = jnp.exp(sums[(lv + 2) * q:(lv + 3) * q]).astype(BF16)
        q_lv.append(qs_bf * fac)
        k_lv.append(key_bf * fac)

    heads = [slice(h * LANES, (h + 1) * LANES) for h in range(HG_HEADS)]
    attn = masks_ref[0][None] * jnp.stack([_dot_nt(qs_bf[:, sl], key_bf[:, sl]) for sl in heads], axis=0)
    for lv in range(levels):
        attn = attn + masks_ref[lv + 1][None] * jnp.stack(
            [_dot_nt(q_lv[lv][:, sl], k_lv[lv][:, sl]) for sl in heads], axis=0)
    attn = attn.astype(BF16)
    st_bf = st_ref[...].astype(BF16)
    o_ref[...] = jnp.concatenate(
        [_dot(attn[h], v_bf[:, sl]) + _dot_nt(q_in[:, sl], st_bf[h]) for h, sl in enumerate(heads)],
        axis=1).astype(o_ref.dtype)
    for h, sl in enumerate(heads):
        st_ref[h] = st_ref[h] * dec[:, sl] + _dot_tn(v_bf[:, sl], k_out[:, sl])


def _gla_scan(qs, v, logf, key):
    q = HG_CHUNK
    n_lat, n_ctx = SEQ // q, CTX_LEN // q
    mats, masks, levels = _gla_consts()

    def chunk(d, b, s):
        return _scan_chunk_index(d, b, s, n_lat, n_ctx)

    return pl.pallas_call(
        functools.partial(_gla_body, levels),
        grid=(2, BATCH, n_lat + n_ctx),
        in_specs=[
            pl.BlockSpec((q, D_MODEL), lambda d, b, s: (chunk(d, b, s), 0)),
            pl.BlockSpec((q, D_MODEL), lambda d, b, s: (chunk(d, b, s), 0)),
            pl.BlockSpec((q, D_MODEL), lambda d, b, s: (chunk(d, b, s), d)),
            pl.BlockSpec((q, D_MODEL), lambda d, b, s: (chunk(d, b, s), d)),
            pl.BlockSpec((None,) + mats.shape[1:], lambda d, b, s: (d, 0, 0)),
            pl.BlockSpec((None,) + masks.shape[1:], lambda d, b, s: (d, 0, 0, 0)),
        ],
        out_specs=pl.BlockSpec((None, q, D_MODEL), lambda d, b, s: (d, chunk(d, b, s), 0)),
        out_shape=jax.ShapeDtypeStruct((2, T_ALL, D_MODEL), BF16),
        scratch_shapes=[pltpu.VMEM((HG_HEADS, LANES, HG_KEY_DIM), F32)],
        compiler_params=_cparams(("arbitrary", "arbitrary", "arbitrary"), 48),
        name="gla_scan",
    )(qs, v, logf, key, mats, masks)


def _na_bias_body(rpb_ref, o_ref):
    h = pl.program_id(0)
    n_dc = 2 * NA_WIN_C - 1
    qc = lax.broadcasted_iota(jnp.int32, (GRID_W, LANES), 0)
    kc = lax.broadcasted_iota(jnp.int32, (GRID_W, LANES), 1) % GRID_W
    c0 = jnp.clip(qc - NA_WIN_C // 2, 0, GRID_W - NA_WIN_C)
    col_in = (kc >= c0) & (kc < c0 + NA_WIN_C)
    dc = jnp.clip(kc - qc + NA_WIN_C - 1, 0, n_dc - 1)
    low = lax.broadcasted_iota(jnp.int32, (GRID_W, LANES), 1) < GRID_W
    tiles = []
    for dr in range(2 * NA_WIN_R - 1):
        t = jnp.zeros((GRID_W, LANES), F32)
        for k in range(n_dc):
            t = jnp.where(dc == k, rpb_ref[(h * (2 * NA_WIN_R - 1) + dr) * n_dc + k], t)
        tiles.append(jnp.where(col_in, t, -jnp.inf))
    for dr in range(2 * NA_WIN_R - 2):
        o_ref[dr] = jnp.where(low, tiles[dr], tiles[dr + 1])


def _na_bias_table(rpb):
    n_dr = 2 * NA_WIN_R - 2
    return pl.pallas_call(
        _na_bias_body,
        grid=(NA_HEADS,),
        in_specs=[pl.BlockSpec(memory_space=pltpu.SMEM)],
        out_specs=pl.BlockSpec((None, n_dr, GRID_W, LANES), lambda h: (h, 0, 0, 0)),
        out_shape=jax.ShapeDtypeStruct((NA_HEADS, n_dr, GRID_W, LANES), F32),
        compiler_params=_cparams(("arbitrary",), 32),
        name="na_bias",
    )(rpb.reshape(-1))


def _rope_tables():
    quarter = NA_HEAD_DIM // 4
    inv = ROPE_BASE ** (-np.arange(quarter, dtype=np.float64) / quarter)
    t = np.arange(SEQ)
    row, col = t // GRID_W, t % GRID_W
    ang = np.concatenate([row[:, None] * inv[None], row[:, None] * inv[None],
                          col[:, None] * inv[None], col[:, None] * inv[None]], axis=1)
    sign = np.tile(np.concatenate([-np.ones(quarter), np.ones(quarter)]), 2)[None]
    return jnp.asarray(np.cos(ang), F32), jnp.asarray(np.sin(ang) * sign, F32)


def _rope(v, cos, sin_signed):
    quarter = NA_HEAD_DIM // 4
    lane = lax.broadcasted_iota(jnp.int32, v.shape, 1)
    first = (lane % (2 * quarter)) < quarter
    partner = jnp.where(first, pltpu.roll(v, NA_HEAD_DIM - quarter, axis=1), pltpu.roll(v, quarter, axis=1))
    return v * cos + partner * sin_signed


def _na_body(rows_per_step, q_ref, k_ref, v_ref, kc_ref, vc_ref, cosq_ref, sinq_ref, cos_ref, sin_ref, bias_ref,
             o_ref, kr_ref, vb_ref):
    rb = pl.program_id(2)
    n_rows = SEQ // GRID_W
    scale = NA_HEAD_DIM ** -0.5

    @pl.when(rb == 0)
    def _():
        kr_ref[...] = _rope(k_ref[...], cos_ref[...], sin_ref[...]).astype(BF16)
        vb_ref[...] = v_ref[...].astype(BF16)

    qr = (_rope(q_ref[...], cosq_ref[...], sinq_ref[...]) * scale).astype(BF16)
    kc = kc_ref[...].astype(BF16)
    vc = vc_ref[...].astype(BF16)
    starts, s_rows = [], []
    for lr in range(rows_per_step):
        r = rb * rows_per_step + lr
        r0 = jnp.clip(r - NA_WIN_R // 2, 0, n_rows - NA_WIN_R)
        start = pl.multiple_of(r0 * GRID_W, GRID_W)
        dr0 = r0 - r + NA_WIN_R - 1
        bias = jnp.concatenate([bias_ref[dr0 + 2 * p] for p in range(NA_WIN_R // 2)], axis=1)
        k_win = kr_ref[pl.ds(start, NA_WIN_R * GRID_W), :]
        s_rows.append(_dot_nt(qr[lr * GRID_W:(lr + 1) * GRID_W], k_win) + bias)
        starts.append(start)
    s_loc = jnp.concatenate(s_rows, axis=0)
    s_ctx = _dot_nt(qr, kc)
    m = jnp.maximum(jnp.max(s_loc, axis=-1, keepdims=True), jnp.max(s_ctx, axis=-1, keepdims=True))
    p_loc = jnp.exp(s_loc - m)
    p_ctx = jnp.exp(s_ctx - m)
    denom = jnp.sum(p_loc, axis=-1, keepdims=True) + jnp.sum(p_ctx, axis=-1, keepdims=True)
    p_loc = p_loc.astype(BF16)
    o_loc = jnp.concatenate(
        [_dot(p_loc[lr * GRID_W:(lr + 1) * GRID_W], vb_ref[pl.ds(starts[lr], NA_WIN_R * GRID_W), :])
         for lr in range(rows_per_step)], axis=0)
    o = o_loc + _dot(p_ctx.astype(BF16), vc)
    o_ref[...] = (o / denom).astype(o_ref.dtype)


def _na_attention(qkv, bias_tbl):
    rows_per_step = 64
    tq = rows_per_step * GRID_W
    steps = SEQ // tq
    cos, sin = _rope_tables()
    hd = NA_HEAD_DIM
    ctx_blk0 = T_LAT // CTX_LEN
    return pl.pallas_call(
        functools.partial(_na_body, rows_per_step),
        grid=(BATCH, NA_HEADS, steps),
        in_specs=[
            pl.BlockSpec((tq, hd), lambda b, h, r: (b * steps + r, h)),
            pl.BlockSpec((SEQ, hd), lambda b, h, r: (b, NA_HEADS + h)),
            pl.BlockSpec((SEQ, hd), lambda b, h, r: (b, 2 * NA_HEADS + h)),
            pl.BlockSpec((CTX_LEN, hd), lambda b, h, r: (ctx_blk0 + b, NA_HEADS + h)),
            pl.BlockSpec((CTX_LEN, hd), lambda b, h, r: (ctx_blk0 + b, 2 * NA_HEADS + h)),
            pl.BlockSpec((tq, hd), lambda b, h, r: (r, 0)),
            pl.BlockSpec((tq, hd), lambda b, h, r: (r, 0)),
            pl.BlockSpec((SEQ, hd), lambda b, h, r: (0, 0)),
            pl.BlockSpec((SEQ, hd), lambda b, h, r: (0, 0)),
            pl.BlockSpec((None,) + bias_tbl.shape[1:], lambda b, h, r: (h, 0, 0, 0)),
        ],
        out_specs=pl.BlockSpec((tq, hd), lambda b, h, r: (b * steps + r, h)),
        out_shape=jax.ShapeDtypeStruct((T_LAT, D_MODEL), BF16),
        scratch_shapes=[pltpu.VMEM((SEQ, hd), BF16), pltpu.VMEM((SEQ, hd), BF16)],
        compiler_params=_cparams(("arbitrary", "arbitrary", "arbitrary"), 48),
        name="na_attention",
    )(qkv, qkv, qkv, qkv, qkv, cos, sin, cos, sin, bias_tbl)


def _ctx_attn_body(q_ref, k_ref, v_ref, o_ref):
    scale = NA_HEAD_DIM ** -0.5
    s = _dot_nt((q_ref[...] * scale).astype(BF16), k_ref[...].astype(BF16))
    p = jnp.exp(s - jnp.max(s, axis=-1, keepdims=True))
    o = _dot(p.astype(BF16), v_ref[...].astype(BF16))
    o_ref[...] = (o / jnp.sum(p, axis=-1, keepdims=True)).astype(o_ref.dtype)


def _ctx_attention(qkv):
    hd = NA_HEAD_DIM
    ctx_blk0 = T_LAT // CTX_LEN
    return pl.pallas_call(
        _ctx_attn_body,
        grid=(BATCH, NA_HEADS),
        in_specs=[
            pl.BlockSpec((CTX_LEN, hd), lambda b, h: (ctx_blk0 + b, h)),
            pl.BlockSpec((CTX_LEN, hd), lambda b, h: (ctx_blk0 + b, NA_HEADS + h)),
            pl.BlockSpec((CTX_LEN, hd), lambda b, h: (ctx_blk0 + b, 2 * NA_HEADS + h)),
        ],
        out_specs=pl.BlockSpec((CTX_LEN, hd), lambda b, h: (b, h)),
        out_shape=jax.ShapeDtypeStruct((T_CTX, D_MODEL), BF16),
        compiler_params=_cparams(("arbitrary", "arbitrary"), 32),
        name="ctx_attention",
    )(qkv, qkv, qkv)


def kernel(x, c, ctx, c_ctx, w_mod, b_mod, norm_g, ffn_w_up, ffn_conv_w, ffn_conv_b, ffn_w_down, ssm_w_in, ssm_conv_w, ssm_conv_b, ssm_dt_bias, ssm_a_log, ssm_d, ssm_norm_g, ssm_w_out, hg_w_in, hg_lb, hg_norm_g, hg_w_out, na_w_qkv, na_rpb, na_w_out):
    cvec8 = jnp.concatenate([c, c_ctx[None], jnp.zeros((SUBLANES - BATCH - 1, D_MODEL), F32)], axis=0)
    mod = _adaln(cvec8, w_mod, b_mod)
    ffn_up_bf, ffn_down_bf = ffn_w_up.astype(BF16), ffn_w_down.astype(BF16)
    ssm_out_bf, hg_out_bf, na_out_bf = ssm_w_out.astype(BF16), hg_w_out.astype(BF16), na_w_out.astype(BF16)

    xs, h = _prep(x.reshape(T_LAT, D_MODEL), ctx.reshape(T_CTX, D_MODEL), norm_g[0, 0], mod[0])
    for i in range(DEPTH):
        last = i == DEPTH - 1
        kind, slot = i % N_MIXERS, i // N_MIXERS
        mod_l = mod[i]
        n_rows = T_LAT if last else T_ALL
        op = dict(xs=xs, g_res=norm_g[i, 1], g_ffn=norm_g[i, 2], mod_l=mod_l, n_rows=n_rows, slot=slot)
        if kind == 0:
            tn = 1024
            z = _in_proj(h, ssm_w_in, slot, tn, "ssd_in_z", n_dim=SSM_INNER)
            xbc = _in_proj_conv(h, ssm_w_in, slot, ssm_conv_w[slot], ssm_conv_b[slot], tn, SSM_INNER // tn,
                                "ssd_in_xbc")
            dt_raw = _in_proj(h, ssm_w_in, slot, 2 * SSM_HEADS, "ssd_dt", n_dim=2 * SSM_HEADS,
                              first_tile=(SSM_INNER + SSM_CONV_DIM) // (2 * SSM_HEADS))
            y2 = _ssd_scan(xbc, dt_raw, ssm_dt_bias[slot], ssm_a_log[slot], ssm_d[slot])
            xs, hf = _out_proj(_ssd_gate_prologue, [((y2, 0), 0), ((y2, 1), 0), (z, 0)],
                               ssm_norm_g[slot].reshape(1, SSM_INNER), ssm_out_bf, tm=256, name="ssd_out", **op)
        elif kind == 1:
            tn = 1024
            per = D_MODEL // tn
            qs = _in_proj(h, hg_w_in, slot, tn, "hg_in_q", n_dim=D_MODEL, epilogue=_silu_epilogue,
                          out_dtypes=(BF16,))
            v = _in_proj(h, hg_w_in, slot, tn, "hg_in_v", n_dim=D_MODEL, first_tile=per, out_dtypes=(BF16,))
            lb_spec = pl.BlockSpec((None, DEPTH, tn), lambda j, r: (j // per, 0, j % per))
            logf, key = _in_proj(h, hg_w_in, slot, tn, "hg_in_f", n_dim=2 * D_MODEL, first_tile=2 * per,
                                 epilogue=functools.partial(_hg_gate_epilogue, i), aux=[(hg_lb, lb_spec)],
                                 out_dtypes=(F32, BF16))
            gate = _in_proj(h, hg_w_in, slot, tn, "hg_in_g", n_dim=D_MODEL, first_tile=4 * per)
            o2 = _gla_scan(qs, v, logf, key)
            xs, hf = _out_proj(_hg_readout_prologue, [((o2, 0), 0), ((o2, 1), 0), (gate, 0)],
                               hg_norm_g[slot].reshape(1, D_MODEL), hg_out_bf, tm=ROW_TILE, name="hg_out", **op)
        else:
            qkv = _in_proj(h, na_w_qkv, slot, 1024, "na_in")
            o_lat = _na_attention(qkv, _na_bias_table(na_rpb[slot]))
            o_ctx = _ctx_attention(qkv)
            assert T_CTX == ROW_TILE
            last_lat = T_LAT // ROW_TILE - 1
            xs, hf = _out_proj(_na_prologue, [(o_lat, 0, functools.partial(jnp.minimum, last_lat)),
                                              (o_ctx, 0, lambda t: 0)],
                               None, na_out_bf, tm=ROW_TILE, name="na_out", **op)
        nxt = min(i + 1, DEPTH - 1)
        xs, h = _ffn(hf, xs, i, ffn_up_bf, ffn_conv_w, ffn_conv_b, ffn_down_bf, norm_g[i, 3], mod_l,
                     norm_g[nxt, 0], mod[nxt], n_rows=n_rows, emit_h=not last)

    return xs.reshape(BATCH, SEQ, D_MODEL)
```

```python
import functools

import numpy as np
import jax
import jax.numpy as jnp
from jax import lax
from jax.experimental import pallas as pl
from jax.experimental.pallas import tpu as pltpu

F32 = jnp.float32
BF16 = jnp.bfloat16

D_MODEL = 2048
BATCH = 2
SEQ = 4096
DEPTH = 4
GRID_W = 64
CTX_LEN = 256
N_MIXERS = 3
RMS_EPS = 1e-6
SSM_INNER = 4096
SSM_HEAD_DIM = 64
SSM_HEADS = 64
SSM_STATE = 128
SSM_GROUPS = 8
SSM_CONV_DIM = SSM_INNER + 2 * SSM_GROUPS * SSM_STATE
SSM_IN_DIM = SSM_INNER + SSM_CONV_DIM + 2 * SSM_HEADS
SSM_CHUNK = 128
HG_HEADS = 16
HG_KEY_DIM = 128
HG_IN_DIM = 5 * D_MODEL
HG_CHUNK = 64
NA_HEADS = 16
NA_HEAD_DIM = 128
NA_WIN_R = 8
NA_WIN_C = 16
ROPE_BASE = 10000.0
FFN_HIDDEN = 5632

T_LAT = BATCH * SEQ
T_CTX = BATCH * CTX_LEN
T_ALL = T_LAT + T_CTX
LANES = 128
SUBLANES = 8
BF16_ROWS = 16
ONES_ROWS = 16
ROW_TILE = 512
FFN_COL_CHUNK = 512
MIB = 1024 * 1024
LOG2E = 1.4426950408889634


def _cparams(sem, vmem_mib):
    return pltpu.CompilerParams(dimension_semantics=sem, vmem_limit_bytes=vmem_mib * MIB)


def _silu(v):
    return v * jax.nn.sigmoid(v)


def _rms(v):
    return v * lax.rsqrt(jnp.mean(v * v, axis=-1, keepdims=True) + RMS_EPS)


def _log1p_unit(e):
    return jnp.log(1.0 + e)


def _softplus(v):
    return jnp.maximum(v, 0.0) + _log1p_unit(jnp.exp(-jnp.abs(v)))


def _dot(a, b):
    return jnp.dot(a, b, preferred_element_type=F32)


def _dot_nt(a, b):
    return lax.dot_general(a, b, (((1,), (1,)), ((), ())), preferred_element_type=F32)


def _dot_tn(a, b):
    return lax.dot_general(a, b, (((0,), (0,)), ((), ())), preferred_element_type=F32)


def _bf16_pieces(v, terms):
    pieces = []
    rem = v
    for _ in range(terms):
        part = rem.astype(BF16)
        rem = rem - part.astype(F32)
        pieces.append(part)
    return pieces


def _dot_sel(m01_rep, v, terms):
    return _dot(m01_rep, jnp.concatenate(_bf16_pieces(v, terms), axis=0))


def _dot_sel_rhs(v, m01_rep, terms):
    return _dot(jnp.concatenate(_bf16_pieces(v, terms), axis=1), m01_rep)


def _mod_row(row0):
    return jnp.where(row0 >= T_LAT, 2, row0 // SEQ)


def _mod_vec(mod_ref, r, idx):
    return mod_ref[pl.ds(r, 1), idx * D_MODEL:(idx + 1) * D_MODEL]


def _norm_mod(x, g, mod_ref, r, shift_idx, scale_idx):
    return _rms(x) * (g * (1.0 + _mod_vec(mod_ref, r, scale_idx))) + _mod_vec(mod_ref, r, shift_idx)


def _seg_edges(row0, rows):
    gr = row0 + lax.broadcasted_iota(jnp.int32, (rows, 1), 0)
    pos = jnp.where(gr >= T_LAT, (gr - T_LAT) % CTX_LEN, gr % SEQ)
    length = jnp.where(gr >= T_LAT, CTX_LEN, SEQ)
    return pos == 0, pos == length - 1


def _shift_rows(cur, prev_row, next_row, row0):
    rows = cur.shape[0]
    first, last = _seg_edges(row0, rows)
    sub = lax.broadcasted_iota(jnp.int32, (SUBLANES, 1), 0)
    up = pltpu.roll(cur, 1, axis=0)
    up = jnp.concatenate([jnp.where(sub == 0, prev_row, up[:SUBLANES]), up[SUBLANES:]], axis=0)
    up = jnp.where(first, 0.0, up)
    dn = pltpu.roll(cur, rows - 1, axis=0)
    dn = jnp.concatenate([dn[:rows - SUBLANES], jnp.where(sub == SUBLANES - 1, next_row, dn[rows - SUBLANES:])],
                         axis=0)
    dn = jnp.where(last, 0.0, dn)
    return up, dn


def _token_conv(cur, prev_row, next_row, row0, w_ref, b_ref):
    up, dn = _shift_rows(cur, prev_row, next_row, row0)
    return b_ref[...] + up * w_ref[0:1, :] + cur * w_ref[1:2, :] + dn * w_ref[2:3, :]


def _halo_specs(tm, halo, width, n_rows, col_fn):
    per = tm // halo
    last_blk = n_rows // halo - 1
    prev = pl.BlockSpec((halo, width), lambda i, j: (jnp.maximum(i * per - 1, 0), col_fn(j)))
    nxt = pl.BlockSpec((halo, width), lambda i, j: (jnp.minimum((i + 1) * per, last_blk), col_fn(j)))
    return prev, nxt


def _adaln_body(c_ref, w_ref, b_ref, o_ref):
    s = _silu(c_ref[...]).astype(BF16)
    o_ref[...] = _dot(s, w_ref[...].astype(BF16)) + b_ref[...]


def _adaln(cvec8, w_mod, b_mod):
    tn = 1024
    n_out = 6 * D_MODEL
    return pl.pallas_call(
        _adaln_body,
        grid=(DEPTH, n_out // tn),
        in_specs=[
            pl.BlockSpec((SUBLANES, D_MODEL), lambda l, j: (0, 0)),
            pl.BlockSpec((None, D_MODEL, tn), lambda l, j: (l, 0, j)),
            pl.BlockSpec((None, 1, tn), lambda l, j: (l, 0, j)),
        ],
        out_specs=pl.BlockSpec((None, SUBLANES, tn), lambda l, j: (l, 0, j)),
        out_shape=jax.ShapeDtypeStruct((DEPTH, SUBLANES, n_out), F32),
        compiler_params=_cparams(("arbitrary", "arbitrary"), 40),
        name="adaln",
    )(cvec8, w_mod, b_mod.reshape(DEPTH, 1, n_out))


def _prep_body(lat_ref, ctx_ref, g_ref, mod_ref, xs_ref, h_ref):
    row0 = pl.program_id(0) * ROW_TILE

    def emit(src_ref):
        xv = src_ref[...]
        xs_ref[...] = xv
        h_ref[...] = _norm_mod(xv, g_ref[...], mod_ref, _mod_row(row0), 0, 1).astype(BF16)

    @pl.when(row0 < T_LAT)
    def _():
        emit(lat_ref)

    @pl.when(row0 >= T_LAT)
    def _():
        emit(ctx_ref)


def _prep(x_lat, x_ctx, g_row, mod_l):
    assert T_CTX == ROW_TILE
    n_lat = T_LAT // ROW_TILE
    row = pl.BlockSpec((ROW_TILE, D_MODEL), lambda i: (i, 0))
    return pl.pallas_call(
        _prep_body,
        grid=(T_ALL // ROW_TILE,),
        in_specs=[pl.BlockSpec((ROW_TILE, D_MODEL), lambda i: (jnp.minimum(i, n_lat - 1), 0)),
                  pl.BlockSpec((ROW_TILE, D_MODEL), lambda i: (0, 0)),
                  pl.BlockSpec((1, D_MODEL), lambda i: (0, 0)), pl.BlockSpec(mod_l.shape, lambda i: (0, 0))],
        out_specs=[row, row],
        out_shape=[jax.ShapeDtypeStruct((T_ALL, D_MODEL), F32), jax.ShapeDtypeStruct((T_ALL, D_MODEL), BF16)],
        compiler_params=_cparams(("arbitrary",), 40),
        name="prep",
    )(x_lat, x_ctx, g_row.reshape(1, D_MODEL), mod_l)


def _in_proj_body(epilogue, n_aux, h_ref, w_ref, *rest):
    aux, outs, wb_ref = rest[:n_aux], rest[n_aux:-1], rest[-1]

    @pl.when(pl.program_id(1) == 0)
    def _():
        wb_ref[...] = w_ref[...].astype(BF16)

    u = _dot(h_ref[...], wb_ref[...])
    vals = (u,) if epilogue is None else epilogue(u, *aux)
    for o_ref, val in zip(outs, vals):
        o_ref[...] = val.astype(o_ref.dtype)


def _in_proj(h, w_stack, slot, tn, name, n_dim=None, first_tile=0, epilogue=None, aux=(), out_dtypes=(F32,)):
    tm = T_ALL // 8
    n_dim = w_stack.shape[2] if n_dim is None else n_dim
    res = pl.pallas_call(
        functools.partial(_in_proj_body, epilogue, len(aux)),
        grid=(n_dim // tn, T_ALL // tm),
        in_specs=[pl.BlockSpec((tm, D_MODEL), lambda j, i: (i, 0)),
                  pl.BlockSpec((None, D_MODEL, tn), lambda j, i: (slot, 0, first_tile + j))]
        + [spec for _, spec in aux],
        out_specs=[pl.BlockSpec((tm, tn), lambda j, i: (i, j)) for _ in out_dtypes],
        out_shape=[jax.ShapeDtypeStruct((T_ALL, n_dim), dt) for dt in out_dtypes],
        scratch_shapes=[pltpu.VMEM((D_MODEL, tn), BF16)],
        compiler_params=_cparams(("arbitrary", "arbitrary"), 56),
        name=name,
    )(h, w_stack, *[a for a, _ in aux])
    return res[0] if len(out_dtypes) == 1 else tuple(res)


def _ssd_gate_prologue(row0, yf_ref, yb_ref, z_ref, g_ref):
    v = (yf_ref[...].astype(F32) + yb_ref[...].astype(F32)) * _silu(z_ref[...])
    return v * g_ref[...], lax.rsqrt(jnp.mean(v * v, axis=-1, keepdims=True) + RMS_EPS)


def _hg_readout_prologue(row0, of_ref, ob_ref, gate_ref, g_ref):
    o = of_ref[...].astype(F32) + ob_ref[...].astype(F32)
    parts = [_rms(o[:, h * LANES:(h + 1) * LANES]) for h in range(o.shape[1] // LANES)]
    return jnp.concatenate(parts, axis=-1) * g_ref[...] * _silu(gate_ref[...]), None


def _na_prologue(row0, o_lat_ref, o_ctx_ref):
    return jnp.where(row0 >= T_LAT, o_ctx_ref[...], o_lat_ref[...]), None


def _out_proj_body(prologue, n_in, tm, *refs):
    ins = refs[:n_in]
    w_ref, x_ref, g1_ref, g2_ref, mod_ref, xo_ref, ho_ref = refs[n_in:]
    part, row_scale = prologue(pl.program_id(0) * tm, *ins)
    y = _dot(part.astype(BF16), w_ref[...])
    if row_scale is not None:
        y = y * row_scale
    r = _mod_row(pl.program_id(0) * tm)
    xn = x_ref[...] + _rms(y) * (_mod_vec(mod_ref, r, 2) * g1_ref[...])
    xo_ref[...] = xn
    ho_ref[...] = _norm_mod(xn, g2_ref[...], mod_ref, r, 3, 4).astype(BF16)


def _out_proj(prologue, row_ins, gain, w_stack, slot, xs, g_res, g_ffn, mod_l, *, n_rows, tm, name):
    k_dim = w_stack.shape[1]
    in_specs, args = [], []
    for arr, cb, *row_fn in row_ins:
        row_fn = row_fn[0] if row_fn else (lambda i: i)
        if isinstance(arr, tuple):
            in_specs.append(pl.BlockSpec((None, tm, k_dim), functools.partial(
                lambda i, ld, cb, row_fn: (ld, row_fn(i), cb), ld=arr[1], cb=cb, row_fn=row_fn)))
            args.append(arr[0])
        else:
            in_specs.append(pl.BlockSpec((tm, k_dim), functools.partial(
                lambda i, cb, row_fn: (row_fn(i), cb), cb=cb, row_fn=row_fn)))
            args.append(arr)
    if gain is not None:
        in_specs.append(pl.BlockSpec((1, k_dim), lambda i: (0, 0)))
        args.append(gain)
    n_in = len(args)
    row = pl.BlockSpec((tm, D_MODEL), lambda i: (i, 0))
    vec = pl.BlockSpec((1, D_MODEL), lambda i: (0, 0))
    in_specs += [pl.BlockSpec((None, k_dim, D_MODEL), lambda i: (slot, 0, 0), pipeline_mode=pl.Buffered(1)),
                 row, vec, vec, pl.BlockSpec(mod_l.shape, lambda i: (0, 0))]
    return pl.pallas_call(
        functools.partial(_out_proj_body, prologue, n_in, tm),
        grid=(n_rows // tm,),
        in_specs=in_specs,
        out_specs=[row, row],
        out_shape=[jax.ShapeDtypeStruct((n_rows, D_MODEL), F32), jax.ShapeDtypeStruct((n_rows, D_MODEL), BF16)],
        compiler_params=_cparams(("arbitrary",), 60),
        name=name,
    )(*args, w_stack, xs, g_res.reshape(1, D_MODEL), g_ffn.reshape(1, D_MODEL), mod_l)


def _ffn_body(emit_h, has_ctx, h_ref, hp_ref, hn_ref, wa_ref, wv_ref, cwa_ref, cwv_ref, cba_ref, cbv_ref, wd_ref,
              x_ref, g_res_ref, mod_ref, g_next_ref, modn_ref, *out_and_scratch):
    if emit_h:
        xo_ref, ho_ref, hall_ref, acc_ref = out_and_scratch
    else:
        xo_ref, hall_ref, acc_ref = out_and_scratch
    k = pl.program_id(1)
    tm = ROW_TILE
    row0 = pl.program_id(0) * tm
    pad = BF16_ROWS
    rows = tm + 2 * pad
    s8 = SUBLANES

    @pl.when(k == 0)
    def _():
        hall_ref[pl.ds(0, pad), :] = hp_ref[...]
        hall_ref[pl.ds(pad, tm), :] = h_ref[...]
        hall_ref[pl.ds(pad + tm, pad), :] = hn_ref[...]
        acc_ref[...] = jnp.zeros_like(acc_ref)

    hall = hall_ref[...]
    edge_first = row0 % SEQ == 0
    edge_last = ((row0 + tm) % SEQ == 0) | (row0 + tm == T_ALL)
    is_ctx = row0 >= T_LAT
    joints = list(range(CTX_LEN, tm, CTX_LEN)) if has_ctx else []
    sub = lax.broadcasted_iota(jnp.int32, (s8, 1), 0)

    def patch(v, start, cond):
        return jnp.concatenate([v[:start], jnp.where(cond, 0.0, v[start:start + s8]), v[start + s8:]], axis=0)

    def conv(w_ref, cw_ref, cb_ref, cs):
        u = _dot(hall, w_ref[:, cs])
        u = patch(u, pad - s8, edge_first)
        u = patch(u, pad + tm, edge_last)
        up = pltpu.roll(u, 1, axis=0)[pad:pad + tm]
        dn = pltpu.roll(u, rows - 1, axis=0)[pad:pad + tm]
        for j in joints:
            up = patch(up, j, is_ctx & (sub == 0))
            dn = patch(dn, j - s8, is_ctx & (sub == s8 - 1))
        return cb_ref[:, cs] + up * cw_ref[0:1, cs] + u[pad:pad + tm] * cw_ref[1:2, cs] + dn * cw_ref[2:3, cs]

    chunks = [pl.ds(c * FFN_COL_CHUNK, FFN_COL_CHUNK) for c in range(wa_ref.shape[1] // FFN_COL_CHUNK)]
    acts = [(_silu(conv(wa_ref, cwa_ref, cba_ref, cs)) * conv(wv_ref, cwv_ref, cbv_ref, cs)).astype(BF16)
            for cs in chunks]
    acc_ref[...] += _dot(jnp.concatenate(acts, axis=1), wd_ref[...])

    @pl.when(k == pl.num_programs(1) - 1)
    def _():
        r = _mod_row(row0)
        xn = x_ref[...] + _rms(acc_ref[...]) * (_mod_vec(mod_ref, r, 5) * g_res_ref[...])
        xo_ref[...] = xn
        if emit_h:
            ho_ref[...] = _norm_mod(xn, g_next_ref[...], modn_ref, r, 0, 1).astype(BF16)


def _ffn(h, xs, layer, w_up, conv_w, conv_b, w_down, g_res, mod_l, g_next, mod_next, *, n_rows, emit_h):
    tm, tk = ROW_TILE, 512
    kt = FFN_HIDDEN // tk
    h_prev, h_next = _halo_specs(tm, BF16_ROWS, D_MODEL, n_rows, lambda k: 0)
    row = pl.BlockSpec((tm, D_MODEL), lambda i, k: (i, 0))
    vec = pl.BlockSpec((1, D_MODEL), lambda i, k: (0, 0))
    tab = pl.BlockSpec(mod_l.shape, lambda i, k: (0, 0))
    out_specs = [row, row] if emit_h else [row]
    out_shape = [jax.ShapeDtypeStruct((n_rows, D_MODEL), F32)]
    if emit_h:
        out_shape.append(jax.ShapeDtypeStruct((n_rows, D_MODEL), BF16))
    conv_b3 = conv_b.reshape(DEPTH, 1, 2 * FFN_HIDDEN)

    res = pl.pallas_call(
        functools.partial(_ffn_body, emit_h, n_rows > T_LAT),
        grid=(n_rows // tm, kt),
        in_specs=[
            row, h_prev, h_next,
            pl.BlockSpec((None, D_MODEL, tk), lambda i, k: (layer, 0, k)),
            pl.BlockSpec((None, D_MODEL, tk), lambda i, k: (layer, 0, k + kt)),
            pl.BlockSpec((None, 3, tk), lambda i, k: (layer, 0, k)),
            pl.BlockSpec((None, 3, tk), lambda i, k: (layer, 0, k + kt)),
            pl.BlockSpec((None, 1, tk), lambda i, k: (layer, 0, k)),
            pl.BlockSpec((None, 1, tk), lambda i, k: (layer, 0, k + kt)),
            pl.BlockSpec((None, tk, D_MODEL), lambda i, k: (layer, k, 0)),
            row, vec, tab, vec, tab,
        ],
        out_specs=out_specs,
        out_shape=out_shape,
        scratch_shapes=[pltpu.VMEM((tm + 2 * BF16_ROWS, D_MODEL), BF16), pltpu.VMEM((tm, D_MODEL), F32)],
        compiler_params=_cparams(("arbitrary", "arbitrary"), 56),
        name="conv_ffn",
    )(h, h, h, w_up, w_up, conv_w, conv_w, conv_b3, conv_b3, w_down, xs, g_res.reshape(1, D_MODEL), mod_l,
      g_next.reshape(1, D_MODEL), mod_next)
    return (res[0], res[1]) if emit_h else (res[0], None)


def _in_proj_conv_body(tm, h_ref, hp_ref, hn_ref, w_ref, cw_ref, cb_ref, o_ref, wb_ref):
    @pl.when(pl.program_id(1) == 0)
    def _():
        wb_ref[...] = w_ref[...].astype(BF16)

    w = wb_ref[...]
    u = _dot(h_ref[...], w)
    uh = _dot(jnp.concatenate([hp_ref[...], hn_ref[...]], axis=0), w)
    conv = _token_conv(u, uh[BF16_ROWS - 1:BF16_ROWS], uh[BF16_ROWS:BF16_ROWS + 1], pl.program_id(1) * tm,
                       cw_ref, cb_ref)
    o_ref[...] = _silu(conv).astype(o_ref.dtype)


def _in_proj_conv(h, w_stack, slot, conv_w, conv_b, tn, first_tile, name):
    tm = T_ALL // 8
    n_dim = conv_w.shape[1]
    per = tm // BF16_ROWS
    last_blk = T_ALL // BF16_ROWS - 1
    return pl.pallas_call(
        functools.partial(_in_proj_conv_body, tm),
        grid=(n_dim // tn, T_ALL // tm),
        in_specs=[pl.BlockSpec((tm, D_MODEL), lambda j, i: (i, 0)),
                  pl.BlockSpec((BF16_ROWS, D_MODEL), lambda j, i: (jnp.maximum(i * per - 1, 0), 0)),
                  pl.BlockSpec((BF16_ROWS, D_MODEL), lambda j, i: (jnp.minimum((i + 1) * per, last_blk), 0)),
                  pl.BlockSpec((None, D_MODEL, tn), lambda j, i: (slot, 0, first_tile + j)),
                  pl.BlockSpec((3, tn), lambda j, i: (0, j)),
                  pl.BlockSpec((1, tn), lambda j, i: (0, j))],
        out_specs=pl.BlockSpec((tm, tn), lambda j, i: (i, j)),
        out_shape=jax.ShapeDtypeStruct((T_ALL, n_dim), BF16),
        scratch_shapes=[pltpu.VMEM((D_MODEL, tn), BF16)],
        compiler_params=_cparams(("arbitrary", "arbitrary"), 56),
        name=name,
    )(h, h, h, w_stack, conv_w, conv_b.reshape(1, -1))


def _scan_chunk_index(d, b, s, n_lat, n_ctx):
    ctx0 = BATCH * n_lat + b * n_ctx
    fwd = jnp.where(s < n_ctx, ctx0 + s, b * n_lat + s - n_ctx)
    bwd = jnp.where(s < n_ctx, ctx0 + n_ctx - 1 - s, b * n_lat + n_lat - 1 - (s - n_ctx))
    return jnp.where(d == 0, fwd, bwd)


SSD_CUM_TERMS = 3
SSD_DT_TERMS = 2


def _ssd_scan_consts():
    q = SSM_CHUNK
    i = np.arange(q)
    pfx = np.zeros((2, q + ONES_ROWS, q), np.float32)
    pfx[0, :q] = i[:, None] >= i[None, :]
    pfx[1, :q] = i[:, None] <= i[None, :]
    pfx[:, q:] = 1.0
    pfx = np.tile(pfx, (1, 1, SSD_CUM_TERMS))
    expand = np.repeat(np.eye(SSM_HEADS, dtype=np.float32), SSM_HEAD_DIM, axis=1)
    expand = np.tile(expand, (SSD_DT_TERMS, 1))
    return jnp.asarray(pfx, BF16), jnp.asarray(expand, BF16)


def _ssd_scan_body(x_ref, b_ref, c_ref, dtraw_ref, dtb_ref, alog_ref, dskip_ref, pfx_ref, exp_ref, y_ref, s_ref):
    q = SSM_CHUNK
    hp = 2 * SSM_HEAD_DIM
    d = pl.program_id(0)

    @pl.when(pl.program_id(2) == 0)
    def _():
        s_ref[...] = jnp.zeros_like(s_ref)

    def pick(v):
        return jnp.where(d == 0, v[:, :SSM_HEADS], v[:, SSM_HEADS:])

    dt2 = _softplus(dtraw_ref[...] + dtb_ref[...])
    cum2 = _dot_sel(pfx_ref[...], dt2 * (-LOG2E * jnp.exp(alog_ref[...])), SSD_CUM_TERMS)
    acum2 = cum2[:q]
    acum = pick(acum2)
    tot = pick(cum2[q:q + SUBLANES])[0:1]
    acum_t2 = acum2.T
    acum_t = jnp.where(d == 0, acum_t2[:SSM_HEADS], acum_t2[SSM_HEADS:])
    dt_wide = _dot_sel_rhs(pick(dt2), exp_ref[...], SSD_DT_TERMS)

    ii = lax.broadcasted_iota(jnp.int32, (q, q), 0)
    jj = lax.broadcasted_iota(jnp.int32, (q, q), 1)
    causal = (jj - ii) * (1 - 2 * d) <= 0
    low_half = lax.broadcasted_iota(jnp.int32, (q, hp), 1) < SSM_HEAD_DIM

    heads_per_group = SSM_HEADS // SSM_GROUPS
    n_pairs = SSM_HEADS // 2

    b_gs = [b_ref[:, g * SSM_STATE:(g + 1) * SSM_STATE].astype(BF16) for g in range(SSM_GROUPS)]
    c_bf = [c_ref[:, g * SSM_STATE:(g + 1) * SSM_STATE].astype(BF16) for g in range(SSM_GROUPS)]
    c_gs = [c.astype(F32) for c in c_bf]
    cbs = [jnp.where(causal, _dot_nt(c_bf[g], b_gs[g]), 0.0) for g in range(SSM_GROUPS)]

    lhs, xdts, xws, decs = [], [], [], []
    for p in range(n_pairs):
        lo = p * hp
        g = (2 * p) // heads_per_group
        cols = []
        for h in (2 * p, 2 * p + 1):
            col = jnp.broadcast_to(acum[:, h:h + 1], (q, q))
            seg = jnp.where(causal, col - acum_t[h:h + 1, :], 0.0)
            scores = jnp.exp2(seg) * cbs[g]
            c_dec = c_gs[g] * jnp.exp2(col)
            lhs.append(jnp.concatenate([scores.astype(BF16), c_dec.astype(BF16)], axis=1))
            cols.append(col)
        xdt = x_ref[:, lo:lo + hp].astype(F32) * dt_wide[:, lo:lo + hp]
        tot_pair = jnp.where(low_half[0:1], jnp.broadcast_to(tot[:, 2 * p:2 * p + 1], (1, hp)),
                             jnp.broadcast_to(tot[:, 2 * p + 1:2 * p + 2], (1, hp)))
        to_end = jnp.exp2(tot_pair - jnp.where(low_half, cols[0], cols[1]))
        xdts.append(xdt.astype(BF16))
        xws.append((xdt * to_end).astype(BF16))
        decs.append(jnp.exp2(tot_pair))

    for p in range(n_pairs):
        lo = p * hp
        rhs = jnp.concatenate([xdts[p], s_ref[:, lo:lo + hp].astype(BF16)], axis=0)
        y_pair = jnp.where(low_half, _dot(lhs[2 * p], rhs), _dot(lhs[2 * p + 1], rhs))
        y_ref[:, lo:lo + hp] = (y_pair + dskip_ref[:, lo:lo + hp] * x_ref[:, lo:lo + hp].astype(F32)
                                ).astype(y_ref.dtype)

    for p in range(n_pairs):
        lo = p * hp
        g = (2 * p) // heads_per_group
        s_ref[:, lo:lo + hp] = s_ref[:, lo:lo + hp] * decs[p] + _dot_tn(b_gs[g], xws[p])


def _ssd_scan(xbc, dt_raw, dt_bias, a_log, d_skip):
    q = SSM_CHUNK
    n_lat, n_ctx = SEQ // q, CTX_LEN // q
    pfx, expand = _ssd_scan_consts()
    n_heads2 = 2 * SSM_HEADS
    b_col = SSM_INNER // (SSM_GROUPS * SSM_STATE)
    d_wide = jnp.repeat(d_skip, SSM_HEAD_DIM, axis=1).reshape(2, 1, SSM_INNER)

    def chunk(d, b, s):
        return _scan_chunk_index(d, b, s, n_lat, n_ctx)

    return pl.pallas_call(
        _ssd_scan_body,
        grid=(2, BATCH, n_lat + n_ctx),
        in_specs=[
            pl.BlockSpec((q, SSM_INNER), lambda d, b, s: (chunk(d, b, s), 0)),
            pl.BlockSpec((q, SSM_GROUPS * SSM_STATE), lambda d, b, s: (chunk(d, b, s), b_col)),
            pl.BlockSpec((q, SSM_GROUPS * SSM_STATE), lambda d, b, s: (chunk(d, b, s), b_col + 1)),
            pl.BlockSpec((q, n_heads2), lambda d, b, s: (chunk(d, b, s), 0)),
            pl.BlockSpec((1, n_heads2), lambda d, b, s: (0, 0)),
            pl.BlockSpec((1, n_heads2), lambda d, b, s: (0, 0)),
            pl.BlockSpec((None, 1, SSM_INNER), lambda d, b, s: (d, 0, 0)),
            pl.BlockSpec((None,) + pfx.shape[1:], lambda d, b, s: (d, 0, 0)),
            pl.BlockSpec(expand.shape, lambda d, b, s: (0, 0)),
        ],
        out_specs=pl.BlockSpec((None, q, SSM_INNER), lambda d, b, s: (d, chunk(d, b, s), 0)),
        out_shape=jax.ShapeDtypeStruct((2, T_ALL, SSM_INNER), BF16),
        scratch_shapes=[pltpu.VMEM((SSM_STATE, SSM_INNER), F32)],
        compiler_params=_cparams(("arbitrary", "arbitrary", "arbitrary"), 48),
        name="ssd_scan",
    )(xbc, xbc, xbc, dt_raw, dt_bias.reshape(1, n_heads2), a_log.reshape(1, n_heads2), d_wide, pfx, expand)


GLA_SUM_TERMS = 2


def _gla_consts():
    q = HG_CHUNK
    levels = int(np.log2(q))
    i = np.arange(q)
    mats = [(i[:, None] >= i[None, :]).astype(np.float32), (i[None, :] > i[:, None]).astype(np.float32)]
    masks = [np.eye(q, dtype=np.float32)]
    for lv in range(levels):
        s = 1 << lv
        blk, pos = i // (2 * s), i % (2 * s)
        bound = blk * 2 * s + s - 1
        right = pos >= s
        t = i[None, :]
        m_right = right[:, None] & (t > bound[:, None]) & (t <= i[:, None])
        m_left = (~right)[:, None] & (t > i[:, None]) & (t <= bound[:, None])
        mats.append((m_right | m_left).astype(np.float32))
        masks.append((right[:, None] & (~right)[None, :] & (blk[:, None] == blk[None, :])).astype(np.float32))
    mats.append(np.ones((ONES_ROWS, q), np.float32))
    fwd_m, fwd_k = np.concatenate(mats, axis=0), np.stack(masks)
    flip_rows = np.concatenate([np.arange(k * q, (k + 1) * q)[::-1] for k in range(levels + 2)]
                               + [np.arange((levels + 2) * q, (levels + 2) * q + ONES_ROWS)])
    bwd_m = fwd_m[flip_rows][:, ::-1]
    bwd_k = fwd_k[:, ::-1, ::-1]
    mats2 = np.tile(np.stack([fwd_m, bwd_m]), (1, 1, GLA_SUM_TERMS))
    return (jnp.asarray(mats2, BF16), jnp.asarray(np.stack([fwd_k, bwd_k]), F32), levels)


def _silu_epilogue(u):
    return (_silu(u),)


def _hg_gate_epilogue(layer, u, lbp_ref):
    lbp = lbp_ref[...]
    e = jnp.exp(lbp - jnp.max(lbp, axis=0, keepdims=True))
    sm = e / jnp.sum(e, axis=0, keepdims=True)
    lb = jnp.sum(sm[1:layer + 1], axis=0, keepdims=True) if layer > 0 else jnp.zeros((1, lbp.shape[1]), F32)
    ef = jnp.exp(-jnp.abs(u))
    log_sig = jnp.minimum(u, 0.0) - _log1p_unit(ef)
    la = jnp.log(lb)
    lc = jnp.log1p(-lb) + log_sig
    logf = jnp.maximum(la, lc) + _log1p_unit(jnp.exp(-jnp.abs(la - lc)))
    key = (1.0 - lb) * (jnp.where(u >= 0.0, ef, 1.0) / (1.0 + ef))
    return logf, key


def _gla_body(levels, q_ref, v_ref, logf_ref, key_ref, mats_ref, masks_ref, o_ref, st_ref):
    q = HG_CHUNK

    @pl.when(pl.program_id(2) == 0)
    def _():
        st_ref[...] = jnp.zeros_like(st_ref)

    qs_bf, key_bf, v_bf = q_ref[...], key_ref[...], v_ref[...]

    sums = _dot_sel(mats_ref[...], logf_ref[...], GLA_SUM_TERMS)
    q_in = qs_bf * jnp.exp(sums[:q]).astype(BF16)
    k_out = key_bf * jnp.exp(sums[q:2 * q]).astype(BF16)
    dec = jnp.exp(sums[(levels + 2) * q:(levels + 2) * q + 1])
    q_lv, k_lv = [], []
    for lv in range(levels):
        fac = jnp.exp(sums[(lv + 2) * q:(lv + 3) * q]).astype(BF16)
        q_lv.append(qs_bf * fac)
        k_lv.append(key_bf * fac)

    heads = [slice(h * LANES, (h + 1) * LANES) for h in range(HG_HEADS)]
    attn = masks_ref[0][None] * jnp.stack([_dot_nt(qs_bf[:, sl], key_bf[:, sl]) for sl in heads], axis=0)
    for lv in range(levels):
        attn = attn + masks_ref[lv + 1][None] * jnp.stack(
            [_dot_nt(q_lv[lv][:, sl], k_lv[lv][:, sl]) for sl in heads], axis=0)
    attn = attn.astype(BF16)
    st_bf = st_ref[...].astype(BF16)
    o_ref[...] = jnp.concatenate(
        [_dot(attn[h], v_bf[:, sl]) + _dot_nt(q_in[:, sl], st_bf[h]) for h, sl in enumerate(heads)],
        axis=1).astype(o_ref.dtype)
    for h, sl in enumerate(heads):
        st_ref[h] = st_ref[h] * dec[:, sl] + _dot_tn(v_bf[:, sl], k_out[:, sl])


def _gla_scan(qs, v, logf, key):
    q = HG_CHUNK
    n_lat, n_ctx = SEQ // q, CTX_LEN // q
    mats, masks, levels = _gla_consts()

    def chunk(d, b, s):
        return _scan_chunk_index(d, b, s, n_lat, n_ctx)

    return pl.pallas_call(
        functools.partial(_gla_body, levels),
        grid=(2, BATCH, n_lat + n_ctx),
        in_specs=[
            pl.BlockSpec((q, D_MODEL), lambda d, b, s: (chunk(d, b, s), 0)),
            pl.BlockSpec((q, D_MODEL), lambda d, b, s: (chunk(d, b, s), 0)),
            pl.BlockSpec((q, D_MODEL), lambda d, b, s: (chunk(d, b, s), d)),
            pl.BlockSpec((q, D_MODEL), lambda d, b, s: (chunk(d, b, s), d)),
            pl.BlockSpec((None,) + mats.shape[1:], lambda d, b, s: (d, 0, 0)),
            pl.BlockSpec((None,) + masks.shape[1:], lambda d, b, s: (d, 0, 0, 0)),
        ],
        out_specs=pl.BlockSpec((None, q, D_MODEL), lambda d, b, s: (d, chunk(d, b, s), 0)),
        out_shape=jax.ShapeDtypeStruct((2, T_ALL, D_MODEL), BF16),
        scratch_shapes=[pltpu.VMEM((HG_HEADS, LANES, HG_KEY_DIM), F32)],
        compiler_params=_cparams(("arbitrary", "arbitrary", "arbitrary"), 48),
        name="gla_scan",
    )(qs, v, logf, key, mats, masks)


def _na_bias_body(rpb_ref, o_ref):
    h = pl.program_id(0)
    n_dc = 2 * NA_WIN_C - 1
    qc = lax.broadcasted_iota(jnp.int32, (GRID_W, LANES), 0)
    kc = lax.broadcasted_iota(jnp.int32, (GRID_W, LANES), 1) % GRID_W
    c0 = jnp.clip(qc - NA_WIN_C // 2, 0, GRID_W - NA_WIN_C)
    col_in = (kc >= c0) & (kc < c0 + NA_WIN_C)
    dc = jnp.clip(kc - qc + NA_WIN_C - 1, 0, n_dc - 1)
    low = lax.broadcasted_iota(jnp.int32, (GRID_W, LANES), 1) < GRID_W
    tiles = []
    for dr in range(2 * NA_WIN_R - 1):
        t = jnp.zeros((GRID_W, LANES), F32)
        for k in range(n_dc):
            t = jnp.where(dc == k, rpb_ref[(h * (2 * NA_WIN_R - 1) + dr) * n_dc + k], t)
        tiles.append(jnp.where(col_in, t, -jnp.inf))
    for dr in range(2 * NA_WIN_R - 2):
        o_ref[dr] = jnp.where(low, tiles[dr], tiles[dr + 1])


def _na_bias_table(rpb):
    n_dr = 2 * NA_WIN_R - 2
    return pl.pallas_call(
        _na_bias_body,
        grid=(NA_HEADS,),
        in_specs=[pl.BlockSpec(memory_space=pltpu.SMEM)],
        out_specs=pl.BlockSpec((None, n_dr, GRID_W, LANES), lambda h: (h, 0, 0, 0)),
        out_shape=jax.ShapeDtypeStruct((NA_HEADS, n_dr, GRID_W, LANES), F32),
        compiler_params=_cparams(("arbitrary",), 32),
        name="na_bias",
    )(rpb.reshape(-1))


def _rope_tables():
    quarter = NA_HEAD_DIM // 4
    inv = ROPE_BASE ** (-np.arange(quarter, dtype=np.float64) / quarter)
    t = np.arange(SEQ)
    row, col = t // GRID_W, t % GRID_W
    ang = np.concatenate([row[:, None] * inv[None], row[:, None] * inv[None],
                          col[:, None] * inv[None], col[:, None] * inv[None]], axis=1)
    sign = np.tile(np.concatenate([-np.ones(quarter), np.ones(quarter)]), 2)[None]
    return jnp.asarray(np.cos(ang), F32), jnp.asarray(np.sin(ang) * sign, F32)


def _rope(v, cos, sin_signed):
    quarter = NA_HEAD_DIM // 4
    lane = lax.broadcasted_iota(jnp.int32, v.shape, 1)
    first = (lane % (2 * quarter)) < quarter
    partner = jnp.where(first, pltpu.roll(v, NA_HEAD_DIM - quarter, axis=1), pltpu.roll(v, quarter, axis=1))
    return v * cos + partner * sin_signed


def _na_body(rows_per_step, q_ref, k_ref, v_ref, kc_ref, vc_ref, cosq_ref, sinq_ref, cos_ref, sin_ref, bias_ref,
             o_ref, kr_ref, vb_ref):
    rb = pl.program_id(2)
    n_rows = SEQ // GRID_W
    scale = NA_HEAD_DIM ** -0.5

    @pl.when(rb == 0)
    def _():
        kr_ref[...] = _rope(k_ref[...], cos_ref[...], sin_ref[...]).astype(BF16)
        vb_ref[...] = v_ref[...].astype(BF16)

    qr = (_rope(q_ref[...], cosq_ref[...], sinq_ref[...]) * scale).astype(BF16)
    kc = kc_ref[...].astype(BF16)
    vc = vc_ref[...].astype(BF16)
    starts, s_rows = [], []
    for lr in range(rows_per_step):
        r = rb * rows_per_step + lr
        r0 = jnp.clip(r - NA_WIN_R // 2, 0, n_rows - NA_WIN_R)
        start = pl.multiple_of(r0 * GRID_W, GRID_W)
        dr0 = r0 - r + NA_WIN_R - 1
        bias = jnp.concatenate([bias_ref[dr0 + 2 * p] for p in range(NA_WIN_R // 2)], axis=1)
        k_win = kr_ref[pl.ds(start, NA_WIN_R * GRID_W), :]
        s_rows.append(_dot_nt(qr[lr * GRID_W:(lr + 1) * GRID_W], k_win) + bias)
        starts.append(start)
    s_loc = jnp.concatenate(s_rows, axis=0)
    s_ctx = _dot_nt(qr, kc)
    m = jnp.maximum(jnp.max(s_loc, axis=-1, keepdims=True), jnp.max(s_ctx, axis=-1, keepdims=True))
    p_loc = jnp.exp(s_loc - m)
    p_ctx = jnp.exp(s_ctx - m)
    denom = jnp.sum(p_loc, axis=-1, keepdims=True) + jnp.sum(p_ctx, axis=-1, keepdims=True)
    p_loc = p_loc.astype(BF16)
    o_loc = jnp.concatenate(
        [_dot(p_loc[lr * GRID_W:(lr + 1) * GRID_W], vb_ref[pl.ds(starts[lr], NA_WIN_R * GRID_W), :])
         for lr in range(rows_per_step)], axis=0)
    o = o_loc + _dot(p_ctx.astype(BF16), vc)
    o_ref[...] = (o / denom).astype(o_ref.dtype)


def _na_attention(qkv, bias_tbl):
    rows_per_step = 64
    tq = rows_per_step * GRID_W
    steps = SEQ // tq
    cos, sin = _rope_tables()
    hd = NA_HEAD_DIM
    ctx_blk0 = T_LAT // CTX_LEN
    return pl.pallas_call(
        functools.partial(_na_body, rows_per_step),
        grid=(BATCH, NA_HEADS, steps),
        in_specs=[
            pl.BlockSpec((tq, hd), lambda b, h, r: (b * steps + r, h)),
            pl.BlockSpec((SEQ, hd), lambda b, h, r: (b, NA_HEADS + h)),
            pl.BlockSpec((SEQ, hd), lambda b, h, r: (b, 2 * NA_HEADS + h)),
            pl.BlockSpec((CTX_LEN, hd), lambda b, h, r: (ctx_blk0 + b, NA_HEADS + h)),
            pl.BlockSpec((CTX_LEN, hd), lambda b, h, r: (ctx_blk0 + b, 2 * NA_HEADS + h)),
            pl.BlockSpec((tq, hd), lambda b, h, r: (r, 0)),
            pl.BlockSpec((tq, hd), lambda b, h, r: (r, 0)),
            pl.BlockSpec((SEQ, hd), lambda b, h, r: (0, 0)),
            pl.BlockSpec((SEQ, hd), lambda b, h, r: (0, 0)),
            pl.BlockSpec((None,) + bias_tbl.shape[1:], lambda b, h, r: (h, 0, 0, 0)),
        ],
        out_specs=pl.BlockSpec((tq, hd), lambda b, h, r: (b * steps + r, h)),
        out_shape=jax.ShapeDtypeStruct((T_LAT, D_MODEL), BF16),
        scratch_shapes=[pltpu.VMEM((SEQ, hd), BF16), pltpu.VMEM((SEQ, hd), BF16)],
        compiler_params=_cparams(("arbitrary", "arbitrary", "arbitrary"), 48),
        name="na_attention",
    )(qkv, qkv, qkv, qkv, qkv, cos, sin, cos, sin, bias_tbl)


def _ctx_attn_body(q_ref, k_ref, v_ref, o_ref):
    scale = NA_HEAD_DIM ** -0.5
    s = _dot_nt((q_ref[...] * scale).astype(BF16), k_ref[...].astype(BF16))
    p = jnp.exp(s - jnp.max(s, axis=-1, keepdims=True))
    o = _dot(p.astype(BF16), v_ref[...].astype(BF16))
    o_ref[...] = (o / jnp.sum(p, axis=-1, keepdims=True)).astype(o_ref.dtype)


def _ctx_attention(qkv):
    hd = NA_HEAD_DIM
    ctx_blk0 = T_LAT // CTX_LEN
    return pl.pallas_call(
        _ctx_attn_body,
        grid=(BATCH, NA_HEADS),
        in_specs=[
            pl.BlockSpec((CTX_LEN, hd), lambda b, h: (ctx_blk0 + b, h)),
            pl.BlockSpec((CTX_LEN, hd), lambda b, h: (ctx_blk0 + b, NA_HEADS + h)),
            pl.BlockSpec((CTX_LEN, hd), lambda b, h: (ctx_blk0 + b, 2 * NA_HEADS + h)),
        ],
        out_specs=pl.BlockSpec((CTX_LEN, hd), lambda b, h: (b, h)),
        out_shape=jax.ShapeDtypeStruct((T_CTX, D_MODEL), BF16),
        compiler_params=_cparams(("arbitrary", "arbitrary"), 32),
        name="ctx_attention",
    )(qkv, qkv, qkv)


def kernel(x, c, ctx, c_ctx, w_mod, b_mod, norm_g, ffn_w_up, ffn_conv_w, ffn_conv_b, ffn_w_down, ssm_w_in, ssm_conv_w, ssm_conv_b, ssm_dt_bias, ssm_a_log, ssm_d, ssm_norm_g, ssm_w_out, hg_w_in, hg_lb, hg_norm_g, hg_w_out, na_w_qkv, na_rpb, na_w_out):
    cvec8 = jnp.concatenate([c, c_ctx[None], jnp.zeros((SUBLANES - BATCH - 1, D_MODEL), F32)], axis=0)
    mod = _adaln(cvec8, w_mod, b_mod)
    ffn_up_bf, ffn_down_bf = ffn_w_up.astype(BF16), ffn_w_down.astype(BF16)
    ssm_out_bf, hg_out_bf, na_out_bf = ssm_w_out.astype(BF16), hg_w_out.astype(BF16), na_w_out.astype(BF16)

    xs, h = _prep(x.reshape(T_LAT, D_MODEL), ctx.reshape(T_CTX, D_MODEL), norm_g[0, 0], mod[0])
    for i in range(DEPTH):
        last = i == DEPTH - 1
        kind, slot = i % N_MIXERS, i // N_MIXERS
        mod_l = mod[i]
        n_rows = T_LAT if last else T_ALL
        op = dict(xs=xs, g_res=norm_g[i, 1], g_ffn=norm_g[i, 2], mod_l=mod_l, n_rows=n_rows, slot=slot)
        if kind == 0:
            tn = 1024
            z = _in_proj(h, ssm_w_in, slot, tn, "ssd_in_z", n_dim=SSM_INNER)
            xbc = _in_proj_conv(h, ssm_w_in, slot, ssm_conv_w[slot], ssm_conv_b[slot], tn, SSM_INNER // tn,
                                "ssd_in_xbc")
            dt_raw = _in_proj(h, ssm_w_in, slot, 2 * SSM_HEADS, "ssd_dt", n_dim=2 * SSM_HEADS,
                              first_tile=(SSM_INNER + SSM_CONV_DIM) // (2 * SSM_HEADS))
            y2 = _ssd_scan(xbc, dt_raw, ssm_dt_bias[slot], ssm_a_log[slot], ssm_d[slot])
            xs, hf = _out_proj(_ssd_gate_prologue, [((y2, 0), 0), ((y2, 1), 0), (z, 0)],
                               ssm_norm_g[slot].reshape(1, SSM_INNER), ssm_out_bf, tm=256, name="ssd_out", **op)
        elif kind == 1:
            tn = 1024
            per = D_MODEL // tn
            qs = _in_proj(h, hg_w_in, slot, tn, "hg_in_q", n_dim=D_MODEL, epilogue=_silu_epilogue,
                          out_dtypes=(BF16,))
            v = _in_proj(h, hg_w_in, slot, tn, "hg_in_v", n_dim=D_MODEL, first_tile=per, out_dtypes=(BF16,))
            lb_spec = pl.BlockSpec((None, DEPTH, tn), lambda j, r: (j // per, 0, j % per))
            logf, key = _in_proj(h, hg_w_in, slot, tn, "hg_in_f", n_dim=2 * D_MODEL, first_tile=2 * per,
                                 epilogue=functools.partial(_hg_gate_epilogue, i), aux=[(hg_lb, lb_spec)],
                                 out_dtypes=(F32, BF16))
            gate = _in_proj(h, hg_w_in, slot, tn, "hg_in_g", n_dim=D_MODEL, first_tile=4 * per)
            o2 = _gla_scan(qs, v, logf, key)
            xs, hf = _out_proj(_hg_readout_prologue, [((o2, 0), 0), ((o2, 1), 0), (gate, 0)],
                               hg_norm_g[slot].reshape(1, D_MODEL), hg_out_bf, tm=ROW_TILE, name="hg_out", **op)
        else:
            qkv = _in_proj(h, na_w_qkv, slot, 1024, "na_in")
            o_lat = _na_attention(qkv, _na_bias_table(na_rpb[slot]))
            o_ctx = _ctx_attention(qkv)
            assert T_CTX == ROW_TILE
            last_lat = T_LAT // ROW_TILE - 1
            xs, hf = _out_proj(_na_prologue, [(o_lat, 0, functools.partial(jnp.minimum, last_lat)),
                                              (o_ctx, 0, lambda t: 0)],
                               None, na_out_bf, tm=ROW_TILE, name="na_out", **op)
        nxt = min(i + 1, DEPTH - 1)
        xs, h = _ffn(hf, xs, i, ffn_up_bf, ffn_conv_w, ffn_conv_b, ffn_down_bf, norm_g[i, 3], mod_l,
                     norm_g[nxt, 0], mod[nxt], n_rows=n_rows, emit_h=not last)

    return xs.reshape(BATCH, SEQ, D_MODEL)
```

```python
import functools

import numpy as np
import jax
import jax.numpy as jnp
from jax import lax
from jax.experimental import pallas as pl
from jax.experimental.pallas import tpu as pltpu

F32 = jnp.float32
BF16 = jnp.bfloat16

D_MODEL = 2048
BATCH = 2
SEQ = 4096
DEPTH = 4
GRID_W = 64
CTX_LEN = 256
N_MIXERS = 3
RMS_EPS = 1e-6
SSM_INNER = 4096
SSM_HEAD_DIM = 64
SSM_HEADS = 64
SSM_STATE = 128
SSM_GROUPS = 8
SSM_CONV_DIM = SSM_INNER + 2 * SSM_GROUPS * SSM_STATE
SSM_IN_DIM = SSM_INNER + SSM_CONV_DIM + 2 * SSM_HEADS
SSM_CHUNK = 128
HG_HEADS = 16
HG_KEY_DIM = 128
HG_IN_DIM = 5 * D_MODEL
HG_CHUNK = 64
NA_HEADS = 16
NA_HEAD_DIM = 128
NA_WIN_R = 8
NA_WIN_C = 16
ROPE_BASE = 10000.0
FFN_HIDDEN = 5632

T_LAT = BATCH * SEQ
T_CTX = BATCH * CTX_LEN
T_ALL = T_LAT + T_CTX
LANES = 128
SUBLANES = 8
BF16_ROWS = 16
ONES_ROWS = 16
ROW_TILE = 512
FFN_COL_CHUNK = 512
MIB = 1024 * 1024
LOG2E = 1.4426950408889634


def _cparams(sem, vmem_mib):
    return pltpu.CompilerParams(dimension_semantics=sem, vmem_limit_bytes=vmem_mib * MIB)


def _silu(v):
    return v * jax.nn.sigmoid(v)


def _rms(v):
    return v * lax.rsqrt(jnp.mean(v * v, axis=-1, keepdims=True) + RMS_EPS)


def _log1p_unit(e):
    return jnp.log(1.0 + e)


def _softplus(v):
    return jnp.maximum(v, 0.0) + _log1p_unit(jnp.exp(-jnp.abs(v)))


def _dot(a, b):
    return jnp.dot(a, b, preferred_element_type=F32)


def _dot_nt(a, b):
    return lax.dot_general(a, b, (((1,), (1,)), ((), ())), preferred_element_type=F32)


def _dot_tn(a, b):
    return lax.dot_general(a, b, (((0,), (0,)), ((), ())), preferred_element_type=F32)


def _bf16_pieces(v, terms):
    pieces = []
    rem = v
    for _ in range(terms):
        part = rem.astype(BF16)
        rem = rem - part.astype(F32)
        pieces.append(part)
    return pieces


def _dot_sel(m01_rep, v, terms):
    return _dot(m01_rep, jnp.concatenate(_bf16_pieces(v, terms), axis=0))


def _dot_sel_rhs(v, m01_rep, terms):
    return _dot(jnp.concatenate(_bf16_pieces(v, terms), axis=1), m01_rep)


def _mod_row(row0):
    return jnp.where(row0 >= T_LAT, 2, row0 // SEQ)


def _mod_vec(mod_ref, r, idx):
    return mod_ref[pl.ds(r, 1), idx * D_MODEL:(idx + 1) * D_MODEL]


def _norm_mod(x, g, mod_ref, r, shift_idx, scale_idx):
    return _rms(x) * (g * (1.0 + _mod_vec(mod_ref, r, scale_idx))) + _mod_vec(mod_ref, r, shift_idx)


def _seg_edges(row0, rows):
    gr = row0 + lax.broadcasted_iota(jnp.int32, (rows, 1), 0)
    pos = jnp.where(gr >= T_LAT, (gr - T_LAT) % CTX_LEN, gr % SEQ)
    length = jnp.where(gr >= T_LAT, CTX_LEN, SEQ)
    return pos == 0, pos == length - 1


def _shift_rows(cur, prev_row, next_row, row0):
    rows = cur.shape[0]
    first, last = _seg_edges(row0, rows)
    sub = lax.broadcasted_iota(jnp.int32, (SUBLANES, 1), 0)
    up = pltpu.roll(cur, 1, axis=0)
    up = jnp.concatenate([jnp.where(sub == 0, prev_row, up[:SUBLANES]), up[SUBLANES:]], axis=0)
    up = jnp.where(first, 0.0, up)
    dn = pltpu.roll(cur, rows - 1, axis=0)
    dn = jnp.concatenate([dn[:rows - SUBLANES], jnp.where(sub == SUBLANES - 1, next_row, dn[rows - SUBLANES:])],
                         axis=0)
    dn = jnp.where(last, 0.0, dn)
    return up, dn


def _token_conv(cur, prev_row, next_row, row0, w_ref, b_ref):
    up, dn = _shift_rows(cur, prev_row, next_row, row0)
    return b_ref[...] + up * w_ref[0:1, :] + cur * w_ref[1:2, :] + dn * w_ref[2:3, :]


def _halo_specs(tm, halo, width, n_rows, col_fn):
    per = tm // halo
    last_blk = n_rows // halo - 1
    prev = pl.BlockSpec((halo, width), lambda i, j: (jnp.maximum(i * per - 1, 0), col_fn(j)))
    nxt = pl.BlockSpec((halo, width), lambda i, j: (jnp.minimum((i + 1) * per, last_blk), col_fn(j)))
    return prev, nxt


def _adaln_body(c_ref, w_ref, b_ref, o_ref):
    s = _silu(c_ref[...]).astype(BF16)
    o_ref[...] = _dot(s, w_ref[...].astype(BF16)) + b_ref[...]


def _adaln(cvec8, w_mod, b_mod):
    tn = 1024
    n_out = 6 * D_MODEL
    return pl.pallas_call(
        _adaln_body,
        grid=(DEPTH, n_out // tn),
        in_specs=[
            pl.BlockSpec((SUBLANES, D_MODEL), lambda l, j: (0, 0)),
            pl.BlockSpec((None, D_MODEL, tn), lambda l, j: (l, 0, j)),
            pl.BlockSpec((None, 1, tn), lambda l, j: (l, 0, j)),
        ],
        out_specs=pl.BlockSpec((None, SUBLANES, tn), lambda l, j: (l, 0, j)),
        out_shape=jax.ShapeDtypeStruct((DEPTH, SUBLANES, n_out), F32),
        compiler_params=_cparams(("arbitrary", "arbitrary"), 40),
        name="adaln",
    )(cvec8, w_mod, b_mod.reshape(DEPTH, 1, n_out))


def _prep_body(lat_ref, ctx_ref, g_ref, mod_ref, xs_ref, h_ref):
    row0 = pl.program_id(0) * ROW_TILE

    def emit(src_ref):
        xv = src_ref[...]
        xs_ref[...] = xv
        h_ref[...] = _norm_mod(xv, g_ref[...], mod_ref, _mod_row(row0), 0, 1).astype(BF16)

    @pl.when(row0 < T_LAT)
    def _():
        emit(lat_ref)

    @pl.when(row0 >= T_LAT)
    def _():
        emit(ctx_ref)


def _prep(x_lat, x_ctx, g_row, mod_l):
    assert T_CTX == ROW_TILE
    n_lat = T_LAT // ROW_TILE
    row = pl.BlockSpec((ROW_TILE, D_MODEL), lambda i: (i, 0))
    return pl.pallas_call(
        _prep_body,
        grid=(T_ALL // ROW_TILE,),
        in_specs=[pl.BlockSpec((ROW_TILE, D_MODEL), lambda i: (jnp.minimum(i, n_lat - 1), 0)),
                  pl.BlockSpec((ROW_TILE, D_MODEL), lambda i: (0, 0)),
                  pl.BlockSpec((1, D_MODEL), lambda i: (0, 0)), pl.BlockSpec(mod_l.shape, lambda i: (0, 0))],
        out_specs=[row, row],
        out_shape=[jax.ShapeDtypeStruct((T_ALL, D_MODEL), F32), jax.ShapeDtypeStruct((T_ALL, D_MODEL), BF16)],
        compiler_params=_cparams(("arbitrary",), 40),
        name="prep",
    )(x_lat, x_ctx, g_row.reshape(1, D_MODEL), mod_l)


def _in_proj_body(epilogue, n_aux, h_ref, w_ref, *rest):
    aux, outs, wb_ref = rest[:n_aux], rest[n_aux:-1], rest[-1]

    @pl.when(pl.program_id(1) == 0)
    def _():
        wb_ref[...] = w_ref[...].astype(BF16)

    u = _dot(h_ref[...], wb_ref[...])
    vals = (u,) if epilogue is None else epilogue(u, *aux)
    for o_ref, val in zip(outs, vals):
        o_ref[...] = val.astype(o_ref.dtype)


def _in_proj(h, w_stack, slot, tn, name, n_dim=None, first_tile=0, epilogue=None, aux=(), out_dtypes=(F32,)):
    tm = T_ALL // 8
    n_dim = w_stack.shape[2] if n_dim is None else n_dim
    res = pl.pallas_call(
        functools.partial(_in_proj_body, epilogue, len(aux)),
        grid=(n_dim // tn, T_ALL // tm),
        in_specs=[pl.BlockSpec((tm, D_MODEL), lambda j, i: (i, 0)),
                  pl.BlockSpec((None, D_MODEL, tn), lambda j, i: (slot, 0, first_tile + j))]
        + [spec for _, spec in aux],
        out_specs=[pl.BlockSpec((tm, tn), lambda j, i: (i, j)) for _ in out_dtypes],
        out_shape=[jax.ShapeDtypeStruct((T_ALL, n_dim), dt) for dt in out_dtypes],
        scratch_shapes=[pltpu.VMEM((D_MODEL, tn), BF16)],
        compiler_params=_cparams(("arbitrary", "arbitrary"), 56),
        name=name,
    )(h, w_stack, *[a for a, _ in aux])
    return res[0] if len(out_dtypes) == 1 else tuple(res)


def _ssd_gate_prologue(row0, yf_ref, yb_ref, z_ref, g_ref):
    v = (yf_ref[...].astype(F32) + yb_ref[...].astype(F32)) * _silu(z_ref[...])
    return v * g_ref[...], lax.rsqrt(jnp.mean(v * v, axis=-1, keepdims=True) + RMS_EPS)


def _hg_readout_prologue(row0, of_ref, ob_ref, gate_ref, g_ref):
    o = of_ref[...].astype(F32) + ob_ref[...].astype(F32)
    parts = [_rms(o[:, h * LANES:(h + 1) * LANES]) for h in range(o.shape[1] // LANES)]
    return jnp.concatenate(parts, axis=-1) * g_ref[...] * _silu(gate_ref[...]), None


def _na_prologue(row0, o_lat_ref, o_ctx_ref):
    return jnp.where(row0 >= T_LAT, o_ctx_ref[...], o_lat_ref[...]), None


def _out_proj_body(prologue, n_in, tm, *refs):
    ins = refs[:n_in]
    w_ref, x_ref, g1_ref, g2_ref, mod_ref, xo_ref, ho_ref = refs[n_in:]
    part, row_scale = prologue(pl.program_id(0) * tm, *ins)
    y = _dot(part.astype(BF16), w_ref[...])
    if row_scale is not None:
        y = y * row_scale
    r = _mod_row(pl.program_id(0) * tm)
    xn = x_ref[...] + _rms(y) * (_mod_vec(mod_ref, r, 2) * g1_ref[...])
    xo_ref[...] = xn
    ho_ref[...] = _norm_mod(xn, g2_ref[...], mod_ref, r, 3, 4).astype(BF16)


def _out_proj(prologue, row_ins, gain, w_stack, slot, xs, g_res, g_ffn, mod_l, *, n_rows, tm, name):
    k_dim = w_stack.shape[1]
    in_specs, args = [], []
    for arr, cb, *row_fn in row_ins:
        row_fn = row_fn[0] if row_fn else (lambda i: i)
        if isinstance(arr, tuple):
            in_specs.append(pl.BlockSpec((None, tm, k_dim), functools.partial(
                lambda i, ld, cb, row_fn: (ld, row_fn(i), cb), ld=arr[1], cb=cb, row_fn=row_fn)))
            args.append(arr[0])
        else:
            in_specs.append(pl.BlockSpec((tm, k_dim), functools.partial(
                lambda i, cb, row_fn: (row_fn(i), cb), cb=cb, row_fn=row_fn)))
            args.append(arr)
    if gain is not None:
        in_specs.append(pl.BlockSpec((1, k_dim), lambda i: (0, 0)))
        args.append(gain)
    n_in = len(args)
    row = pl.BlockSpec((tm, D_MODEL), lambda i: (i, 0))
    vec = pl.BlockSpec((1, D_MODEL), lambda i: (0, 0))
    in_specs += [pl.BlockSpec((None, k_dim, D_MODEL), lambda i: (slot, 0, 0), pipeline_mode=pl.Buffered(1)),
                 row, vec, vec, pl.BlockSpec(mod_l.shape, lambda i: (0, 0))]
    return pl.pallas_call(
        functools.partial(_out_proj_body, prologue, n_in, tm),
        grid=(n_rows // tm,),
        in_specs=in_specs,
        out_specs=[row, row],
        out_shape=[jax.ShapeDtypeStruct((n_rows, D_MODEL), F32), jax.ShapeDtypeStruct((n_rows, D_MODEL), BF16)],
        compiler_params=_cparams(("arbitrary",), 60),
        name=name,
    )(*args, w_stack, xs, g_res.reshape(1, D_MODEL), g_ffn.reshape(1, D_MODEL), mod_l)


def _ffn_body(emit_h, has_ctx, h_ref, hp_ref, hn_ref, wa_ref, wv_ref, cwa_ref, cwv_ref, cba_ref, cbv_ref, wd_ref,
              x_ref, g_res_ref, mod_ref, g_next_ref, modn_ref, *out_and_scratch):
    if emit_h:
        xo_ref, ho_ref, hall_ref, acc_ref = out_and_scratch
    else:
        xo_ref, hall_ref, acc_ref = out_and_scratch
    k = pl.program_id(1)
    tm = ROW_TILE
    row0 = pl.program_id(0) * tm
    pad = BF16_ROWS
    rows = tm + 2 * pad
    s8 = SUBLANES

    @pl.when(k == 0)
    def _():
        hall_ref[pl.ds(0, pad), :] = hp_ref[...]
        hall_ref[pl.ds(pad, tm), :] = h_ref[...]
        hall_ref[pl.ds(pad + tm, pad), :] = hn_ref[...]
        acc_ref[...] = jnp.zeros_like(acc_ref)

    hall = hall_ref[...]
    edge_first = row0 % SEQ == 0
    edge_last = ((row0 + tm) % SEQ == 0) | (row0 + tm == T_ALL)
    is_ctx = row0 >= T_LAT
    joints = list(range(CTX_LEN, tm, CTX_LEN)) if has_ctx else []
    sub = lax.broadcasted_iota(jnp.int32, (s8, 1), 0)

    def patch(v, start, cond):
        return jnp.concatenate([v[:start], jnp.where(cond, 0.0, v[start:start + s8]), v[start + s8:]], axis=0)

    def conv(w_ref, cw_ref, cb_ref, cs):
        u = _dot(hall, w_ref[:, cs])
        u = patch(u, pad - s8, edge_first)
        u = patch(u, pad + tm, edge_last)
        up = pltpu.roll(u, 1, axis=0)[pad:pad + tm]
        dn = pltpu.roll(u, rows - 1, axis=0)[pad:pad + tm]
        for j in joints:
            up = patch(up, j, is_ctx & (sub == 0))
            dn = patch(dn, j - s8, is_ctx & (sub == s8 - 1))
        return cb_ref[:, cs] + up * cw_ref[0:1, cs] + u[pad:pad + tm] * cw_ref[1:2, cs] + dn * cw_ref[2:3, cs]

    chunks = [pl.ds(c * FFN_COL_CHUNK, FFN_COL_CHUNK) for c in range(wa_ref.shape[1] // FFN_COL_CHUNK)]
    acts = [(_silu(conv(wa_ref, cwa_ref, cba_ref, cs)) * conv(wv_ref, cwv_ref, cbv_ref, cs)).astype(BF16)
            for cs in chunks]
    acc_ref[...] += _dot(jnp.concatenate(acts, axis=1), wd_ref[...])

    @pl.when(k == pl.num_programs(1) - 1)
    def _():
        r = _mod_row(row0)
        xn = x_ref[...] + _rms(acc_ref[...]) * (_mod_vec(mod_ref, r, 5) * g_res_ref[...])
        xo_ref[...] = xn
        if emit_h:
            ho_ref[...] = _norm_mod(xn, g_next_ref[...], modn_ref, r, 0, 1).astype(BF16)


def _ffn(h, xs, layer, w_up, conv_w, conv_b, w_down, g_res, mod_l, g_next, mod_next, *, n_rows, emit_h):
    tm, tk = ROW_TILE, 512
    kt = FFN_HIDDEN // tk
    h_prev, h_next = _halo_specs(tm, BF16_ROWS, D_MODEL, n_rows, lambda k: 0)
    row = pl.BlockSpec((tm, D_MODEL), lambda i, k: (i, 0))
    vec = pl.BlockSpec((1, D_MODEL), lambda i, k: (0, 0))
    tab = pl.BlockSpec(mod_l.shape, lambda i, k: (0, 0))
    out_specs = [row, row] if emit_h else [row]
    out_shape = [jax.ShapeDtypeStruct((n_rows, D_MODEL), F32)]
    if emit_h:
        out_shape.append(jax.ShapeDtypeStruct((n_rows, D_MODEL), BF16))
    conv_b3 = conv_b.reshape(DEPTH, 1, 2 * FFN_HIDDEN)

    res = pl.pallas_call(
        functools.partial(_ffn_body, emit_h, n_rows > T_LAT),
        grid=(n_rows // tm, kt),
        in_specs=[
            row, h_prev, h_next,
            pl.BlockSpec((None, D_MODEL, tk), lambda i, k: (layer, 0, k)),
            pl.BlockSpec((None, D_MODEL, tk), lambda i, k: (layer, 0, k + kt)),
            pl.BlockSpec((None, 3, tk), lambda i, k: (layer, 0, k)),
            pl.BlockSpec((None, 3, tk), lambda i, k: (layer, 0, k + kt)),
            pl.BlockSpec((None, 1, tk), lambda i, k: (layer, 0, k)),
            pl.BlockSpec((None, 1, tk), lambda i, k: (layer, 0, k + kt)),
            pl.BlockSpec((None, tk, D_MODEL), lambda i, k: (layer, k, 0)),
            row, vec, tab, vec, tab,
        ],
        out_specs=out_specs,
        out_shape=out_shape,
        scratch_shapes=[pltpu.VMEM((tm + 2 * BF16_ROWS, D_MODEL), BF16), pltpu.VMEM((tm, D_MODEL), F32)],
        compiler_params=_cparams(("arbitrary", "arbitrary"), 56),
        name="conv_ffn",
    )(h, h, h, w_up, w_up, conv_w, conv_w, conv_b3, conv_b3, w_down, xs, g_res.reshape(1, D_MODEL), mod_l,
      g_next.reshape(1, D_MODEL), mod_next)
    return (res[0], res[1]) if emit_h else (res[0], None)


def _in_proj_conv_body(tm, h_ref, hp_ref, hn_ref, w_ref, cw_ref, cb_ref, o_ref, wb_ref):
    @pl.when(pl.program_id(1) == 0)
    def _():
        wb_ref[...] = w_ref[...].astype(BF16)

    w = wb_ref[...]
    u = _dot(h_ref[...], w)
    uh = _dot(jnp.concatenate([hp_ref[...], hn_ref[...]], axis=0), w)
    conv = _token_conv(u, uh[BF16_ROWS - 1:BF16_ROWS], uh[BF16_ROWS:BF16_ROWS + 1], pl.program_id(1) * tm,
                       cw_ref, cb_ref)
    o_ref[...] = _silu(conv).astype(o_ref.dtype)


def _in_proj_conv(h, w_stack, slot, conv_w, conv_b, tn, first_tile, name):
    tm = T_ALL // 8
    n_dim = conv_w.shape[1]
    per = tm // BF16_ROWS
    last_blk = T_ALL // BF16_ROWS - 1
    return pl.pallas_call(
        functools.partial(_in_proj_conv_body, tm),
        grid=(n_dim // tn, T_ALL // tm),
        in_specs=[pl.BlockSpec((tm, D_MODEL), lambda j, i: (i, 0)),
                  pl.BlockSpec((BF16_ROWS, D_MODEL), lambda j, i: (jnp.maximum(i * per - 1, 0), 0)),
                  pl.BlockSpec((BF16_ROWS, D_MODEL), lambda j, i: (jnp.minimum((i + 1) * per, last_blk), 0)),
                  pl.BlockSpec((None, D_MODEL, tn), lambda j, i: (slot, 0, first_tile + j)),
                  pl.BlockSpec((3, tn), lambda j, i: (0, j)),
                  pl.BlockSpec((1, tn), lambda j, i: (0, j))],
        out_specs=pl.BlockSpec((tm, tn), lambda j, i: (i, j)),
        out_shape=jax.ShapeDtypeStruct((T_ALL, n_dim), BF16),
        scratch_shapes=[pltpu.VMEM((D_MODEL, tn), BF16)],
        compiler_params=_cparams(("arbitrary", "arbitrary"), 56),
        name=name,
    )(h, h, h, w_stack, conv_w, conv_b.reshape(1, -1))


def _scan_chunk_index(d, b, s, n_lat, n_ctx):
    ctx0 = BATCH * n_lat + b * n_ctx
    fwd = jnp.where(s < n_ctx, ctx0 + s, b * n_lat + s - n_ctx)
    bwd = jnp.where(s < n_ctx, ctx0 + n_ctx - 1 - s, b * n_lat + n_lat - 1 - (s - n_ctx))
    return jnp.where(d == 0, fwd, bwd)


SSD_CUM_TERMS = 3
SSD_DT_TERMS = 2


def _ssd_scan_consts():
    q = SSM_CHUNK
    i = np.arange(q)
    pfx = np.zeros((2, q + ONES_ROWS, q), np.float32)
    pfx[0, :q] = i[:, None] >= i[None, :]
    pfx[1, :q] = i[:, None] <= i[None, :]
    pfx[:, q:] = 1.0
    pfx = np.tile(pfx, (1, 1, SSD_CUM_TERMS))
    expand = np.repeat(np.eye(SSM_HEADS, dtype=np.float32), SSM_HEAD_DIM, axis=1)
    expand = np.tile(expand, (SSD_DT_TERMS, 1))
    return jnp.asarray(pfx, BF16), jnp.asarray(expand, BF16)


def _ssd_scan_body(x_ref, b_ref, c_ref, dtraw_ref, dtb_ref, alog_ref, dskip_ref, pfx_ref, exp_ref, y_ref, s_ref):
    q = SSM_CHUNK
    hp = 2 * SSM_HEAD_DIM
    d = pl.program_id(0)

    @pl.when(pl.program_id(2) == 0)
    def _():
        s_ref[...] = jnp.zeros_like(s_ref)

    def pick(v):
        return jnp.where(d == 0, v[:, :SSM_HEADS], v[:, SSM_HEADS:])

    dt2 = _softplus(dtraw_ref[...] + dtb_ref[...])
    cum2 = _dot_sel(pfx_ref[...], dt2 * (-LOG2E * jnp.exp(alog_ref[...])), SSD_CUM_TERMS)
    acum2 = cum2[:q]
    acum = pick(acum2)
    tot = pick(cum2[q:q + SUBLANES])[0:1]
    acum_t2 = acum2.T
    acum_t = jnp.where(d == 0, acum_t2[:SSM_HEADS], acum_t2[SSM_HEADS:])
    dt_wide = _dot_sel_rhs(pick(dt2), exp_ref[...], SSD_DT_TERMS)

    ii = lax.broadcasted_iota(jnp.int32, (q, q), 0)
    jj = lax.broadcasted_iota(jnp.int32, (q, q), 1)
    causal = (jj - ii) * (1 - 2 * d) <= 0
    low_half = lax.broadcasted_iota(jnp.int32, (q, hp), 1) < SSM_HEAD_DIM

    heads_per_group = SSM_HEADS // SSM_GROUPS
    n_pairs = SSM_HEADS // 2

    b_gs = [b_ref[:, g * SSM_STATE:(g + 1) * SSM_STATE].astype(BF16) for g in range(SSM_GROUPS)]
    c_bf = [c_ref[:, g * SSM_STATE:(g + 1) * SSM_STATE].astype(BF16) for g in range(SSM_GROUPS)]
    c_gs = [c.astype(F32) for c in c_bf]
    cbs = [jnp.where(causal, _dot_nt(c_bf[g], b_gs[g]), 0.0) for g in range(SSM_GROUPS)]

    lhs, xdts, xws, decs = [], [], [], []
    for p in range(n_pairs):
        lo = p * hp
        g = (2 * p) // heads_per_group
        cols = []
        for h in (2 * p, 2 * p + 1):
            col = jnp.broadcast_to(acum[:, h:h + 1], (q, q))
            seg = jnp.where(causal, col - acum_t[h:h + 1, :], 0.0)
            scores = jnp.exp2(seg) * cbs[g]
            c_dec = c_gs[g] * jnp.exp2(col)
            lhs.append(jnp.concatenate([scores.astype(BF16), c_dec.astype(BF16)], axis=1))
            cols.append(col)
        xdt = x_ref[:, lo:lo + hp].astype(F32) * dt_wide[:, lo:lo + hp]
        tot_pair = jnp.where(low_half[0:1], jnp.broadcast_to(tot[:, 2 * p:2 * p + 1], (1, hp)),
                             jnp.broadcast_to(tot[:, 2 * p + 1:2 * p + 2], (1, hp)))
        to_end = jnp.exp2(tot_pair - jnp.where(low_half, cols[0], cols[1]))
        xdts.append(xdt.astype(BF16))
        xws.append((xdt * to_end).astype(BF16))
        decs.append(jnp.exp2(tot_pair))

    for p in range(n_pairs):
        lo = p * hp
        rhs = jnp.concatenate([xdts[p], s_ref[:, lo:lo + hp].astype(BF16)], axis=0)
        y_pair = jnp.where(low_half, _dot(lhs[2 * p], rhs), _dot(lhs[2 * p + 1], rhs))
        y_ref[:, lo:lo + hp] = (y_pair + dskip_ref[:, lo:lo + hp] * x_ref[:, lo:lo + hp].astype(F32)
                                ).astype(y_ref.dtype)

    for p in range(n_pairs):
        lo = p * hp
        g = (2 * p) // heads_per_group
        s_ref[:, lo:lo + hp] = s_ref[:, lo:lo + hp] * decs[p] + _dot_tn(b_gs[g], xws[p])


def _ssd_scan(xbc, dt_raw, dt_bias, a_log, d_skip):
    q = SSM_CHUNK
    n_lat, n_ctx = SEQ // q, CTX_LEN // q
    pfx, expand = _ssd_scan_consts()
    n_heads2 = 2 * SSM_HEADS
    b_col = SSM_INNER // (SSM_GROUPS * SSM_STATE)
    d_wide = jnp.repeat(d_skip, SSM_HEAD_DIM, axis=1).reshape(2, 1, SSM_INNER)

    def chunk(d, b, s):
        return _scan_chunk_index(d, b, s, n_lat, n_ctx)

    return pl.pallas_call(
        _ssd_scan_body,
        grid=(2, BATCH, n_lat + n_ctx),
        in_specs=[
            pl.BlockSpec((q, SSM_INNER), lambda d, b, s: (chunk(d, b, s), 0)),
            pl.BlockSpec((q, SSM_GROUPS * SSM_STATE), lambda d, b, s: (chunk(d, b, s), b_col)),
            pl.BlockSpec((q, SSM_GROUPS * SSM_STATE), lambda d, b, s: (chunk(d, b, s), b_col + 1)),
            pl.BlockSpec((q, n_heads2), lambda d, b, s: (chunk(d, b, s), 0)),
            pl.BlockSpec((1, n_heads2), lambda d, b, s: (0, 0)),
            pl.BlockSpec((1, n_heads2), lambda d, b, s: (0, 0)),
            pl.BlockSpec((None, 1, SSM_INNER), lambda d, b, s: (d, 0, 0)),
            pl.BlockSpec((None,) + pfx.shape[1:], lambda d, b, s: (d, 0, 0)),
            pl.BlockSpec(expand.shape, lambda d, b, s: (0, 0)),
        ],
        out_specs=pl.BlockSpec((None, q, SSM_INNER), lambda d, b, s: (d, chunk(d, b, s), 0)),
        out_shape=jax.ShapeDtypeStruct((2, T_ALL, SSM_INNER), BF16),
        scratch_shapes=[pltpu.VMEM((SSM_STATE, SSM_INNER), F32)],
        compiler_params=_cparams(("arbitrary", "arbitrary", "arbitrary"), 48),
        name="ssd_scan",
    )(xbc, xbc, xbc, dt_raw, dt_bias.reshape(1, n_heads2), a_log.reshape(1, n_heads2), d_wide, pfx, expand)


GLA_SUM_TERMS = 2


def _gla_consts():
    q = HG_CHUNK
    levels = int(np.log2(q))
    i = np.arange(q)
    mats = [(i[:, None] >= i[None, :]).astype(np.float32), (i[None, :] > i[:, None]).astype(np.float32)]
    masks = [np.eye(q, dtype=np.float32)]
    for lv in range(levels):
        s = 1 << lv
        blk, pos = i // (2 * s), i % (2 * s)
        bound = blk * 2 * s + s - 1
        right = pos >= s
        t = i[None, :]
        m_right = right[:, None] & (t > bound[:, None]) & (t <= i[:, None])
        m_left = (~right)[:, None] & (t > i[:, None]) & (t <= bound[:, None])
        mats.append((m_right | m_left).astype(np.float32))
        masks.append((right[:, None] & (~right)[None, :] & (blk[:, None] == blk[None, :])).astype(np.float32))
    mats.append(np.ones((ONES_ROWS, q), np.float32))
    fwd_m, fwd_k = np.concatenate(mats, axis=0), np.stack(masks)
    flip_rows = np.concatenate([np.arange(k * q, (k + 1) * q)[::-1] for k in range(levels + 2)]
                               + [np.arange((levels + 2) * q, (levels + 2) * q + ONES_ROWS)])
    bwd_m = fwd_m[flip_rows][:, ::-1]
    bwd_k = fwd_k[:, ::-1, ::-1]
    mats2 = np.tile(np.stack([fwd_m, bwd_m]), (1, 1, GLA_SUM_TERMS))
    return (jnp.asarray(mats2, BF16), jnp.asarray(np.stack([fwd_k, bwd_k]), F32), levels)


def _silu_epilogue(u):
    return (_silu(u),)


def _hg_gate_epilogue(layer, u, lbp_ref):
    lbp = lbp_ref[...]
    e = jnp.exp(lbp - jnp.max(lbp, axis=0, keepdims=True))
    sm = e / jnp.sum(e, axis=0, keepdims=True)
    lb = jnp.sum(sm[1:layer + 1], axis=0, keepdims=True) if layer > 0 else jnp.zeros((1, lbp.shape[1]), F32)
    ef = jnp.exp(-jnp.abs(u))
    log_sig = jnp.minimum(u, 0.0) - _log1p_unit(ef)
    la = jnp.log(lb)
    lc = jnp.log1p(-lb) + log_sig
    logf = jnp.maximum(la, lc) + _log1p_unit(jnp.exp(-jnp.abs(la - lc)))
    return logf * LOG2E, 1.0 - jnp.exp(logf)


def _gla_body(levels, q_ref, v_ref, logf_ref, key_ref, mats_ref, masks_ref, o_ref, st_ref):
    q = HG_CHUNK

    @pl.when(pl.program_id(2) == 0)
    def _():
        st_ref[...] = jnp.zeros_like(st_ref)

    qs_bf, key_bf, v_bf = q_ref[...], key_ref[...], v_ref[...]

    sums = _dot_sel(mats_ref[...], logf_ref[...], GLA_SUM_TERMS)
    q_in = qs_bf * jnp.exp2(sums[:q]).astype(BF16)
    k_out = key_bf * jnp.exp2(sums[q:2 * q]).astype(BF16)
    dec = jnp.exp2(sums[(levels + 2) * q:(levels + 2) * q + 1])
    q_lv, k_lv = [], []
    for lv in range(levels):
        fac = jnp.exp2(sums[(lv + 2) * q:(lv + 3) * q]).astype(BF16)
        q_lv.append(qs_bf * fac)
        k_lv.append(key_bf * fac)

    heads = [slice(h * LANES, (h + 1) * LANES) for h in range(HG_HEADS)]
    attn = masks_ref[0][None] * jnp.stack([_dot_nt(qs_bf[:, sl], key_bf[:, sl]) for sl in heads], axis=0)
    for lv in range(levels):
        attn = attn + masks_ref[lv + 1][None] * jnp.stack(
            [_dot_nt(q_lv[lv][:, sl], k_lv[lv][:, sl]) for sl in heads], axis=0)
    attn = attn.astype(BF16)
    st_bf = st_ref[...].astype(BF16)
    o_ref[...] = jnp.concatenate(
        [_dot(attn[h], v_bf[:, sl]) + _dot_nt(q_in[:, sl], st_bf[h]) for h, sl in enumerate(heads)],
        axis=1).astype(o_ref.dtype)
    for h, sl in enumerate(heads):
        st_ref[h] = st_ref[h] * dec[:, sl] + _dot_tn(v_bf[:, sl], k_out[:, sl])


def _gla_scan(qs, v, logf, key):
    q = HG_CHUNK
    n_lat, n_ctx = SEQ // q, CTX_LEN // q
    mats, masks, levels = _gla_consts()

    def chunk(d, b, s):
        return _scan_chunk_index(d, b, s, n_lat, n_ctx)

    return pl.pallas_call(
        functools.partial(_gla_body, levels),
        grid=(2, BATCH, n_lat + n_ctx),
        in_specs=[
            pl.BlockSpec((q, D_MODEL), lambda d, b, s: (chunk(d, b, s), 0)),
            pl.BlockSpec((q, D_MODEL), lambda d, b, s: (chunk(d, b, s), 0)),
            pl.BlockSpec((q, D_MODEL), lambda d, b, s: (chunk(d, b, s), d)),
            pl.BlockSpec((q, D_MODEL), lambda d, b, s: (chunk(d, b, s), d)),
            pl.BlockSpec((None,) + mats.shape[1:], lambda d, b, s: (d, 0, 0)),
            pl.BlockSpec((None,) + masks.shape[1:], lambda d, b, s: (d, 0, 0, 0)),
        ],
        out_specs=pl.BlockSpec((None, q, D_MODEL), lambda d, b, s: (d, chunk(d, b, s), 0)),
        out_shape=jax.ShapeDtypeStruct((2, T_ALL, D_MODEL), BF16),
        scratch_shapes=[pltpu.VMEM((HG_HEADS, LANES, HG_KEY_DIM), F32)],
        compiler_params=_cparams(("arbitrary", "arbitrary", "arbitrary"), 48),
        name="gla_scan",
    )(qs, v, logf, key, mats, masks)


def _na_bias_body(rpb_ref, o_ref):
    h = pl.program_id(0)
    n_dc = 2 * NA_WIN_C - 1
    qc = lax.broadcasted_iota(jnp.int32, (GRID_W, LANES), 0)
    kc = lax.broadcasted_iota(jnp.int32, (GRID_W, LANES), 1) % GRID_W
    c0 = jnp.clip(qc - NA_WIN_C // 2, 0, GRID_W - NA_WIN_C)
    col_in = (kc >= c0) & (kc < c0 + NA_WIN_C)
    dc = jnp.clip(kc - qc + NA_WIN_C - 1, 0, n_dc - 1)
    low = lax.broadcasted_iota(jnp.int32, (GRID_W, LANES), 1) < GRID_W
    tiles = []
    for dr in range(2 * NA_WIN_R - 1):
        t = jnp.zeros((GRID_W, LANES), F32)
        for k in range(n_dc):
            t = jnp.where(dc == k, rpb_ref[(h * (2 * NA_WIN_R - 1) + dr) * n_dc + k], t)
        tiles.append(jnp.where(col_in, t, -jnp.inf))
    for dr in range(2 * NA_WIN_R - 2):
        o_ref[dr] = jnp.where(low, tiles[dr], tiles[dr + 1])


def _na_bias_table(rpb):
    n_dr = 2 * NA_WIN_R - 2
    return pl.pallas_call(
        _na_bias_body,
        grid=(NA_HEADS,),
        in_specs=[pl.BlockSpec(memory_space=pltpu.SMEM)],
        out_specs=pl.BlockSpec((None, n_dr, GRID_W, LANES), lambda h: (h, 0, 0, 0)),
        out_shape=jax.ShapeDtypeStruct((NA_HEADS, n_dr, GRID_W, LANES), F32),
        compiler_params=_cparams(("arbitrary",), 32),
        name="na_bias",
    )(rpb.reshape(-1))


def _rope_tables():
    quarter = NA_HEAD_DIM // 4
    inv = ROPE_BASE ** (-np.arange(quarter, dtype=np.float64) / quarter)
    t = np.arange(SEQ)
    row, col = t // GRID_W, t % GRID_W
    ang = np.concatenate([row[:, None] * inv[None], row[:, None] * inv[None],
                          col[:, None] * inv[None], col[:, None] * inv[None]], axis=1)
    sign = np.tile(np.concatenate([-np.ones(quarter), np.ones(quarter)]), 2)[None]
    return jnp.asarray(np.cos(ang), F32), jnp.asarray(np.sin(ang) * sign, F32)


def _rope(v, cos, sin_signed):
    quarter = NA_HEAD_DIM // 4
    lane = lax.broadcasted_iota(jnp.int32, v.shape, 1)
    first = (lane % (2 * quarter)) < quarter
    partner = jnp.where(first, pltpu.roll(v, NA_HEAD_DIM - quarter, axis=1), pltpu.roll(v, quarter, axis=1))
    return v * cos + partner * sin_signed


def _na_body(rows_per_step, q_ref, k_ref, v_ref, kc_ref, vc_ref, cosq_ref, sinq_ref, cos_ref, sin_ref, bias_ref,
             o_ref, kr_ref, vb_ref):
    rb = pl.program_id(2)
    n_rows = SEQ // GRID_W
    scale = NA_HEAD_DIM ** -0.5

    @pl.when(rb == 0)
    def _():
        kr_ref[...] = _rope(k_ref[...], cos_ref[...], sin_ref[...]).astype(BF16)
        vb_ref[...] = v_ref[...].astype(BF16)

    qr = (_rope(q_ref[...], cosq_ref[...], sinq_ref[...]) * scale).astype(BF16)
    kc = kc_ref[...].astype(BF16)
    vc = vc_ref[...].astype(BF16)
    starts, s_rows = [], []
    for lr in range(rows_per_step):
        r = rb * rows_per_step + lr
        r0 = jnp.clip(r - NA_WIN_R // 2, 0, n_rows - NA_WIN_R)
        start = pl.multiple_of(r0 * GRID_W, GRID_W)
        dr0 = r0 - r + NA_WIN_R - 1
        bias = jnp.concatenate([bias_ref[dr0 + 2 * p] for p in range(NA_WIN_R // 2)], axis=1)
        k_win = kr_ref[pl.ds(start, NA_WIN_R * GRID_W), :]
        s_rows.append(_dot_nt(qr[lr * GRID_W:(lr + 1) * GRID_W], k_win) + bias)
        starts.append(start)
    s_loc = jnp.concatenate(s_rows, axis=0)
    s_ctx = _dot_nt(qr, kc)
    m = jnp.maximum(jnp.max(s_loc, axis=-1, keepdims=True), jnp.max(s_ctx, axis=-1, keepdims=True))
    p_loc = jnp.exp(s_loc - m)
    p_ctx = jnp.exp(s_ctx - m)
    denom = jnp.sum(p_loc, axis=-1, keepdims=True) + jnp.sum(p_ctx, axis=-1, keepdims=True)
    p_loc = p_loc.astype(BF16)
    o_loc = jnp.concatenate(
        [_dot(p_loc[lr * GRID_W:(lr + 1) * GRID_W], vb_ref[pl.ds(starts[lr], NA_WIN_R * GRID_W), :])
         for lr in range(rows_per_step)], axis=0)
    o = o_loc + _dot(p_ctx.astype(BF16), vc)
    o_ref[...] = (o / denom).astype(o_ref.dtype)


def _na_attention(qkv, bias_tbl):
    rows_per_step = 64
    tq = rows_per_step * GRID_W
    steps = SEQ // tq
    cos, sin = _rope_tables()
    hd = NA_HEAD_DIM
    ctx_blk0 = T_LAT // CTX_LEN
    return pl.pallas_call(
        functools.partial(_na_body, rows_per_step),
        grid=(BATCH, NA_HEADS, steps),
        in_specs=[
            pl.BlockSpec((tq, hd), lambda b, h, r: (b * steps + r, h)),
            pl.BlockSpec((SEQ, hd), lambda b, h, r: (b, NA_HEADS + h)),
            pl.BlockSpec((SEQ, hd), lambda b, h, r: (b, 2 * NA_HEADS + h)),
            pl.BlockSpec((CTX_LEN, hd), lambda b, h, r: (ctx_blk0 + b, NA_HEADS + h)),
            pl.BlockSpec((CTX_LEN, hd), lambda b, h, r: (ctx_blk0 + b, 2 * NA_HEADS + h)),
            pl.BlockSpec((tq, hd), lambda b, h, r: (r, 0)),
            pl.BlockSpec((tq, hd), lambda b, h, r: (r, 0)),
            pl.BlockSpec((SEQ, hd), lambda b, h, r: (0, 0)),
            pl.BlockSpec((SEQ, hd), lambda b, h, r: (0, 0)),
            pl.BlockSpec((None,) + bias_tbl.shape[1:], lambda b, h, r: (h, 0, 0, 0)),
        ],
        out_specs=pl.BlockSpec((tq, hd), lambda b, h, r: (b * steps + r, h)),
        out_shape=jax.ShapeDtypeStruct((T_LAT, D_MODEL), BF16),
        scratch_shapes=[pltpu.VMEM((SEQ, hd), BF16), pltpu.VMEM((SEQ, hd), BF16)],
        compiler_params=_cparams(("arbitrary", "arbitrary", "arbitrary"), 48),
        name="na_attention",
    )(qkv, qkv, qkv, qkv, qkv, cos, sin, cos, sin, bias_tbl)


def _ctx_attn_body(q_ref, k_ref, v_ref, o_ref):
    scale = NA_HEAD_DIM ** -0.5
    s = _dot_nt((q_ref[...] * scale).astype(BF16), k_ref[...].astype(BF16))
    p = jnp.exp(s - jnp.max(s, axis=-1, keepdims=True))
    o = _dot(p.astype(BF16), v_ref[...].astype(BF16))
    o_ref[...] = (o / jnp.sum(p, axis=-1, keepdims=True)).astype(o_ref.dtype)


def _ctx_attention(qkv):
    hd = NA_HEAD_DIM
    ctx_blk0 = T_LAT // CTX_LEN
    return pl.pallas_call(
        _ctx_attn_body,
        grid=(BATCH, NA_HEADS),
        in_specs=[
            pl.BlockSpec((CTX_LEN, hd), lambda b, h: (ctx_blk0 + b, h)),
            pl.BlockSpec((CTX_LEN, hd), lambda b, h: (ctx_blk0 + b, NA_HEADS + h)),
            pl.BlockSpec((CTX_LEN, hd), lambda b, h: (ctx_blk0 + b, 2 * NA_HEADS + h)),
        ],
        out_specs=pl.BlockSpec((CTX_LEN, hd), lambda b, h: (b, h)),
        out_shape=jax.ShapeDtypeStruct((T_CTX, D_MODEL), BF16),
        compiler_params=_cparams(("arbitrary", "arbitrary"), 32),
        name="ctx_attention",
    )(qkv, qkv, qkv)


def kernel(x, c, ctx, c_ctx, w_mod, b_mod, norm_g, ffn_w_up, ffn_conv_w, ffn_conv_b, ffn_w_down, ssm_w_in, ssm_conv_w, ssm_conv_b, ssm_dt_bias, ssm_a_log, ssm_d, ssm_norm_g, ssm_w_out, hg_w_in, hg_lb, hg_norm_g, hg_w_out, na_w_qkv, na_rpb, na_w_out):
    cvec8 = jnp.concatenate([c, c_ctx[None], jnp.zeros((SUBLANES - BATCH - 1, D_MODEL), F32)], axis=0)
    mod = _adaln(cvec8, w_mod, b_mod)
    ffn_up_bf, ffn_down_bf = ffn_w_up.astype(BF16), ffn_w_down.astype(BF16)
    ssm_out_bf, hg_out_bf, na_out_bf = ssm_w_out.astype(BF16), hg_w_out.astype(BF16), na_w_out.astype(BF16)

    xs, h = _prep(x.reshape(T_LAT, D_MODEL), ctx.reshape(T_CTX, D_MODEL), norm_g[0, 0], mod[0])
    for i in range(DEPTH):
        last = i == DEPTH - 1
        kind, slot = i % N_MIXERS, i // N_MIXERS
        mod_l = mod[i]
        n_rows = T_LAT if last else T_ALL
        op = dict(xs=xs, g_res=norm_g[i, 1], g_ffn=norm_g[i, 2], mod_l=mod_l, n_rows=n_rows, slot=slot)
        if kind == 0:
            tn = 1024
            z = _in_proj(h, ssm_w_in, slot, tn, "ssd_in_z", n_dim=SSM_INNER)
            xbc = _in_proj_conv(h, ssm_w_in, slot, ssm_conv_w[slot], ssm_conv_b[slot], tn, SSM_INNER // tn,
                                "ssd_in_xbc")
            dt_raw = _in_proj(h, ssm_w_in, slot, 2 * SSM_HEADS, "ssd_dt", n_dim=2 * SSM_HEADS,
                              first_tile=(SSM_INNER + SSM_CONV_DIM) // (2 * SSM_HEADS))
            y2 = _ssd_scan(xbc, dt_raw, ssm_dt_bias[slot], ssm_a_log[slot], ssm_d[slot])
            xs, hf = _out_proj(_ssd_gate_prologue, [((y2, 0), 0), ((y2, 1), 0), (z, 0)],
                               ssm_norm_g[slot].reshape(1, SSM_INNER), ssm_out_bf, tm=256, name="ssd_out", **op)
        elif kind == 1:
            tn = 1024
            per = D_MODEL // tn
            qs = _in_proj(h, hg_w_in, slot, tn, "hg_in_q", n_dim=D_MODEL, epilogue=_silu_epilogue,
                          out_dtypes=(BF16,))
            v = _in_proj(h, hg_w_in, slot, tn, "hg_in_v", n_dim=D_MODEL, first_tile=per, out_dtypes=(BF16,))
            lb_spec = pl.BlockSpec((None, DEPTH, tn), lambda j, r: (j // per, 0, j % per))
            logf, key = _in_proj(h, hg_w_in, slot, tn, "hg_in_f", n_dim=2 * D_MODEL, first_tile=2 * per,
                                 epilogue=functools.partial(_hg_gate_epilogue, i), aux=[(hg_lb, lb_spec)],
                                 out_dtypes=(F32, BF16))
            gate = _in_proj(h, hg_w_in, slot, tn, "hg_in_g", n_dim=D_MODEL, first_tile=4 * per)
            o2 = _gla_scan(qs, v, logf, key)
            xs, hf = _out_proj(_hg_readout_prologue, [((o2, 0), 0), ((o2, 1), 0), (gate, 0)],
                               hg_norm_g[slot].reshape(1, D_MODEL), hg_out_bf, tm=ROW_TILE, name="hg_out", **op)
        else:
            qkv = _in_proj(h, na_w_qkv, slot, 1024, "na_in")
            o_lat = _na_attention(qkv, _na_bias_table(na_rpb[slot]))
            o_ctx = _ctx_attention(qkv)
            assert T_CTX == ROW_TILE
            last_lat = T_LAT // ROW_TILE - 1
            xs, hf = _out_proj(_na_prologue, [(o_lat, 0, functools.partial(jnp.minimum, last_lat)),
                                              (o_ctx, 0, lambda t: 0)],
                               None, na_out_bf, tm=ROW_TILE, name="na_out", **op)
        nxt = min(i + 1, DEPTH - 1)
        xs, h = _ffn(hf, xs, i, ffn_up_bf, ffn_conv_w, ffn_conv_b, ffn_down_bf, norm_g[i, 3], mod_l,
                     norm_g[nxt, 0], mod[nxt], n_rows=n_rows, emit_h=not last)

    return xs.reshape(BATCH, SEQ, D_MODEL)
```

```python
import functools

import numpy as np
import jax
import jax.numpy as jnp
from jax import lax
from jax.experimental import pallas as pl
from jax.experimental.pallas import tpu as pltpu

F32 = jnp.float32
BF16 = jnp.bfloat16

D_MODEL = 2048
BATCH = 2
SEQ = 4096
DEPTH = 4
GRID_W = 64
CTX_LEN = 256
N_MIXERS = 3
RMS_EPS = 1e-6
SSM_INNER = 4096
SSM_HEAD_DIM = 64
SSM_HEADS = 64
SSM_STATE = 128
SSM_GROUPS = 8
SSM_CONV_DIM = SSM_INNER + 2 * SSM_GROUPS * SSM_STATE
SSM_IN_DIM = SSM_INNER + SSM_CONV_DIM + 2 * SSM_HEADS
SSM_CHUNK = 128
HG_HEADS = 16
HG_KEY_DIM = 128
HG_IN_DIM = 5 * D_MODEL
HG_CHUNK = 64
NA_HEADS = 16
NA_HEAD_DIM = 128
NA_WIN_R = 8
NA_WIN_C = 16
ROPE_BASE = 10000.0
FFN_HIDDEN = 5632

T_LAT = BATCH * SEQ
T_CTX = BATCH * CTX_LEN
T_ALL = T_LAT + T_CTX
LANES = 128
SUBLANES = 8
BF16_ROWS = 16
ONES_ROWS = 16
ROW_TILE = 512
FFN_COL_CHUNK = 512
MIB = 1024 * 1024
LOG2E = 1.4426950408889634


def _cparams(sem, vmem_mib):
    return pltpu.CompilerParams(dimension_semantics=sem, vmem_limit_bytes=vmem_mib * MIB)


def _silu(v):
    return v * jax.nn.sigmoid(v)


def _rms(v):
    return v * lax.rsqrt(jnp.mean(v * v, axis=-1, keepdims=True) + RMS_EPS)


def _log1p_unit(e):
    return jnp.log(1.0 + e)


def _softplus(v):
    return jnp.maximum(v, 0.0) + _log1p_unit(jnp.exp(-jnp.abs(v)))


def _dot(a, b):
    return jnp.dot(a, b, preferred_element_type=F32)


def _dot_nt(a, b):
    return lax.dot_general(a, b, (((1,), (1,)), ((), ())), preferred_element_type=F32)


def _dot_tn(a, b):
    return lax.dot_general(a, b, (((0,), (0,)), ((), ())), preferred_element_type=F32)


def _bf16_pieces(v, terms):
    pieces = []
    rem = v
    for _ in range(terms):
        part = rem.astype(BF16)
        rem = rem - part.astype(F32)
        pieces.append(part)
    return pieces


def _dot_sel(m01_rep, v, terms):
    return _dot(m01_rep, jnp.concatenate(_bf16_pieces(v, terms), axis=0))


def _dot_sel_rhs(v, m01_rep, terms):
    return _dot(jnp.concatenate(_bf16_pieces(v, terms), axis=1), m01_rep)


def _mod_row(row0):
    return jnp.where(row0 >= T_LAT, 2, row0 // SEQ)


def _mod_vec(mod_ref, r, idx):
    return mod_ref[pl.ds(r, 1), idx * D_MODEL:(idx + 1) * D_MODEL]


def _norm_mod(x, g, mod_ref, r, shift_idx, scale_idx):
    return _rms(x) * (g * (1.0 + _mod_vec(mod_ref, r, scale_idx))) + _mod_vec(mod_ref, r, shift_idx)


def _seg_edges(row0, rows):
    gr = row0 + lax.broadcasted_iota(jnp.int32, (rows, 1), 0)
    pos = jnp.where(gr >= T_LAT, (gr - T_LAT) % CTX_LEN, gr % SEQ)
    length = jnp.where(gr >= T_LAT, CTX_LEN, SEQ)
    return pos == 0, pos == length - 1


def _shift_rows(cur, prev_row, next_row, row0):
    rows = cur.shape[0]
    first, last = _seg_edges(row0, rows)
    sub = lax.broadcasted_iota(jnp.int32, (SUBLANES, 1), 0)
    up = pltpu.roll(cur, 1, axis=0)
    up = jnp.concatenate([jnp.where(sub == 0, prev_row, up[:SUBLANES]), up[SUBLANES:]], axis=0)
    up = jnp.where(first, 0.0, up)
    dn = pltpu.roll(cur, rows - 1, axis=0)
    dn = jnp.concatenate([dn[:rows - SUBLANES], jnp.where(sub == SUBLANES - 1, next_row, dn[rows - SUBLANES:])],
                         axis=0)
    dn = jnp.where(last, 0.0, dn)
    return up, dn


def _token_conv(cur, prev_row, next_row, row0, w_ref, b_ref):
    up, dn = _shift_rows(cur, prev_row, next_row, row0)
    return b_ref[...] + up * w_ref[0:1, :] + cur * w_ref[1:2, :] + dn * w_ref[2:3, :]


def _halo_specs(tm, halo, width, n_rows, col_fn):
    per = tm // halo
    last_blk = n_rows // halo - 1
    prev = pl.BlockSpec((halo, width), lambda i, j: (jnp.maximum(i * per - 1, 0), col_fn(j)))
    nxt = pl.BlockSpec((halo, width), lambda i, j: (jnp.minimum((i + 1) * per, last_blk), col_fn(j)))
    return prev, nxt


def _adaln_body(c_ref, w_ref, b_ref, o_ref):
    s = _silu(c_ref[...]).astype(BF16)
    o_ref[...] = _dot(s, w_ref[...].astype(BF16)) + b_ref[...]


def _adaln(cvec8, w_mod, b_mod):
    tn = 1024
    n_out = 6 * D_MODEL
    return pl.pallas_call(
        _adaln_body,
        grid=(DEPTH, n_out // tn),
        in_specs=[
            pl.BlockSpec((SUBLANES, D_MODEL), lambda l, j: (0, 0)),
            pl.BlockSpec((None, D_MODEL, tn), lambda l, j: (l, 0, j)),
            pl.BlockSpec((None, 1, tn), lambda l, j: (l, 0, j)),
        ],
        out_specs=pl.BlockSpec((None, SUBLANES, tn), lambda l, j: (l, 0, j)),
        out_shape=jax.ShapeDtypeStruct((DEPTH, SUBLANES, n_out), F32),
        compiler_params=_cparams(("arbitrary", "arbitrary"), 40),
        name="adaln",
    )(cvec8, w_mod, b_mod.reshape(DEPTH, 1, n_out))


def _prep_body(lat_ref, ctx_ref, g_ref, mod_ref, xs_ref, h_ref):
    row0 = pl.program_id(0) * ROW_TILE

    def emit(src_ref):
        xv = src_ref[...]
        xs_ref[...] = xv
        h_ref[...] = _norm_mod(xv, g_ref[...], mod_ref, _mod_row(row0), 0, 1).astype(BF16)

    @pl.when(row0 < T_LAT)
    def _():
        emit(lat_ref)

    @pl.when(row0 >= T_LAT)
    def _():
        emit(ctx_ref)


def _prep(x_lat, x_ctx, g_row, mod_l):
    assert T_CTX == ROW_TILE
    n_lat = T_LAT // ROW_TILE
    row = pl.BlockSpec((ROW_TILE, D_MODEL), lambda i: (i, 0))
    return pl.pallas_call(
        _prep_body,
        grid=(T_ALL // ROW_TILE,),
        in_specs=[pl.BlockSpec((ROW_TILE, D_MODEL), lambda i: (jnp.minimum(i, n_lat - 1), 0)),
                  pl.BlockSpec((ROW_TILE, D_MODEL), lambda i: (0, 0)),
                  pl.BlockSpec((1, D_MODEL), lambda i: (0, 0)), pl.BlockSpec(mod_l.shape, lambda i: (0, 0))],
        out_specs=[row, row],
        out_shape=[jax.ShapeDtypeStruct((T_ALL, D_MODEL), F32), jax.ShapeDtypeStruct((T_ALL, D_MODEL), BF16)],
        compiler_params=_cparams(("arbitrary",), 40),
        name="prep",
    )(x_lat, x_ctx, g_row.reshape(1, D_MODEL), mod_l)


def _in_proj_body(epilogue, n_aux, h_ref, w_ref, *rest):
    aux, outs, wb_ref = rest[:n_aux], rest[n_aux:-1], rest[-1]

    @pl.when(pl.program_id(1) == 0)
    def _():
        wb_ref[...] = w_ref[...].astype(BF16)

    u = _dot(h_ref[...], wb_ref[...])
    vals = (u,) if epilogue is None else epilogue(u, *aux)
    for o_ref, val in zip(outs, vals):
        o_ref[...] = val.astype(o_ref.dtype)


def _in_proj(h, w_stack, slot, tn, name, n_dim=None, first_tile=0, epilogue=None, aux=(), out_dtypes=(F32,)):
    tm = T_ALL // 8
    n_dim = w_stack.shape[2] if n_dim is None else n_dim
    res = pl.pallas_call(
        functools.partial(_in_proj_body, epilogue, len(aux)),
        grid=(n_dim // tn, T_ALL // tm),
        in_specs=[pl.BlockSpec((tm, D_MODEL), lambda j, i: (i, 0)),
                  pl.BlockSpec((None, D_MODEL, tn), lambda j, i: (slot, 0, first_tile + j))]
        + [spec for _, spec in aux],
        out_specs=[pl.BlockSpec((tm, tn), lambda j, i: (i, j)) for _ in out_dtypes],
        out_shape=[jax.ShapeDtypeStruct((T_ALL, n_dim), dt) for dt in out_dtypes],
        scratch_shapes=[pltpu.VMEM((D_MODEL, tn), BF16)],
        compiler_params=_cparams(("arbitrary", "arbitrary"), 56),
        name=name,
    )(h, w_stack, *[a for a, _ in aux])
    return res[0] if len(out_dtypes) == 1 else tuple(res)


def _ssd_gate_prologue(row0, yf_ref, yb_ref, z_ref, g_ref):
    v = (yf_ref[...].astype(F32) + yb_ref[...].astype(F32)) * _silu(z_ref[...])
    return v * g_ref[...], lax.rsqrt(jnp.mean(v * v, axis=-1, keepdims=True) + RMS_EPS)


def _hg_readout_prologue(row0, of_ref, ob_ref, gate_ref, g_ref):
    o = of_ref[...].astype(F32) + ob_ref[...].astype(F32)
    parts = [_rms(o[:, h * LANES:(h + 1) * LANES]) for h in range(o.shape[1] // LANES)]
    return jnp.concatenate(parts, axis=-1) * g_ref[...] * _silu(gate_ref[...]), None


def _na_prologue(row0, o_lat_ref, o_ctx_ref):
    return jnp.where(row0 >= T_LAT, o_ctx_ref[...], o_lat_ref[...]), None


def _out_proj_body(prologue, n_in, tm, *refs):
    ins = refs[:n_in]
    w_ref, x_ref, g1_ref, g2_ref, mod_ref, xo_ref, ho_ref = refs[n_in:]
    part, row_scale = prologue(pl.program_id(0) * tm, *ins)
    y = _dot(part.astype(BF16), w_ref[...])
    if row_scale is not None:
        y = y * row_scale
    r = _mod_row(pl.program_id(0) * tm)
    xn = x_ref[...] + _rms(y) * (_mod_vec(mod_ref, r, 2) * g1_ref[...])
    xo_ref[...] = xn
    ho_ref[...] = _norm_mod(xn, g2_ref[...], mod_ref, r, 3, 4).astype(BF16)


def _out_proj(prologue, row_ins, gain, w_stack, slot, xs, g_res, g_ffn, mod_l, *, n_rows, tm, name):
    k_dim = w_stack.shape[1]
    in_specs, args = [], []
    for arr, cb, *row_fn in row_ins:
        row_fn = row_fn[0] if row_fn else (lambda i: i)
        if isinstance(arr, tuple):
            in_specs.append(pl.BlockSpec((None, tm, k_dim), functools.partial(
                lambda i, ld, cb, row_fn: (ld, row_fn(i), cb), ld=arr[1], cb=cb, row_fn=row_fn)))
            args.append(arr[0])
        else:
            in_specs.append(pl.BlockSpec((tm, k_dim), functools.partial(
                lambda i, cb, row_fn: (row_fn(i), cb), cb=cb, row_fn=row_fn)))
            args.append(arr)
    if gain is not None:
        in_specs.append(pl.BlockSpec((1, k_dim), lambda i: (0, 0)))
        args.append(gain)
    n_in = len(args)
    row = pl.BlockSpec((tm, D_MODEL), lambda i: (i, 0))
    vec = pl.BlockSpec((1, D_MODEL), lambda i: (0, 0))
    in_specs += [pl.BlockSpec((None, k_dim, D_MODEL), lambda i: (slot, 0, 0), pipeline_mode=pl.Buffered(1)),
                 row, vec, vec, pl.BlockSpec(mod_l.shape, lambda i: (0, 0))]
    return pl.pallas_call(
        functools.partial(_out_proj_body, prologue, n_in, tm),
        grid=(n_rows // tm,),
        in_specs=in_specs,
        out_specs=[row, row],
        out_shape=[jax.ShapeDtypeStruct((n_rows, D_MODEL), F32), jax.ShapeDtypeStruct((n_rows, D_MODEL), BF16)],
        compiler_params=_cparams(("arbitrary",), 60),
        name=name,
    )(*args, w_stack, xs, g_res.reshape(1, D_MODEL), g_ffn.reshape(1, D_MODEL), mod_l)


def _ffn_body(emit_h, has_ctx, h_ref, hp_ref, hn_ref, wa_ref, wv_ref, cwa_ref, cwv_ref, cba_ref, cbv_ref, wd_ref,
              x_ref, g_res_ref, mod_ref, g_next_ref, modn_ref, *out_and_scratch):
    if emit_h:
        xo_ref, ho_ref, hall_ref, acc_ref = out_and_scratch
    else:
        xo_ref, hall_ref, acc_ref = out_and_scratch
    k = pl.program_id(1)
    tm = ROW_TILE
    row0 = pl.program_id(0) * tm
    pad = BF16_ROWS
    rows = tm + 2 * pad
    s8 = SUBLANES

    @pl.when(k == 0)
    def _():
        hall_ref[pl.ds(0, pad), :] = hp_ref[...]
        hall_ref[pl.ds(pad, tm), :] = h_ref[...]
        hall_ref[pl.ds(pad + tm, pad), :] = hn_ref[...]
        acc_ref[...] = jnp.zeros_like(acc_ref)

    hall = hall_ref[...]
    edge_first = row0 % SEQ == 0
    edge_last = ((row0 + tm) % SEQ == 0) | (row0 + tm == T_ALL)
    is_ctx = row0 >= T_LAT
    joints = list(range(CTX_LEN, tm, CTX_LEN)) if has_ctx else []
    sub = lax.broadcasted_iota(jnp.int32, (s8, 1), 0)

    def patch(v, start, cond):
        return jnp.concatenate([v[:start], jnp.where(cond, 0.0, v[start:start + s8]), v[start + s8:]], axis=0)

    def conv(w_ref, cw_ref, cb_ref, cs):
        u = _dot(hall, w_ref[:, cs])
        u = patch(u, pad - s8, edge_first)
        u = patch(u, pad + tm, edge_last)
        up = pltpu.roll(u, 1, axis=0)[pad:pad + tm]
        dn = pltpu.roll(u, rows - 1, axis=0)[pad:pad + tm]
        for j in joints:
            up = patch(up, j, is_ctx & (sub == 0))
            dn = patch(dn, j - s8, is_ctx & (sub == s8 - 1))
        return cb_ref[:, cs] + up * cw_ref[0:1, cs] + u[pad:pad + tm] * cw_ref[1:2, cs] + dn * cw_ref[2:3, cs]

    chunks = [pl.ds(c * FFN_COL_CHUNK, FFN_COL_CHUNK) for c in range(wa_ref.shape[1] // FFN_COL_CHUNK)]
    acts = [(_silu(conv(wa_ref, cwa_ref, cba_ref, cs)) * conv(wv_ref, cwv_ref, cbv_ref, cs)).astype(BF16)
            for cs in chunks]
    acc_ref[...] += _dot(jnp.concatenate(acts, axis=1), wd_ref[...])

    @pl.when(k == pl.num_programs(1) - 1)
    def _():
        r = _mod_row(row0)
        xn = x_ref[...] + _rms(acc_ref[...]) * (_mod_vec(mod_ref, r, 5) * g_res_ref[...])
        xo_ref[...] = xn
        if emit_h:
            ho_ref[...] = _norm_mod(xn, g_next_ref[...], modn_ref, r, 0, 1).astype(BF16)


def _ffn(h, xs, layer, w_up, conv_w, conv_b, w_down, g_res, mod_l, g_next, mod_next, *, n_rows, emit_h):
    tm, tk = ROW_TILE, 512
    kt = FFN_HIDDEN // tk
    h_prev, h_next = _halo_specs(tm, BF16_ROWS, D_MODEL, n_rows, lambda k: 0)
    row = pl.BlockSpec((tm, D_MODEL), lambda i, k: (i, 0))
    vec = pl.BlockSpec((1, D_MODEL), lambda i, k: (0, 0))
    tab = pl.BlockSpec(mod_l.shape, lambda i, k: (0, 0))
    out_specs = [row, row] if emit_h else [row]
    out_shape = [jax.ShapeDtypeStruct((n_rows, D_MODEL), F32)]
    if emit_h:
        out_shape.append(jax.ShapeDtypeStruct((n_rows, D_MODEL), BF16))
    conv_b3 = conv_b.reshape(DEPTH, 1, 2 * FFN_HIDDEN)

    res = pl.pallas_call(
        functools.partial(_ffn_body, emit_h, n_rows > T_LAT),
        grid=(n_rows // tm, kt),
        in_specs=[
            row, h_prev, h_next,
            pl.BlockSpec((None, D_MODEL, tk), lambda i, k: (layer, 0, k)),
            pl.BlockSpec((None, D_MODEL, tk), lambda i, k: (layer, 0, k + kt)),
            pl.BlockSpec((None, 3, tk), lambda i, k: (layer, 0, k)),
            pl.BlockSpec((None, 3, tk), lambda i, k: (layer, 0, k + kt)),
            pl.BlockSpec((None, 1, tk), lambda i, k: (layer, 0, k)),
            pl.BlockSpec((None, 1, tk), lambda i, k: (layer, 0, k + kt)),
            pl.BlockSpec((None, tk, D_MODEL), lambda i, k: (layer, k, 0)),
            row, vec, tab, vec, tab,
        ],
        out_specs=out_specs,
        out_shape=out_shape,
        scratch_shapes=[pltpu.VMEM((tm + 2 * BF16_ROWS, D_MODEL), BF16), pltpu.VMEM((tm, D_MODEL), F32)],
        compiler_params=_cparams(("arbitrary", "arbitrary"), 56),
        name="conv_ffn",
    )(h, h, h, w_up, w_up, conv_w, conv_w, conv_b3, conv_b3, w_down, xs, g_res.reshape(1, D_MODEL), mod_l,
      g_next.reshape(1, D_MODEL), mod_next)
    return (res[0], res[1]) if emit_h else (res[0], None)


def _in_proj_conv_body(tm, h_ref, hp_ref, hn_ref, w_ref, cw_ref, cb_ref, o_ref, wb_ref):
    @pl.when(pl.program_id(1) == 0)
    def _():
        wb_ref[...] = w_ref[...].astype(BF16)

    w = wb_ref[...]
    u = _dot(h_ref[...], w)
    uh = _dot(jnp.concatenate([hp_ref[...], hn_ref[...]], axis=0), w)
    conv = _token_conv(u, uh[BF16_ROWS - 1:BF16_ROWS], uh[BF16_ROWS:BF16_ROWS + 1], pl.program_id(1) * tm,
                       cw_ref, cb_ref)
    o_ref[...] = _silu(conv).astype(o_ref.dtype)


def _in_proj_conv(h, w_stack, slot, conv_w, conv_b, tn, first_tile, name):
    tm = T_ALL // 8
    n_dim = conv_w.shape[1]
    per = tm // BF16_ROWS
    last_blk = T_ALL // BF16_ROWS - 1
    return pl.pallas_call(
        functools.partial(_in_proj_conv_body, tm),
        grid=(n_dim // tn, T_ALL // tm),
        in_specs=[pl.BlockSpec((tm, D_MODEL), lambda j, i: (i, 0)),
                  pl.BlockSpec((BF16_ROWS, D_MODEL), lambda j, i: (jnp.maximum(i * per - 1, 0), 0)),
                  pl.BlockSpec((BF16_ROWS, D_MODEL), lambda j, i: (jnp.minimum((i + 1) * per, last_blk), 0)),
                  pl.BlockSpec((None, D_MODEL, tn), lambda j, i: (slot, 0, first_tile + j)),
                  pl.BlockSpec((3, tn), lambda j, i: (0, j)),
                  pl.BlockSpec((1, tn), lambda j, i: (0, j))],
        out_specs=pl.BlockSpec((tm, tn), lambda j, i: (i, j)),
        out_shape=jax.ShapeDtypeStruct((T_ALL, n_dim), BF16),
        scratch_shapes=[pltpu.VMEM((D_MODEL, tn), BF16)],
        compiler_params=_cparams(("arbitrary", "arbitrary"), 56),
        name=name,
    )(h, h, h, w_stack, conv_w, conv_b.reshape(1, -1))


def _scan_chunk_index(d, b, s, n_lat, n_ctx):
    ctx0 = BATCH * n_lat + b * n_ctx
    fwd = jnp.where(s < n_ctx, ctx0 + s, b * n_lat + s - n_ctx)
    bwd = jnp.where(s < n_ctx, ctx0 + n_ctx - 1 - s, b * n_lat + n_lat - 1 - (s - n_ctx))
    return jnp.where(d == 0, fwd, bwd)


SSD_CUM_TERMS = 3
SSD_DT_TERMS = 2


def _ssd_scan_consts():
    q = SSM_CHUNK
    i = np.arange(q)
    pfx = np.zeros((2, q + ONES_ROWS, q), np.float32)
    pfx[0, :q] = i[:, None] >= i[None, :]
    pfx[1, :q] = i[:, None] <= i[None, :]
    pfx[:, q:] = 1.0
    pfx = np.tile(pfx, (1, 1, SSD_CUM_TERMS))
    expand = np.repeat(np.eye(SSM_HEADS, dtype=np.float32), SSM_HEAD_DIM, axis=1)
    expand = np.tile(expand, (SSD_DT_TERMS, 1))
    return jnp.asarray(pfx, BF16), jnp.asarray(expand, BF16)


def _ssd_scan_body(x_ref, b_ref, c_ref, dtraw_ref, dtb_ref, alog_ref, dskip_ref, pfx_ref, exp_ref, y_ref, s_ref):
    q = SSM_CHUNK
    hp = 2 * SSM_HEAD_DIM
    d = pl.program_id(0)

    @pl.when(pl.program_id(2) == 0)
    def _():
        s_ref[...] = jnp.zeros_like(s_ref)

    def pick(v):
        return jnp.where(d == 0, v[:, :SSM_HEADS], v[:, SSM_HEADS:])

    dt2 = _softplus(dtraw_ref[...] + dtb_ref[...])
    cum2 = _dot_sel(pfx_ref[...], dt2 * (-LOG2E * jnp.exp(alog_ref[...])), SSD_CUM_TERMS)
    acum2 = cum2[:q]
    acum = pick(acum2)
    tot = pick(cum2[q:q + SUBLANES])[0:1]
    acum_t2 = acum2.T
    acum_t = jnp.where(d == 0, acum_t2[:SSM_HEADS], acum_t2[SSM_HEADS:])
    dt_wide = _dot_sel_rhs(pick(dt2), exp_ref[...], SSD_DT_TERMS)

    ii = lax.broadcasted_iota(jnp.int32, (q, q), 0)
    jj = lax.broadcasted_iota(jnp.int32, (q, q), 1)
    causal = (jj - ii) * (1 - 2 * d) <= 0
    low_half = lax.broadcasted_iota(jnp.int32, (q, hp), 1) < SSM_HEAD_DIM

    heads_per_group = SSM_HEADS // SSM_GROUPS
    n_pairs = SSM_HEADS // 2

    b_gs = [b_ref[:, g * SSM_STATE:(g + 1) * SSM_STATE].astype(BF16) for g in range(SSM_GROUPS)]
    c_bf = [c_ref[:, g * SSM_STATE:(g + 1) * SSM_STATE].astype(BF16) for g in range(SSM_GROUPS)]
    c_gs = [c.astype(F32) for c in c_bf]
    cbs = [jnp.where(causal, _dot_nt(c_bf[g], b_gs[g]), 0.0) for g in range(SSM_GROUPS)]

    lhs, xdts, xws, decs = [], [], [], []
    for p in range(n_pairs):
        lo = p * hp
        g = (2 * p) // heads_per_group
        cols = []
        for h in (2 * p, 2 * p + 1):
            col = jnp.broadcast_to(acum[:, h:h + 1], (q, q))
            seg = jnp.where(causal, col - acum_t[h:h + 1, :], 0.0)
            scores = jnp.exp2(seg) * cbs[g]
            c_dec = c_gs[g] * jnp.exp2(col)
            lhs.append(jnp.concatenate([scores.astype(BF16), c_dec.astype(BF16)], axis=1))
            cols.append(col)
        xdt = x_ref[:, lo:lo + hp].astype(F32) * dt_wide[:, lo:lo + hp]
        tot_pair = jnp.where(low_half[0:1], jnp.broadcast_to(tot[:, 2 * p:2 * p + 1], (1, hp)),
                             jnp.broadcast_to(tot[:, 2 * p + 1:2 * p + 2], (1, hp)))
        to_end = jnp.exp2(tot_pair - jnp.where(low_half, cols[0], cols[1]))
        xdts.append(xdt.astype(BF16))
        xws.append((xdt * to_end).astype(BF16))
        decs.append(jnp.exp2(tot_pair))

    for p in range(n_pairs):
        lo = p * hp
        rhs = jnp.concatenate([xdts[p], s_ref[:, lo:lo + hp].astype(BF16)], axis=0)
        y_pair = jnp.where(low_half, _dot(lhs[2 * p], rhs), _dot(lhs[2 * p + 1], rhs))
        y_ref[:, lo:lo + hp] = (y_pair + dskip_ref[:, lo:lo + hp] * x_ref[:, lo:lo + hp].astype(F32)
                                ).astype(y_ref.dtype)

    for p in range(n_pairs):
        lo = p * hp
        g = (2 * p) // heads_per_group
        s_ref[:, lo:lo + hp] = s_ref[:, lo:lo + hp] * decs[p] + _dot_tn(b_gs[g], xws[p])


def _ssd_scan(xbc, dt_raw, dt_bias, a_log, d_skip):
    q = SSM_CHUNK
    n_lat, n_ctx = SEQ // q, CTX_LEN // q
    pfx, expand = _ssd_scan_consts()
    n_heads2 = 2 * SSM_HEADS
    b_col = SSM_INNER // (SSM_GROUPS * SSM_STATE)
    d_wide = jnp.repeat(d_skip, SSM_HEAD_DIM, axis=1).reshape(2, 1, SSM_INNER)

    def chunk(d, b, s):
        return _scan_chunk_index(d, b, s, n_lat, n_ctx)

    return pl.pallas_call(
        _ssd_scan_body,
        grid=(2, BATCH, n_lat + n_ctx),
        in_specs=[
            pl.BlockSpec((q, SSM_INNER), lambda d, b, s: (chunk(d, b, s), 0)),
            pl.BlockSpec((q, SSM_GROUPS * SSM_STATE), lambda d, b, s: (chunk(d, b, s), b_col)),
            pl.BlockSpec((q, SSM_GROUPS * SSM_STATE), lambda d, b, s: (chunk(d, b, s), b_col + 1)),
            pl.BlockSpec((q, n_heads2), lambda d, b, s: (chunk(d, b, s), 0)),
            pl.BlockSpec((1, n_heads2), lambda d, b, s: (0, 0)),
            pl.BlockSpec((1, n_heads2), lambda d, b, s: (0, 0)),
            pl.BlockSpec((None, 1, SSM_INNER), lambda d, b, s: (d, 0, 0)),
            pl.BlockSpec((None,) + pfx.shape[1:], lambda d, b, s: (d, 0, 0)),
            pl.BlockSpec(expand.shape, lambda d, b, s: (0, 0)),
        ],
        out_specs=pl.BlockSpec((None, q, SSM_INNER), lambda d, b, s: (d, chunk(d, b, s), 0)),
        out_shape=jax.ShapeDtypeStruct((2, T_ALL, SSM_INNER), BF16),
        scratch_shapes=[pltpu.VMEM((SSM_STATE, SSM_INNER), F32)],
        compiler_params=_cparams(("arbitrary", "arbitrary", "arbitrary"), 48),
        name="ssd_scan",
    )(xbc, xbc, xbc, dt_raw, dt_bias.reshape(1, n_heads2), a_log.reshape(1, n_heads2), d_wide, pfx, expand)


GLA_SUM_TERMS = 2


def _gla_consts():
    q = HG_CHUNK
    levels = int(np.log2(q))
    i = np.arange(q)
    mats = [(i[:, None] >= i[None, :]).astype(np.float32), (i[None, :] > i[:, None]).astype(np.float32)]
    masks = [np.eye(q, dtype=np.float32)]
    for lv in range(levels):
        s = 1 << lv
        blk, pos = i // (2 * s), i % (2 * s)
        bound = blk * 2 * s + s - 1
        right = pos >= s
        t = i[None, :]
        m_right = right[:, None] & (t > bound[:, None]) & (t <= i[:, None])
        m_left = (~right)[:, None] & (t > i[:, None]) & (t <= bound[:, None])
        mats.append((m_right | m_left).astype(np.float32))
        masks.append((right[:, None] & (~right)[None, :] & (blk[:, None] == blk[None, :])).astype(np.float32))
    mats.append(np.ones((ONES_ROWS, q), np.float32))
    fwd_m, fwd_k = np.concatenate(mats, axis=0), np.stack(masks)
    flip_rows = np.concatenate([np.arange(k * q, (k + 1) * q)[::-1] for k in range(levels + 2)]
                               + [np.arange((levels + 2) * q, (levels + 2) * q + ONES_ROWS)])
    bwd_m = fwd_m[flip_rows][:, ::-1]
    bwd_k = fwd_k[:, ::-1, ::-1]
    mats2 = np.tile(np.stack([fwd_m, bwd_m]), (1, 1, GLA_SUM_TERMS))
    return (jnp.asarray(mats2, BF16), jnp.asarray(np.stack([fwd_k, bwd_k]), F32), levels)


def _silu_epilogue(u):
    return (_silu(u),)


def _hg_gate_epilogue(layer, u, lbp_ref):
    lbp = lbp_ref[...]
    e = jnp.exp(lbp - jnp.max(lbp, axis=0, keepdims=True))
    sm = e / jnp.sum(e, axis=0, keepdims=True)
    lb = jnp.sum(sm[1:layer + 1], axis=0, keepdims=True) if layer > 0 else jnp.zeros((1, lbp.shape[1]), F32)
    ef = jnp.exp(-jnp.abs(u))
    log_sig = jnp.minimum(u, 0.0) - _log1p_unit(ef)
    la = jnp.log(lb)
    lc = jnp.log1p(-lb) + log_sig
    logf = jnp.maximum(la, lc) + _log1p_unit(jnp.exp(-jnp.abs(la - lc)))
    return logf * LOG2E, 1.0 - jnp.exp(logf)


def _gla_body(levels, q_ref, v_ref, logf_ref, key_ref, mats_ref, masks_ref, o_ref, st_ref):
    q = HG_CHUNK

    @pl.when(pl.program_id(2) == 0)
    def _():
        st_ref[...] = jnp.zeros_like(st_ref)

    qs_bf, key_bf, v_bf = q_ref[...], key_ref[...], v_ref[...]

    sums = _dot_sel(mats_ref[...], logf_ref[...], GLA_SUM_TERMS)
    q_in = qs_bf * jnp.exp2(sums[:q]).astype(BF16)
    k_out = key_bf * jnp.exp2(sums[q:2 * q]).astype(BF16)
    dec = jnp.exp2(sums[(levels + 2) * q:(levels + 2) * q + 1])
    q_lv, k_lv = [], []
    for lv in range(levels):
        fac = jnp.exp2(sums[(lv + 2) * q:(lv + 3) * q]).astype(BF16)
        q_lv.append(qs_bf * fac)
        k_lv.append(key_bf * fac)

    heads = [slice(h * LANES, (h + 1) * LANES) for h in range(HG_HEADS)]
    attn = masks_ref[0][None] * jnp.stack([_dot_nt(qs_bf[:, sl], key_bf[:, sl]) for sl in heads], axis=0)
    for lv in range(levels):
        attn = attn + masks_ref[lv + 1][None] * jnp.stack(
            [_dot_nt(q_lv[lv][:, sl], k_lv[lv][:, sl]) for sl in heads], axis=0)
    attn = attn.astype(BF16)
    st_bf = st_ref[...].astype(BF16)
    o_ref[...] = jnp.concatenate(
        [_dot(attn[h], v_bf[:, sl]) + _dot_nt(q_in[:, sl], st_bf[h]) for h, sl in enumerate(heads)],
        axis=1).astype(o_ref.dtype)
    for h, sl in enumerate(heads):
        st_ref[h] = st_ref[h] * dec[:, sl] + _dot_tn(v_bf[:, sl], k_out[:, sl])


def _gla_scan(qs, v, logf, key):
    q = HG_CHUNK
    n_lat, n_ctx = SEQ // q, CTX_LEN // q
    mats, masks, levels = _gla_consts()

    def chunk(d, b, s):
        return _scan_chunk_index(d, b, s, n_lat, n_ctx)

    return pl.pallas_call(
        functools.partial(_gla_body, levels),
        grid=(2, BATCH, n_lat + n_ctx),
        in_specs=[
            pl.BlockSpec((q, D_MODEL), lambda d, b, s: (chunk(d, b, s), 0)),
            pl.BlockSpec((q, D_MODEL), lambda d, b, s: (chunk(d, b, s), 0)),
            pl.BlockSpec((q, D_MODEL), lambda d, b, s: (chunk(d, b, s), d)),
            pl.BlockSpec((q, D_MODEL), lambda d, b, s: (chunk(d, b, s), d)),
            pl.BlockSpec((None,) + mats.shape[1:], lambda d, b, s: (d, 0, 0)),
            pl.BlockSpec((None,) + masks.shape[1:], lambda d, b, s: (d, 0, 0, 0)),
        ],
        out_specs=pl.BlockSpec((None, q, D_MODEL), lambda d, b, s: (d, chunk(d, b, s), 0)),
        out_shape=jax.ShapeDtypeStruct((2, T_ALL, D_MODEL), BF16),
        scratch_shapes=[pltpu.VMEM((HG_HEADS, LANES, HG_KEY_DIM), F32)],
        compiler_params=_cparams(("arbitrary", "arbitrary", "arbitrary"), 48),
        name="gla_scan",
    )(qs, v, logf, key, mats, masks)


def _na_bias_body(rpb_ref, o_ref):
    h = pl.program_id(0)
    n_dc = 2 * NA_WIN_C - 1
    qc = lax.broadcasted_iota(jnp.int32, (GRID_W, LANES), 0)
    kc = lax.broadcasted_iota(jnp.int32, (GRID_W, LANES), 1) % GRID_W
    c0 = jnp.clip(qc - NA_WIN_C // 2, 0, GRID_W - NA_WIN_C)
    col_in = (kc >= c0) & (kc < c0 + NA_WIN_C)
    dc = jnp.clip(kc - qc + NA_WIN_C - 1, 0, n_dc - 1)
    low = lax.broadcasted_iota(jnp.int32, (GRID_W, LANES), 1) < GRID_W
    tiles = []
    for dr in range(2 * NA_WIN_R - 1):
        t = jnp.zeros((GRID_W, LANES), F32)
        for k in range(n_dc):
            t = jnp.where(dc == k, rpb_ref[(h * (2 * NA_WIN_R - 1) + dr) * n_dc + k], t)
        tiles.append(jnp.where(col_in, t, -jnp.inf))
    for dr in range(2 * NA_WIN_R - 2):
        o_ref[dr] = jnp.where(low, tiles[dr], tiles[dr + 1])


def _na_bias_table(rpb):
    n_dr = 2 * NA_WIN_R - 2
    return pl.pallas_call(
        _na_bias_body,
        grid=(NA_HEADS,),
        in_specs=[pl.BlockSpec(memory_space=pltpu.SMEM)],
        out_specs=pl.BlockSpec((None, n_dr, GRID_W, LANES), lambda h: (h, 0, 0, 0)),
        out_shape=jax.ShapeDtypeStruct((NA_HEADS, n_dr, GRID_W, LANES), F32),
        compiler_params=_cparams(("arbitrary",), 32),
        name="na_bias",
    )(rpb.reshape(-1))


def _rope_tables():
    quarter = NA_HEAD_DIM // 4
    inv = ROPE_BASE ** (-np.arange(quarter, dtype=np.float64) / quarter)
    t = np.arange(SEQ)
    row, col = t // GRID_W, t % GRID_W
    ang = np.concatenate([row[:, None] * inv[None], row[:, None] * inv[None],
                          col[:, None] * inv[None], col[:, None] * inv[None]], axis=1)
    sign = np.tile(np.concatenate([-np.ones(quarter), np.ones(quarter)]), 2)[None]
    return jnp.asarray(np.cos(ang), F32), jnp.asarray(np.sin(ang) * sign, F32)


ROPE_PERM_TERMS = 2


def _rope_partner_perm():
    quarter = NA_HEAD_DIM // 4
    lane = np.arange(NA_HEAD_DIM)
    partner = np.where(lane % (2 * quarter) < quarter, lane + quarter, lane - quarter)
    perm = np.zeros((NA_HEAD_DIM, NA_HEAD_DIM), np.float32)
    perm[partner, lane] = 1.0
    return jnp.asarray(np.tile(perm, (ROPE_PERM_TERMS, 1)), BF16)


def _rope(v, cos, sin_signed, perm_rep):
    return v * cos + _dot_sel_rhs(v, perm_rep, ROPE_PERM_TERMS) * sin_signed


def _na_body(rows_per_step, q_ref, k_ref, v_ref, kc_ref, vc_ref, cosq_ref, sinq_ref, cos_ref, sin_ref, bias_ref,
             perm_ref, o_ref, kr_ref, vb_ref):
    rb = pl.program_id(2)
    n_rows = SEQ // GRID_W
    scale = NA_HEAD_DIM ** -0.5

    @pl.when(rb == 0)
    def _():
        kr_ref[...] = _rope(k_ref[...], cos_ref[...], sin_ref[...], perm_ref[...]).astype(BF16)
        vb_ref[...] = v_ref[...].astype(BF16)

    qr = (_rope(q_ref[...], cosq_ref[...], sinq_ref[...], perm_ref[...]) * scale).astype(BF16)
    kc = kc_ref[...].astype(BF16)
    vc = vc_ref[...].astype(BF16)
    starts, s_rows = [], []
    for lr in range(rows_per_step):
        r = rb * rows_per_step + lr
        r0 = jnp.clip(r - NA_WIN_R // 2, 0, n_rows - NA_WIN_R)
        start = pl.multiple_of(r0 * GRID_W, GRID_W)
        dr0 = r0 - r + NA_WIN_R - 1
        bias = jnp.concatenate([bias_ref[dr0 + 2 * p] for p in range(NA_WIN_R // 2)], axis=1)
        k_win = kr_ref[pl.ds(start, NA_WIN_R * GRID_W), :]
        s_rows.append(_dot_nt(qr[lr * GRID_W:(lr + 1) * GRID_W], k_win) + bias)
        starts.append(start)
    s_loc = jnp.concatenate(s_rows, axis=0)
    s_ctx = _dot_nt(qr, kc)
    m = jnp.maximum(jnp.max(s_loc, axis=-1, keepdims=True), jnp.max(s_ctx, axis=-1, keepdims=True))
    p_loc = jnp.exp(s_loc - m)
    p_ctx = jnp.exp(s_ctx - m)
    denom = jnp.sum(p_loc, axis=-1, keepdims=True) + jnp.sum(p_ctx, axis=-1, keepdims=True)
    p_loc = p_loc.astype(BF16)
    o_loc = jnp.concatenate(
        [_dot(p_loc[lr * GRID_W:(lr + 1) * GRID_W], vb_ref[pl.ds(starts[lr], NA_WIN_R * GRID_W), :])
         for lr in range(rows_per_step)], axis=0)
    o = o_loc + _dot(p_ctx.astype(BF16), vc)
    o_ref[...] = (o / denom).astype(o_ref.dtype)


def _na_attention(qkv, bias_tbl):
    rows_per_step = 64
    tq = rows_per_step * GRID_W
    steps = SEQ // tq
    cos, sin = _rope_tables()
    hd = NA_HEAD_DIM
    ctx_blk0 = T_LAT // CTX_LEN
    return pl.pallas_call(
        functools.partial(_na_body, rows_per_step),
        grid=(BATCH, NA_HEADS, steps),
        in_specs=[
            pl.BlockSpec((tq, hd), lambda b, h, r: (b * steps + r, h)),
            pl.BlockSpec((SEQ, hd), lambda b, h, r: (b, NA_HEADS + h)),
            pl.BlockSpec((SEQ, hd), lambda b, h, r: (b, 2 * NA_HEADS + h)),
            pl.BlockSpec((CTX_LEN, hd), lambda b, h, r: (ctx_blk0 + b, NA_HEADS + h)),
            pl.BlockSpec((CTX_LEN, hd), lambda b, h, r: (ctx_blk0 + b, 2 * NA_HEADS + h)),
            pl.BlockSpec((tq, hd), lambda b, h, r: (r, 0)),
            pl.BlockSpec((tq, hd), lambda b, h, r: (r, 0)),
            pl.BlockSpec((SEQ, hd), lambda b, h, r: (0, 0)),
            pl.BlockSpec((SEQ, hd), lambda b, h, r: (0, 0)),
            pl.BlockSpec((None,) + bias_tbl.shape[1:], lambda b, h, r: (h, 0, 0, 0)),
            pl.BlockSpec((ROPE_PERM_TERMS * hd, hd), lambda b, h, r: (0, 0)),
        ],
        out_specs=pl.BlockSpec((tq, hd), lambda b, h, r: (b * steps + r, h)),
        out_shape=jax.ShapeDtypeStruct((T_LAT, D_MODEL), BF16),
        scratch_shapes=[pltpu.VMEM((SEQ, hd), BF16), pltpu.VMEM((SEQ, hd), BF16)],
        compiler_params=_cparams(("arbitrary", "arbitrary", "arbitrary"), 48),
        name="na_attention",
    )(qkv, qkv, qkv, qkv, qkv, cos, sin, cos, sin, bias_tbl, _rope_partner_perm())


def _ctx_attn_body(q_ref, k_ref, v_ref, o_ref):
    scale = NA_HEAD_DIM ** -0.5
    s = _dot_nt((q_ref[...] * scale).astype(BF16), k_ref[...].astype(BF16))
    p = jnp.exp(s - jnp.max(s, axis=-1, keepdims=True))
    o = _dot(p.astype(BF16), v_ref[...].astype(BF16))
    o_ref[...] = (o / jnp.sum(p, axis=-1, keepdims=True)).astype(o_ref.dtype)


def _ctx_attention(qkv):
    hd = NA_HEAD_DIM
    ctx_blk0 = T_LAT // CTX_LEN
    return pl.pallas_call(
        _ctx_attn_body,
        grid=(BATCH, NA_HEADS),
        in_specs=[
            pl.BlockSpec((CTX_LEN, hd), lambda b, h: (ctx_blk0 + b, h)),
            pl.BlockSpec((CTX_LEN, hd), lambda b, h: (ctx_blk0 + b, NA_HEADS + h)),
            pl.BlockSpec((CTX_LEN, hd), lambda b, h: (ctx_blk0 + b, 2 * NA_HEADS + h)),
        ],
        out_specs=pl.BlockSpec((CTX_LEN, hd), lambda b, h: (b, h)),
        out_shape=jax.ShapeDtypeStruct((T_CTX, D_MODEL), BF16),
        compiler_params=_cparams(("arbitrary", "arbitrary"), 32),
        name="ctx_attention",
    )(qkv, qkv, qkv)


def kernel(x, c, ctx, c_ctx, w_mod, b_mod, norm_g, ffn_w_up, ffn_conv_w, ffn_conv_b, ffn_w_down, ssm_w_in, ssm_conv_w, ssm_conv_b, ssm_dt_bias, ssm_a_log, ssm_d, ssm_norm_g, ssm_w_out, hg_w_in, hg_lb, hg_norm_g, hg_w_out, na_w_qkv, na_rpb, na_w_out):
    cvec8 = jnp.concatenate([c, c_ctx[None], jnp.zeros((SUBLANES - BATCH - 1, D_MODEL), F32)], axis=0)
    mod = _adaln(cvec8, w_mod, b_mod)
    ffn_up_bf, ffn_down_bf = ffn_w_up.astype(BF16), ffn_w_down.astype(BF16)
    ssm_out_bf, hg_out_bf, na_out_bf = ssm_w_out.astype(BF16), hg_w_out.astype(BF16), na_w_out.astype(BF16)

    xs, h = _prep(x.reshape(T_LAT, D_MODEL), ctx.reshape(T_CTX, D_MODEL), norm_g[0, 0], mod[0])
    for i in range(DEPTH):
        last = i == DEPTH - 1
        kind, slot = i % N_MIXERS, i // N_MIXERS
        mod_l = mod[i]
        n_rows = T_LAT if last else T_ALL
        op = dict(xs=xs, g_res=norm_g[i, 1], g_ffn=norm_g[i, 2], mod_l=mod_l, n_rows=n_rows, slot=slot)
        if kind == 0:
            tn = 1024
            z = _in_proj(h, ssm_w_in, slot, tn, "ssd_in_z", n_dim=SSM_INNER)
            xbc = _in_proj_conv(h, ssm_w_in, slot, ssm_conv_w[slot], ssm_conv_b[slot], tn, SSM_INNER // tn,
                                "ssd_in_xbc")
            dt_raw = _in_proj(h, ssm_w_in, slot, 2 * SSM_HEADS, "ssd_dt", n_dim=2 * SSM_HEADS,
                              first_tile=(SSM_INNER + SSM_CONV_DIM) // (2 * SSM_HEADS))
            y2 = _ssd_scan(xbc, dt_raw, ssm_dt_bias[slot], ssm_a_log[slot], ssm_d[slot])
            xs, hf = _out_proj(_ssd_gate_prologue, [((y2, 0), 0), ((y2, 1), 0), (z, 0)],
                               ssm_norm_g[slot].reshape(1, SSM_INNER), ssm_out_bf, tm=256, name="ssd_out", **op)
        elif kind == 1:
            tn = 1024
            per = D_MODEL // tn
            qs = _in_proj(h, hg_w_in, slot, tn, "hg_in_q", n_dim=D_MODEL, epilogue=_silu_epilogue,
                          out_dtypes=(BF16,))
            v = _in_proj(h, hg_w_in, slot, tn, "hg_in_v", n_dim=D_MODEL, first_tile=per, out_dtypes=(BF16,))
            lb_spec = pl.BlockSpec((None, DEPTH, tn), lambda j, r: (j // per, 0, j % per))
            logf, key = _in_proj(h, hg_w_in, slot, tn, "hg_in_f", n_dim=2 * D_MODEL, first_tile=2 * per,
                                 epilogue=functools.partial(_hg_gate_epilogue, i), aux=[(hg_lb, lb_spec)],
                                 out_dtypes=(F32, BF16))
            gate = _in_proj(h, hg_w_in, slot, tn, "hg_in_g", n_dim=D_MODEL, first_tile=4 * per)
            o2 = _gla_scan(qs, v, logf, key)
            xs, hf = _out_proj(_hg_readout_prologue, [((o2, 0), 0), ((o2, 1), 0), (gate, 0)],
                               hg_norm_g[slot].reshape(1, D_MODEL), hg_out_bf, tm=ROW_TILE, name="hg_out", **op)
        else:
            qkv = _in_proj(h, na_w_qkv, slot, 1024, "na_in")
            o_lat = _na_attention(qkv, _na_bias_table(na_rpb[slot]))
            o_ctx = _ctx_attention(qkv)
            assert T_CTX == ROW_TILE
            last_lat = T_LAT // ROW_TILE - 1
            xs, hf = _out_proj(_na_prologue, [(o_lat, 0, functools.partial(jnp.minimum, last_lat)),
                                              (o_ctx, 0, lambda t: 0)],
                               None, na_out_bf, tm=ROW_TILE, name="na_out", **op)
        nxt = min(i + 1, DEPTH - 1)
        xs, h = _ffn(hf, xs, i, ffn_up_bf, ffn_conv_w, ffn_conv_b, ffn_down_bf, norm_g[i, 3], mod_l,
                     norm_g[nxt, 0], mod[nxt], n_rows=n_rows, emit_h=not last)

    return xs.reshape(BATCH, SEQ, D_MODEL)
```
